```python
import math
import jax
import jax.numpy as jnp
from jax import lax
import numpy as np

D_MODEL = 1024
BATCH = 4
SEQ = 4096
DEPTH = 2

GRID_W = 64
CTX_LEN = 256
Q_BLOCK = 128
NORM_EPS = 1e-6
ROPE_BASE = 10000.0

LRU_WIDTH = 512
LRU_BLOCKS = 8
LRU_BLOCK_W = LRU_WIDTH // LRU_BLOCKS
LRU_C = 8.0
CONV_W = 4
CONV_PAD_L = 2
LRU_A_MIN = 0.9
LRU_A_MAX = 0.999

MLA_HEADS = 4
MLA_Q_RANK = 384
MLA_KV_RANK = 256
MLA_NOPE = 128
MLA_ROPE = 64
MLA_V = 128
MLA_QK = MLA_NOPE + MLA_ROPE

EVEN_IN = 2 * LRU_WIDTH + MLA_Q_RANK + MLA_KV_RANK + MLA_ROPE
EVEN_MIX = LRU_WIDTH + MLA_HEADS * MLA_V

DIFF_HEADS = 8
DIFF_HEAD_DIM = 64
DIFF_V = 2 * DIFF_HEAD_DIM
ODD_IN = 3 * DIFF_HEADS * 2 * DIFF_HEAD_DIM
ODD_MIX = DIFF_HEADS * DIFF_V

N_EXPERTS = 32
TOP_K = 4
D_FF = 1024
SWIGLU_LIMIT = 7.0
SWIGLU_ALPHA = 1.702
MOE_BLOCK = 128

N_EVEN = (DEPTH + 1) // 2
N_ODD = DEPTH // 2

kernel_name = "hybrid_rglru_mla_diffattn_moe_dit"

F32 = jnp.float32


def _normal(key, shape, scale):
    return jax.random.normal(key, shape, F32) * scale


def rms_norm(x, g):
    xf = x.astype(F32)
    y = xf * lax.rsqrt(jnp.mean(xf * xf, axis=-1, keepdims=True) + NORM_EPS)
    return (y * g.astype(F32)).astype(x.dtype)


def axial_rope_tables(n_tokens, rot_dim):
    n_rows = n_tokens // GRID_W
    rows = jnp.repeat(jnp.arange(n_rows, dtype=F32), GRID_W)
    cols = jnp.tile(jnp.arange(GRID_W, dtype=F32), n_rows)
    axis_dim = rot_dim // 2
    inv_freq = ROPE_BASE ** (-jnp.arange(0, axis_dim, 2, dtype=F32) / axis_dim)
    ang_r = rows[:, None] * inv_freq
    ang_c = cols[:, None] * inv_freq
    ang = jnp.concatenate([ang_r, ang_r, ang_c, ang_c], axis=-1)
    return jnp.cos(ang), jnp.sin(ang)


def apply_axial_rope(x, cos, sin):
    r = x.shape[-1]
    quarter = r // 4
    xf = x.astype(F32)
    xs = xf.reshape(x.shape[:-1] + (2, 2, quarter))
    rot = jnp.stack([-xs[..., 1, :], xs[..., 0, :]], axis=-2).reshape(x.shape)
    bshape = (1, x.shape[1]) + (1,) * (x.ndim - 3) + (r,)
    return (xf * cos.reshape(bshape) + rot * sin.reshape(bshape)).astype(x.dtype)


def _to_q_blocks(q):
    b, h, s, d = q.shape
    return jnp.moveaxis(q.reshape(b, h, s // Q_BLOCK, Q_BLOCK, d), 2, 0)


def _from_q_blocks(o):
    nb, b, h, qb, d = o.shape
    return jnp.moveaxis(o, 0, 2).reshape(b, h, nb * qb, d)


def attend(q, k, v):
    scale = q.shape[-1] ** -0.5

    def one(qi):
        s = jnp.einsum('bhqd,bhkd->bhqk', qi, k).astype(F32) * scale
        p = jax.nn.softmax(s, axis=-1)
        return jnp.einsum('bhqk,bhkd->bhqd', p.astype(v.dtype), v)

    return _from_q_blocks(lax.map(one, _to_q_blocks(q)))


def diff_attend(q1, q2, k1, k2, v, lam):
    scale = q1.shape[-1] ** -0.5

    def one(qs):
        a, b = qs
        p1 = jax.nn.softmax(jnp.einsum('bhqd,bhkd->bhqk', a, k1).astype(F32) * scale, axis=-1)
        p2 = jax.nn.softmax(jnp.einsum('bhqd,bhkd->bhqk', b, k2).astype(F32) * scale, axis=-1)
        return jnp.einsum('bhqk,bhkd->bhqd', (p1 - lam * p2).astype(v.dtype), v)

    return _from_q_blocks(lax.map(one, (_to_q_blocks(q1), _to_q_blocks(q2))))


def heads_to_tokens(o):
    b, h, s, d = o.shape
    return o.transpose(0, 2, 1, 3).reshape(b, s, h * d)


def centred_dwconv(x, w, b):
    y = lax.conv_general_dilated(
        x, w[:, None, :], window_strides=(1,),
        padding=[(CONV_PAD_L, CONV_W - 1 - CONV_PAD_L)],
        dimension_numbers=('NWC', 'WIO', 'NWC'),
        feature_group_count=x.shape[-1])
    return y + b


def rglru_coeffs(xr, wa, ba, wx, bx, lam):
    b, l, w = xr.shape
    xb = xr.reshape(b, l, LRU_BLOCKS, LRU_BLOCK_W)
    r = jax.nn.sigmoid((jnp.einsum('blnd,nde->blne', xb, wa).reshape(b, l, w) + ba).astype(F32))
    i = jax.nn.sigmoid((jnp.einsum('blnd,nde->blne', xb, wx).reshape(b, l, w) + bx).astype(F32))
    log_a = -LRU_C * r * jax.nn.softplus(-lam.astype(F32))
    a = jnp.exp(log_a)
    u = jnp.sqrt(-jnp.expm1(2.0 * log_a)) * i * xr.astype(F32)
    return a, u


def linear_scan(a, u, h0, reverse):
    if reverse:
        a, u = jnp.flip(a, 1), jnp.flip(u, 1)

    def comb(left, right):
        return left[0] * right[0], right[0] * left[1] + right[1]

    a_cum, h = lax.associative_scan(comb, (a, u), axis=1)
    if h0 is not None:
        h = h + a_cum * h0[:, None, :]
    return jnp.flip(h, 1) if reverse else h


def _rope_tail(t, rope):
    cos, sin = rope
    return jnp.concatenate([t[..., :MLA_NOPE], apply_axial_rope(t[..., MLA_NOPE:], cos, sin)], axis=-1)


def mla_q(q_lat, q_norm_g, w_uq, qn_g, rope):
    b, l, _ = q_lat.shape
    q = (rms_norm(q_lat, q_norm_g) @ w_uq).reshape(b, l, MLA_HEADS, MLA_QK)
    q = rms_norm(q, qn_g)
    if rope is not None:
        q = _rope_tail(q, rope)
    return q.transpose(0, 2, 1, 3)


def mla_kv(kv_lat, k_rope, kv_norm_g, w_ukv, kn_g, rope):
    b, l, _ = kv_lat.shape
    kv = (rms_norm(kv_lat, kv_norm_g) @ w_ukv).reshape(b, l, MLA_HEADS, MLA_NOPE + MLA_V)
    k_nope, v = jnp.split(kv, [MLA_NOPE], axis=-1)
    k_r = jnp.broadcast_to(k_rope[:, :, None, :], (b, l, MLA_HEADS, MLA_ROPE))
    k = rms_norm(jnp.concatenate([k_nope, k_r], axis=-1), kn_g)
    if rope is not None:
        k = _rope_tail(k, rope)
    return k.transpose(0, 2, 1, 3), v.transpose(0, 2, 1, 3)


def lru_mla_mixer(hl, hc, w_in, w_out, conv_w, conv_b, wa, ba, wx, bx, lam,
                  q_norm_g, w_uq, kv_norm_g, w_ukv, qn_g, kn_g, need_ctx_out):
    splits = [LRU_WIDTH, 2 * LRU_WIDTH, 2 * LRU_WIDTH + MLA_Q_RANK,
              2 * LRU_WIDTH + MLA_Q_RANK + MLA_KV_RANK]
    gl, rl, qcl, kvcl, krl = jnp.split(hl @ w_in, splits, axis=-1)
    gc, rc, qcc, kvcc, krc = jnp.split(hc @ w_in, splits, axis=-1)

    xl = centred_dwconv(rl, conv_w, conv_b)
    xc = centred_dwconv(rc, conv_w, conv_b)
    rec_l, rec_c = [], []
    for d, rev in enumerate((False, True)):
        ac, uc = rglru_coeffs(xc, wa[d], ba[d], wx[d], bx[d], lam[d])
        hcs = linear_scan(ac, uc, None, rev)
        h0 = hcs[:, 0] if rev else hcs[:, -1]
        al, ul = rglru_coeffs(xl, wa[d], ba[d], wx[d], bx[d], lam[d])
        rec_l.append(linear_scan(al, ul, h0, rev))
        rec_c.append(hcs)
    lru_l = jax.nn.gelu(gl) * (rec_l[0] + rec_l[1]).astype(hl.dtype)

    rope = axial_rope_tables(hl.shape[1], MLA_ROPE)
    kl, vl = mla_kv(kvcl, krl, kv_norm_g, w_ukv, kn_g, rope)
    kc, vc = mla_kv(kvcc, krc, kv_norm_g, w_ukv, kn_g, None)
    ql = mla_q(qcl, q_norm_g, w_uq, qn_g, rope)
    mla_l = attend(ql, jnp.concatenate([kc, kl], axis=2), jnp.concatenate([vc, vl], axis=2))
    out_l = jnp.concatenate([lru_l, heads_to_tokens(mla_l)], axis=-1) @ w_out
    if not need_ctx_out:
        return out_l, None

    lru_c = jax.nn.gelu(gc) * (rec_c[0] + rec_c[1]).astype(hc.dtype)
    qc = mla_q(qcc, q_norm_g, w_uq, qn_g, None)
    mla_c = attend(qc, kc, vc)
    out_c = jnp.concatenate([lru_c, heads_to_tokens(mla_c)], axis=-1) @ w_out
    return out_l, out_c


def diff_mixer(hl, hc, w_in, w_out, qn_g, kn_g, lq1, lk1, lq2, lk2, subln_g, lam_init, need_ctx_out):
    def qkv(z, rope):
        b, l, _ = z.shape
        q, k, v = jnp.split(z, 3, axis=-1)
        q = rms_norm(q.reshape(b, l, DIFF_HEADS, 2, DIFF_HEAD_DIM), qn_g)
        k = rms_norm(k.reshape(b, l, DIFF_HEADS, 2, DIFF_HEAD_DIM), kn_g)
        if rope is not None:
            q = apply_axial_rope(q, *rope)
            k = apply_axial_rope(k, *rope)
        q = q.transpose(0, 2, 1, 3, 4)
        k = k.transpose(0, 2, 1, 3, 4)
        v = v.reshape(b, l, DIFF_HEADS, DIFF_V).transpose(0, 2, 1, 3)
        return q, k, v

    lam = (jnp.exp(jnp.sum(lq1.astype(F32) * lk1.astype(F32)))
           - jnp.exp(jnp.sum(lq2.astype(F32) * lk2.astype(F32))) + lam_init)

    def finish(o):
        o = rms_norm(o, subln_g) * (1.0 - lam_init)
        return heads_to_tokens(o) @ w_out

    rope = axial_rope_tables(hl.shape[1], DIFF_HEAD_DIM)
    ql, kl, vl = qkv(hl @ w_in, rope)
    qc, kc, vc = qkv(hc @ w_in, None)
    k_all = jnp.concatenate([kc, kl], axis=2)
    v_all = jnp.concatenate([vc, vl], axis=2)
    out_l = finish(diff_attend(ql[..., 0, :], ql[..., 1, :], k_all[..., 0, :], k_all[..., 1, :], v_all, lam))
    if not need_ctx_out:
        return out_l, None
    out_c = finish(diff_attend(qc[..., 0, :], qc[..., 1, :], kc[..., 0, :], kc[..., 1, :], vc, lam))
    return out_l, out_c


def moe_ffn(h, router_w, router_b, w_gu, b_gu, w_down, b_down):
    t, d = h.shape
    logits = (h @ router_w).astype(F32) + router_b.astype(F32)
    top_logit, top_e = lax.top_k(logits, TOP_K)
    top_w = jax.nn.softmax(top_logit, axis=-1)

    n_assign = t * TOP_K
    flat_e = top_e.reshape(-1)
    flat_tok = jnp.arange(n_assign, dtype=jnp.int32) // TOP_K
    flat_w = top_w.reshape(-1)
    order = jnp.argsort(flat_e)
    se = flat_e[order]
    counts = jnp.bincount(flat_e, length=N_EXPERTS)
    padded = (counts + MOE_BLOCK - 1) // MOE_BLOCK * MOE_BLOCK
    starts = jnp.cumsum(counts) - counts
    pends = jnp.cumsum(padded)
    pstarts = pends - padded
    dest = pstarts[se] + jnp.arange(n_assign, dtype=jnp.int32) - starts[se]

    n_blocks = -(-n_assign // MOE_BLOCK) + N_EXPERTS
    n_rows = n_blocks * MOE_BLOCK
    row_tok = jnp.zeros((n_rows,), jnp.int32).at[dest].set(flat_tok[order])
    row_w = jnp.zeros((n_rows,), F32).at[dest].set(flat_w[order])
    block_start = jnp.arange(n_blocks, dtype=jnp.int32) * MOE_BLOCK
    block_e = jnp.minimum(jnp.searchsorted(pends, block_start, side='right'), N_EXPERTS - 1)
    xs = h[row_tok].reshape(n_blocks, MOE_BLOCK, d)

    def expert_block(args):
        xb, e = args
        gu = xb @ w_gu[e] + b_gu[e]
        g, u = jnp.split(gu, 2, axis=-1)
        g = jnp.minimum(g, SWIGLU_LIMIT)
        u = jnp.clip(u, -SWIGLU_LIMIT, SWIGLU_LIMIT)
        act = (u + 1.0) * g * jax.nn.sigmoid(SWIGLU_ALPHA * g)
        return act @ w_down[e] + b_down[e]

    ys = lax.map(expert_block, (xs, block_e)).reshape(n_rows, d)
    out = jax.ops.segment_sum(ys.astype(F32) * row_w[:, None], row_tok, num_segments=t)
    return out.astype(h.dtype)


def setup_inputs(seed: int = 0) -> dict:
    key = jax.random.key(seed)
    ks = jax.random.split(key, 40)
    d = D_MODEL
    gain = lambda k, shape: 1.0 + _normal(k, shape, 0.02)
    a0 = jax.random.uniform(ks[16], (N_EVEN, 2, LRU_WIDTH), F32, LRU_A_MIN, LRU_A_MAX)
    s = a0 ** (1.0 / LRU_C)
    lru_lambda = jnp.log(s) - jnp.log1p(-s)
    return {
        "x": _normal(ks[0], (BATCH, SEQ, d), 1.0),
        "c": _normal(ks[1], (BATCH, d), 1.0),
        "ctx": _normal(ks[2], (BATCH, CTX_LEN, d), 1.0),
        "c_ctx": _normal(ks[3], (d,), 1.0),
        "mod_w": _normal(ks[4], (DEPTH, d, 6 * d), 0.5 * d ** -0.5),
        "mod_b": _normal(ks[5], (DEPTH, 6 * d), 0.02),
        "norm1_g": gain(ks[6], (DEPTH, d)),
        "norm2_g": gain(ks[7], (DEPTH, d)),
        "ev_w_in": _normal(ks[8], (N_EVEN, d, EVEN_IN), d ** -0.5),
        "ev_w_out": _normal(ks[9], (N_EVEN, EVEN_MIX, d), EVEN_MIX ** -0.5),
        "lru_conv_w": _normal(ks[10], (N_EVEN, CONV_W, LRU_WIDTH), CONV_W ** -0.5),
        "lru_conv_b": _normal(ks[11], (N_EVEN, LRU_WIDTH), 0.01),
        "lru_wa": _normal(ks[12], (N_EVEN, 2, LRU_BLOCKS, LRU_BLOCK_W, LRU_BLOCK_W), LRU_BLOCK_W ** -0.5),
        "lru_ba": _normal(ks[13], (N_EVEN, 2, LRU_WIDTH), 0.01),
        "lru_wx": _normal(ks[14], (N_EVEN, 2, LRU_BLOCKS, LRU_BLOCK_W, LRU_BLOCK_W), LRU_BLOCK_W ** -0.5),
        "lru_bx": _normal(ks[15], (N_EVEN, 2, LRU_WIDTH), 0.01),
        "lru_lambda": lru_lambda,
        "mla_q_norm_g": gain(ks[17], (N_EVEN, MLA_Q_RANK)),
        "mla_w_uq": _normal(ks[18], (N_EVEN, MLA_Q_RANK, MLA_HEADS * MLA_QK), MLA_Q_RANK ** -0.5),
        "mla_kv_norm_g": gain(ks[19], (N_EVEN, MLA_KV_RANK)),
        "mla_w_ukv": _normal(ks[20], (N_EVEN, MLA_KV_RANK, MLA_HEADS * (MLA_NOPE + MLA_V)), MLA_KV_RANK ** -0.5),
        "mla_qn_g": gain(ks[21], (N_EVEN, MLA_QK)),
        "mla_kn_g": gain(ks[22], (N_EVEN, MLA_QK)),
        "od_w_in": _normal(ks[23], (N_ODD, d, ODD_IN), d ** -0.5),
        "od_w_out": _normal(ks[24], (N_ODD, ODD_MIX, d), ODD_MIX ** -0.5),
        "diff_qn_g": gain(ks[25], (N_ODD, DIFF_HEAD_DIM)),
        "diff_kn_g": gain(ks[26], (N_ODD, DIFF_HEAD_DIM)),
        "diff_lq1": _normal(ks[27], (N_ODD, DIFF_HEAD_DIM), 0.1),
        "diff_lk1": _normal(ks[28], (N_ODD, DIFF_HEAD_DIM), 0.1),
        "diff_lq2": _normal(ks[29], (N_ODD, DIFF_HEAD_DIM), 0.1),
        "diff_lk2": _normal(ks[30], (N_ODD, DIFF_HEAD_DIM), 0.1),
        "diff_subln_g": gain(ks[31], (N_ODD, DIFF_V)),
        "router_w": _normal(ks[32], (DEPTH, d, N_EXPERTS), d ** -0.5),
        "router_b": _normal(ks[33], (DEPTH, N_EXPERTS), 0.01),
        "moe_w_gu": _normal(ks[34], (DEPTH, N_EXPERTS, d, 2 * D_FF), d ** -0.5),
        "moe_b_gu": _normal(ks[35], (DEPTH, N_EXPERTS, 2 * D_FF), 0.01),
        "moe_w_down": _normal(ks[36], (DEPTH, N_EXPERTS, D_FF, d), D_FF ** -0.5),
        "moe_b_down": _normal(ks[37], (DEPTH, N_EXPERTS, d), 0.01),
    }


def reference(x, c, ctx, c_ctx, mod_w, mod_b, norm1_g, norm2_g,
              ev_w_in, ev_w_out, lru_conv_w, lru_conv_b, lru_wa, lru_ba, lru_wx, lru_bx, lru_lambda,
              mla_q_norm_g, mla_w_uq, mla_kv_norm_g, mla_w_ukv, mla_qn_g, mla_kn_g,
              od_w_in, od_w_out, diff_qn_g, diff_kn_g, diff_lq1, diff_lk1, diff_lq2, diff_lk2, diff_subln_g,
              router_w, router_b, moe_w_gu, moe_b_gu, moe_w_down, moe_b_down):
    b, s, d = x.shape
    xl, xc = x, ctx
    for layer in range(DEPTH):
        last = layer == DEPTH - 1
        mod_l = jax.nn.silu(c) @ mod_w[layer] + mod_b[layer]
        mod_c = jax.nn.silu(c_ctx) @ mod_w[layer] + mod_b[layer]
        sh1, sc1, g1, sh2, sc2, g2 = jnp.split(mod_l[:, None, :], 6, axis=-1)
        csh1, csc1, cg1, csh2, csc2, cg2 = jnp.split(mod_c, 6, axis=-1)

        hl = rms_norm(xl, norm1_g[layer]) * (1.0 + sc1) + sh1
        hc = rms_norm(xc, norm1_g[layer]) * (1.0 + csc1) + csh1
        i = layer // 2
        if layer % 2 == 0:
            ml, mc = lru_mla_mixer(hl, hc, ev_w_in[i], ev_w_out[i], lru_conv_w[i], lru_conv_b[i],
                                   lru_wa[i], lru_ba[i], lru_wx[i], lru_bx[i], lru_lambda[i],
                                   mla_q_norm_g[i], mla_w_uq[i], mla_kv_norm_g[i], mla_w_ukv[i],
                                   mla_qn_g[i], mla_kn_g[i], not last)
        else:
            lam_init = 0.8 - 0.6 * math.exp(-0.3 * layer)
            ml, mc = diff_mixer(hl, hc, od_w_in[i], od_w_out[i], diff_qn_g[i], diff_kn_g[i],
                                diff_lq1[i], diff_lk1[i], diff_lq2[i], diff_lk2[i], diff_subln_g[i],
                                lam_init, not last)
        xl = xl + g1 * ml
        hl2 = rms_norm(xl, norm2_g[layer]) * (1.0 + sc2) + sh2
        moe_args = (router_w[layer], router_b[layer], moe_w_gu[layer], moe_b_gu[layer],
                    moe_w_down[layer], moe_b_down[layer])
        if last:
            fl = moe_ffn(hl2.reshape(b * s, d), *moe_args)
            xl = xl + g2 * fl.reshape(b, s, d)
        else:
            xc = xc + cg1 * mc
            hc2 = rms_norm(xc, norm2_g[layer]) * (1.0 + csc2) + csh2
            f = moe_ffn(jnp.concatenate([hl2.reshape(b * s, d), hc2.reshape(-1, d)], axis=0), *moe_args)
            xl = xl + g2 * f[:b * s].reshape(b, s, d)
            xc = xc + cg2 * f[b * s:].reshape(xc.shape)
    return xl
```

```python
import functools
import math

import jax
import jax.numpy as jnp
from jax import lax
from jax.experimental import pallas as pl
from jax.experimental.pallas import tpu as pltpu

F32 = jnp.float32
BF16 = jnp.bfloat16
HIGHEST = lax.Precision.HIGHEST

GRID_W = 64
NORM_EPS = 1e-6
ROPE_BASE = 10000.0
LRU_WIDTH = 512
LRU_BLOCKS = 8
LRU_BLOCK_W = LRU_WIDTH // LRU_BLOCKS
LRU_C = 8.0
CONV_W = 4
MLA_HEADS = 4
MLA_Q_RANK = 384
MLA_KV_RANK = 256
MLA_NOPE = 128
MLA_ROPE = 64
MLA_V = 128
MLA_QK = MLA_NOPE + MLA_ROPE
DIFF_HEADS = 8
DIFF_HEAD_DIM = 64
DIFF_V = 2 * DIFF_HEAD_DIM
N_EXPERTS = 32
TOP_K = 4
SWIGLU_LIMIT = 7.0
SWIGLU_ALPHA = 1.702

LANES = 128
SUBLANES = 8
VMEM_LIMIT = 52 * 1024 * 1024

TM = 256
TQ = 256
TK = 512
LRU_CT = 128
LRU_CHUNK = 128
LRU_SEG = LRU_CHUNK // SUBLANES
LRU_PAD = 8
MOE_BM = 256
NEG_BIG = -1e30


def _cparams(sem, vmem=VMEM_LIMIT):
    return pltpu.CompilerParams(dimension_semantics=sem, vmem_limit_bytes=vmem)


def _adaln_kernel(c_ref, w_ref, b_ref, o_ref):
    c = c_ref[...]
    s = c * jax.nn.sigmoid(c)
    o_ref[...] = jnp.dot(s, w_ref[...], preferred_element_type=F32, precision=HIGHEST) + b_ref[...]


def _adaln(cvec, w, b):
    rows, d = cvec.shape
    n = w.shape[1]
    tn = 1536
    return pl.pallas_call(
        _adaln_kernel,
        grid=(n // tn,),
        in_specs=[pl.BlockSpec((rows, d), lambda j: (0, 0)),
                  pl.BlockSpec((d, tn), lambda j: (0, j)),
                  pl.BlockSpec((1, tn), lambda j: (0, j))],
        out_specs=pl.BlockSpec((rows, tn), lambda j: (0, j)),
        out_shape=jax.ShapeDtypeStruct((rows, n), F32),
        compiler_params=_cparams(("arbitrary",)),
        name="adaln_mod",
    )(cvec, w, b.reshape(1, n))


def _norm_mod(x, g, sc, sh):
    var = jnp.mean(x * x, axis=-1, keepdims=True)
    y = x * lax.rsqrt(var + NORM_EPS) * g
    return y * (1.0 + sc) + sh


def _rope_tile(x, cos, sin_a, sin_b):
    up = pltpu.roll(x, LANES - 16, axis=1)
    dn = pltpu.roll(x, 16, axis=1)
    return x * cos + up * sin_a + dn * sin_b


def _rope_tables(seq, rot_dim):
    n_rows = seq // GRID_W
    rows = jnp.repeat(jnp.arange(n_rows, dtype=F32), GRID_W)
    cols = jnp.tile(jnp.arange(GRID_W, dtype=F32), n_rows)
    axis_dim = rot_dim // 2
    inv_freq = ROPE_BASE ** (-jnp.arange(0, axis_dim, 2, dtype=F32) / axis_dim)
    ang_r = rows[:, None] * inv_freq
    ang_c = cols[:, None] * inv_freq
    ang = jnp.concatenate([ang_r, ang_r, ang_c, ang_c], axis=-1)
    cos, sin = jnp.cos(ang), jnp.sin(ang)
    quarter = rot_dim // 4
    first = (jnp.arange(rot_dim) % (2 * quarter)) < quarter
    sin_a = jnp.where(first, -sin, 0.0)
    sin_b = jnp.where(first, 0.0, sin)
    return cos, sin_a, sin_b


def _rope_tables_128(seq, ctx, rot_dim, tile_groups):
    cos, sin_a, sin_b = _rope_tables(seq, rot_dim)
    if tile_groups:
        reps = LANES // rot_dim
        cos, sin_a, sin_b = (jnp.tile(t, (1, reps)) for t in (cos, sin_a, sin_b))
    else:
        pad = LANES - rot_dim
        cos = jnp.concatenate([cos, jnp.ones((seq, pad), F32)], axis=-1)
        sin_a = jnp.concatenate([sin_a, jnp.zeros((seq, pad), F32)], axis=-1)
        sin_b = jnp.concatenate([sin_b, jnp.zeros((seq, pad), F32)], axis=-1)
    cos = jnp.concatenate([jnp.ones((ctx, LANES), F32), cos], axis=0)
    sin_a = jnp.concatenate([jnp.zeros((ctx, LANES), F32), sin_a], axis=0)
    sin_b = jnp.concatenate([jnp.zeros((ctx, LANES), F32), sin_b], axis=0)
    return cos, sin_a, sin_b


def _mod_index(tpb, nb):
    def idx(i):
        return jnp.where(i % tpb == 0, nb, i // tpb)
    return idx


def _even_proj_kernel(x_ref, g_ref, sc_ref, sh_ref, wg_ref, wr_ref, wq_ref, wkv_ref, wkr_ref,
                      og_ref, or_ref, oq_ref, okv_ref, okr_ref):
    h = _norm_mod(x_ref[...], g_ref[...], sc_ref[...], sh_ref[...]).astype(BF16)
    for w_ref, o_ref in ((wg_ref, og_ref), (wr_ref, or_ref), (wq_ref, oq_ref),
                         (wkv_ref, okv_ref), (wkr_ref, okr_ref)):
        o_ref[...] = jnp.dot(h, w_ref[...], preferred_element_type=F32)


def _even_proj(x, gain, sc, sh, weights, tpb, nb):
    t, d = x.shape
    midx = _mod_index(tpb, nb)
    w_specs = [pl.BlockSpec(w.shape, lambda i: (0, 0)) for w in weights]
    return pl.pallas_call(
        _even_proj_kernel,
        grid=(t // TM,),
        in_specs=[pl.BlockSpec((TM, d), lambda i: (i, 0)),
                  pl.BlockSpec((1, d), lambda i: (0, 0)),
                  pl.BlockSpec((None, 1, d), lambda i: (midx(i), 0, 0)),
                  pl.BlockSpec((None, 1, d), lambda i: (midx(i), 0, 0))] + w_specs,
        out_specs=[pl.BlockSpec((TM, w.shape[1]), lambda i: (i, 0)) for w in weights],
        out_shape=[jax.ShapeDtypeStruct((t, w.shape[1]), F32) for w in weights],
        compiler_params=_cparams(("parallel",)),
        name="even_in_proj",
    )(x, gain, sc, sh, *weights)


def _mla_prep_kernel(qc_ref, kvc_ref, kr_ref, qng_ref, kvng_ref, wqn_ref, wqr_ref, wkv_ref,
                     qgn_ref, qgr_ref, kgn_ref, kgr_ref, cos_ref, sa_ref, sb_ref,
                     q_ref, k_ref, v_ref):
    cos, sa, sb = cos_ref[...], sa_ref[...], sb_ref[...]
    inv_qk = 1.0 / MLA_QK

    qc = qc_ref[...]
    hq = (qc * lax.rsqrt(jnp.mean(qc * qc, axis=-1, keepdims=True) + NORM_EPS) * qng_ref[...]).astype(BF16)
    q_nope = jnp.dot(hq, wqn_ref[...], preferred_element_type=F32)
    q_rope = jnp.dot(hq, wqr_ref[...], preferred_element_type=F32)
    q_scale = MLA_QK ** -0.5
    for h in range(MLA_HEADS):
        qn = q_nope[:, h * LANES:(h + 1) * LANES]
        qr = q_rope[:, h * LANES:(h + 1) * LANES]
        ms = (jnp.sum(qn * qn, axis=-1, keepdims=True) + jnp.sum(qr * qr, axis=-1, keepdims=True)) * inv_qk
        rs = lax.rsqrt(ms + NORM_EPS) * q_scale
        q_ref[:, (2 * h) * LANES:(2 * h + 1) * LANES] = (qn * rs * qgn_ref[...]).astype(BF16)
        q_ref[:, (2 * h + 1) * LANES:(2 * h + 2) * LANES] = _rope_tile(qr * rs * qgr_ref[...], cos, sa, sb).astype(BF16)

    kvc = kvc_ref[...]
    hkv = (kvc * lax.rsqrt(jnp.mean(kvc * kvc, axis=-1, keepdims=True) + NORM_EPS) * kvng_ref[...]).astype(BF16)
    kv = jnp.dot(hkv, wkv_ref[...], preferred_element_type=F32)
    kr = kr_ref[...]
    kr_ss = jnp.sum(kr * kr, axis=-1, keepdims=True)
    kr_rot = _rope_tile(kr * kgr_ref[...], cos, sa, sb)
    for h in range(MLA_HEADS):
        kn = kv[:, (2 * h) * LANES:(2 * h + 1) * LANES]
        ms = (jnp.sum(kn * kn, axis=-1, keepdims=True) + kr_ss) * inv_qk
        rs = lax.rsqrt(ms + NORM_EPS)
        k_ref[:, (2 * h) * LANES:(2 * h + 1) * LANES] = (kn * rs * kgn_ref[...]).astype(BF16)
        k_ref[:, (2 * h + 1) * LANES:(2 * h + 2) * LANES] = (kr_rot * rs).astype(BF16)
        v_ref[:, h * LANES:(h + 1) * LANES] = kv[:, (2 * h + 1) * LANES:(2 * h + 2) * LANES].astype(BF16)


def _mla_prep(qc, kvc, kr, q_norm_g, kv_norm_g, w_q_nope, w_q_rope, w_ukv, q_gn, q_gr, k_gn, k_gr,
              rope, tpb):
    t = qc.shape[0]
    cos, sa, sb = rope

    def const(a):
        return pl.BlockSpec(a.shape, lambda i: (0, 0))

    def rows(a):
        return pl.BlockSpec((TM, a.shape[1]), lambda i: (i, 0))

    tab = pl.BlockSpec((TM, LANES), lambda i: (i % tpb, 0))
    hq, hv = MLA_HEADS * 2 * LANES, MLA_HEADS * LANES
    return pl.pallas_call(
        _mla_prep_kernel,
        grid=(t // TM,),
        in_specs=[rows(qc), rows(kvc), rows(kr), const(q_norm_g), const(kv_norm_g),
                  const(w_q_nope), const(w_q_rope), const(w_ukv),
                  const(q_gn), const(q_gr), const(k_gn), const(k_gr), tab, tab, tab],
        out_specs=[pl.BlockSpec((TM, hq), lambda i: (i, 0)),
                   pl.BlockSpec((TM, hq), lambda i: (i, 0)),
                   pl.BlockSpec((TM, hv), lambda i: (i, 0))],
        out_shape=[jax.ShapeDtypeStruct((t, hq), BF16),
                   jax.ShapeDtypeStruct((t, hq), BF16),
                   jax.ShapeDtypeStruct((t, hv), BF16)],
        compiler_params=_cparams(("parallel",)),
        name="mla_prep",
    )(qc, kvc, kr, q_norm_g, kv_norm_g, w_q_nope, w_q_rope, w_ukv, q_gn, q_gr, k_gn, k_gr, cos, sa, sb)


def _sublane_iota():
    return lax.broadcasted_iota(jnp.int32, (SUBLANES, LANES), 0)


def _scan_chunk(a_chunk, u_chunk, h_chunk, carry, reverse):
    steps = range(LRU_SEG - 1, -1, -1) if reverse else range(LRU_SEG)
    h_loc, p_loc = [None] * LRU_SEG, [None] * LRU_SEG
    h = p = None
    for j in steps:
        a = a_chunk[pl.ds(j, SUBLANES, stride=LRU_SEG), :]
        u = u_chunk[pl.ds(j, SUBLANES, stride=LRU_SEG), :]
        if h is None:
            h, p = u, a
        else:
            h, p = a * h + u, a * p
        h_loc[j], p_loc[j] = h, p
    sub = _sublane_iota()
    seg_p, seg_h = p, h
    for d in (1, 2, 4):
        shift = SUBLANES - d if reverse else d
        prev_p = pltpu.roll(seg_p, shift, axis=0)
        prev_h = pltpu.roll(seg_h, shift, axis=0)
        valid = (sub < SUBLANES - d) if reverse else (sub >= d)
        seg_h = jnp.where(valid, seg_p * prev_h + seg_h, seg_h)
        seg_p = jnp.where(valid, seg_p * prev_p, seg_p)
    h_end = seg_h + seg_p * carry
    if reverse:
        h_in = jnp.where(sub == SUBLANES - 1, carry, pltpu.roll(h_end, SUBLANES - 1, axis=0))
        new_carry = h_end[0:1, :]
    else:
        h_in = jnp.where(sub == 0, carry, pltpu.roll(h_end, 1, axis=0))
        new_carry = h_end[SUBLANES - 1:SUBLANES, :]
    for j in range(LRU_SEG):
        h_chunk[pl.ds(j, SUBLANES, stride=LRU_SEG), :] = h_loc[j] + p_loc[j] * h_in
    return jnp.broadcast_to(new_carry, (SUBLANES, LANES))


def _lru_kernel(g_ref, r_ref, cw_ref, cb_ref, wg_ref, bg_ref, lam_ref, o_ref,
                rp_ref, af_ref, uf_ref, ab_ref, ub_ref, hf_ref, hb_ref, *, ctx, seq):
    n_ctx, n_lat = ctx // LRU_CHUNK, seq // LRU_CHUNK
    zeros_pad = jnp.zeros((LRU_PAD, LRU_CT), F32)
    lat0 = ctx + 2 * LRU_PAD
    rp_ref[0:LRU_PAD, :] = zeros_pad
    rp_ref[LRU_PAD:LRU_PAD + ctx, :] = r_ref[0:ctx, :]
    rp_ref[LRU_PAD + ctx:lat0, :] = zeros_pad
    rp_ref[lat0:lat0 + seq, :] = r_ref[ctx:ctx + seq, :]
    rp_ref[lat0 + seq:lat0 + seq + LRU_PAD, :] = zeros_pad

    cw = cw_ref[...]
    cb = cb_ref[...]
    wg = wg_ref[...]
    bg = bg_ref[...]
    lam = lam_ref[...]
    sp = jnp.maximum(-lam, 0.0) + jnp.log1p(jnp.exp(-jnp.abs(lam)))

    def coeff_chunk(c, pad_off, row_off):
        start = pl.multiple_of(pad_off + c * LRU_CHUNK, SUBLANES)
        ext = rp_ref[pl.ds(start, LRU_CHUNK + 2 * LRU_PAD), :]
        x = cb
        for tap in range(CONV_W):
            lo = LRU_PAD - 2 + tap
            x = x + ext[lo:lo + LRU_CHUNK, :] * cw[tap:tap + 1, :]
        gates = jnp.dot(x.astype(BF16), wg, preferred_element_type=F32) + bg
        out_row = pl.multiple_of(row_off + c * LRU_CHUNK, SUBLANES)
        for d, (a_ref, u_ref) in enumerate(((af_ref, uf_ref), (ab_ref, ub_ref))):
            r = jax.nn.sigmoid(gates[:, (2 * d) * LRU_CT:(2 * d + 1) * LRU_CT])
            i = jax.nn.sigmoid(gates[:, (2 * d + 1) * LRU_CT:(2 * d + 2) * LRU_CT])
            log_a = -LRU_C * r * sp[d:d + 1, :]
            a_ref[pl.ds(out_row, LRU_CHUNK), :] = jnp.exp(log_a)
            th = jnp.tanh(log_a)
            u_ref[pl.ds(out_row, LRU_CHUNK), :] = jnp.sqrt(-2.0 * th / (1.0 - th)) * i * x
        return None

    def coeff_ctx(c, _):
        coeff_chunk(c, 0, 0)
        return 0

    def coeff_lat(c, _):
        coeff_chunk(c, ctx + LRU_PAD, ctx)
        return 0

    lax.fori_loop(0, n_ctx, coeff_ctx, 0)
    lax.fori_loop(0, n_lat, coeff_lat, 0)

    def scan_pair(n, row_off):
        def body(c, carry):
            cf, cb_ = carry
            f_row = pl.multiple_of(row_off + c * LRU_CHUNK, SUBLANES)
            b_row = pl.multiple_of(row_off + (n - 1 - c) * LRU_CHUNK, SUBLANES)
            cf = _scan_chunk(af_ref.at[pl.ds(f_row, LRU_CHUNK), :], uf_ref.at[pl.ds(f_row, LRU_CHUNK), :],
                             hf_ref.at[pl.ds(f_row, LRU_CHUNK), :], cf, False)
            cb_ = _scan_chunk(ab_ref.at[pl.ds(b_row, LRU_CHUNK), :], ub_ref.at[pl.ds(b_row, LRU_CHUNK), :],
                              hb_ref.at[pl.ds(b_row, LRU_CHUNK), :], cb_, True)
            return cf, cb_
        return body

    zero = jnp.zeros((SUBLANES, LANES), F32)
    carry = lax.fori_loop(0, n_ctx, scan_pair(n_ctx, 0), (zero, zero))
    lax.fori_loop(0, n_lat, scan_pair(n_lat, ctx), carry)

    g = g_ref[...]
    gelu = 0.5 * g * (1.0 + jnp.tanh(math.sqrt(2.0 / math.pi) * (g + 0.044715 * (g * g * g))))
    o_ref[...] = (gelu * (hf_ref[...] + hb_ref[...])).astype(BF16)


def _lru(g, r, conv_w, conv_b, w_gates, b_gates, lam, nb, ctx, seq):
    l = ctx + seq
    width = g.shape[1]
    n_ct = width // LRU_CT
    g3 = g.reshape(nb, l, width)
    r3 = r.reshape(nb, l, width)
    seq_spec = pl.BlockSpec((None, l, LRU_CT), lambda b, c: (b, 0, c))
    scratch = [pltpu.VMEM((l + 3 * LRU_PAD, LRU_CT), F32)] + [pltpu.VMEM((l, LRU_CT), F32)] * 6
    out = pl.pallas_call(
        functools.partial(_lru_kernel, ctx=ctx, seq=seq),
        grid=(nb, n_ct),
        in_specs=[seq_spec, seq_spec,
                  pl.BlockSpec((CONV_W, LRU_CT), lambda b, c: (0, c)),
                  pl.BlockSpec((1, LRU_CT), lambda b, c: (0, c)),
                  pl.BlockSpec((None, LRU_CT, 4 * LRU_CT), lambda b, c: (c, 0, 0)),
                  pl.BlockSpec((None, 1, 4 * LRU_CT), lambda b, c: (c, 0, 0)),
                  pl.BlockSpec((2, LRU_CT), lambda b, c: (0, c))],
        out_specs=seq_spec,
        out_shape=jax.ShapeDtypeStruct((nb, l, width), BF16),
        scratch_shapes=scratch,
        compiler_params=_cparams(("parallel", "parallel")),
        name="rglru",
    )(g3, r3, conv_w, conv_b, w_gates, b_gates, lam)
    return out.reshape(nb * l, width)


def _lru_gate_weights(wa, ba, wx, bx):
    per = LRU_CT // LRU_BLOCK_W
    n_ct = LRU_BLOCKS // per
    eye = jnp.eye(per, dtype=F32)

    def dense(w):
        w4 = w.reshape(n_ct, per, LRU_BLOCK_W, LRU_BLOCK_W)
        return jnp.einsum('cide,ij->cidje', w4, eye).reshape(n_ct, LRU_CT, LRU_CT)

    w = jnp.concatenate([dense(wa[0]), dense(wx[0]), dense(wa[1]), dense(wx[1])], axis=-1)
    b = jnp.concatenate([v.reshape(n_ct, 1, LRU_CT) for v in (ba[0], bx[0], ba[1], bx[1])], axis=-1)
    return w.astype(BF16), b


def _online_step(s, m, l, acc, v):
    m_new = jnp.maximum(m, jnp.max(s, axis=-1, keepdims=True))
    alpha = jnp.exp(m - m_new)
    p = jnp.exp(s - m_new)
    l = alpha * l + jnp.sum(p, axis=-1, keepdims=True)
    acc = alpha * acc + jnp.dot(p.astype(BF16), v, preferred_element_type=F32)
    return m_new, l, acc


def _n_latent_steps(n_lat, has_ctx_tile):
    if not has_ctx_tile:
        return n_lat
    return jnp.where(pl.program_id(2) == 0, 0, n_lat)


def _mla_attn_kernel(q_ref, kc_ref, kl_ref, v_ref, o_ref, *, ctx, n_lat, has_ctx_tile):
    q = q_ref[...]
    dv = v_ref.shape[-1]
    init = (jnp.full((TQ, 1), NEG_BIG, F32), jnp.zeros((TQ, 1), F32), jnp.zeros((TQ, dv), F32))
    s = jnp.dot(q, kc_ref[...], preferred_element_type=F32)
    carry = _online_step(s, *init, v_ref[0:ctx, :])

    def body(j, carry):
        s = jnp.dot(q, kl_ref[j], preferred_element_type=F32)
        row = pl.multiple_of(ctx + j * TK, TK)
        return _online_step(s, *carry, v_ref[pl.ds(row, TK), :])

    m, l, acc = lax.fori_loop(0, _n_latent_steps(n_lat, has_ctx_tile), body, carry)
    o_ref[...] = (acc / l).astype(o_ref.dtype)


def _diff_attn_kernel(q_ref, kc_ref, kl_ref, v_ref, g_ref, lam_ref, o_ref, *, ctx, n_lat, has_ctx_tile,
                      out_scale):
    q = q_ref[...]
    lane = lax.broadcasted_iota(jnp.int32, q.shape, 1)
    zero = jnp.zeros_like(q)
    q1 = jnp.where(lane < DIFF_HEAD_DIM, q, zero)
    q2 = jnp.where(lane < DIFF_HEAD_DIM, zero, q)
    dv = v_ref.shape[-1]
    init = (jnp.full((TQ, 1), NEG_BIG, F32), jnp.zeros((TQ, 1), F32), jnp.zeros((TQ, dv), F32))

    def step(kt, v, c1, c2):
        c1 = _online_step(jnp.dot(q1, kt, preferred_element_type=F32), *c1, v)
        c2 = _online_step(jnp.dot(q2, kt, preferred_element_type=F32), *c2, v)
        return c1, c2

    carry = step(kc_ref[...], v_ref[0:ctx, :], init, init)

    def body(j, carry):
        row = pl.multiple_of(ctx + j * TK, TK)
        return step(kl_ref[j], v_ref[pl.ds(row, TK), :], *carry)

    (m1, l1, acc1), (m2, l2, acc2) = lax.fori_loop(0, _n_latent_steps(n_lat, has_ctx_tile), body, carry)
    o = acc1 / l1 - lam_ref[0] * (acc2 / l2)
    y = o * lax.rsqrt(jnp.mean(o * o, axis=-1, keepdims=True) + NORM_EPS) * g_ref[...]
    o_ref[...] = (y * out_scale).astype(o_ref.dtype)


def _split_keys(k, nb, ctx, seq, heads, dk):
    l = ctx + seq
    k4 = k.reshape(nb, l, heads, dk)
    kc = jnp.transpose(k4[:, :ctx], (0, 2, 3, 1))
    kl = jnp.transpose(k4[:, ctx:].reshape(nb, seq // TK, TK, heads, dk), (0, 3, 1, 4, 2))
    return kc, kl


def _attention(kernel_fn, name, q, k, v, nb, ctx, seq, heads, dk, dv, with_ctx_queries, extra_in=(),
               extra_specs=()):
    l = ctx + seq
    n_lat = seq // TK
    kc, kl = _split_keys(k, nb, ctx, seq, heads, dk)
    q3 = q.reshape(nb, l, heads * dk)
    v3 = v.reshape(nb, l, heads * dv)
    n_q = l // TQ if with_ctx_queries else seq // TQ
    q_off = 0 if with_ctx_queries else ctx // TQ
    out_rows = l if with_ctx_queries else seq
    in_specs = [pl.BlockSpec((None, TQ, dk), lambda b, h, i: (b, i + q_off, h)),
                pl.BlockSpec((None, None, dk, ctx), lambda b, h, i: (b, h, 0, 0)),
                pl.BlockSpec((None, None, n_lat, dk, TK), lambda b, h, i: (b, h, 0, 0, 0)),
                pl.BlockSpec((None, l, dv), lambda b, h, i: (b, 0, h))]
    out = pl.pallas_call(
        functools.partial(kernel_fn, ctx=ctx, n_lat=n_lat, has_ctx_tile=with_ctx_queries),
        grid=(nb, heads, n_q),
        in_specs=in_specs + list(extra_specs),
        out_specs=pl.BlockSpec((None, TQ, dv), lambda b, h, i: (b, i, h)),
        out_shape=jax.ShapeDtypeStruct((nb, out_rows, heads * dv), BF16),
        compiler_params=_cparams(("parallel", "parallel", "arbitrary")),
        name=name,
    )(q3, kc, kl, v3, *extra_in)
    return out.reshape(nb * out_rows, heads * dv)


def _group_ms(x, gmat):
    return jnp.dot((x * x).astype(BF16), gmat, preferred_element_type=F32)


def _odd_proj_kernel(x_ref, g_ref, sc_ref, sh_ref, wq_ref, wk_ref, wv_ref, qg_ref, kg_ref, gm_ref,
                     cos_ref, sa_ref, sb_ref, q_ref, k_ref, v_ref):
    h = _norm_mod(x_ref[...], g_ref[...], sc_ref[...], sh_ref[...]).astype(BF16)
    cos, sa, sb = cos_ref[...], sa_ref[...], sb_ref[...]
    gmat = gm_ref[...]
    q_scale = DIFF_HEAD_DIM ** -0.5
    for w_ref, gain_ref, o_ref, scale in ((wq_ref, qg_ref, q_ref, q_scale), (wk_ref, kg_ref, k_ref, 1.0)):
        raw = jnp.dot(h, w_ref[...], preferred_element_type=F32)
        gain = gain_ref[...] * scale
        for c in range(raw.shape[1] // LANES):
            t = raw[:, c * LANES:(c + 1) * LANES]
            t = t * lax.rsqrt(_group_ms(t, gmat) + NORM_EPS) * gain
            o_ref[:, c * LANES:(c + 1) * LANES] = _rope_tile(t, cos, sa, sb).astype(BF16)
    v_ref[...] = jnp.dot(h, wv_ref[...], preferred_element_type=F32).astype(BF16)


def _odd_proj(x, gain, sc, sh, wq, wk, wv, q_gain, k_gain, gmat, rope, tpb, nb):
    t, d = x.shape
    midx = _mod_index(tpb, nb)
    cos, sa, sb = rope

    def const(a):
        return pl.BlockSpec(a.shape, lambda i: (0, 0))

    tab = pl.BlockSpec((TM, LANES), lambda i: (i % tpb, 0))
    n = wq.shape[1]
    return pl.pallas_call(
        _odd_proj_kernel,
        grid=(t // TM,),
        in_specs=[pl.BlockSpec((TM, d), lambda i: (i, 0)), const(gain),
                  pl.BlockSpec((None, 1, d), lambda i: (midx(i), 0, 0)),
                  pl.BlockSpec((None, 1, d), lambda i: (midx(i), 0, 0)),
                  const(wq), const(wk), const(wv), const(q_gain), const(k_gain), const(gmat),
                  tab, tab, tab],
        out_specs=[pl.BlockSpec((TM, n), lambda i: (i, 0))] * 3,
        out_shape=[jax.ShapeDtypeStruct((t, n), BF16)] * 3,
        compiler_params=_cparams(("parallel",)),
        name="odd_in_proj",
    )(x, gain, sc, sh, wq, wk, wv, q_gain, k_gain, gmat, cos, sa, sb)


def _post_mix_kernel(*refs, n_mix):
    mix_refs = refs[:n_mix]
    w_refs = refs[n_mix:2 * n_mix]
    (x_ref, g1_ref, n2_ref, sc_ref, sh_ref, rw_ref, rb_ref,
     xo_ref, h2_ref, te_ref, tw_ref, tr_ref, cnt_ref, base_ref) = refs[2 * n_mix:]
    i = pl.program_id(0)

    @pl.when(i == 0)
    def _():
        base_ref[...] = jnp.zeros_like(base_ref)

    m = jnp.dot(mix_refs[0][...], w_refs[0][...], preferred_element_type=F32)
    for a_ref, w_ref in zip(mix_refs[1:], w_refs[1:]):
        m = m + jnp.dot(a_ref[...], w_ref[...], preferred_element_type=F32)
    x = x_ref[...] + g1_ref[...] * m
    xo_ref[...] = x
    h2 = _norm_mod(x, n2_ref[...], sc_ref[...], sh_ref[...])
    h2_ref[...] = h2.astype(BF16)

    logits = jnp.dot(h2, rw_ref[...], preferred_element_type=F32, precision=HIGHEST) + rb_ref[...]
    lane = lax.broadcasted_iota(jnp.int32, logits.shape, 1)
    lane_f = lane.astype(F32)
    vals = jnp.where(lane < N_EXPERTS, logits, NEG_BIG)
    tops, hots = [], []
    for _ in range(TOP_K):
        top = jnp.max(vals, axis=-1, keepdims=True)
        first = jnp.min(jnp.where(vals == top, lane_f, float(LANES)), axis=-1, keepdims=True)
        hot = lane_f == first
        tops.append(top)
        hots.append(hot)
        vals = jnp.where(hot, 2.0 * NEG_BIG, vals)
    exps = [jnp.exp(t - tops[0]) for t in tops]
    denom = exps[0] + exps[1] + exps[2] + exps[3]

    picked = (hots[0] | hots[1]) | (hots[2] | hots[3])
    cnt = jnp.where(picked, 1.0, 0.0)
    r_io = lax.broadcasted_iota(jnp.int32, (TM, TM), 0)
    c_io = lax.broadcasted_iota(jnp.int32, (TM, TM), 1)
    tri = jnp.where(c_io < r_io, 1.0, 0.0).astype(BF16)
    before = jnp.dot(tri, cnt.astype(BF16), preferred_element_type=F32) + base_ref[...]
    te = jnp.zeros(logits.shape, F32)
    tw = jnp.zeros(logits.shape, F32)
    tr = jnp.zeros(logits.shape, F32)
    for k in range(TOP_K):
        e_k = jnp.sum(jnp.where(hots[k], lane_f, 0.0), axis=-1, keepdims=True)
        r_k = jnp.sum(jnp.where(hots[k], before, 0.0), axis=-1, keepdims=True)
        te = jnp.where(lane == k, e_k, te)
        tw = jnp.where(lane == k, exps[k] / denom, tw)
        tr = jnp.where(lane == k, r_k, tr)
    te_ref[...] = te.astype(jnp.int32)
    tw_ref[...] = tw
    tr_ref[...] = tr.astype(jnp.int32)
    base_ref[...] += jnp.sum(cnt, axis=0, keepdims=True)
    cnt_ref[...] = base_ref[...]


def _post_mix(mixes, weights, x, g1, n2g, sc2, sh2, router_w, router_b, tpb, nb, latent_only):
    d = x.shape[1]
    tiles_in = tpb
    if latent_only:
        tiles_out = tpb - 1
        n_tiles = nb * tiles_out

        def in_row(i):
            return (i // tiles_out) * tiles_in + 1 + i % tiles_out

        def midx(i):
            return i // tiles_out
    else:
        n_tiles = nb * tpb

        def in_row(i):
            return i

        midx = _mod_index(tpb, nb)
    t_out = n_tiles * TM

    def const(a):
        return pl.BlockSpec(a.shape, lambda i: (0, 0))

    def mod(a):
        return pl.BlockSpec((None, 1, d), lambda i: (midx(i), 0, 0))

    in_specs = [pl.BlockSpec((TM, a.shape[1]), lambda i: (i, 0)) for a in mixes]
    in_specs += [const(w) for w in weights]
    in_specs += [pl.BlockSpec((TM, d), lambda i: (in_row(i), 0)), mod(g1), const(n2g), mod(sc2), mod(sh2),
                 const(router_w), const(router_b)]
    row_out = lambda width: pl.BlockSpec((TM, width), lambda i: (i, 0))
    return pl.pallas_call(
        functools.partial(_post_mix_kernel, n_mix=len(mixes)),
        grid=(n_tiles,),
        in_specs=in_specs,
        out_specs=[row_out(d), row_out(d), row_out(LANES), row_out(LANES), row_out(LANES),
                   pl.BlockSpec((1, LANES), lambda i: (0, 0))],
        out_shape=[jax.ShapeDtypeStruct((t_out, d), F32), jax.ShapeDtypeStruct((t_out, d), BF16),
                   jax.ShapeDtypeStruct((t_out, LANES), jnp.int32), jax.ShapeDtypeStruct((t_out, LANES), F32),
                   jax.ShapeDtypeStruct((t_out, LANES), jnp.int32), jax.ShapeDtypeStruct((1, LANES), F32)],
        scratch_shapes=[pltpu.VMEM((1, LANES), F32)],
        compiler_params=_cparams(("arbitrary",)),
        name="post_mix_router",
    )(*mixes, *weights, x, g1, n2g, sc2, sh2, router_w, router_b)


def _moe_ffn_kernel(be_ref, nu_ref, x_ref, wgu_ref, bgu_ref, wd_ref, bd_ref, o_ref, wgu_bf, wd_bf):
    i = pl.program_id(0)
    prev = be_ref[jnp.maximum(i - 1, 0)]
    fresh = jnp.logical_or(i == 0, be_ref[i] != prev)

    @pl.when(jnp.logical_and(fresh, i < nu_ref[0]))
    def _():
        wgu_bf[...] = wgu_ref[...].astype(BF16)
        wd_bf[...] = wd_ref[...].astype(BF16)

    @pl.when(i < nu_ref[0])
    def _():
        d_ff = wd_bf.shape[0]
        gu = jnp.dot(x_ref[...], wgu_bf[...], preferred_element_type=F32) + bgu_ref[...]
        g = jnp.minimum(gu[:, :d_ff], SWIGLU_LIMIT)
        u = jnp.clip(gu[:, d_ff:], -SWIGLU_LIMIT, SWIGLU_LIMIT)
        act = (u + 1.0) * g * jax.nn.sigmoid(SWIGLU_ALPHA * g)
        y = jnp.dot(act.astype(BF16), wd_bf[...], preferred_element_type=F32) + bd_ref[...]
        o_ref[...] = y.astype(o_ref.dtype)

    @pl.when(i >= nu_ref[0])
    def _():
        o_ref[...] = jnp.zeros_like(o_ref)


def _moe_ffn(block_e, n_used, xs, w_gu, b_gu, w_down, b_down):
    n_rows, d = xs.shape
    n_e, _, two_ff = w_gu.shape
    d_ff = two_ff // 2
    n_blocks = n_rows // MOE_BM
    grid_spec = pltpu.PrefetchScalarGridSpec(
        num_scalar_prefetch=2,
        grid=(n_blocks,),
        in_specs=[pl.BlockSpec((MOE_BM, d), lambda i, be, nu: (i, 0)),
                  pl.BlockSpec((None, d, two_ff), lambda i, be, nu: (be[i], 0, 0)),
                  pl.BlockSpec((None, 1, two_ff), lambda i, be, nu: (be[i], 0, 0)),
                  pl.BlockSpec((None, d_ff, d), lambda i, be, nu: (be[i], 0, 0)),
                  pl.BlockSpec((None, 1, d), lambda i, be, nu: (be[i], 0, 0))],
        out_specs=pl.BlockSpec((MOE_BM, d), lambda i, be, nu: (i, 0)),
        scratch_shapes=[pltpu.VMEM((d, two_ff), BF16), pltpu.VMEM((d_ff, d), BF16)],
    )
    return pl.pallas_call(
        _moe_ffn_kernel,
        grid_spec=grid_spec,
        out_shape=jax.ShapeDtypeStruct((n_rows, d), BF16),
        compiler_params=_cparams(("arbitrary",)),
        name="moe_ffn",
    )(block_e, n_used, xs, w_gu, b_gu.reshape(n_e, 1, two_ff), w_down, b_down.reshape(n_e, 1, d))


def _moe(h2, top_e, top_w, top_r, counts, w_gu, b_gu, w_down, b_down):
    t, d = h2.shape
    n_assign = t * TOP_K
    n_blocks = -(-n_assign // MOE_BM) + N_EXPERTS
    n_rows = n_blocks * MOE_BM
    cnt = counts[0, :N_EXPERTS].astype(jnp.int32)
    padded = (cnt + MOE_BM - 1) // MOE_BM * MOE_BM
    pends = jnp.cumsum(padded)
    pstarts = pends - padded
    e = top_e[:, :TOP_K]
    dest = pstarts[e] + top_r[:, :TOP_K]
    block_start = jnp.arange(n_blocks, dtype=jnp.int32) * MOE_BM
    block_e = jnp.minimum(jnp.searchsorted(pends, block_start, side='right'), N_EXPERTS - 1).astype(jnp.int32)
    n_used = (pends[-1] // MOE_BM).astype(jnp.int32).reshape(1)

    tok = jnp.broadcast_to(jnp.arange(t, dtype=jnp.int32)[:, None], (t, TOP_K))
    row_tok = jnp.zeros((n_rows,), jnp.int32).at[dest.reshape(-1)].set(tok.reshape(-1))
    xs = h2[row_tok]
    ys = _moe_ffn(block_e, n_used, xs, w_gu, b_gu, w_down, b_down)
    picked = ys[dest.reshape(-1)].reshape(t, TOP_K, d).astype(F32)
    return jnp.sum(picked * top_w[:, :TOP_K, None], axis=1)


def _pad_cols(w, width):
    return jnp.concatenate([w, jnp.zeros((w.shape[0], width - w.shape[1]), w.dtype)], axis=1)


def kernel(x, c, ctx, c_ctx, mod_w, mod_b, norm1_g, norm2_g, ev_w_in, ev_w_out, lru_conv_w, lru_conv_b, lru_wa, lru_ba, lru_wx, lru_bx, lru_lambda, mla_q_norm_g, mla_w_uq, mla_kv_norm_g, mla_w_ukv, mla_qn_g, mla_kn_g, od_w_in, od_w_out, diff_qn_g, diff_kn_g, diff_lq1, diff_lk1, diff_lq2, diff_lk2, diff_subln_g, router_w, router_b, moe_w_gu, moe_b_gu, moe_w_down, moe_b_down):
    nb, seq, d = x.shape
    n_ctx = ctx.shape[1]
    depth = mod_w.shape[0]
    l = n_ctx + seq
    tpb = l // TM
    assert n_ctx == TM and seq % TK == 0 and seq % GRID_W == 0

    xa = jnp.concatenate([ctx, x], axis=1).reshape(nb * l, d)
    cvec = jnp.concatenate([c, c_ctx[None, :], jnp.zeros((SUBLANES - nb - 1, d), F32)], axis=0)
    router_w_p = jnp.concatenate([router_w, jnp.zeros((depth, d, LANES - N_EXPERTS), F32)], axis=-1)
    router_b_p = jnp.concatenate([router_b, jnp.zeros((depth, LANES - N_EXPERTS), F32)], axis=-1)

    for layer in range(depth):
        last = layer == depth - 1
        i = layer // 2
        mod = _adaln(cvec, mod_w[layer], mod_b[layer])[:nb + 1]
        sh1, sc1, g1, sh2, sc2, g2 = (mod[:, k * d:(k + 1) * d].reshape(nb + 1, 1, d) for k in range(6))
        n1g = norm1_g[layer].reshape(1, d)
        n2g = norm2_g[layer].reshape(1, d)

        if layer % 2 == 0:
            w_in = ev_w_in[i].astype(BF16)
            o = 2 * LRU_WIDTH
            splits = [w_in[:, :LRU_WIDTH], w_in[:, LRU_WIDTH:o], w_in[:, o:o + MLA_Q_RANK],
                      w_in[:, o + MLA_Q_RANK:o + MLA_Q_RANK + MLA_KV_RANK],
                      _pad_cols(w_in[:, o + MLA_Q_RANK + MLA_KV_RANK:], LANES)]
            gl, rl, qc, kvc, kr = _even_proj(xa, n1g, sc1, sh1, splits, tpb, nb)

            w_gates, b_gates = _lru_gate_weights(lru_wa[i], lru_ba[i], lru_wx[i], lru_bx[i])
            lru = _lru(gl, rl, lru_conv_w[i], lru_conv_b[i].reshape(1, LRU_WIDTH), w_gates, b_gates,
                       lru_lambda[i], nb, n_ctx, seq)

            w_uq = mla_w_uq[i].astype(BF16).reshape(MLA_Q_RANK, MLA_HEADS, MLA_QK)
            w_q_nope = w_uq[:, :, :MLA_NOPE].reshape(MLA_Q_RANK, MLA_HEADS * MLA_NOPE)
            w_q_rope = jnp.concatenate(
                [w_uq[:, :, MLA_NOPE:], jnp.zeros((MLA_Q_RANK, MLA_HEADS, LANES - MLA_ROPE), BF16)],
                axis=-1).reshape(MLA_Q_RANK, MLA_HEADS * LANES)
            pad_g = lambda g: jnp.concatenate([g, jnp.zeros((LANES - MLA_ROPE,), F32)]).reshape(1, LANES)
            rope = _rope_tables_128(seq, n_ctx, MLA_ROPE, tile_groups=False)
            q, k, v = _mla_prep(qc, kvc, kr, mla_q_norm_g[i].reshape(1, -1), mla_kv_norm_g[i].reshape(1, -1),
                                w_q_nope, w_q_rope, mla_w_ukv[i].astype(BF16),
                                mla_qn_g[i][:MLA_NOPE].reshape(1, LANES), pad_g(mla_qn_g[i][MLA_NOPE:]),
                                mla_kn_g[i][:MLA_NOPE].reshape(1, LANES), pad_g(mla_kn_g[i][MLA_NOPE:]),
                                rope, tpb)
            att = _attention(_mla_attn_kernel, "mla_attention", q, k, v, nb, n_ctx, seq, MLA_HEADS,
                             2 * LANES, MLA_V, with_ctx_queries=not last)
            w_out = ev_w_out[i].astype(BF16)
            if last:
                lru = lru.reshape(nb, l, LRU_WIDTH)[:, n_ctx:].reshape(nb * seq, LRU_WIDTH)
            mixes = [lru, att]
            weights = [w_out[:LRU_WIDTH], w_out[LRU_WIDTH:]]
        else:
            lam_init = 0.8 - 0.6 * math.exp(-0.3 * layer)
            w_in = od_w_in[i].astype(BF16)
            n_qk = DIFF_HEADS * 2 * DIFF_HEAD_DIM
            gidx = jnp.arange(LANES) // DIFF_HEAD_DIM
            gmat = jnp.where(gidx[:, None] == gidx[None, :], 1.0 / DIFF_HEAD_DIM, 0.0).astype(BF16)
            rope = _rope_tables_128(seq, n_ctx, DIFF_HEAD_DIM, tile_groups=True)
            tile_g = lambda g: jnp.tile(g, LANES // DIFF_HEAD_DIM).reshape(1, LANES)
            q, k, v = _odd_proj(xa, n1g, sc1, sh1, w_in[:, :n_qk], w_in[:, n_qk:2 * n_qk], w_in[:, 2 * n_qk:],
                                tile_g(diff_qn_g[i]), tile_g(diff_kn_g[i]), gmat, rope, tpb, nb)
            lam = (jnp.exp(jnp.sum(diff_lq1[i] * diff_lk1[i])) - jnp.exp(jnp.sum(diff_lq2[i] * diff_lk2[i]))
                   + lam_init).reshape(1).astype(F32)
            att = _attention(
                functools.partial(_diff_attn_kernel, out_scale=1.0 - lam_init), "diff_attention",
                q, k, v, nb, n_ctx, seq, DIFF_HEADS, 2 * DIFF_HEAD_DIM, DIFF_V, with_ctx_queries=not last,
                extra_in=(diff_subln_g[i].reshape(1, DIFF_V), lam),
                extra_specs=(pl.BlockSpec((1, DIFF_V), lambda b, h, i: (0, 0)),
                             pl.BlockSpec(memory_space=pltpu.SMEM)))
            mixes = [att]
            weights = [od_w_out[i].astype(BF16)]

        xo, h2, te, tw, tr, counts = _post_mix(mixes, weights, xa, g1, n2g, sc2, sh2, router_w_p[layer],
                                               router_b_p[layer].reshape(1, LANES), tpb, nb, last)
        f = _moe(h2, te, tw, tr, counts, moe_w_gu[layer], moe_b_gu[layer], moe_w_down[layer], moe_b_down[layer])
        if last:
            g2_rows = jnp.repeat(g2[:nb, 0, :], seq, axis=0)
            return (xo + g2_rows * f).reshape(nb, seq, d)
        is_ctx = (jnp.arange(nb * l) % l) < n_ctx
        g2_lat = jnp.repeat(g2[:nb, 0, :], l, axis=0)
        xa = xo + jnp.where(is_ctx[:, None], g2[nb, 0, :][None, :], g2_lat) * f
    return None
```

```python
import functools
import math

import jax
import jax.numpy as jnp
from jax import lax
from jax.experimental import pallas as pl
from jax.experimental.pallas import tpu as pltpu

F32 = jnp.float32
BF16 = jnp.bfloat16
HIGHEST = lax.Precision.HIGHEST

GRID_W = 64
NORM_EPS = 1e-6
ROPE_BASE = 10000.0
LRU_WIDTH = 512
LRU_BLOCKS = 8
LRU_BLOCK_W = LRU_WIDTH // LRU_BLOCKS
LRU_C = 8.0
CONV_W = 4
MLA_HEADS = 4
MLA_Q_RANK = 384
MLA_KV_RANK = 256
MLA_NOPE = 128
MLA_ROPE = 64
MLA_V = 128
MLA_QK = MLA_NOPE + MLA_ROPE
DIFF_HEADS = 8
DIFF_HEAD_DIM = 64
DIFF_V = 2 * DIFF_HEAD_DIM
N_EXPERTS = 32
TOP_K = 4
SWIGLU_LIMIT = 7.0
SWIGLU_ALPHA = 1.702

LANES = 128
SUBLANES = 8
VMEM_LIMIT = 52 * 1024 * 1024

TM = 256
TQ = 512
LRU_CT = 128
LRU_CHUNK = 128
LRU_SEG = LRU_CHUNK // SUBLANES
LRU_PAD = 8
MOE_BM = 256
NEG_BIG = -1e30
LOG2_E = math.log2(math.e)


def _cparams(sem, vmem=VMEM_LIMIT):
    return pltpu.CompilerParams(dimension_semantics=sem, vmem_limit_bytes=vmem)


def _adaln_kernel(c_ref, w_ref, b_ref, o_ref):
    c = c_ref[...]
    s = c * jax.nn.sigmoid(c)
    o_ref[...] = jnp.dot(s, w_ref[...], preferred_element_type=F32, precision=HIGHEST) + b_ref[...]


def _adaln(cvec, w, b):
    rows, d = cvec.shape
    n = w.shape[1]
    tn = 1536
    return pl.pallas_call(
        _adaln_kernel,
        grid=(n // tn,),
        in_specs=[pl.BlockSpec((rows, d), lambda j: (0, 0)),
                  pl.BlockSpec((d, tn), lambda j: (0, j)),
                  pl.BlockSpec((1, tn), lambda j: (0, j))],
        out_specs=pl.BlockSpec((rows, tn), lambda j: (0, j)),
        out_shape=jax.ShapeDtypeStruct((rows, n), F32),
        compiler_params=_cparams(("arbitrary",)),
        name="adaln_mod",
    )(cvec, w, b.reshape(1, n))


def _norm_mod(x, g, sc, sh):
    var = jnp.mean(x * x, axis=-1, keepdims=True)
    y = x * lax.rsqrt(var + NORM_EPS) * g
    return y * (1.0 + sc) + sh


def _ones_column(shape):
    lane = lax.broadcasted_iota(jnp.int32, shape, 1)
    return jnp.where(lane == 0, 1.0, 0.0).astype(BF16)


def _rope_tile(x, cos, sin_a, sin_b):
    up = pltpu.roll(x, LANES - 16, axis=1)
    dn = pltpu.roll(x, 16, axis=1)
    return x * cos + up * sin_a + dn * sin_b


def _rope_tables(seq, rot_dim):
    n_rows = seq // GRID_W
    rows = jnp.repeat(jnp.arange(n_rows, dtype=F32), GRID_W)
    cols = jnp.tile(jnp.arange(GRID_W, dtype=F32), n_rows)
    axis_dim = rot_dim // 2
    inv_freq = ROPE_BASE ** (-jnp.arange(0, axis_dim, 2, dtype=F32) / axis_dim)
    ang_r = rows[:, None] * inv_freq
    ang_c = cols[:, None] * inv_freq
    ang = jnp.concatenate([ang_r, ang_r, ang_c, ang_c], axis=-1)
    cos, sin = jnp.cos(ang), jnp.sin(ang)
    quarter = rot_dim // 4
    first = (jnp.arange(rot_dim) % (2 * quarter)) < quarter
    sin_a = jnp.where(first, -sin, 0.0)
    sin_b = jnp.where(first, 0.0, sin)
    return cos, sin_a, sin_b


def _rope_tables_128(seq, ctx, rot_dim, tile_groups):
    cos, sin_a, sin_b = _rope_tables(seq, rot_dim)
    if tile_groups:
        reps = LANES // rot_dim
        cos, sin_a, sin_b = (jnp.tile(t, (1, reps)) for t in (cos, sin_a, sin_b))
    else:
        pad = LANES - rot_dim
        cos = jnp.concatenate([cos, jnp.ones((seq, pad), F32)], axis=-1)
        sin_a = jnp.concatenate([sin_a, jnp.zeros((seq, pad), F32)], axis=-1)
        sin_b = jnp.concatenate([sin_b, jnp.zeros((seq, pad), F32)], axis=-1)
    cos = jnp.concatenate([cos, jnp.ones((ctx, LANES), F32)], axis=0)
    sin_a = jnp.concatenate([sin_a, jnp.zeros((ctx, LANES), F32)], axis=0)
    sin_b = jnp.concatenate([sin_b, jnp.zeros((ctx, LANES), F32)], axis=0)
    return cos, sin_a, sin_b


def _mod_index(tpb, nb):
    def idx(i):
        return jnp.where(i % tpb == tpb - 1, nb, i // tpb)
    return idx


def _even_proj_kernel(x_ref, g_ref, sc_ref, sh_ref, wg_ref, wr_ref, wq_ref, wkv_ref, wkr_ref,
                      og_ref, or_ref, oq_ref, okv_ref, okr_ref):
    h = _norm_mod(x_ref[...], g_ref[...], sc_ref[...], sh_ref[...]).astype(BF16)
    for w_ref, o_ref in ((wg_ref, og_ref), (wr_ref, or_ref), (wq_ref, oq_ref),
                         (wkv_ref, okv_ref), (wkr_ref, okr_ref)):
        o_ref[...] = jnp.dot(h, w_ref[...], preferred_element_type=F32)


def _even_proj(x, gain, sc, sh, weights, tpb, nb):
    t, d = x.shape
    midx = _mod_index(tpb, nb)
    w_specs = [pl.BlockSpec(w.shape, lambda i: (0, 0)) for w in weights]
    return pl.pallas_call(
        _even_proj_kernel,
        grid=(t // TM,),
        in_specs=[pl.BlockSpec((TM, d), lambda i: (i, 0)),
                  pl.BlockSpec((1, d), lambda i: (0, 0)),
                  pl.BlockSpec((None, 1, d), lambda i: (midx(i), 0, 0)),
                  pl.BlockSpec((None, 1, d), lambda i: (midx(i), 0, 0))] + w_specs,
        out_specs=[pl.BlockSpec((TM, w.shape[1]), lambda i: (i, 0)) for w in weights],
        out_shape=[jax.ShapeDtypeStruct((t, w.shape[1]), F32) for w in weights],
        compiler_params=_cparams(("parallel",)),
        name="even_in_proj",
    )(x, gain, sc, sh, *weights)


def _mla_prep_kernel(qc_ref, kvc_ref, kr_ref, qng_ref, kvng_ref, wqn_ref, wqr_ref, wkv_ref,
                     qgn_ref, qgr_ref, kgn_ref, kgr_ref, cos_ref, sa_ref, sb_ref,
                     q_ref, k_ref, v_ref):
    cos, sa, sb = cos_ref[...], sa_ref[...], sb_ref[...]
    inv_qk = 1.0 / MLA_QK

    qc = qc_ref[...]
    hq = (qc * lax.rsqrt(jnp.mean(qc * qc, axis=-1, keepdims=True) + NORM_EPS) * qng_ref[...]).astype(BF16)
    q_nope = jnp.dot(hq, wqn_ref[...], preferred_element_type=F32)
    q_rope = jnp.dot(hq, wqr_ref[...], preferred_element_type=F32)
    q_scale = MLA_QK ** -0.5 * LOG2_E
    for h in range(MLA_HEADS):
        qn = q_nope[:, h * LANES:(h + 1) * LANES]
        qr = q_rope[:, h * LANES:(h + 1) * LANES]
        ms = (jnp.sum(qn * qn, axis=-1, keepdims=True) + jnp.sum(qr * qr, axis=-1, keepdims=True)) * inv_qk
        rs = lax.rsqrt(ms + NORM_EPS) * q_scale
        q_ref[:, (2 * h) * LANES:(2 * h + 1) * LANES] = (qn * rs * qgn_ref[...]).astype(BF16)
        q_ref[:, (2 * h + 1) * LANES:(2 * h + 2) * LANES] = _rope_tile(qr * rs * qgr_ref[...], cos, sa, sb).astype(BF16)

    kvc = kvc_ref[...]
    hkv = (kvc * lax.rsqrt(jnp.mean(kvc * kvc, axis=-1, keepdims=True) + NORM_EPS) * kvng_ref[...]).astype(BF16)
    kv = jnp.dot(hkv, wkv_ref[...], preferred_element_type=F32)
    kr = kr_ref[...]
    kr_ss = jnp.sum(kr * kr, axis=-1, keepdims=True)
    kr_rot = _rope_tile(kr * kgr_ref[...], cos, sa, sb)
    for h in range(MLA_HEADS):
        kn = kv[:, (2 * h) * LANES:(2 * h + 1) * LANES]
        ms = (jnp.sum(kn * kn, axis=-1, keepdims=True) + kr_ss) * inv_qk
        rs = lax.rsqrt(ms + NORM_EPS)
        k_ref[:, (2 * h) * LANES:(2 * h + 1) * LANES] = (kn * rs * kgn_ref[...]).astype(BF16)
        k_ref[:, (2 * h + 1) * LANES:(2 * h + 2) * LANES] = (kr_rot * rs).astype(BF16)
        v_ref[:, (2 * h) * LANES:(2 * h + 1) * LANES] = kv[:, (2 * h + 1) * LANES:(2 * h + 2) * LANES].astype(BF16)
        v_ref[:, (2 * h + 1) * LANES:(2 * h + 2) * LANES] = _ones_column((kv.shape[0], LANES))


def _mla_prep(qc, kvc, kr, q_norm_g, kv_norm_g, w_q_nope, w_q_rope, w_ukv, q_gn, q_gr, k_gn, k_gr,
              rope, tpb):
    t = qc.shape[0]
    cos, sa, sb = rope

    def const(a):
        return pl.BlockSpec(a.shape, lambda i: (0, 0))

    def rows(a):
        return pl.BlockSpec((TM, a.shape[1]), lambda i: (i, 0))

    tab = pl.BlockSpec((TM, LANES), lambda i: (i % tpb, 0))
    hq, hv = MLA_HEADS * 2 * LANES, MLA_HEADS * 2 * MLA_V
    return pl.pallas_call(
        _mla_prep_kernel,
        grid=(t // TM,),
        in_specs=[rows(qc), rows(kvc), rows(kr), const(q_norm_g), const(kv_norm_g),
                  const(w_q_nope), const(w_q_rope), const(w_ukv),
                  const(q_gn), const(q_gr), const(k_gn), const(k_gr), tab, tab, tab],
        out_specs=[pl.BlockSpec((TM, hq), lambda i: (i, 0)),
                   pl.BlockSpec((TM, hq), lambda i: (i, 0)),
                   pl.BlockSpec((TM, hv), lambda i: (i, 0))],
        out_shape=[jax.ShapeDtypeStruct((t, hq), BF16),
                   jax.ShapeDtypeStruct((t, hq), BF16),
                   jax.ShapeDtypeStruct((t, hv), BF16)],
        compiler_params=_cparams(("parallel",)),
        name="mla_prep",
    )(qc, kvc, kr, q_norm_g, kv_norm_g, w_q_nope, w_q_rope, w_ukv, q_gn, q_gr, k_gn, k_gr, cos, sa, sb)


def _sublane_iota():
    return lax.broadcasted_iota(jnp.int32, (SUBLANES, LANES), 0)


def _scan_chunk(a_chunk, u_chunk, h_chunk, carry, reverse):
    steps = range(LRU_SEG - 1, -1, -1) if reverse else range(LRU_SEG)
    h_loc, p_loc = [None] * LRU_SEG, [None] * LRU_SEG
    h = p = None
    for j in steps:
        a = a_chunk[pl.ds(j, SUBLANES, stride=LRU_SEG), :]
        u = u_chunk[pl.ds(j, SUBLANES, stride=LRU_SEG), :]
        if h is None:
            h, p = u, a
        else:
            h, p = a * h + u, a * p
        h_loc[j], p_loc[j] = h, p
    sub = _sublane_iota()
    seg_p, seg_h = p, h
    for d in (1, 2, 4):
        shift = SUBLANES - d if reverse else d
        prev_p = pltpu.roll(seg_p, shift, axis=0)
        prev_h = pltpu.roll(seg_h, shift, axis=0)
        valid = (sub < SUBLANES - d) if reverse else (sub >= d)
        seg_h = jnp.where(valid, seg_p * prev_h + seg_h, seg_h)
        seg_p = jnp.where(valid, seg_p * prev_p, seg_p)
    h_end = seg_h + seg_p * carry
    if reverse:
        h_in = jnp.where(sub == SUBLANES - 1, carry, pltpu.roll(h_end, SUBLANES - 1, axis=0))
        new_carry = h_end[0:1, :]
    else:
        h_in = jnp.where(sub == 0, carry, pltpu.roll(h_end, 1, axis=0))
        new_carry = h_end[SUBLANES - 1:SUBLANES, :]
    for j in range(LRU_SEG):
        h_chunk[pl.ds(j, SUBLANES, stride=LRU_SEG), :] = h_loc[j] + p_loc[j] * h_in
    return jnp.broadcast_to(new_carry, (SUBLANES, LANES))


def _lru_kernel(g_ref, r_ref, cw_ref, cb_ref, wg_ref, bg_ref, lam_ref, o_ref,
                rp_ref, af_ref, uf_ref, ab_ref, ub_ref, hf_ref, hb_ref, *, ctx, seq):
    n_ctx, n_lat = ctx // LRU_CHUNK, seq // LRU_CHUNK
    zeros_pad = jnp.zeros((LRU_PAD, LRU_CT), F32)
    ctx0 = seq + 2 * LRU_PAD
    rp_ref[0:LRU_PAD, :] = zeros_pad
    rp_ref[LRU_PAD:LRU_PAD + seq, :] = r_ref[0:seq, :]
    rp_ref[LRU_PAD + seq:ctx0, :] = zeros_pad
    rp_ref[ctx0:ctx0 + ctx, :] = r_ref[seq:seq + ctx, :]
    rp_ref[ctx0 + ctx:ctx0 + ctx + LRU_PAD, :] = zeros_pad

    cw = cw_ref[...]
    cb = cb_ref[...]
    wg = wg_ref[...]
    bg = bg_ref[...]
    lam = lam_ref[...]
    sp = jnp.maximum(-lam, 0.0) + jnp.log1p(jnp.exp(-jnp.abs(lam)))

    def coeff_chunk(c, pad_off, row_off):
        start = pl.multiple_of(pad_off + c * LRU_CHUNK, SUBLANES)
        ext = rp_ref[pl.ds(start, LRU_CHUNK + 2 * LRU_PAD), :]
        x = cb
        for tap in range(CONV_W):
            lo = LRU_PAD - 2 + tap
            x = x + ext[lo:lo + LRU_CHUNK, :] * cw[tap:tap + 1, :]
        gates = jnp.dot(x.astype(BF16), wg, preferred_element_type=F32) + bg
        out_row = pl.multiple_of(row_off + c * LRU_CHUNK, SUBLANES)
        for d, (a_ref, u_ref) in enumerate(((af_ref, uf_ref), (ab_ref, ub_ref))):
            r = jax.nn.sigmoid(gates[:, (2 * d) * LRU_CT:(2 * d + 1) * LRU_CT])
            i = jax.nn.sigmoid(gates[:, (2 * d + 1) * LRU_CT:(2 * d + 2) * LRU_CT])
            log_a = -LRU_C * r * sp[d:d + 1, :]
            a_ref[pl.ds(out_row, LRU_CHUNK), :] = jnp.exp(log_a)
            th = jnp.tanh(log_a)
            u_ref[pl.ds(out_row, LRU_CHUNK), :] = jnp.sqrt(-2.0 * th / (1.0 - th)) * i * x
        return None

    def coeff_ctx(c, _):
        coeff_chunk(c, seq + LRU_PAD, seq)
        return 0

    def coeff_lat(c, _):
        coeff_chunk(c, 0, 0)
        return 0

    lax.fori_loop(0, n_ctx, coeff_ctx, 0)
    lax.fori_loop(0, n_lat, coeff_lat, 0)

    def scan_pair(n, row_off):
        def body(c, carry):
            cf, cb_ = carry
            f_row = pl.multiple_of(row_off + c * LRU_CHUNK, SUBLANES)
            b_row = pl.multiple_of(row_off + (n - 1 - c) * LRU_CHUNK, SUBLANES)
            cf = _scan_chunk(af_ref.at[pl.ds(f_row, LRU_CHUNK), :], uf_ref.at[pl.ds(f_row, LRU_CHUNK), :],
                             hf_ref.at[pl.ds(f_row, LRU_CHUNK), :], cf, False)
            cb_ = _scan_chunk(ab_ref.at[pl.ds(b_row, LRU_CHUNK), :], ub_ref.at[pl.ds(b_row, LRU_CHUNK), :],
                              hb_ref.at[pl.ds(b_row, LRU_CHUNK), :], cb_, True)
            return cf, cb_
        return body

    zero = jnp.zeros((SUBLANES, LANES), F32)
    carry = lax.fori_loop(0, n_ctx, scan_pair(n_ctx, seq), (zero, zero))
    lax.fori_loop(0, n_lat, scan_pair(n_lat, 0), carry)

    g = g_ref[...]
    gelu = 0.5 * g * (1.0 + jnp.tanh(math.sqrt(2.0 / math.pi) * (g + 0.044715 * (g * g * g))))
    o_ref[...] = (gelu * (hf_ref[...] + hb_ref[...])).astype(BF16)


def _lru(g, r, conv_w, conv_b, w_gates, b_gates, lam, nb, ctx, seq):
    l = ctx + seq
    width = g.shape[1]
    n_ct = width // LRU_CT
    g3 = g.reshape(nb, l, width)
    r3 = r.reshape(nb, l, width)
    seq_spec = pl.BlockSpec((None, l, LRU_CT), lambda b, c: (b, 0, c))
    scratch = [pltpu.VMEM((l + 3 * LRU_PAD, LRU_CT), F32)] + [pltpu.VMEM((l, LRU_CT), F32)] * 6
    out = pl.pallas_call(
        functools.partial(_lru_kernel, ctx=ctx, seq=seq),
        grid=(nb, n_ct),
        in_specs=[seq_spec, seq_spec,
                  pl.BlockSpec((CONV_W, LRU_CT), lambda b, c: (0, c)),
                  pl.BlockSpec((1, LRU_CT), lambda b, c: (0, c)),
                  pl.BlockSpec((None, LRU_CT, 4 * LRU_CT), lambda b, c: (c, 0, 0)),
                  pl.BlockSpec((None, 1, 4 * LRU_CT), lambda b, c: (c, 0, 0)),
                  pl.BlockSpec((2, LRU_CT), lambda b, c: (0, c))],
        out_specs=seq_spec,
        out_shape=jax.ShapeDtypeStruct((nb, l, width), BF16),
        scratch_shapes=scratch,
        compiler_params=_cparams(("parallel", "parallel")),
        name="rglru",
    )(g3, r3, conv_w, conv_b, w_gates, b_gates, lam)
    return out.reshape(nb * l, width)


def _lru_gate_weights(wa, ba, wx, bx):
    per = LRU_CT // LRU_BLOCK_W
    n_ct = LRU_BLOCKS // per
    eye = jnp.eye(per, dtype=F32)

    def dense(w):
        w4 = w.reshape(n_ct, per, LRU_BLOCK_W, LRU_BLOCK_W)
        return jnp.einsum('cide,ij->cidje', w4, eye).reshape(n_ct, LRU_CT, LRU_CT)

    w = jnp.concatenate([dense(wa[0]), dense(wx[0]), dense(wa[1]), dense(wx[1])], axis=-1)
    b = jnp.concatenate([v.reshape(n_ct, 1, LRU_CT) for v in (ba[0], bx[0], ba[1], bx[1])], axis=-1)
    return w.astype(BF16), b


def _softmax_pv(q, kt_ref, v_ref, s_ref, p_ref):
    n_keys = kt_ref.shape[-1]
    half = n_keys // 2 if n_keys % (2 * LANES) == 0 else n_keys
    spans = [(lo, min(lo + half, n_keys)) for lo in range(0, n_keys, half)]
    for lo, hi in spans:
        s_ref[:, lo:hi] = jnp.dot(q, kt_ref[:, lo:hi], preferred_element_type=F32)
    m = jnp.max(s_ref[...], axis=-1, keepdims=True)
    acc = None
    for lo, hi in spans:
        p_ref[:, lo:hi] = jnp.exp2(s_ref[:, lo:hi] - m).astype(BF16)
        part = jnp.dot(p_ref[:, lo:hi], v_ref[lo:hi, :], preferred_element_type=F32)
        acc = part if acc is None else acc + part
    return acc


def _mla_attn_kernel(q_ref, kt_ref, v_ref, o_ref, s_ref, p_ref):
    dv = o_ref.shape[-1]
    acc = _softmax_pv(q_ref[...], kt_ref, v_ref, s_ref.at[0], p_ref.at[0])
    o_ref[...] = (acc[:, :dv] / acc[:, dv:dv + 1]).astype(o_ref.dtype)


def _diff_attn_kernel(q_ref, kt_ref, v_ref, g_ref, lam_ref, o_ref, s_ref, p_ref, *, out_scale):
    dv = o_ref.shape[-1]
    q = q_ref[...]
    lane = lax.broadcasted_iota(jnp.int32, q.shape, 1)
    zero = jnp.zeros_like(q)
    q1 = jnp.where(lane < DIFF_HEAD_DIM, q, zero)
    q2 = jnp.where(lane < DIFF_HEAD_DIM, zero, q)
    a1 = _softmax_pv(q1, kt_ref, v_ref, s_ref.at[0], p_ref.at[0])
    a2 = _softmax_pv(q2, kt_ref, v_ref, s_ref.at[1], p_ref.at[1])
    o = a1[:, :dv] / a1[:, dv:dv + 1] - lam_ref[0] * (a2[:, :dv] / a2[:, dv:dv + 1])
    y = o * lax.rsqrt(jnp.mean(o * o, axis=-1, keepdims=True) + NORM_EPS) * g_ref[...]
    o_ref[...] = (y * out_scale).astype(o_ref.dtype)


def _attention(kernel_fn, name, n_softmax, q, kt, v, nb, ctx, seq, heads, dk, dv, ctx_queries,
               extra_in=(), extra_specs=()):
    l = ctx + seq
    q3 = q.reshape(nb, l, heads * dk)
    v3 = v.reshape(nb, l, heads * 2 * dv)
    if ctx_queries:
        tq, n_q, n_keys, rows = ctx, 1, ctx, ctx
        blk = seq // ctx
        in_specs = [pl.BlockSpec((None, tq, dk), lambda b, h, i: (b, blk, h)),
                    pl.BlockSpec((None, None, dk, ctx), lambda b, h, i: (b, h, 0, blk)),
                    pl.BlockSpec((None, ctx, 2 * dv), lambda b, h, i: (b, blk, h))]
    else:
        tq, n_q, n_keys, rows = TQ, seq // TQ, l, seq
        in_specs = [pl.BlockSpec((None, tq, dk), lambda b, h, i: (b, i, h)),
                    pl.BlockSpec((None, None, dk, l), lambda b, h, i: (b, h, 0, 0)),
                    pl.BlockSpec((None, l, 2 * dv), lambda b, h, i: (b, 0, h))]
    out = pl.pallas_call(
        kernel_fn,
        grid=(nb, heads, n_q),
        in_specs=in_specs + list(extra_specs),
        out_specs=pl.BlockSpec((None, tq, dv), lambda b, h, i: (b, i, h)),
        out_shape=jax.ShapeDtypeStruct((nb, rows, heads * dv), BF16),
        scratch_shapes=[pltpu.VMEM((n_softmax, tq, n_keys), F32), pltpu.VMEM((n_softmax, tq, n_keys), BF16)],
        compiler_params=_cparams(("parallel", "parallel", "arbitrary")),
        name=name,
    )(q3, kt, v3, *extra_in)
    return out


def _key_transpose(k, nb, l, heads, dk):
    return jnp.transpose(k.reshape(nb, l, heads, dk), (0, 2, 3, 1))


def _attend_all(kernel_fn, name, n_softmax, q, k, v, nb, ctx, seq, heads, dk, dv, need_ctx, **extra):
    kt = _key_transpose(k, nb, ctx + seq, heads, dk)
    lat = _attention(kernel_fn, name, n_softmax, q, kt, v, nb, ctx, seq, heads, dk, dv, False, **extra)
    if not need_ctx:
        return lat.reshape(nb * seq, heads * dv)
    cx = _attention(kernel_fn, name + "_ctx", n_softmax, q, kt, v, nb, ctx, seq, heads, dk, dv, True, **extra)
    return jnp.concatenate([lat, cx], axis=1).reshape(nb * (ctx + seq), heads * dv)


def _group_ms(x, gmat):
    return jnp.dot((x * x).astype(BF16), gmat, preferred_element_type=F32)


def _odd_proj_kernel(x_ref, g_ref, sc_ref, sh_ref, wq_ref, wk_ref, wv_ref, qg_ref, kg_ref, gm_ref,
                     cos_ref, sa_ref, sb_ref, q_ref, k_ref, v_ref):
    h = _norm_mod(x_ref[...], g_ref[...], sc_ref[...], sh_ref[...]).astype(BF16)
    cos, sa, sb = cos_ref[...], sa_ref[...], sb_ref[...]
    gmat = gm_ref[...]
    q_scale = DIFF_HEAD_DIM ** -0.5 * LOG2_E
    for w_ref, gain_ref, o_ref, scale in ((wq_ref, qg_ref, q_ref, q_scale), (wk_ref, kg_ref, k_ref, 1.0)):
        raw = jnp.dot(h, w_ref[...], preferred_element_type=F32)
        gain = gain_ref[...] * scale
        for c in range(raw.shape[1] // LANES):
            t = raw[:, c * LANES:(c + 1) * LANES]
            t = t * lax.rsqrt(_group_ms(t, gmat) + NORM_EPS) * gain
            o_ref[:, c * LANES:(c + 1) * LANES] = _rope_tile(t, cos, sa, sb).astype(BF16)
    v = jnp.dot(h, wv_ref[...], preferred_element_type=F32).astype(BF16)
    for c in range(v.shape[1] // DIFF_V):
        v_ref[:, (2 * c) * DIFF_V:(2 * c + 1) * DIFF_V] = v[:, c * DIFF_V:(c + 1) * DIFF_V]
        v_ref[:, (2 * c + 1) * DIFF_V:(2 * c + 2) * DIFF_V] = _ones_column((v.shape[0], DIFF_V))


def _odd_proj(x, gain, sc, sh, wq, wk, wv, q_gain, k_gain, gmat, rope, tpb, nb):
    t, d = x.shape
    midx = _mod_index(tpb, nb)
    cos, sa, sb = rope

    def const(a):
        return pl.BlockSpec(a.shape, lambda i: (0, 0))

    tab = pl.BlockSpec((TM, LANES), lambda i: (i % tpb, 0))
    n = wq.shape[1]
    return pl.pallas_call(
        _odd_proj_kernel,
        grid=(t // TM,),
        in_specs=[pl.BlockSpec((TM, d), lambda i: (i, 0)), const(gain),
                  pl.BlockSpec((None, 1, d), lambda i: (midx(i), 0, 0)),
                  pl.BlockSpec((None, 1, d), lambda i: (midx(i), 0, 0)),
                  const(wq), const(wk), const(wv), const(q_gain), const(k_gain), const(gmat),
                  tab, tab, tab],
        out_specs=[pl.BlockSpec((TM, n), lambda i: (i, 0))] * 2 + [pl.BlockSpec((TM, 2 * n), lambda i: (i, 0))],
        out_shape=[jax.ShapeDtypeStruct((t, n), BF16)] * 2 + [jax.ShapeDtypeStruct((t, 2 * n), BF16)],
        compiler_params=_cparams(("parallel",)),
        name="odd_in_proj",
    )(x, gain, sc, sh, wq, wk, wv, q_gain, k_gain, gmat, cos, sa, sb)


def _post_mix_kernel(*refs, n_mix):
    mix_refs = refs[:n_mix]
    w_refs = refs[n_mix:2 * n_mix]
    (x_ref, g1_ref, n2_ref, sc_ref, sh_ref, rw_ref, rb_ref,
     xo_ref, h2_ref, te_ref, tw_ref, tr_ref, cnt_ref, base_ref) = refs[2 * n_mix:]
    i = pl.program_id(0)

    @pl.when(i == 0)
    def _():
        base_ref[...] = jnp.zeros_like(base_ref)

    m = jnp.dot(mix_refs[0][...], w_refs[0][...], preferred_element_type=F32)
    for a_ref, w_ref in zip(mix_refs[1:], w_refs[1:]):
        m = m + jnp.dot(a_ref[...], w_ref[...], preferred_element_type=F32)
    x = x_ref[...] + g1_ref[...] * m
    xo_ref[...] = x
    h2 = _norm_mod(x, n2_ref[...], sc_ref[...], sh_ref[...])
    h2_ref[...] = h2.astype(BF16)

    logits = jnp.dot(h2, rw_ref[...], preferred_element_type=F32, precision=HIGHEST) + rb_ref[...]
    lane = lax.broadcasted_iota(jnp.int32, logits.shape, 1)
    lane_f = lane.astype(F32)
    vals = jnp.where(lane < N_EXPERTS, logits, NEG_BIG)
    tops, hots = [], []
    for _ in range(TOP_K):
        top = jnp.max(vals, axis=-1, keepdims=True)
        first = jnp.min(jnp.where(vals == top, lane_f, float(LANES)), axis=-1, keepdims=True)
        hot = lane_f == first
        tops.append(top)
        hots.append(hot)
        vals = jnp.where(hot, 2.0 * NEG_BIG, vals)
    exps = [jnp.exp(t - tops[0]) for t in tops]
    denom = exps[0] + exps[1] + exps[2] + exps[3]

    picked = (hots[0] | hots[1]) | (hots[2] | hots[3])
    cnt = jnp.where(picked, 1.0, 0.0)
    r_io = lax.broadcasted_iota(jnp.int32, (TM, TM), 0)
    c_io = lax.broadcasted_iota(jnp.int32, (TM, TM), 1)
    tri = jnp.where(c_io < r_io, 1.0, 0.0).astype(BF16)
    before = jnp.dot(tri, cnt.astype(BF16), preferred_element_type=F32) + base_ref[...]
    te = jnp.zeros(logits.shape, F32)
    tw = jnp.zeros(logits.shape, F32)
    tr = jnp.zeros(logits.shape, F32)
    for k in range(TOP_K):
        e_k = jnp.sum(jnp.where(hots[k], lane_f, 0.0), axis=-1, keepdims=True)
        r_k = jnp.sum(jnp.where(hots[k], before, 0.0), axis=-1, keepdims=True)
        te = jnp.where(lane == k, e_k, te)
        tw = jnp.where(lane == k, exps[k] / denom, tw)
        tr = jnp.where(lane == k, r_k, tr)
    te_ref[...] = te.astype(jnp.int32)
    tw_ref[...] = tw
    tr_ref[...] = tr.astype(jnp.int32)
    base_ref[...] += jnp.sum(cnt, axis=0, keepdims=True)
    cnt_ref[...] = base_ref[...]


def _post_mix(mixes, weights, x, g1, n2g, sc2, sh2, router_w, router_b, tpb, nb, latent_only):
    d = x.shape[1]
    tiles_in = tpb
    if latent_only:
        tiles_out = tpb - 1
        n_tiles = nb * tiles_out

        def in_row(i):
            return (i // tiles_out) * tiles_in + i % tiles_out

        def midx(i):
            return i // tiles_out
    else:
        n_tiles = nb * tpb

        def in_row(i):
            return i

        midx = _mod_index(tpb, nb)
    t_out = n_tiles * TM

    def const(a):
        return pl.BlockSpec(a.shape, lambda i: (0, 0))

    def mod(a):
        return pl.BlockSpec((None, 1, d), lambda i: (midx(i), 0, 0))

    in_specs = [pl.BlockSpec((TM, a.shape[1]), lambda i: (i, 0)) for a in mixes]
    in_specs += [const(w) for w in weights]
    in_specs += [pl.BlockSpec((TM, d), lambda i: (in_row(i), 0)), mod(g1), const(n2g), mod(sc2), mod(sh2),
                 const(router_w), const(router_b)]
    row_out = lambda width: pl.BlockSpec((TM, width), lambda i: (i, 0))
    return pl.pallas_call(
        functools.partial(_post_mix_kernel, n_mix=len(mixes)),
        grid=(n_tiles,),
        in_specs=in_specs,
        out_specs=[row_out(d), row_out(d), row_out(LANES), row_out(LANES), row_out(LANES),
                   pl.BlockSpec((1, LANES), lambda i: (0, 0))],
        out_shape=[jax.ShapeDtypeStruct((t_out, d), F32), jax.ShapeDtypeStruct((t_out, d), BF16),
                   jax.ShapeDtypeStruct((t_out, LANES), jnp.int32), jax.ShapeDtypeStruct((t_out, LANES), F32),
                   jax.ShapeDtypeStruct((t_out, LANES), jnp.int32), jax.ShapeDtypeStruct((1, LANES), F32)],
        scratch_shapes=[pltpu.VMEM((1, LANES), F32)],
        compiler_params=_cparams(("arbitrary",)),
        name="post_mix_router",
    )(*mixes, *weights, x, g1, n2g, sc2, sh2, router_w, router_b)


def _moe_ffn_kernel(be_ref, nu_ref, x_ref, wgu_ref, bgu_ref, wd_ref, bd_ref, o_ref, wgu_bf, wd_bf):
    i = pl.program_id(0)
    prev = be_ref[jnp.maximum(i - 1, 0)]
    fresh = jnp.logical_or(i == 0, be_ref[i] != prev)

    @pl.when(jnp.logical_and(fresh, i < nu_ref[0]))
    def _():
        wgu_bf[...] = wgu_ref[...].astype(BF16)
        wd_bf[...] = wd_ref[...].astype(BF16)

    @pl.when(i < nu_ref[0])
    def _():
        d_ff = wd_bf.shape[0]
        gu = jnp.dot(x_ref[...], wgu_bf[...], preferred_element_type=F32) + bgu_ref[...]
        g = jnp.minimum(gu[:, :d_ff], SWIGLU_LIMIT)
        u = jnp.clip(gu[:, d_ff:], -SWIGLU_LIMIT, SWIGLU_LIMIT)
        act = (u + 1.0) * g * jax.nn.sigmoid(SWIGLU_ALPHA * g)
        y = jnp.dot(act.astype(BF16), wd_bf[...], preferred_element_type=F32) + bd_ref[...]
        o_ref[...] = y.astype(o_ref.dtype)

    @pl.when(i >= nu_ref[0])
    def _():
        o_ref[...] = jnp.zeros_like(o_ref)


def _moe_ffn(block_e, n_used, xs, w_gu, b_gu, w_down, b_down):
    n_rows, d = xs.shape
    n_e, _, two_ff = w_gu.shape
    d_ff = two_ff // 2
    n_blocks = n_rows // MOE_BM
    grid_spec = pltpu.PrefetchScalarGridSpec(
        num_scalar_prefetch=2,
        grid=(n_blocks,),
        in_specs=[pl.BlockSpec((MOE_BM, d), lambda i, be, nu: (i, 0)),
                  pl.BlockSpec((None, d, two_ff), lambda i, be, nu: (be[i], 0, 0)),
                  pl.BlockSpec((None, 1, two_ff), lambda i, be, nu: (be[i], 0, 0)),
                  pl.BlockSpec((None, d_ff, d), lambda i, be, nu: (be[i], 0, 0)),
                  pl.BlockSpec((None, 1, d), lambda i, be, nu: (be[i], 0, 0))],
        out_specs=pl.BlockSpec((MOE_BM, d), lambda i, be, nu: (i, 0)),
        scratch_shapes=[pltpu.VMEM((d, two_ff), BF16), pltpu.VMEM((d_ff, d), BF16)],
    )
    return pl.pallas_call(
        _moe_ffn_kernel,
        grid_spec=grid_spec,
        out_shape=jax.ShapeDtypeStruct((n_rows, d), BF16),
        compiler_params=_cparams(("arbitrary",)),
        name="moe_ffn",
    )(block_e, n_used, xs, w_gu, b_gu.reshape(n_e, 1, two_ff), w_down, b_down.reshape(n_e, 1, d))


def _moe_combine_kernel(x_ref, g2_ref, w_ref, y_ref, o_ref):
    d = o_ref.shape[-1]
    w = w_ref[...]
    f = w[:, 0:1] * y_ref[:, 0:d].astype(F32)
    for k in range(1, TOP_K):
        f = f + w[:, k:k + 1] * y_ref[:, k * d:(k + 1) * d].astype(F32)
    o_ref[...] = x_ref[...] + g2_ref[...] * f


def _moe_combine(x, g2, top_w, picked, midx):
    t, d = x.shape
    return pl.pallas_call(
        _moe_combine_kernel,
        grid=(t // TM,),
        in_specs=[pl.BlockSpec((TM, d), lambda i: (i, 0)),
                  pl.BlockSpec((None, 1, d), lambda i: (midx(i), 0, 0)),
                  pl.BlockSpec((TM, LANES), lambda i: (i, 0)),
                  pl.BlockSpec((TM, TOP_K * d), lambda i: (i, 0))],
        out_specs=pl.BlockSpec((TM, d), lambda i: (i, 0)),
        out_shape=jax.ShapeDtypeStruct((t, d), F32),
        compiler_params=_cparams(("parallel",)),
        name="moe_combine",
    )(x, g2, top_w, picked)


def _moe(x, g2, midx, h2, top_e, top_w, top_r, counts, w_gu, b_gu, w_down, b_down):
    t, d = h2.shape
    n_assign = t * TOP_K
    n_blocks = -(-n_assign // MOE_BM) + N_EXPERTS
    n_rows = n_blocks * MOE_BM
    cnt = counts[0, :N_EXPERTS].astype(jnp.int32)
    padded = (cnt + MOE_BM - 1) // MOE_BM * MOE_BM
    pends = jnp.cumsum(padded)
    pstarts = pends - padded
    e = top_e[:, :TOP_K]
    dest = pstarts[e] + top_r[:, :TOP_K]
    block_start = jnp.arange(n_blocks, dtype=jnp.int32) * MOE_BM
    block_e = jnp.sum((block_start[:, None] >= pends[None, :]).astype(jnp.int32), axis=1)
    block_e = jnp.minimum(block_e, N_EXPERTS - 1)
    n_used = (pends[-1] // MOE_BM).astype(jnp.int32).reshape(1)

    tok = jnp.broadcast_to(jnp.arange(t, dtype=jnp.int32)[:, None], (t, TOP_K))
    row_tok = jnp.zeros((n_rows,), jnp.int32).at[dest.reshape(-1)].set(tok.reshape(-1))
    xs = h2[row_tok]
    ys = _moe_ffn(block_e, n_used, xs, w_gu, b_gu, w_down, b_down)
    picked = ys[dest.reshape(-1)].reshape(t, TOP_K * d)
    return _moe_combine(x, g2, top_w, picked, midx)


def _pad_cols(w, width):
    return jnp.concatenate([w, jnp.zeros((w.shape[0], width - w.shape[1]), w.dtype)], axis=1)


def kernel(x, c, ctx, c_ctx, mod_w, mod_b, norm1_g, norm2_g, ev_w_in, ev_w_out, lru_conv_w, lru_conv_b, lru_wa, lru_ba, lru_wx, lru_bx, lru_lambda, mla_q_norm_g, mla_w_uq, mla_kv_norm_g, mla_w_ukv, mla_qn_g, mla_kn_g, od_w_in, od_w_out, diff_qn_g, diff_kn_g, diff_lq1, diff_lk1, diff_lq2, diff_lk2, diff_subln_g, router_w, router_b, moe_w_gu, moe_b_gu, moe_w_down, moe_b_down):
    nb, seq, d = x.shape
    n_ctx = ctx.shape[1]
    depth = mod_w.shape[0]
    l = n_ctx + seq
    tpb = l // TM
    assert n_ctx == TM and seq % TQ == 0 and seq % GRID_W == 0

    xa = jnp.concatenate([x, ctx], axis=1).reshape(nb * l, d)
    cvec = jnp.concatenate([c, c_ctx[None, :], jnp.zeros((SUBLANES - nb - 1, d), F32)], axis=0)
    router_w_p = jnp.concatenate([router_w, jnp.zeros((depth, d, LANES - N_EXPERTS), F32)], axis=-1)
    router_b_p = jnp.concatenate([router_b, jnp.zeros((depth, LANES - N_EXPERTS), F32)], axis=-1)

    for layer in range(depth):
        last = layer == depth - 1
        i = layer // 2
        mod = _adaln(cvec, mod_w[layer], mod_b[layer])[:nb + 1]
        sh1, sc1, g1, sh2, sc2, g2 = (mod[:, k * d:(k + 1) * d].reshape(nb + 1, 1, d) for k in range(6))
        n1g = norm1_g[layer].reshape(1, d)
        n2g = norm2_g[layer].reshape(1, d)

        if layer % 2 == 0:
            w_in = ev_w_in[i].astype(BF16)
            o = 2 * LRU_WIDTH
            splits = [w_in[:, :LRU_WIDTH], w_in[:, LRU_WIDTH:o], w_in[:, o:o + MLA_Q_RANK],
                      w_in[:, o + MLA_Q_RANK:o + MLA_Q_RANK + MLA_KV_RANK],
                      _pad_cols(w_in[:, o + MLA_Q_RANK + MLA_KV_RANK:], LANES)]
            gl, rl, qc, kvc, kr = _even_proj(xa, n1g, sc1, sh1, splits, tpb, nb)

            w_gates, b_gates = _lru_gate_weights(lru_wa[i], lru_ba[i], lru_wx[i], lru_bx[i])
            lru = _lru(gl, rl, lru_conv_w[i], lru_conv_b[i].reshape(1, LRU_WIDTH), w_gates, b_gates,
                       lru_lambda[i], nb, n_ctx, seq)

            w_uq = mla_w_uq[i].astype(BF16).reshape(MLA_Q_RANK, MLA_HEADS, MLA_QK)
            w_q_nope = w_uq[:, :, :MLA_NOPE].reshape(MLA_Q_RANK, MLA_HEADS * MLA_NOPE)
            w_q_rope = jnp.concatenate(
                [w_uq[:, :, MLA_NOPE:], jnp.zeros((MLA_Q_RANK, MLA_HEADS, LANES - MLA_ROPE), BF16)],
                axis=-1).reshape(MLA_Q_RANK, MLA_HEADS * LANES)
            pad_g = lambda g: jnp.concatenate([g, jnp.zeros((LANES - MLA_ROPE,), F32)]).reshape(1, LANES)
            rope = _rope_tables_128(seq, n_ctx, MLA_ROPE, tile_groups=False)
            q, k, v = _mla_prep(qc, kvc, kr, mla_q_norm_g[i].reshape(1, -1), mla_kv_norm_g[i].reshape(1, -1),
                                w_q_nope, w_q_rope, mla_w_ukv[i].astype(BF16),
                                mla_qn_g[i][:MLA_NOPE].reshape(1, LANES), pad_g(mla_qn_g[i][MLA_NOPE:]),
                                mla_kn_g[i][:MLA_NOPE].reshape(1, LANES), pad_g(mla_kn_g[i][MLA_NOPE:]),
                                rope, tpb)
            att = _attend_all(_mla_attn_kernel, "mla_attention", 1, q, k, v, nb, n_ctx, seq, MLA_HEADS,
                              2 * LANES, MLA_V, need_ctx=not last)
            w_out = ev_w_out[i].astype(BF16)
            if last:
                lru = lru.reshape(nb, l, LRU_WIDTH)[:, :seq].reshape(nb * seq, LRU_WIDTH)
            mixes = [lru, att]
            weights = [w_out[:LRU_WIDTH], w_out[LRU_WIDTH:]]
        else:
            lam_init = 0.8 - 0.6 * math.exp(-0.3 * layer)
            w_in = od_w_in[i].astype(BF16)
            n_qk = DIFF_HEADS * 2 * DIFF_HEAD_DIM
            gidx = jnp.arange(LANES) // DIFF_HEAD_DIM
            gmat = jnp.where(gidx[:, None] == gidx[None, :], 1.0 / DIFF_HEAD_DIM, 0.0).astype(BF16)
            rope = _rope_tables_128(seq, n_ctx, DIFF_HEAD_DIM, tile_groups=True)
            tile_g = lambda g: jnp.tile(g, LANES // DIFF_HEAD_DIM).reshape(1, LANES)
            q, k, v = _odd_proj(xa, n1g, sc1, sh1, w_in[:, :n_qk], w_in[:, n_qk:2 * n_qk], w_in[:, 2 * n_qk:],
                                tile_g(diff_qn_g[i]), tile_g(diff_kn_g[i]), gmat, rope, tpb, nb)
            lam = (jnp.exp(jnp.sum(diff_lq1[i] * diff_lk1[i])) - jnp.exp(jnp.sum(diff_lq2[i] * diff_lk2[i]))
                   + lam_init).reshape(1).astype(F32)
            att = _attend_all(
                functools.partial(_diff_attn_kernel, out_scale=1.0 - lam_init), "diff_attention", 2,
                q, k, v, nb, n_ctx, seq, DIFF_HEADS, 2 * DIFF_HEAD_DIM, DIFF_V, need_ctx=not last,
                extra_in=(diff_subln_g[i].reshape(1, DIFF_V), lam),
                extra_specs=(pl.BlockSpec((1, DIFF_V), lambda b, h, i: (0, 0)),
                             pl.BlockSpec(memory_space=pltpu.SMEM)))
            mixes = [att]
            weights = [od_w_out[i].astype(BF16)]

        xo, h2, te, tw, tr, counts = _post_mix(mixes, weights, xa, g1, n2g, sc2, sh2, router_w_p[layer],
                                               router_b_p[layer].reshape(1, LANES), tpb, nb, last)
        midx = (lambda t: t // (tpb - 1)) if last else _mod_index(tpb, nb)
        xa = _moe(xo, g2, midx, h2, te, tw, tr, counts, moe_w_gu[layer], moe_b_gu[layer], moe_w_down[layer],
                  moe_b_down[layer])
    return xa.reshape(nb, seq, d)
```

```python
import functools
import math

import jax
import jax.numpy as jnp
from jax import lax
from jax.experimental import pallas as pl
from jax.experimental.pallas import tpu as pltpu

F32 = jnp.float32
BF16 = jnp.bfloat16
HIGHEST = lax.Precision.HIGHEST

GRID_W = 64
NORM_EPS = 1e-6
ROPE_BASE = 10000.0
LRU_WIDTH = 512
LRU_BLOCKS = 8
LRU_BLOCK_W = LRU_WIDTH // LRU_BLOCKS
LRU_C = 8.0
CONV_W = 4
MLA_HEADS = 4
MLA_Q_RANK = 384
MLA_KV_RANK = 256
MLA_NOPE = 128
MLA_ROPE = 64
MLA_V = 128
MLA_QK = MLA_NOPE + MLA_ROPE
DIFF_HEADS = 8
DIFF_HEAD_DIM = 64
DIFF_V = 2 * DIFF_HEAD_DIM
N_EXPERTS = 32
TOP_K = 4
SWIGLU_LIMIT = 7.0
SWIGLU_ALPHA = 1.702

LANES = 128
SUBLANES = 8
VMEM_LIMIT = 52 * 1024 * 1024

TM = 256
TQ = 512
LRU_CT = 128
LRU_CHUNK = 128
LRU_SEG = LRU_CHUNK // SUBLANES
LRU_PAD = 8
MOE_BM = 256
NEG_BIG = -1e30
LOG2_E = math.log2(math.e)


def _cparams(sem, vmem=VMEM_LIMIT):
    return pltpu.CompilerParams(dimension_semantics=sem, vmem_limit_bytes=vmem)


def _adaln_kernel(c_ref, w_ref, b_ref, o_ref):
    c = c_ref[...]
    s = c * jax.nn.sigmoid(c)
    o_ref[...] = jnp.dot(s, w_ref[...], preferred_element_type=F32, precision=HIGHEST) + b_ref[...]


def _adaln(cvec, w, b):
    rows, d = cvec.shape
    n = w.shape[1]
    tn = 1536
    return pl.pallas_call(
        _adaln_kernel,
        grid=(n // tn,),
        in_specs=[pl.BlockSpec((rows, d), lambda j: (0, 0)),
                  pl.BlockSpec((d, tn), lambda j: (0, j)),
                  pl.BlockSpec((1, tn), lambda j: (0, j))],
        out_specs=pl.BlockSpec((rows, tn), lambda j: (0, j)),
        out_shape=jax.ShapeDtypeStruct((rows, n), F32),
        compiler_params=_cparams(("arbitrary",)),
        name="adaln_mod",
    )(cvec, w, b.reshape(1, n))


def _norm_mod(x, g, sc, sh):
    var = jnp.mean(x * x, axis=-1, keepdims=True)
    y = x * lax.rsqrt(var + NORM_EPS) * g
    return y * (1.0 + sc) + sh


def _ones_column(shape):
    lane = lax.broadcasted_iota(jnp.int32, shape, 1)
    return jnp.where(lane == 0, 1.0, 0.0).astype(BF16)


def _rope_tile(x, cos, sin_a, sin_b):
    up = pltpu.roll(x, LANES - 16, axis=1)
    dn = pltpu.roll(x, 16, axis=1)
    return x * cos + up * sin_a + dn * sin_b


def _rope_tables(seq, rot_dim):
    n_rows = seq // GRID_W
    rows = jnp.repeat(jnp.arange(n_rows, dtype=F32), GRID_W)
    cols = jnp.tile(jnp.arange(GRID_W, dtype=F32), n_rows)
    axis_dim = rot_dim // 2
    inv_freq = ROPE_BASE ** (-jnp.arange(0, axis_dim, 2, dtype=F32) / axis_dim)
    ang_r = rows[:, None] * inv_freq
    ang_c = cols[:, None] * inv_freq
    ang = jnp.concatenate([ang_r, ang_r, ang_c, ang_c], axis=-1)
    cos, sin = jnp.cos(ang), jnp.sin(ang)
    quarter = rot_dim // 4
    first = (jnp.arange(rot_dim) % (2 * quarter)) < quarter
    sin_a = jnp.where(first, -sin, 0.0)
    sin_b = jnp.where(first, 0.0, sin)
    return cos, sin_a, sin_b


def _rope_tables_128(seq, ctx, rot_dim, tile_groups):
    cos, sin_a, sin_b = _rope_tables(seq, rot_dim)
    if tile_groups:
        reps = LANES // rot_dim
        cos, sin_a, sin_b = (jnp.tile(t, (1, reps)) for t in (cos, sin_a, sin_b))
    else:
        pad = LANES - rot_dim
        cos = jnp.concatenate([cos, jnp.ones((seq, pad), F32)], axis=-1)
        sin_a = jnp.concatenate([sin_a, jnp.zeros((seq, pad), F32)], axis=-1)
        sin_b = jnp.concatenate([sin_b, jnp.zeros((seq, pad), F32)], axis=-1)
    cos = jnp.concatenate([cos, jnp.ones((ctx, LANES), F32)], axis=0)
    sin_a = jnp.concatenate([sin_a, jnp.zeros((ctx, LANES), F32)], axis=0)
    sin_b = jnp.concatenate([sin_b, jnp.zeros((ctx, LANES), F32)], axis=0)
    return cos, sin_a, sin_b


def _mod_index(tpb, nb):
    def idx(i):
        return jnp.where(i % tpb == tpb - 1, nb, i // tpb)
    return idx


def _even_proj_kernel(x_ref, g_ref, sc_ref, sh_ref, wg_ref, wr_ref, wq_ref, wkv_ref, wkr_ref,
                      og_ref, or_ref, oq_ref, okv_ref, okr_ref):
    h = _norm_mod(x_ref[...], g_ref[...], sc_ref[...], sh_ref[...]).astype(BF16)
    for w_ref, o_ref in ((wg_ref, og_ref), (wr_ref, or_ref), (wq_ref, oq_ref),
                         (wkv_ref, okv_ref), (wkr_ref, okr_ref)):
        o_ref[...] = jnp.dot(h, w_ref[...], preferred_element_type=F32)


def _even_proj(x, gain, sc, sh, weights, tpb, nb):
    t, d = x.shape
    midx = _mod_index(tpb, nb)
    w_specs = [pl.BlockSpec(w.shape, lambda i: (0, 0)) for w in weights]
    return pl.pallas_call(
        _even_proj_kernel,
        grid=(t // TM,),
        in_specs=[pl.BlockSpec((TM, d), lambda i: (i, 0)),
                  pl.BlockSpec((1, d), lambda i: (0, 0)),
                  pl.BlockSpec((None, 1, d), lambda i: (midx(i), 0, 0)),
                  pl.BlockSpec((None, 1, d), lambda i: (midx(i), 0, 0))] + w_specs,
        out_specs=[pl.BlockSpec((TM, w.shape[1]), lambda i: (i, 0)) for w in weights],
        out_shape=[jax.ShapeDtypeStruct((t, w.shape[1]), F32) for w in weights],
        compiler_params=_cparams(("parallel",)),
        name="even_in_proj",
    )(x, gain, sc, sh, *weights)


def _mla_prep_kernel(qc_ref, kvc_ref, kr_ref, qng_ref, kvng_ref, wqn_ref, wqr_ref, wkv_ref,
                     qgn_ref, qgr_ref, kgn_ref, kgr_ref, cos_ref, sa_ref, sb_ref,
                     q_ref, k_ref, v_ref):
    cos, sa, sb = cos_ref[...], sa_ref[...], sb_ref[...]
    inv_qk = 1.0 / MLA_QK

    qc = qc_ref[...]
    hq = (qc * lax.rsqrt(jnp.mean(qc * qc, axis=-1, keepdims=True) + NORM_EPS) * qng_ref[...]).astype(BF16)
    q_nope = jnp.dot(hq, wqn_ref[...], preferred_element_type=F32)
    q_rope = jnp.dot(hq, wqr_ref[...], preferred_element_type=F32)
    q_scale = MLA_QK ** -0.5 * LOG2_E
    for h in range(MLA_HEADS):
        qn = q_nope[:, h * LANES:(h + 1) * LANES]
        qr = q_rope[:, h * LANES:(h + 1) * LANES]
        ms = (jnp.sum(qn * qn, axis=-1, keepdims=True) + jnp.sum(qr * qr, axis=-1, keepdims=True)) * inv_qk
        rs = lax.rsqrt(ms + NORM_EPS) * q_scale
        q_ref[:, (2 * h) * LANES:(2 * h + 1) * LANES] = (qn * rs * qgn_ref[...]).astype(BF16)
        q_ref[:, (2 * h + 1) * LANES:(2 * h + 2) * LANES] = _rope_tile(qr * rs * qgr_ref[...], cos, sa, sb).astype(BF16)

    kvc = kvc_ref[...]
    hkv = (kvc * lax.rsqrt(jnp.mean(kvc * kvc, axis=-1, keepdims=True) + NORM_EPS) * kvng_ref[...]).astype(BF16)
    kv = jnp.dot(hkv, wkv_ref[...], preferred_element_type=F32)
    kr = kr_ref[...]
    kr_ss = jnp.sum(kr * kr, axis=-1, keepdims=True)
    kr_rot = _rope_tile(kr * kgr_ref[...], cos, sa, sb)
    for h in range(MLA_HEADS):
        kn = kv[:, (2 * h) * LANES:(2 * h + 1) * LANES]
        ms = (jnp.sum(kn * kn, axis=-1, keepdims=True) + kr_ss) * inv_qk
        rs = lax.rsqrt(ms + NORM_EPS)
        k_ref[:, (2 * h) * LANES:(2 * h + 1) * LANES] = (kn * rs * kgn_ref[...]).astype(BF16)
        k_ref[:, (2 * h + 1) * LANES:(2 * h + 2) * LANES] = (kr_rot * rs).astype(BF16)
        v_ref[:, (2 * h) * LANES:(2 * h + 1) * LANES] = kv[:, (2 * h + 1) * LANES:(2 * h + 2) * LANES].astype(BF16)
        v_ref[:, (2 * h + 1) * LANES:(2 * h + 2) * LANES] = _ones_column((kv.shape[0], LANES))


def _mla_prep(qc, kvc, kr, q_norm_g, kv_norm_g, w_q_nope, w_q_rope, w_ukv, q_gn, q_gr, k_gn, k_gr,
              rope, tpb):
    t = qc.shape[0]
    cos, sa, sb = rope

    def const(a):
        return pl.BlockSpec(a.shape, lambda i: (0, 0))

    def rows(a):
        return pl.BlockSpec((TM, a.shape[1]), lambda i: (i, 0))

    tab = pl.BlockSpec((TM, LANES), lambda i: (i % tpb, 0))
    hq, hv = MLA_HEADS * 2 * LANES, MLA_HEADS * 2 * MLA_V
    return pl.pallas_call(
        _mla_prep_kernel,
        grid=(t // TM,),
        in_specs=[rows(qc), rows(kvc), rows(kr), const(q_norm_g), const(kv_norm_g),
                  const(w_q_nope), const(w_q_rope), const(w_ukv),
                  const(q_gn), const(q_gr), const(k_gn), const(k_gr), tab, tab, tab],
        out_specs=[pl.BlockSpec((TM, hq), lambda i: (i, 0)),
                   pl.BlockSpec((TM, hq), lambda i: (i, 0)),
                   pl.BlockSpec((TM, hv), lambda i: (i, 0))],
        out_shape=[jax.ShapeDtypeStruct((t, hq), BF16),
                   jax.ShapeDtypeStruct((t, hq), BF16),
                   jax.ShapeDtypeStruct((t, hv), BF16)],
        compiler_params=_cparams(("parallel",)),
        name="mla_prep",
    )(qc, kvc, kr, q_norm_g, kv_norm_g, w_q_nope, w_q_rope, w_ukv, q_gn, q_gr, k_gn, k_gr, cos, sa, sb)


def _sublane_iota():
    return lax.broadcasted_iota(jnp.int32, (SUBLANES, LANES), 0)


def _scan_chunk(a_chunk, u_chunk, h_chunk, carry, reverse):
    steps = range(LRU_SEG - 1, -1, -1) if reverse else range(LRU_SEG)
    h_loc, p_loc = [None] * LRU_SEG, [None] * LRU_SEG
    h = p = None
    for j in steps:
        a = a_chunk[pl.ds(j, SUBLANES, stride=LRU_SEG), :]
        u = u_chunk[pl.ds(j, SUBLANES, stride=LRU_SEG), :]
        if h is None:
            h, p = u, a
        else:
            h, p = a * h + u, a * p
        h_loc[j], p_loc[j] = h, p
    sub = _sublane_iota()
    seg_p, seg_h = p, h
    for d in (1, 2, 4):
        shift = SUBLANES - d if reverse else d
        prev_p = pltpu.roll(seg_p, shift, axis=0)
        prev_h = pltpu.roll(seg_h, shift, axis=0)
        valid = (sub < SUBLANES - d) if reverse else (sub >= d)
        seg_h = jnp.where(valid, seg_p * prev_h + seg_h, seg_h)
        seg_p = jnp.where(valid, seg_p * prev_p, seg_p)
    h_end = seg_h + seg_p * carry
    if reverse:
        h_in = jnp.where(sub == SUBLANES - 1, carry, pltpu.roll(h_end, SUBLANES - 1, axis=0))
        new_carry = h_end[0:1, :]
    else:
        h_in = jnp.where(sub == 0, carry, pltpu.roll(h_end, 1, axis=0))
        new_carry = h_end[SUBLANES - 1:SUBLANES, :]
    for j in range(LRU_SEG):
        h_chunk[pl.ds(j, SUBLANES, stride=LRU_SEG), :] = h_loc[j] + p_loc[j] * h_in
    return jnp.broadcast_to(new_carry, (SUBLANES, LANES))


def _lru_kernel(g_ref, r_ref, cw_ref, cb_ref, wg_ref, bg_ref, lam_ref, o_ref,
                rp_ref, af_ref, uf_ref, ab_ref, ub_ref, hf_ref, hb_ref, *, ctx, seq):
    n_ctx, n_lat = ctx // LRU_CHUNK, seq // LRU_CHUNK
    zeros_pad = jnp.zeros((LRU_PAD, LRU_CT), F32)
    ctx0 = seq + 2 * LRU_PAD
    rp_ref[0:LRU_PAD, :] = zeros_pad
    rp_ref[LRU_PAD:LRU_PAD + seq, :] = r_ref[0:seq, :]
    rp_ref[LRU_PAD + seq:ctx0, :] = zeros_pad
    rp_ref[ctx0:ctx0 + ctx, :] = r_ref[seq:seq + ctx, :]
    rp_ref[ctx0 + ctx:ctx0 + ctx + LRU_PAD, :] = zeros_pad

    cw = cw_ref[...]
    cb = cb_ref[...]
    wg = wg_ref[...]
    bg = bg_ref[...]
    lam = lam_ref[...]
    sp = jnp.maximum(-lam, 0.0) + jnp.log1p(jnp.exp(-jnp.abs(lam)))

    def coeff_chunk(c, pad_off, row_off):
        start = pl.multiple_of(pad_off + c * LRU_CHUNK, SUBLANES)
        ext = rp_ref[pl.ds(start, LRU_CHUNK + 2 * LRU_PAD), :]
        x = cb
        for tap in range(CONV_W):
            lo = LRU_PAD - 2 + tap
            x = x + ext[lo:lo + LRU_CHUNK, :] * cw[tap:tap + 1, :]
        gates = jnp.dot(x.astype(BF16), wg, preferred_element_type=F32) + bg
        out_row = pl.multiple_of(row_off + c * LRU_CHUNK, SUBLANES)
        for d, (a_ref, u_ref) in enumerate(((af_ref, uf_ref), (ab_ref, ub_ref))):
            r = jax.nn.sigmoid(gates[:, (2 * d) * LRU_CT:(2 * d + 1) * LRU_CT])
            i = jax.nn.sigmoid(gates[:, (2 * d + 1) * LRU_CT:(2 * d + 2) * LRU_CT])
            log_a = -LRU_C * r * sp[d:d + 1, :]
            a_ref[pl.ds(out_row, LRU_CHUNK), :] = jnp.exp(log_a)
            th = jnp.tanh(log_a)
            u_ref[pl.ds(out_row, LRU_CHUNK), :] = jnp.sqrt(-2.0 * th / (1.0 - th)) * i * x
        return None

    def coeff_ctx(c, _):
        coeff_chunk(c, seq + LRU_PAD, seq)
        return 0

    def coeff_lat(c, _):
        coeff_chunk(c, 0, 0)
        return 0

    lax.fori_loop(0, n_ctx, coeff_ctx, 0)
    lax.fori_loop(0, n_lat, coeff_lat, 0)

    def scan_pair(n, row_off):
        def body(c, carry):
            cf, cb_ = carry
            f_row = pl.multiple_of(row_off + c * LRU_CHUNK, SUBLANES)
            b_row = pl.multiple_of(row_off + (n - 1 - c) * LRU_CHUNK, SUBLANES)
            cf = _scan_chunk(af_ref.at[pl.ds(f_row, LRU_CHUNK), :], uf_ref.at[pl.ds(f_row, LRU_CHUNK), :],
                             hf_ref.at[pl.ds(f_row, LRU_CHUNK), :], cf, False)
            cb_ = _scan_chunk(ab_ref.at[pl.ds(b_row, LRU_CHUNK), :], ub_ref.at[pl.ds(b_row, LRU_CHUNK), :],
                              hb_ref.at[pl.ds(b_row, LRU_CHUNK), :], cb_, True)
            return cf, cb_
        return body

    zero = jnp.zeros((SUBLANES, LANES), F32)
    carry = lax.fori_loop(0, n_ctx, scan_pair(n_ctx, seq), (zero, zero))
    lax.fori_loop(0, n_lat, scan_pair(n_lat, 0), carry)

    g = g_ref[...]
    gelu = 0.5 * g * (1.0 + jnp.tanh(math.sqrt(2.0 / math.pi) * (g + 0.044715 * (g * g * g))))
    o_ref[...] = (gelu * (hf_ref[...] + hb_ref[...])).astype(BF16)


def _lru(g, r, conv_w, conv_b, w_gates, b_gates, lam, nb, ctx, seq):
    l = ctx + seq
    width = g.shape[1]
    n_ct = width // LRU_CT
    g3 = g.reshape(nb, l, width)
    r3 = r.reshape(nb, l, width)
    seq_spec = pl.BlockSpec((None, l, LRU_CT), lambda b, c: (b, 0, c))
    scratch = [pltpu.VMEM((l + 3 * LRU_PAD, LRU_CT), F32)] + [pltpu.VMEM((l, LRU_CT), F32)] * 6
    out = pl.pallas_call(
        functools.partial(_lru_kernel, ctx=ctx, seq=seq),
        grid=(nb, n_ct),
        in_specs=[seq_spec, seq_spec,
                  pl.BlockSpec((CONV_W, LRU_CT), lambda b, c: (0, c)),
                  pl.BlockSpec((1, LRU_CT), lambda b, c: (0, c)),
                  pl.BlockSpec((None, LRU_CT, 4 * LRU_CT), lambda b, c: (c, 0, 0)),
                  pl.BlockSpec((None, 1, 4 * LRU_CT), lambda b, c: (c, 0, 0)),
                  pl.BlockSpec((2, LRU_CT), lambda b, c: (0, c))],
        out_specs=seq_spec,
        out_shape=jax.ShapeDtypeStruct((nb, l, width), BF16),
        scratch_shapes=scratch,
        compiler_params=_cparams(("parallel", "parallel")),
        name="rglru",
    )(g3, r3, conv_w, conv_b, w_gates, b_gates, lam)
    return out.reshape(nb * l, width)


def _lru_gate_weights(wa, ba, wx, bx):
    per = LRU_CT // LRU_BLOCK_W
    n_ct = LRU_BLOCKS // per
    eye = jnp.eye(per, dtype=F32)

    def dense(w):
        w4 = w.reshape(n_ct, per, LRU_BLOCK_W, LRU_BLOCK_W)
        return jnp.einsum('cide,ij->cidje', w4, eye).reshape(n_ct, LRU_CT, LRU_CT)

    w = jnp.concatenate([dense(wa[0]), dense(wx[0]), dense(wa[1]), dense(wx[1])], axis=-1)
    b = jnp.concatenate([v.reshape(n_ct, 1, LRU_CT) for v in (ba[0], bx[0], ba[1], bx[1])], axis=-1)
    return w.astype(BF16), b


def _key_spans(n_keys):
    half = n_keys // 2 if n_keys % (2 * LANES) == 0 else n_keys
    return [(lo, lo + half) for lo in range(0, n_keys, half)]


def _attn_scores(q_ops, kt_ref, s_ref, m_ref):
    for t, q in enumerate(q_ops):
        for lo, hi in _key_spans(kt_ref.shape[-1]):
            s_ref[t, :, lo:hi] = jnp.dot(q, kt_ref[:, lo:hi], preferred_element_type=F32)
        m_ref[t] = jnp.max(s_ref[t], axis=-1, keepdims=True)


def _attn_values(v_ref, s_ref, m_ref, p_ref):
    accs = []
    for t in range(s_ref.shape[0]):
        m = m_ref[t]
        acc = None
        for lo, hi in _key_spans(s_ref.shape[-1]):
            p_ref[t, :, lo:hi] = jnp.exp2(s_ref[t, :, lo:hi] - m).astype(BF16)
            part = jnp.dot(p_ref[t, :, lo:hi], v_ref[lo:hi, :], preferred_element_type=F32)
            acc = part if acc is None else acc + part
        accs.append(acc)
    return accs


def _attn_pipeline(q_ref, kt_ref, v_ref, o_ref, bufs, p_ref, q_ops_fn, finish_fn):
    s0, m0 = bufs[0]
    tq = s0.shape[1]
    n = q_ref.shape[0] // tq

    def scores(t, s_ref, m_ref):
        row = t * tq if isinstance(t, int) else pl.multiple_of(t * tq, tq)
        _attn_scores(q_ops_fn(q_ref[pl.ds(row, tq), :]), kt_ref, s_ref, m_ref)

    def values(t, s_ref, m_ref):
        row = t * tq if isinstance(t, int) else pl.multiple_of(t * tq, tq)
        o_ref[pl.ds(row, tq), :] = finish_fn(_attn_values(v_ref, s_ref, m_ref, p_ref)).astype(o_ref.dtype)

    if len(bufs) == 1:
        def one(t, _):
            scores(t, s0, m0)
            values(t, s0, m0)
            return 0

        if n == 1:
            one(0, 0)
        else:
            lax.fori_loop(0, n, one, 0)
        return

    s1, m1 = bufs[1]
    scores(0, s0, m0)
    if n == 1:
        values(0, s0, m0)
        return
    assert n % 2 == 0

    def pair(k, _):
        t = 2 * k
        scores(t + 1, s1, m1)
        values(t, s0, m0)
        scores(t + 2, s0, m0)
        values(t + 1, s1, m1)
        return 0

    lax.fori_loop(0, n // 2 - 1, pair, 0)
    scores(n - 1, s1, m1)
    values(n - 2, s0, m0)
    values(n - 1, s1, m1)


def _score_bufs(scratch):
    return tuple(zip(scratch[0:-1:2], scratch[1:-1:2])), scratch[-1]


def _mla_attn_kernel(q_ref, kt_ref, v_ref, o_ref, *scratch):
    dv = o_ref.shape[-1]
    bufs, p_ref = _score_bufs(scratch)

    def finish(accs):
        return accs[0][:, :dv] / accs[0][:, dv:dv + 1]

    _attn_pipeline(q_ref, kt_ref, v_ref, o_ref, bufs, p_ref, lambda q: [q], finish)


def _diff_attn_kernel(q_ref, kt_ref, v_ref, g_ref, lam_ref, o_ref, *scratch, out_scale):
    dv = o_ref.shape[-1]
    bufs, p_ref = _score_bufs(scratch)

    def q_ops(q):
        lane = lax.broadcasted_iota(jnp.int32, q.shape, 1)
        zero = jnp.zeros_like(q)
        return [jnp.where(lane < DIFF_HEAD_DIM, q, zero), jnp.where(lane < DIFF_HEAD_DIM, zero, q)]

    def finish(accs):
        a1, a2 = accs
        o = a1[:, :dv] / a1[:, dv:dv + 1] - lam_ref[0] * (a2[:, :dv] / a2[:, dv:dv + 1])
        y = o * lax.rsqrt(jnp.mean(o * o, axis=-1, keepdims=True) + NORM_EPS) * g_ref[...]
        return y * out_scale

    _attn_pipeline(q_ref, kt_ref, v_ref, o_ref, bufs, p_ref, q_ops, finish)


def _attention(kernel_fn, name, n_softmax, q, kt, v, nb, ctx, seq, heads, dk, dv, ctx_queries,
               extra_in=(), extra_specs=()):
    l = ctx + seq
    q3 = q.reshape(nb, l, heads * dk)
    v3 = v.reshape(nb, l, heads * 2 * dv)
    if ctx_queries:
        tq, n_keys, rows = ctx, ctx, ctx
        blk = seq // ctx
        in_specs = [pl.BlockSpec((None, rows, dk), lambda b, h: (b, blk, h)),
                    pl.BlockSpec((None, None, dk, ctx), lambda b, h: (b, h, 0, blk)),
                    pl.BlockSpec((None, ctx, 2 * dv), lambda b, h: (b, blk, h))]
    else:
        tq, n_keys, rows = TQ, l, seq
        in_specs = [pl.BlockSpec((None, rows, dk), lambda b, h: (b, 0, h)),
                    pl.BlockSpec((None, None, dk, l), lambda b, h: (b, h, 0, 0)),
                    pl.BlockSpec((None, l, 2 * dv), lambda b, h: (b, 0, h))]
    score_buf = [pltpu.VMEM((n_softmax, tq, n_keys), F32), pltpu.VMEM((n_softmax, tq, 1), F32)]
    out = pl.pallas_call(
        kernel_fn,
        grid=(nb, heads),
        in_specs=in_specs + list(extra_specs),
        out_specs=pl.BlockSpec((None, rows, dv), lambda b, h: (b, 0, h)),
        out_shape=jax.ShapeDtypeStruct((nb, rows, heads * dv), BF16),
        scratch_shapes=score_buf * (2 if n_softmax == 1 else 1) + [pltpu.VMEM((n_softmax, tq, n_keys), BF16)],
        compiler_params=_cparams(("parallel", "parallel")),
        name=name,
    )(q3, kt, v3, *extra_in)
    return out


def _key_transpose(k, nb, l, heads, dk):
    return jnp.transpose(k.reshape(nb, l, heads, dk), (0, 2, 3, 1))


def _attend_all(kernel_fn, name, n_softmax, q, k, v, nb, ctx, seq, heads, dk, dv, need_ctx, **extra):
    kt = _key_transpose(k, nb, ctx + seq, heads, dk)
    lat = _attention(kernel_fn, name, n_softmax, q, kt, v, nb, ctx, seq, heads, dk, dv, False, **extra)
    if not need_ctx:
        return lat.reshape(nb * seq, heads * dv)
    cx = _attention(kernel_fn, name + "_ctx", n_softmax, q, kt, v, nb, ctx, seq, heads, dk, dv, True, **extra)
    return jnp.concatenate([lat, cx], axis=1).reshape(nb * (ctx + seq), heads * dv)


def _group_ms(x, gmat):
    return jnp.dot((x * x).astype(BF16), gmat, preferred_element_type=F32)


def _odd_proj_kernel(x_ref, g_ref, sc_ref, sh_ref, wq_ref, wk_ref, wv_ref, qg_ref, kg_ref, gm_ref,
                     cos_ref, sa_ref, sb_ref, q_ref, k_ref, v_ref):
    h = _norm_mod(x_ref[...], g_ref[...], sc_ref[...], sh_ref[...]).astype(BF16)
    cos, sa, sb = cos_ref[...], sa_ref[...], sb_ref[...]
    gmat = gm_ref[...]
    q_scale = DIFF_HEAD_DIM ** -0.5 * LOG2_E
    for w_ref, gain_ref, o_ref, scale in ((wq_ref, qg_ref, q_ref, q_scale), (wk_ref, kg_ref, k_ref, 1.0)):
        raw = jnp.dot(h, w_ref[...], preferred_element_type=F32)
        gain = gain_ref[...] * scale
        for c in range(raw.shape[1] // LANES):
            t = raw[:, c * LANES:(c + 1) * LANES]
            t = t * lax.rsqrt(_group_ms(t, gmat) + NORM_EPS) * gain
            o_ref[:, c * LANES:(c + 1) * LANES] = _rope_tile(t, cos, sa, sb).astype(BF16)
    v = jnp.dot(h, wv_ref[...], preferred_element_type=F32).astype(BF16)
    for c in range(v.shape[1] // DIFF_V):
        v_ref[:, (2 * c) * DIFF_V:(2 * c + 1) * DIFF_V] = v[:, c * DIFF_V:(c + 1) * DIFF_V]
        v_ref[:, (2 * c + 1) * DIFF_V:(2 * c + 2) * DIFF_V] = _ones_column((v.shape[0], DIFF_V))


def _odd_proj(x, gain, sc, sh, wq, wk, wv, q_gain, k_gain, gmat, rope, tpb, nb):
    t, d = x.shape
    midx = _mod_index(tpb, nb)
    cos, sa, sb = rope

    def const(a):
        return pl.BlockSpec(a.shape, lambda i: (0, 0))

    tab = pl.BlockSpec((TM, LANES), lambda i: (i % tpb, 0))
    n = wq.shape[1]
    return pl.pallas_call(
        _odd_proj_kernel,
        grid=(t // TM,),
        in_specs=[pl.BlockSpec((TM, d), lambda i: (i, 0)), const(gain),
                  pl.BlockSpec((None, 1, d), lambda i: (midx(i), 0, 0)),
                  pl.BlockSpec((None, 1, d), lambda i: (midx(i), 0, 0)),
                  const(wq), const(wk), const(wv), const(q_gain), const(k_gain), const(gmat),
                  tab, tab, tab],
        out_specs=[pl.BlockSpec((TM, n), lambda i: (i, 0))] * 2 + [pl.BlockSpec((TM, 2 * n), lambda i: (i, 0))],
        out_shape=[jax.ShapeDtypeStruct((t, n), BF16)] * 2 + [jax.ShapeDtypeStruct((t, 2 * n), BF16)],
        compiler_params=_cparams(("parallel",)),
        name="odd_in_proj",
    )(x, gain, sc, sh, wq, wk, wv, q_gain, k_gain, gmat, cos, sa, sb)


def _post_mix_kernel(*refs, n_mix):
    mix_refs = refs[:n_mix]
    w_refs = refs[n_mix:2 * n_mix]
    (x_ref, g1_ref, n2_ref, sc_ref, sh_ref, rw_ref, rb_ref,
     xo_ref, h2_ref, te_ref, tw_ref, tr_ref, cnt_ref, base_ref) = refs[2 * n_mix:]
    i = pl.program_id(0)

    @pl.when(i == 0)
    def _():
        base_ref[...] = jnp.zeros_like(base_ref)

    m = jnp.dot(mix_refs[0][...], w_refs[0][...], preferred_element_type=F32)
    for a_ref, w_ref in zip(mix_refs[1:], w_refs[1:]):
        m = m + jnp.dot(a_ref[...], w_ref[...], preferred_element_type=F32)
    x = x_ref[...] + g1_ref[...] * m
    xo_ref[...] = x
    h2 = _norm_mod(x, n2_ref[...], sc_ref[...], sh_ref[...])
    h2_ref[...] = h2.astype(BF16)

    logits = jnp.dot(h2, rw_ref[...], preferred_element_type=F32, precision=HIGHEST) + rb_ref[...]
    lane = lax.broadcasted_iota(jnp.int32, logits.shape, 1)
    lane_f = lane.astype(F32)
    vals = jnp.where(lane < N_EXPERTS, logits, NEG_BIG)
    tops, hots = [], []
    for _ in range(TOP_K):
        top = jnp.max(vals, axis=-1, keepdims=True)
        first = jnp.min(jnp.where(vals == top, lane_f, float(LANES)), axis=-1, keepdims=True)
        hot = lane_f == first
        tops.append(top)
        hots.append(hot)
        vals = jnp.where(hot, 2.0 * NEG_BIG, vals)
    exps = [jnp.exp(t - tops[0]) for t in tops]
    denom = exps[0] + exps[1] + exps[2] + exps[3]

    picked = (hots[0] | hots[1]) | (hots[2] | hots[3])
    cnt = jnp.where(picked, 1.0, 0.0)
    r_io = lax.broadcasted_iota(jnp.int32, (TM, TM), 0)
    c_io = lax.broadcasted_iota(jnp.int32, (TM, TM), 1)
    tri = jnp.where(c_io < r_io, 1.0, 0.0).astype(BF16)
    before = jnp.dot(tri, cnt.astype(BF16), preferred_element_type=F32) + base_ref[...]
    te = jnp.zeros(logits.shape, F32)
    tw = jnp.zeros(logits.shape, F32)
    tr = jnp.zeros(logits.shape, F32)
    for k in range(TOP_K):
        e_k = jnp.sum(jnp.where(hots[k], lane_f, 0.0), axis=-1, keepdims=True)
        r_k = jnp.sum(jnp.where(hots[k], before, 0.0), axis=-1, keepdims=True)
        te = jnp.where(lane == k, e_k, te)
        tw = jnp.where(lane == k, exps[k] / denom, tw)
        tr = jnp.where(lane == k, r_k, tr)
    te_ref[...] = te.astype(jnp.int32)
    tw_ref[...] = tw
    tr_ref[...] = tr.astype(jnp.int32)
    base_ref[...] += jnp.sum(cnt, axis=0, keepdims=True)
    cnt_ref[...] = base_ref[...]


def _post_mix(mixes, weights, x, g1, n2g, sc2, sh2, router_w, router_b, tpb, nb, latent_only):
    d = x.shape[1]
    tiles_in = tpb
    if latent_only:
        tiles_out = tpb - 1
        n_tiles = nb * tiles_out

        def in_row(i):
            return (i // tiles_out) * tiles_in + i % tiles_out

        def midx(i):
            return i // tiles_out
    else:
        n_tiles = nb * tpb

        def in_row(i):
            return i

        midx = _mod_index(tpb, nb)
    t_out = n_tiles * TM

    def const(a):
        return pl.BlockSpec(a.shape, lambda i: (0, 0))

    def mod(a):
        return pl.BlockSpec((None, 1, d), lambda i: (midx(i), 0, 0))

    in_specs = [pl.BlockSpec((TM, a.shape[1]), lambda i: (i, 0)) for a in mixes]
    in_specs += [const(w) for w in weights]
    in_specs += [pl.BlockSpec((TM, d), lambda i: (in_row(i), 0)), mod(g1), const(n2g), mod(sc2), mod(sh2),
                 const(router_w), const(router_b)]
    row_out = lambda width: pl.BlockSpec((TM, width), lambda i: (i, 0))
    return pl.pallas_call(
        functools.partial(_post_mix_kernel, n_mix=len(mixes)),
        grid=(n_tiles,),
        in_specs=in_specs,
        out_specs=[row_out(d), row_out(d), row_out(LANES), row_out(LANES), row_out(LANES),
                   pl.BlockSpec((1, LANES), lambda i: (0, 0))],
        out_shape=[jax.ShapeDtypeStruct((t_out, d), F32), jax.ShapeDtypeStruct((t_out, d), BF16),
                   jax.ShapeDtypeStruct((t_out, LANES), jnp.int32), jax.ShapeDtypeStruct((t_out, LANES), F32),
                   jax.ShapeDtypeStruct((t_out, LANES), jnp.int32), jax.ShapeDtypeStruct((1, LANES), F32)],
        scratch_shapes=[pltpu.VMEM((1, LANES), F32)],
        compiler_params=_cparams(("arbitrary",)),
        name="post_mix_router",
    )(*mixes, *weights, x, g1, n2g, sc2, sh2, router_w, router_b)


def _moe_ffn_kernel(be_ref, nu_ref, x_ref, wgu_ref, bgu_ref, wd_ref, bd_ref, o_ref, wgu_bf, wd_bf):
    i = pl.program_id(0)
    prev = be_ref[jnp.maximum(i - 1, 0)]
    fresh = jnp.logical_or(i == 0, be_ref[i] != prev)

    @pl.when(jnp.logical_and(fresh, i < nu_ref[0]))
    def _():
        wgu_bf[...] = wgu_ref[...].astype(BF16)
        wd_bf[...] = wd_ref[...].astype(BF16)

    @pl.when(i < nu_ref[0])
    def _():
        d_ff = wd_bf.shape[0]
        gu = jnp.dot(x_ref[...], wgu_bf[...], preferred_element_type=F32) + bgu_ref[...]
        g = jnp.minimum(gu[:, :d_ff], SWIGLU_LIMIT)
        u = jnp.clip(gu[:, d_ff:], -SWIGLU_LIMIT, SWIGLU_LIMIT)
        act = (u + 1.0) * g * jax.nn.sigmoid(SWIGLU_ALPHA * g)
        y = jnp.dot(act.astype(BF16), wd_bf[...], preferred_element_type=F32) + bd_ref[...]
        o_ref[...] = y.astype(o_ref.dtype)

    @pl.when(i >= nu_ref[0])
    def _():
        o_ref[...] = jnp.zeros_like(o_ref)


def _moe_ffn(block_e, n_used, xs, w_gu, b_gu, w_down, b_down, layer):
    n_rows, d = xs.shape
    depth, n_e, _, two_ff = w_gu.shape
    d_ff = two_ff // 2
    n_blocks = n_rows // MOE_BM
    grid_spec = pltpu.PrefetchScalarGridSpec(
        num_scalar_prefetch=2,
        grid=(n_blocks,),
        in_specs=[pl.BlockSpec((MOE_BM, d), lambda i, be, nu: (i, 0)),
                  pl.BlockSpec((None, None, d, two_ff), lambda i, be, nu: (layer, be[i], 0, 0)),
                  pl.BlockSpec((None, None, 1, two_ff), lambda i, be, nu: (layer, be[i], 0, 0)),
                  pl.BlockSpec((None, None, d_ff, d), lambda i, be, nu: (layer, be[i], 0, 0)),
                  pl.BlockSpec((None, None, 1, d), lambda i, be, nu: (layer, be[i], 0, 0))],
        out_specs=pl.BlockSpec((MOE_BM, d), lambda i, be, nu: (i, 0)),
        scratch_shapes=[pltpu.VMEM((d, two_ff), BF16), pltpu.VMEM((d_ff, d), BF16)],
    )
    return pl.pallas_call(
        _moe_ffn_kernel,
        grid_spec=grid_spec,
        out_shape=jax.ShapeDtypeStruct((n_rows, d), BF16),
        compiler_params=_cparams(("arbitrary",)),
        name="moe_ffn",
    )(block_e, n_used, xs, w_gu, b_gu.reshape(depth, n_e, 1, two_ff), w_down, b_down.reshape(depth, n_e, 1, d))


def _moe_combine_kernel(x_ref, g2_ref, w_ref, y_ref, o_ref):
    w = w_ref[...]
    f = w[:, 0:1] * y_ref[0].astype(F32)
    for k in range(1, TOP_K):
        f = f + w[:, k:k + 1] * y_ref[k].astype(F32)
    o_ref[...] = x_ref[...] + g2_ref[...] * f


def _moe_combine(x, g2, top_w, picked, midx):
    t, d = x.shape
    return pl.pallas_call(
        _moe_combine_kernel,
        grid=(t // TM,),
        in_specs=[pl.BlockSpec((TM, d), lambda i: (i, 0)),
                  pl.BlockSpec((None, 1, d), lambda i: (midx(i), 0, 0)),
                  pl.BlockSpec((TM, LANES), lambda i: (i, 0)),
                  pl.BlockSpec((TOP_K, TM, d), lambda i: (0, i, 0))],
        out_specs=pl.BlockSpec((TM, d), lambda i: (i, 0)),
        out_shape=jax.ShapeDtypeStruct((t, d), F32),
        compiler_params=_cparams(("parallel",)),
        name="moe_combine",
    )(x, g2, top_w, picked)


def _moe(x, g2, midx, h2, top_e, top_w, top_r, counts, w_gu, b_gu, w_down, b_down, layer):
    t, d = h2.shape
    n_assign = t * TOP_K
    n_blocks = -(-n_assign // MOE_BM) + N_EXPERTS
    n_rows = n_blocks * MOE_BM
    cnt = counts[0, :N_EXPERTS].astype(jnp.int32)
    padded = (cnt + MOE_BM - 1) // MOE_BM * MOE_BM
    pends = jnp.cumsum(padded)
    pstarts = pends - padded
    e = top_e[:, :TOP_K]
    expert_ids = jnp.arange(N_EXPERTS, dtype=jnp.int32)
    pstart_of_pick = jnp.sum(jnp.where(e[:, :, None] == expert_ids, pstarts, 0), axis=-1)
    dest = pstart_of_pick + top_r[:, :TOP_K]
    block_start = jnp.arange(n_blocks, dtype=jnp.int32) * MOE_BM
    block_e = jnp.sum((block_start[:, None] >= pends[None, :]).astype(jnp.int32), axis=1)
    block_e = jnp.minimum(block_e, N_EXPERTS - 1)
    n_used = (pends[-1] // MOE_BM).astype(jnp.int32).reshape(1)

    tok = jnp.broadcast_to(jnp.arange(t, dtype=jnp.int32)[:, None], (t, TOP_K))
    row_tok = jnp.zeros((n_rows,), jnp.int32).at[dest.reshape(-1)].set(
        tok.reshape(-1), unique_indices=True, mode='promise_in_bounds')
    xs = h2[row_tok]
    ys = _moe_ffn(block_e, n_used, xs, w_gu, b_gu, w_down, b_down, layer)
    picked = ys[dest.T.reshape(-1)].reshape(TOP_K, t, d)
    return _moe_combine(x, g2, top_w, picked, midx)


def _pad_cols(w, width):
    return jnp.concatenate([w, jnp.zeros((w.shape[0], width - w.shape[1]), w.dtype)], axis=1)


def kernel(x, c, ctx, c_ctx, mod_w, mod_b, norm1_g, norm2_g, ev_w_in, ev_w_out, lru_conv_w, lru_conv_b, lru_wa, lru_ba, lru_wx, lru_bx, lru_lambda, mla_q_norm_g, mla_w_uq, mla_kv_norm_g, mla_w_ukv, mla_qn_g, mla_kn_g, od_w_in, od_w_out, diff_qn_g, diff_kn_g, diff_lq1, diff_lk1, diff_lq2, diff_lk2, diff_subln_g, router_w, router_b, moe_w_gu, moe_b_gu, moe_w_down, moe_b_down):
    nb, seq, d = x.shape
    n_ctx = ctx.shape[1]
    depth = mod_w.shape[0]
    l = n_ctx + seq
    tpb = l // TM
    assert n_ctx == TM and seq % TQ == 0 and seq % GRID_W == 0

    xa = jnp.concatenate([x, ctx], axis=1).reshape(nb * l, d)
    cvec = jnp.concatenate([c, c_ctx[None, :], jnp.zeros((SUBLANES - nb - 1, d), F32)], axis=0)
    router_w_p = jnp.concatenate([router_w, jnp.zeros((depth, d, LANES - N_EXPERTS), F32)], axis=-1)
    router_b_p = jnp.concatenate([router_b, jnp.zeros((depth, LANES - N_EXPERTS), F32)], axis=-1)

    for layer in range(depth):
        last = layer == depth - 1
        i = layer // 2
        mod = _adaln(cvec, mod_w[layer], mod_b[layer])[:nb + 1]
        sh1, sc1, g1, sh2, sc2, g2 = (mod[:, k * d:(k + 1) * d].reshape(nb + 1, 1, d) for k in range(6))
        n1g = norm1_g[layer].reshape(1, d)
        n2g = norm2_g[layer].reshape(1, d)

        if layer % 2 == 0:
            w_in = ev_w_in[i].astype(BF16)
            o = 2 * LRU_WIDTH
            splits = [w_in[:, :LRU_WIDTH], w_in[:, LRU_WIDTH:o], w_in[:, o:o + MLA_Q_RANK],
                      w_in[:, o + MLA_Q_RANK:o + MLA_Q_RANK + MLA_KV_RANK],
                      _pad_cols(w_in[:, o + MLA_Q_RANK + MLA_KV_RANK:], LANES)]
            gl, rl, qc, kvc, kr = _even_proj(xa, n1g, sc1, sh1, splits, tpb, nb)

            w_gates, b_gates = _lru_gate_weights(lru_wa[i], lru_ba[i], lru_wx[i], lru_bx[i])
            lru = _lru(gl, rl, lru_conv_w[i], lru_conv_b[i].reshape(1, LRU_WIDTH), w_gates, b_gates,
                       lru_lambda[i], nb, n_ctx, seq)

            w_uq = mla_w_uq[i].astype(BF16).reshape(MLA_Q_RANK, MLA_HEADS, MLA_QK)
            w_q_nope = w_uq[:, :, :MLA_NOPE].reshape(MLA_Q_RANK, MLA_HEADS * MLA_NOPE)
            w_q_rope = jnp.concatenate(
                [w_uq[:, :, MLA_NOPE:], jnp.zeros((MLA_Q_RANK, MLA_HEADS, LANES - MLA_ROPE), BF16)],
                axis=-1).reshape(MLA_Q_RANK, MLA_HEADS * LANES)
            pad_g = lambda g: jnp.concatenate([g, jnp.zeros((LANES - MLA_ROPE,), F32)]).reshape(1, LANES)
            rope = _rope_tables_128(seq, n_ctx, MLA_ROPE, tile_groups=False)
            q, k, v = _mla_prep(qc, kvc, kr, mla_q_norm_g[i].reshape(1, -1), mla_kv_norm_g[i].reshape(1, -1),
                                w_q_nope, w_q_rope, mla_w_ukv[i].astype(BF16),
                                mla_qn_g[i][:MLA_NOPE].reshape(1, LANES), pad_g(mla_qn_g[i][MLA_NOPE:]),
                                mla_kn_g[i][:MLA_NOPE].reshape(1, LANES), pad_g(mla_kn_g[i][MLA_NOPE:]),
                                rope, tpb)
            att = _attend_all(_mla_attn_kernel, "mla_attention", 1, q, k, v, nb, n_ctx, seq, MLA_HEADS,
                              2 * LANES, MLA_V, need_ctx=not last)
            w_out = ev_w_out[i].astype(BF16)
            if last:
                lru = lru.reshape(nb, l, LRU_WIDTH)[:, :seq].reshape(nb * seq, LRU_WIDTH)
            mixes = [lru, att]
            weights = [w_out[:LRU_WIDTH], w_out[LRU_WIDTH:]]
        else:
            lam_init = 0.8 - 0.6 * math.exp(-0.3 * layer)
            w_in = od_w_in[i].astype(BF16)
            n_qk = DIFF_HEADS * 2 * DIFF_HEAD_DIM
            gidx = jnp.arange(LANES) // DIFF_HEAD_DIM
            gmat = jnp.where(gidx[:, None] == gidx[None, :], 1.0 / DIFF_HEAD_DIM, 0.0).astype(BF16)
            rope = _rope_tables_128(seq, n_ctx, DIFF_HEAD_DIM, tile_groups=True)
            tile_g = lambda g: jnp.tile(g, LANES // DIFF_HEAD_DIM).reshape(1, LANES)
            q, k, v = _odd_proj(xa, n1g, sc1, sh1, w_in[:, :n_qk], w_in[:, n_qk:2 * n_qk], w_in[:, 2 * n_qk:],
                                tile_g(diff_qn_g[i]), tile_g(diff_kn_g[i]), gmat, rope, tpb, nb)
            lam = (jnp.exp(jnp.sum(diff_lq1[i] * diff_lk1[i])) - jnp.exp(jnp.sum(diff_lq2[i] * diff_lk2[i]))
                   + lam_init).reshape(1).astype(F32)
            att = _attend_all(
                functools.partial(_diff_attn_kernel, out_scale=1.0 - lam_init), "diff_attention", 2,
                q, k, v, nb, n_ctx, seq, DIFF_HEADS, 2 * DIFF_HEAD_DIM, DIFF_V, need_ctx=not last,
                extra_in=(diff_subln_g[i].reshape(1, DIFF_V), lam),
                extra_specs=(pl.BlockSpec((1, DIFF_V), lambda b, h: (0, 0)),
                             pl.BlockSpec(memory_space=pltpu.SMEM)))
            mixes = [att]
            weights = [od_w_out[i].astype(BF16)]

        xo, h2, te, tw, tr, counts = _post_mix(mixes, weights, xa, g1, n2g, sc2, sh2, router_w_p[layer],
                                               router_b_p[layer].reshape(1, LANES), tpb, nb, last)
        midx = (lambda t: t // (tpb - 1)) if last else _mod_index(tpb, nb)
        xa = _moe(xo, g2, midx, h2, te, tw, tr, counts, moe_w_gu, moe_b_gu, moe_w_down, moe_b_down, layer)
    return xa.reshape(nb, seq, d)
```

```python
import functools
import math

import jax
import jax.numpy as jnp
from jax import lax
from jax.experimental import pallas as pl
from jax.experimental.pallas import tpu as pltpu

F32 = jnp.float32
BF16 = jnp.bfloat16
HIGHEST = lax.Precision.HIGHEST

GRID_W = 64
NORM_EPS = 1e-6
ROPE_BASE = 10000.0
LRU_WIDTH = 512
LRU_BLOCKS = 8
LRU_BLOCK_W = LRU_WIDTH // LRU_BLOCKS
LRU_C = 8.0
CONV_W = 4
MLA_HEADS = 4
MLA_Q_RANK = 384
MLA_KV_RANK = 256
MLA_NOPE = 128
MLA_ROPE = 64
MLA_V = 128
MLA_QK = MLA_NOPE + MLA_ROPE
DIFF_HEADS = 8
DIFF_HEAD_DIM = 64
DIFF_V = 2 * DIFF_HEAD_DIM
N_EXPERTS = 32
TOP_K = 4
SWIGLU_LIMIT = 7.0
SWIGLU_ALPHA = 1.702

LANES = 128
SUBLANES = 8
VMEM_LIMIT = 52 * 1024 * 1024

TM = 256
TQ = 512
LRU_CT = 128
LRU_CHUNK = 128
LRU_SEG = LRU_CHUNK // SUBLANES
LRU_PAD = 8
MOE_BM = 256
NEG_BIG = -1e30
LOG2_E = math.log2(math.e)


def _cparams(sem, vmem=VMEM_LIMIT):
    return pltpu.CompilerParams(dimension_semantics=sem, vmem_limit_bytes=vmem)


def _adaln_kernel(c_ref, w_ref, b_ref, o_ref):
    c = c_ref[...]
    s = c * jax.nn.sigmoid(c)
    o_ref[...] = jnp.dot(s, w_ref[...], preferred_element_type=F32, precision=HIGHEST) + b_ref[...]


def _adaln(cvec, w, b):
    rows, d = cvec.shape
    n = w.shape[1]
    tn = 1536
    return pl.pallas_call(
        _adaln_kernel,
        grid=(n // tn,),
        in_specs=[pl.BlockSpec((rows, d), lambda j: (0, 0)),
                  pl.BlockSpec((d, tn), lambda j: (0, j)),
                  pl.BlockSpec((1, tn), lambda j: (0, j))],
        out_specs=pl.BlockSpec((rows, tn), lambda j: (0, j)),
        out_shape=jax.ShapeDtypeStruct((rows, n), F32),
        compiler_params=_cparams(("arbitrary",)),
        name="adaln_mod",
    )(cvec, w, b.reshape(1, n))


def _norm_mod(x, g, sc, sh):
    var = jnp.mean(x * x, axis=-1, keepdims=True)
    y = x * lax.rsqrt(var + NORM_EPS) * g
    return y * (1.0 + sc) + sh


def _ones_column(shape):
    lane = lax.broadcasted_iota(jnp.int32, shape, 1)
    return jnp.where(lane == 0, 1.0, 0.0).astype(BF16)


def _rope_tile(x, cos, sin_a, sin_b):
    up = pltpu.roll(x, LANES - 16, axis=1)
    dn = pltpu.roll(x, 16, axis=1)
    return x * cos + up * sin_a + dn * sin_b


def _rope_tables(seq, rot_dim):
    n_rows = seq // GRID_W
    rows = jnp.repeat(jnp.arange(n_rows, dtype=F32), GRID_W)
    cols = jnp.tile(jnp.arange(GRID_W, dtype=F32), n_rows)
    axis_dim = rot_dim // 2
    inv_freq = ROPE_BASE ** (-jnp.arange(0, axis_dim, 2, dtype=F32) / axis_dim)
    ang_r = rows[:, None] * inv_freq
    ang_c = cols[:, None] * inv_freq
    ang = jnp.concatenate([ang_r, ang_r, ang_c, ang_c], axis=-1)
    cos, sin = jnp.cos(ang), jnp.sin(ang)
    quarter = rot_dim // 4
    first = (jnp.arange(rot_dim) % (2 * quarter)) < quarter
    sin_a = jnp.where(first, -sin, 0.0)
    sin_b = jnp.where(first, 0.0, sin)
    return cos, sin_a, sin_b


def _rope_tables_128(seq, ctx, rot_dim, tile_groups):
    cos, sin_a, sin_b = _rope_tables(seq, rot_dim)
    if tile_groups:
        reps = LANES // rot_dim
        cos, sin_a, sin_b = (jnp.tile(t, (1, reps)) for t in (cos, sin_a, sin_b))
    else:
        pad = LANES - rot_dim
        cos = jnp.concatenate([cos, jnp.ones((seq, pad), F32)], axis=-1)
        sin_a = jnp.concatenate([sin_a, jnp.zeros((seq, pad), F32)], axis=-1)
        sin_b = jnp.concatenate([sin_b, jnp.zeros((seq, pad), F32)], axis=-1)
    cos = jnp.concatenate([cos, jnp.ones((ctx, LANES), F32)], axis=0)
    sin_a = jnp.concatenate([sin_a, jnp.zeros((ctx, LANES), F32)], axis=0)
    sin_b = jnp.concatenate([sin_b, jnp.zeros((ctx, LANES), F32)], axis=0)
    return cos, sin_a, sin_b


def _mod_index(tpb, nb):
    def idx(i):
        return jnp.where(i % tpb == tpb - 1, nb, i // tpb)
    return idx


def _even_proj_kernel(x_ref, g_ref, sc_ref, sh_ref, wg_ref, wr_ref, wq_ref, wkv_ref, wkr_ref,
                      og_ref, or_ref, oq_ref, okv_ref, okr_ref):
    h = _norm_mod(x_ref[...], g_ref[...], sc_ref[...], sh_ref[...]).astype(BF16)
    for w_ref, o_ref in ((wg_ref, og_ref), (wr_ref, or_ref), (wq_ref, oq_ref),
                         (wkv_ref, okv_ref), (wkr_ref, okr_ref)):
        o_ref[...] = jnp.dot(h, w_ref[...], preferred_element_type=F32)


def _even_proj(x, gain, sc, sh, weights, tpb, nb):
    t, d = x.shape
    midx = _mod_index(tpb, nb)
    w_specs = [pl.BlockSpec(w.shape, lambda i: (0, 0)) for w in weights]
    return pl.pallas_call(
        _even_proj_kernel,
        grid=(t // TM,),
        in_specs=[pl.BlockSpec((TM, d), lambda i: (i, 0)),
                  pl.BlockSpec((1, d), lambda i: (0, 0)),
                  pl.BlockSpec((None, 1, d), lambda i: (midx(i), 0, 0)),
                  pl.BlockSpec((None, 1, d), lambda i: (midx(i), 0, 0))] + w_specs,
        out_specs=[pl.BlockSpec((TM, w.shape[1]), lambda i: (i, 0)) for w in weights],
        out_shape=[jax.ShapeDtypeStruct((t, w.shape[1]), F32) for w in weights],
        compiler_params=_cparams(("parallel",)),
        name="even_in_proj",
    )(x, gain, sc, sh, *weights)


def _mla_prep_kernel(qc_ref, kvc_ref, kr_ref, qng_ref, kvng_ref, wqn_ref, wqr_ref, wkv_ref,
                     qgn_ref, qgr_ref, kgn_ref, kgr_ref, cos_ref, sa_ref, sb_ref,
                     q_ref, k_ref, v_ref):
    cos, sa, sb = cos_ref[...], sa_ref[...], sb_ref[...]
    inv_qk = 1.0 / MLA_QK

    qc = qc_ref[...]
    hq = (qc * lax.rsqrt(jnp.mean(qc * qc, axis=-1, keepdims=True) + NORM_EPS) * qng_ref[...]).astype(BF16)
    q_nope = jnp.dot(hq, wqn_ref[...], preferred_element_type=F32)
    q_rope = jnp.dot(hq, wqr_ref[...], preferred_element_type=F32)
    q_scale = MLA_QK ** -0.5 * LOG2_E
    for h in range(MLA_HEADS):
        qn = q_nope[:, h * LANES:(h + 1) * LANES]
        qr = q_rope[:, h * LANES:(h + 1) * LANES]
        ms = (jnp.sum(qn * qn, axis=-1, keepdims=True) + jnp.sum(qr * qr, axis=-1, keepdims=True)) * inv_qk
        rs = lax.rsqrt(ms + NORM_EPS) * q_scale
        q_ref[:, (2 * h) * LANES:(2 * h + 1) * LANES] = (qn * rs * qgn_ref[...]).astype(BF16)
        q_ref[:, (2 * h + 1) * LANES:(2 * h + 2) * LANES] = _rope_tile(qr * rs * qgr_ref[...], cos, sa, sb).astype(BF16)

    kvc = kvc_ref[...]
    hkv = (kvc * lax.rsqrt(jnp.mean(kvc * kvc, axis=-1, keepdims=True) + NORM_EPS) * kvng_ref[...]).astype(BF16)
    kv = jnp.dot(hkv, wkv_ref[...], preferred_element_type=F32)
    kr = kr_ref[...]
    kr_ss = jnp.sum(kr * kr, axis=-1, keepdims=True)
    kr_rot = _rope_tile(kr * kgr_ref[...], cos, sa, sb)
    for h in range(MLA_HEADS):
        kn = kv[:, (2 * h) * LANES:(2 * h + 1) * LANES]
        ms = (jnp.sum(kn * kn, axis=-1, keepdims=True) + kr_ss) * inv_qk
        rs = lax.rsqrt(ms + NORM_EPS)
        k_ref[:, (2 * h) * LANES:(2 * h + 1) * LANES] = (kn * rs * kgn_ref[...]).astype(BF16)
        k_ref[:, (2 * h + 1) * LANES:(2 * h + 2) * LANES] = (kr_rot * rs).astype(BF16)
        v_ref[:, (2 * h) * LANES:(2 * h + 1) * LANES] = kv[:, (2 * h + 1) * LANES:(2 * h + 2) * LANES].astype(BF16)
        v_ref[:, (2 * h + 1) * LANES:(2 * h + 2) * LANES] = _ones_column((kv.shape[0], LANES))


def _mla_prep(qc, kvc, kr, q_norm_g, kv_norm_g, w_q_nope, w_q_rope, w_ukv, q_gn, q_gr, k_gn, k_gr,
              rope, tpb):
    t = qc.shape[0]
    cos, sa, sb = rope

    def const(a):
        return pl.BlockSpec(a.shape, lambda i: (0, 0))

    def rows(a):
        return pl.BlockSpec((TM, a.shape[1]), lambda i: (i, 0))

    tab = pl.BlockSpec((TM, LANES), lambda i: (i % tpb, 0))
    hq, hv = MLA_HEADS * 2 * LANES, MLA_HEADS * 2 * MLA_V
    return pl.pallas_call(
        _mla_prep_kernel,
        grid=(t // TM,),
        in_specs=[rows(qc), rows(kvc), rows(kr), const(q_norm_g), const(kv_norm_g),
                  const(w_q_nope), const(w_q_rope), const(w_ukv),
                  const(q_gn), const(q_gr), const(k_gn), const(k_gr), tab, tab, tab],
        out_specs=[pl.BlockSpec((TM, hq), lambda i: (i, 0)),
                   pl.BlockSpec((TM, hq), lambda i: (i, 0)),
                   pl.BlockSpec((TM, hv), lambda i: (i, 0))],
        out_shape=[jax.ShapeDtypeStruct((t, hq), BF16),
                   jax.ShapeDtypeStruct((t, hq), BF16),
                   jax.ShapeDtypeStruct((t, hv), BF16)],
        compiler_params=_cparams(("parallel",)),
        name="mla_prep",
    )(qc, kvc, kr, q_norm_g, kv_norm_g, w_q_nope, w_q_rope, w_ukv, q_gn, q_gr, k_gn, k_gr, cos, sa, sb)


def _sublane_iota():
    return lax.broadcasted_iota(jnp.int32, (SUBLANES, LANES), 0)


def _scan_chunk(a_chunk, u_chunk, h_chunk, carry, reverse):
    steps = range(LRU_SEG - 1, -1, -1) if reverse else range(LRU_SEG)
    h_loc, p_loc = [None] * LRU_SEG, [None] * LRU_SEG
    h = p = None
    for j in steps:
        a = a_chunk[pl.ds(j, SUBLANES, stride=LRU_SEG), :]
        u = u_chunk[pl.ds(j, SUBLANES, stride=LRU_SEG), :]
        if h is None:
            h, p = u, a
        else:
            h, p = a * h + u, a * p
        h_loc[j], p_loc[j] = h, p
    sub = _sublane_iota()
    seg_p, seg_h = p, h
    for d in (1, 2, 4):
        shift = SUBLANES - d if reverse else d
        prev_p = pltpu.roll(seg_p, shift, axis=0)
        prev_h = pltpu.roll(seg_h, shift, axis=0)
        valid = (sub < SUBLANES - d) if reverse else (sub >= d)
        seg_h = jnp.where(valid, seg_p * prev_h + seg_h, seg_h)
        seg_p = jnp.where(valid, seg_p * prev_p, seg_p)
    h_end = seg_h + seg_p * carry
    if reverse:
        h_in = jnp.where(sub == SUBLANES - 1, carry, pltpu.roll(h_end, SUBLANES - 1, axis=0))
        new_carry = h_end[0:1, :]
    else:
        h_in = jnp.where(sub == 0, carry, pltpu.roll(h_end, 1, axis=0))
        new_carry = h_end[SUBLANES - 1:SUBLANES, :]
    for j in range(LRU_SEG):
        h_chunk[pl.ds(j, SUBLANES, stride=LRU_SEG), :] = h_loc[j] + p_loc[j] * h_in
    return jnp.broadcast_to(new_carry, (SUBLANES, LANES))


def _lru_kernel(g_ref, r_ref, cw_ref, cb_ref, wg_ref, bg_ref, lam_ref, o_ref,
                rp_ref, af_ref, uf_ref, ab_ref, ub_ref, hf_ref, hb_ref, *, ctx, seq):
    n_ctx, n_lat = ctx // LRU_CHUNK, seq // LRU_CHUNK
    zeros_pad = jnp.zeros((LRU_PAD, LRU_CT), F32)
    ctx0 = seq + 2 * LRU_PAD
    rp_ref[0:LRU_PAD, :] = zeros_pad
    rp_ref[LRU_PAD:LRU_PAD + seq, :] = r_ref[0:seq, :]
    rp_ref[LRU_PAD + seq:ctx0, :] = zeros_pad
    rp_ref[ctx0:ctx0 + ctx, :] = r_ref[seq:seq + ctx, :]
    rp_ref[ctx0 + ctx:ctx0 + ctx + LRU_PAD, :] = zeros_pad

    cw = cw_ref[...]
    cb = cb_ref[...]
    wg = wg_ref[...]
    bg = bg_ref[...]
    lam = lam_ref[...]
    sp = jnp.maximum(-lam, 0.0) + jnp.log1p(jnp.exp(-jnp.abs(lam)))

    def coeff_chunk(c, pad_off, row_off):
        start = pl.multiple_of(pad_off + c * LRU_CHUNK, SUBLANES)
        ext = rp_ref[pl.ds(start, LRU_CHUNK + 2 * LRU_PAD), :]
        x = cb
        for tap in range(CONV_W):
            lo = LRU_PAD - 2 + tap
            x = x + ext[lo:lo + LRU_CHUNK, :] * cw[tap:tap + 1, :]
        gates = jnp.dot(x.astype(BF16), wg, preferred_element_type=F32) + bg
        out_row = pl.multiple_of(row_off + c * LRU_CHUNK, SUBLANES)
        for d, (a_ref, u_ref) in enumerate(((af_ref, uf_ref), (ab_ref, ub_ref))):
            r = jax.nn.sigmoid(gates[:, (2 * d) * LRU_CT:(2 * d + 1) * LRU_CT])
            i = jax.nn.sigmoid(gates[:, (2 * d + 1) * LRU_CT:(2 * d + 2) * LRU_CT])
            log_a = -LRU_C * r * sp[d:d + 1, :]
            a_ref[pl.ds(out_row, LRU_CHUNK), :] = jnp.exp(log_a)
            th = jnp.tanh(log_a)
            u_ref[pl.ds(out_row, LRU_CHUNK), :] = jnp.sqrt(-2.0 * th / (1.0 - th)) * i * x
        return None

    def coeff_ctx(c, _):
        coeff_chunk(c, seq + LRU_PAD, seq)
        return 0

    def coeff_lat(c, _):
        coeff_chunk(c, 0, 0)
        return 0

    lax.fori_loop(0, n_ctx, coeff_ctx, 0)
    lax.fori_loop(0, n_lat, coeff_lat, 0)

    def scan_pair(n, row_off):
        def body(c, carry):
            cf, cb_ = carry
            f_row = pl.multiple_of(row_off + c * LRU_CHUNK, SUBLANES)
            b_row = pl.multiple_of(row_off + (n - 1 - c) * LRU_CHUNK, SUBLANES)
            cf = _scan_chunk(af_ref.at[pl.ds(f_row, LRU_CHUNK), :], uf_ref.at[pl.ds(f_row, LRU_CHUNK), :],
                             hf_ref.at[pl.ds(f_row, LRU_CHUNK), :], cf, False)
            cb_ = _scan_chunk(ab_ref.at[pl.ds(b_row, LRU_CHUNK), :], ub_ref.at[pl.ds(b_row, LRU_CHUNK), :],
                              hb_ref.at[pl.ds(b_row, LRU_CHUNK), :], cb_, True)
            return cf, cb_
        return body

    zero = jnp.zeros((SUBLANES, LANES), F32)
    carry = lax.fori_loop(0, n_ctx, scan_pair(n_ctx, seq), (zero, zero))
    lax.fori_loop(0, n_lat, scan_pair(n_lat, 0), carry)

    g = g_ref[...]
    gelu = 0.5 * g * (1.0 + jnp.tanh(math.sqrt(2.0 / math.pi) * (g + 0.044715 * (g * g * g))))
    o_ref[...] = (gelu * (hf_ref[...] + hb_ref[...])).astype(BF16)


def _lru(g, r, conv_w, conv_b, w_gates, b_gates, lam, nb, ctx, seq):
    l = ctx + seq
    width = g.shape[1]
    n_ct = width // LRU_CT
    g3 = g.reshape(nb, l, width)
    r3 = r.reshape(nb, l, width)
    seq_spec = pl.BlockSpec((None, l, LRU_CT), lambda b, c: (b, 0, c))
    scratch = [pltpu.VMEM((l + 3 * LRU_PAD, LRU_CT), F32)] + [pltpu.VMEM((l, LRU_CT), F32)] * 6
    out = pl.pallas_call(
        functools.partial(_lru_kernel, ctx=ctx, seq=seq),
        grid=(nb, n_ct),
        in_specs=[seq_spec, seq_spec,
                  pl.BlockSpec((CONV_W, LRU_CT), lambda b, c: (0, c)),
                  pl.BlockSpec((1, LRU_CT), lambda b, c: (0, c)),
                  pl.BlockSpec((None, LRU_CT, 4 * LRU_CT), lambda b, c: (c, 0, 0)),
                  pl.BlockSpec((None, 1, 4 * LRU_CT), lambda b, c: (c, 0, 0)),
                  pl.BlockSpec((2, LRU_CT), lambda b, c: (0, c))],
        out_specs=seq_spec,
        out_shape=jax.ShapeDtypeStruct((nb, l, width), BF16),
        scratch_shapes=scratch,
        compiler_params=_cparams(("parallel", "parallel")),
        name="rglru",
    )(g3, r3, conv_w, conv_b, w_gates, b_gates, lam)
    return out.reshape(nb * l, width)


def _lru_gate_weights(wa, ba, wx, bx):
    per = LRU_CT // LRU_BLOCK_W
    n_ct = LRU_BLOCKS // per
    eye = jnp.eye(per, dtype=F32)

    def dense(w):
        w4 = w.reshape(n_ct, per, LRU_BLOCK_W, LRU_BLOCK_W)
        return jnp.einsum('cide,ij->cidje', w4, eye).reshape(n_ct, LRU_CT, LRU_CT)

    w = jnp.concatenate([dense(wa[0]), dense(wx[0]), dense(wa[1]), dense(wx[1])], axis=-1)
    b = jnp.concatenate([v.reshape(n_ct, 1, LRU_CT) for v in (ba[0], bx[0], ba[1], bx[1])], axis=-1)
    return w.astype(BF16), b


def _key_spans(n_keys):
    half = n_keys // 2 if n_keys % (2 * LANES) == 0 else n_keys
    return [(lo, lo + half) for lo in range(0, n_keys, half)]


def _attn_scores(q_ops, kt_ref, s_ref, m_ref):
    for t, q in enumerate(q_ops):
        for lo, hi in _key_spans(kt_ref.shape[-1]):
            s_ref[t, :, lo:hi] = jnp.dot(q, kt_ref[:, lo:hi], preferred_element_type=F32)
        m_ref[t] = jnp.max(s_ref[t], axis=-1, keepdims=True)


def _attn_values(v_ref, s_ref, m_ref, p_ref):
    accs = []
    for t in range(s_ref.shape[0]):
        m = m_ref[t]
        acc = None
        for lo, hi in _key_spans(s_ref.shape[-1]):
            p_ref[t, :, lo:hi] = jnp.exp2(s_ref[t, :, lo:hi] - m).astype(BF16)
            part = jnp.dot(p_ref[t, :, lo:hi], v_ref[lo:hi, :], preferred_element_type=F32)
            acc = part if acc is None else acc + part
        accs.append(acc)
    return accs


def _attn_pipeline(q_ref, kt_ref, v_ref, o_ref, bufs, p_ref, q_ops_fn, finish_fn):
    s0, m0 = bufs[0]
    tq = s0.shape[1]
    n = q_ref.shape[0] // tq

    def scores(t, s_ref, m_ref):
        row = t * tq if isinstance(t, int) else pl.multiple_of(t * tq, tq)
        _attn_scores(q_ops_fn(q_ref[pl.ds(row, tq), :]), kt_ref, s_ref, m_ref)

    def values(t, s_ref, m_ref):
        row = t * tq if isinstance(t, int) else pl.multiple_of(t * tq, tq)
        o_ref[pl.ds(row, tq), :] = finish_fn(_attn_values(v_ref, s_ref, m_ref, p_ref)).astype(o_ref.dtype)

    if len(bufs) == 1:
        def one(t, _):
            scores(t, s0, m0)
            values(t, s0, m0)
            return 0

        if n == 1:
            one(0, 0)
        else:
            lax.fori_loop(0, n, one, 0)
        return

    s1, m1 = bufs[1]
    scores(0, s0, m0)
    if n == 1:
        values(0, s0, m0)
        return
    assert n % 2 == 0

    def pair(k, _):
        t = 2 * k
        scores(t + 1, s1, m1)
        values(t, s0, m0)
        scores(t + 2, s0, m0)
        values(t + 1, s1, m1)
        return 0

    lax.fori_loop(0, n // 2 - 1, pair, 0)
    scores(n - 1, s1, m1)
    values(n - 2, s0, m0)
    values(n - 1, s1, m1)


def _score_bufs(scratch):
    return tuple(zip(scratch[0:-1:2], scratch[1:-1:2])), scratch[-1]


def _mla_attn_kernel(q_ref, kt_ref, v_ref, o_ref, *scratch):
    dv = o_ref.shape[-1]
    bufs, p_ref = _score_bufs(scratch)

    def finish(accs):
        return accs[0][:, :dv] / accs[0][:, dv:dv + 1]

    _attn_pipeline(q_ref, kt_ref, v_ref, o_ref, bufs, p_ref, lambda q: [q], finish)


def _diff_attn_kernel(q_ref, kt_ref, v_ref, g_ref, lam_ref, o_ref, *scratch, out_scale):
    dv = o_ref.shape[-1]
    bufs, p_ref = _score_bufs(scratch)

    def q_ops(q):
        lane = lax.broadcasted_iota(jnp.int32, q.shape, 1)
        zero = jnp.zeros_like(q)
        return [jnp.where(lane < DIFF_HEAD_DIM, q, zero), jnp.where(lane < DIFF_HEAD_DIM, zero, q)]

    def finish(accs):
        a1, a2 = accs
        o = a1[:, :dv] / a1[:, dv:dv + 1] - lam_ref[0] * (a2[:, :dv] / a2[:, dv:dv + 1])
        y = o * lax.rsqrt(jnp.mean(o * o, axis=-1, keepdims=True) + NORM_EPS) * g_ref[...]
        return y * out_scale

    _attn_pipeline(q_ref, kt_ref, v_ref, o_ref, bufs, p_ref, q_ops, finish)


def _attention(kernel_fn, name, n_softmax, q, kt, v, nb, ctx, seq, heads, dk, dv, ctx_queries,
               extra_in=(), extra_specs=()):
    l = ctx + seq
    q3 = q.reshape(nb, l, heads * dk)
    v3 = v.reshape(nb, l, heads * 2 * dv)
    if ctx_queries:
        tq, n_keys, rows = ctx, ctx, ctx
        blk = seq // ctx
        in_specs = [pl.BlockSpec((None, rows, dk), lambda b, h: (b, blk, h)),
                    pl.BlockSpec((None, None, dk, ctx), lambda b, h: (b, h, 0, blk)),
                    pl.BlockSpec((None, ctx, 2 * dv), lambda b, h: (b, blk, h))]
    else:
        tq, n_keys, rows = TQ, l, seq
        in_specs = [pl.BlockSpec((None, rows, dk), lambda b, h: (b, 0, h)),
                    pl.BlockSpec((None, None, dk, l), lambda b, h: (b, h, 0, 0)),
                    pl.BlockSpec((None, l, 2 * dv), lambda b, h: (b, 0, h))]
    score_buf = [pltpu.VMEM((n_softmax, tq, n_keys), F32), pltpu.VMEM((n_softmax, tq, 1), F32)]
    out = pl.pallas_call(
        kernel_fn,
        grid=(nb, heads),
        in_specs=in_specs + list(extra_specs),
        out_specs=pl.BlockSpec((None, rows, dv), lambda b, h: (b, 0, h)),
        out_shape=jax.ShapeDtypeStruct((nb, rows, heads * dv), BF16),
        scratch_shapes=score_buf * (2 if n_softmax == 1 else 1) + [pltpu.VMEM((n_softmax, tq, n_keys), BF16)],
        compiler_params=_cparams(("parallel", "parallel")),
        name=name,
    )(q3, kt, v3, *extra_in)
    return out


def _key_transpose(k, nb, l, heads, dk):
    return jnp.transpose(k.reshape(nb, l, heads, dk), (0, 2, 3, 1))


def _attend_all(kernel_fn, name, n_softmax, q, k, v, nb, ctx, seq, heads, dk, dv, need_ctx, **extra):
    kt = _key_transpose(k, nb, ctx + seq, heads, dk)
    lat = _attention(kernel_fn, name, n_softmax, q, kt, v, nb, ctx, seq, heads, dk, dv, False, **extra)
    if not need_ctx:
        return lat.reshape(nb * seq, heads * dv)
    cx = _attention(kernel_fn, name + "_ctx", n_softmax, q, kt, v, nb, ctx, seq, heads, dk, dv, True, **extra)
    return jnp.concatenate([lat, cx], axis=1).reshape(nb * (ctx + seq), heads * dv)


def _group_ms(x, gmat):
    return jnp.dot((x * x).astype(BF16), gmat, preferred_element_type=F32)


def _odd_proj_kernel(x_ref, g_ref, sc_ref, sh_ref, wq_ref, wk_ref, wv_ref, qg_ref, kg_ref, gm_ref,
                     cos_ref, sa_ref, sb_ref, q_ref, k_ref, v_ref):
    h = _norm_mod(x_ref[...], g_ref[...], sc_ref[...], sh_ref[...]).astype(BF16)
    cos, sa, sb = cos_ref[...], sa_ref[...], sb_ref[...]
    gmat = gm_ref[...]
    q_scale = DIFF_HEAD_DIM ** -0.5 * LOG2_E
    for w_ref, gain_ref, o_ref, scale in ((wq_ref, qg_ref, q_ref, q_scale), (wk_ref, kg_ref, k_ref, 1.0)):
        raw = jnp.dot(h, w_ref[...], preferred_element_type=F32)
        gain = gain_ref[...] * scale
        for c in range(raw.shape[1] // LANES):
            t = raw[:, c * LANES:(c + 1) * LANES]
            t = t * lax.rsqrt(_group_ms(t, gmat) + NORM_EPS) * gain
            o_ref[:, c * LANES:(c + 1) * LANES] = _rope_tile(t, cos, sa, sb).astype(BF16)
    v = jnp.dot(h, wv_ref[...], preferred_element_type=F32).astype(BF16)
    for c in range(v.shape[1] // DIFF_V):
        v_ref[:, (2 * c) * DIFF_V:(2 * c + 1) * DIFF_V] = v[:, c * DIFF_V:(c + 1) * DIFF_V]
        v_ref[:, (2 * c + 1) * DIFF_V:(2 * c + 2) * DIFF_V] = _ones_column((v.shape[0], DIFF_V))


def _odd_proj(x, gain, sc, sh, wq, wk, wv, q_gain, k_gain, gmat, rope, tpb, nb):
    t, d = x.shape
    midx = _mod_index(tpb, nb)
    cos, sa, sb = rope

    def const(a):
        return pl.BlockSpec(a.shape, lambda i: (0, 0))

    tab = pl.BlockSpec((TM, LANES), lambda i: (i % tpb, 0))
    n = wq.shape[1]
    return pl.pallas_call(
        _odd_proj_kernel,
        grid=(t // TM,),
        in_specs=[pl.BlockSpec((TM, d), lambda i: (i, 0)), const(gain),
                  pl.BlockSpec((None, 1, d), lambda i: (midx(i), 0, 0)),
                  pl.BlockSpec((None, 1, d), lambda i: (midx(i), 0, 0)),
                  const(wq), const(wk), const(wv), const(q_gain), const(k_gain), const(gmat),
                  tab, tab, tab],
        out_specs=[pl.BlockSpec((TM, n), lambda i: (i, 0))] * 2 + [pl.BlockSpec((TM, 2 * n), lambda i: (i, 0))],
        out_shape=[jax.ShapeDtypeStruct((t, n), BF16)] * 2 + [jax.ShapeDtypeStruct((t, 2 * n), BF16)],
        compiler_params=_cparams(("parallel",)),
        name="odd_in_proj",
    )(x, gain, sc, sh, wq, wk, wv, q_gain, k_gain, gmat, cos, sa, sb)


def _post_mix_kernel(*refs, n_mix):
    mix_refs = refs[:n_mix]
    w_refs = refs[n_mix:2 * n_mix]
    (x_ref, g1_ref, n2_ref, sc_ref, sh_ref, rw_ref, rb_ref,
     xo_ref, h2_ref, te_ref, tw_ref, tr_ref, cnt_ref, base_ref) = refs[2 * n_mix:]
    i = pl.program_id(0)

    @pl.when(i == 0)
    def _():
        base_ref[...] = jnp.zeros_like(base_ref)

    m = jnp.dot(mix_refs[0][...], w_refs[0][...], preferred_element_type=F32)
    for a_ref, w_ref in zip(mix_refs[1:], w_refs[1:]):
        m = m + jnp.dot(a_ref[...], w_ref[...], preferred_element_type=F32)
    x = x_ref[...] + g1_ref[...] * m
    xo_ref[...] = x
    h2 = _norm_mod(x, n2_ref[...], sc_ref[...], sh_ref[...])
    h2_ref[...] = h2.astype(BF16)

    logits = jnp.dot(h2, rw_ref[...], preferred_element_type=F32, precision=HIGHEST) + rb_ref[...]
    lane = lax.broadcasted_iota(jnp.int32, logits.shape, 1)
    lane_f = lane.astype(F32)
    vals = jnp.where(lane < N_EXPERTS, logits, NEG_BIG)
    tops, hots = [], []
    for _ in range(TOP_K):
        top = jnp.max(vals, axis=-1, keepdims=True)
        first = jnp.min(jnp.where(vals == top, lane_f, float(LANES)), axis=-1, keepdims=True)
        hot = lane_f == first
        tops.append(top)
        hots.append(hot)
        vals = jnp.where(hot, 2.0 * NEG_BIG, vals)
    exps = [jnp.exp(t - tops[0]) for t in tops]
    denom = exps[0] + exps[1] + exps[2] + exps[3]

    picked = (hots[0] | hots[1]) | (hots[2] | hots[3])
    cnt = jnp.where(picked, 1.0, 0.0)
    r_io = lax.broadcasted_iota(jnp.int32, (TM, TM), 0)
    c_io = lax.broadcasted_iota(jnp.int32, (TM, TM), 1)
    tri = jnp.where(c_io < r_io, 1.0, 0.0).astype(BF16)
    before = jnp.dot(tri, cnt.astype(BF16), preferred_element_type=F32) + base_ref[...]
    te = jnp.zeros(logits.shape, F32)
    tw = jnp.zeros(logits.shape, F32)
    tr = jnp.zeros(logits.shape, F32)
    for k in range(TOP_K):
        e_k = jnp.sum(jnp.where(hots[k], lane_f, 0.0), axis=-1, keepdims=True)
        r_k = jnp.sum(jnp.where(hots[k], before, 0.0), axis=-1, keepdims=True)
        te = jnp.where(lane == k, e_k, te)
        tw = jnp.where(lane == k, exps[k] / denom, tw)
        tr = jnp.where(lane == k, r_k, tr)
    te_ref[...] = te.astype(jnp.int32)
    tw_ref[...] = tw
    tr_ref[...] = tr.astype(jnp.int32)
    base_ref[...] += jnp.sum(cnt, axis=0, keepdims=True)
    cnt_ref[...] = base_ref[...]


def _post_mix(mixes, weights, x, g1, n2g, sc2, sh2, router_w, router_b, tpb, nb, latent_only):
    d = x.shape[1]
    tiles_in = tpb
    if latent_only:
        tiles_out = tpb - 1
        n_tiles = nb * tiles_out

        def in_row(i):
            return (i // tiles_out) * tiles_in + i % tiles_out

        def midx(i):
            return i // tiles_out
    else:
        n_tiles = nb * tpb

        def in_row(i):
            return i

        midx = _mod_index(tpb, nb)
    t_out = n_tiles * TM

    def const(a):
        return pl.BlockSpec(a.shape, lambda i: (0, 0))

    def mod(a):
        return pl.BlockSpec((None, 1, d), lambda i: (midx(i), 0, 0))

    in_specs = [pl.BlockSpec((TM, a.shape[1]), lambda i: (i, 0)) for a in mixes]
    in_specs += [const(w) for w in weights]
    in_specs += [pl.BlockSpec((TM, d), lambda i: (in_row(i), 0)), mod(g1), const(n2g), mod(sc2), mod(sh2),
                 const(router_w), const(router_b)]
    row_out = lambda width: pl.BlockSpec((TM, width), lambda i: (i, 0))
    return pl.pallas_call(
        functools.partial(_post_mix_kernel, n_mix=len(mixes)),
        grid=(n_tiles,),
        in_specs=in_specs,
        out_specs=[row_out(d), row_out(d), row_out(LANES), row_out(LANES), row_out(LANES),
                   pl.BlockSpec((1, LANES), lambda i: (0, 0))],
        out_shape=[jax.ShapeDtypeStruct((t_out, d), F32), jax.ShapeDtypeStruct((t_out, d), BF16),
                   jax.ShapeDtypeStruct((t_out, LANES), jnp.int32), jax.ShapeDtypeStruct((t_out, LANES), F32),
                   jax.ShapeDtypeStruct((t_out, LANES), jnp.int32), jax.ShapeDtypeStruct((1, LANES), F32)],
        scratch_shapes=[pltpu.VMEM((1, LANES), F32)],
        compiler_params=_cparams(("arbitrary",)),
        name="post_mix_router",
    )(*mixes, *weights, x, g1, n2g, sc2, sh2, router_w, router_b)


def _moe_ffn_kernel(be_ref, nu_ref, x_ref, wgu_ref, bgu_ref, wd_ref, bd_ref, o_ref, wgu_bf, wd_bf):
    i = pl.program_id(0)
    prev = be_ref[jnp.maximum(i - 1, 0)]
    fresh = jnp.logical_or(i == 0, be_ref[i] != prev)

    @pl.when(jnp.logical_and(fresh, i < nu_ref[0]))
    def _():
        wgu_bf[...] = wgu_ref[...].astype(BF16)
        wd_bf[...] = wd_ref[...].astype(BF16)

    @pl.when(i < nu_ref[0])
    def _():
        d_ff = wd_bf.shape[0]
        gu = jnp.dot(x_ref[...], wgu_bf[...], preferred_element_type=F32) + bgu_ref[...]
        g = jnp.minimum(gu[:, :d_ff], SWIGLU_LIMIT)
        u = jnp.clip(gu[:, d_ff:], -SWIGLU_LIMIT, SWIGLU_LIMIT)
        act = (u + 1.0) * g * jax.nn.sigmoid(SWIGLU_ALPHA * g)
        y = jnp.dot(act.astype(BF16), wd_bf[...], preferred_element_type=F32) + bd_ref[...]
        o_ref[...] = y.astype(o_ref.dtype)

    @pl.when(i >= nu_ref[0])
    def _():
        o_ref[...] = jnp.zeros_like(o_ref)


def _moe_ffn(block_e, n_used, xs, w_gu, b_gu, w_down, b_down, layer):
    n_rows, d = xs.shape
    depth, n_e, _, two_ff = w_gu.shape
    d_ff = two_ff // 2
    n_blocks = n_rows // MOE_BM
    grid_spec = pltpu.PrefetchScalarGridSpec(
        num_scalar_prefetch=2,
        grid=(n_blocks,),
        in_specs=[pl.BlockSpec((MOE_BM, d), lambda i, be, nu: (i, 0)),
                  pl.BlockSpec((None, None, d, two_ff), lambda i, be, nu: (layer, be[i], 0, 0)),
                  pl.BlockSpec((None, None, 1, two_ff), lambda i, be, nu: (layer, be[i], 0, 0)),
                  pl.BlockSpec((None, None, d_ff, d), lambda i, be, nu: (layer, be[i], 0, 0)),
                  pl.BlockSpec((None, None, 1, d), lambda i, be, nu: (layer, be[i], 0, 0))],
        out_specs=pl.BlockSpec((MOE_BM, d), lambda i, be, nu: (i, 0)),
        scratch_shapes=[pltpu.VMEM((d, two_ff), BF16), pltpu.VMEM((d_ff, d), BF16)],
    )
    return pl.pallas_call(
        _moe_ffn_kernel,
        grid_spec=grid_spec,
        out_shape=jax.ShapeDtypeStruct((n_rows, d), BF16),
        compiler_params=_cparams(("arbitrary",)),
        name="moe_ffn",
    )(block_e, n_used, xs, w_gu, b_gu.reshape(depth, n_e, 1, two_ff), w_down, b_down.reshape(depth, n_e, 1, d))


def _moe_combine_kernel(x_ref, g2_ref, w_ref, y_ref, o_ref):
    w = w_ref[...]
    f = w[:, 0:1] * y_ref[0].astype(F32)
    for k in range(1, TOP_K):
        f = f + w[:, k:k + 1] * y_ref[k].astype(F32)
    o_ref[...] = x_ref[...] + g2_ref[...] * f


def _moe_combine(x, g2, top_w, picked, midx):
    t, d = x.shape
    return pl.pallas_call(
        _moe_combine_kernel,
        grid=(t // TM,),
        in_specs=[pl.BlockSpec((TM, d), lambda i: (i, 0)),
                  pl.BlockSpec((None, 1, d), lambda i: (midx(i), 0, 0)),
                  pl.BlockSpec((TM, LANES), lambda i: (i, 0)),
                  pl.BlockSpec((TOP_K, TM, d), lambda i: (0, i, 0))],
        out_specs=pl.BlockSpec((TM, d), lambda i: (i, 0)),
        out_shape=jax.ShapeDtypeStruct((t, d), F32),
        compiler_params=_cparams(("parallel",)),
        name="moe_combine",
    )(x, g2, top_w, picked)


def _moe(x, g2, midx, h2, top_e, top_w, top_r, counts, w_gu, b_gu, w_down, b_down, layer):
    t, d = h2.shape
    n_assign = t * TOP_K
    n_blocks = -(-n_assign // MOE_BM) + N_EXPERTS
    n_rows = n_blocks * MOE_BM
    cnt = counts[0, :N_EXPERTS].astype(jnp.int32)
    padded = (cnt + MOE_BM - 1) // MOE_BM * MOE_BM
    pends = jnp.cumsum(padded)
    pstarts = pends - padded
    e = top_e[:, :TOP_K]
    expert_ids = jnp.arange(N_EXPERTS, dtype=jnp.int32)
    pstart_of_pick = jnp.sum(jnp.where(e[:, :, None] == expert_ids, pstarts, 0), axis=-1)
    dest = pstart_of_pick + top_r[:, :TOP_K]
    block_start = jnp.arange(n_blocks, dtype=jnp.int32) * MOE_BM
    block_e = jnp.sum((block_start[:, None] >= pends[None, :]).astype(jnp.int32), axis=1)
    block_e = jnp.minimum(block_e, N_EXPERTS - 1)
    n_used = (pends[-1] // MOE_BM).astype(jnp.int32).reshape(1)

    tok = jnp.broadcast_to(jnp.arange(t, dtype=jnp.int32)[:, None], (t, TOP_K))
    row_tok = jnp.zeros((n_rows,), jnp.int32).at[dest.reshape(-1)].set(
        tok.reshape(-1), unique_indices=True, mode='promise_in_bounds')
    h2_big = jnp.concatenate([h2, jnp.zeros((n_rows - t, d), h2.dtype)], axis=0)
    xs = h2_big[row_tok]
    ys = _moe_ffn(block_e, n_used, xs, w_gu, b_gu, w_down, b_down, layer)
    picked = ys[dest.T.reshape(-1)].reshape(TOP_K, t, d)
    return _moe_combine(x, g2, top_w, picked, midx)


def _pad_cols(w, width):
    return jnp.concatenate([w, jnp.zeros((w.shape[0], width - w.shape[1]), w.dtype)], axis=1)


def kernel(x, c, ctx, c_ctx, mod_w, mod_b, norm1_g, norm2_g, ev_w_in, ev_w_out, lru_conv_w, lru_conv_b, lru_wa, lru_ba, lru_wx, lru_bx, lru_lambda, mla_q_norm_g, mla_w_uq, mla_kv_norm_g, mla_w_ukv, mla_qn_g, mla_kn_g, od_w_in, od_w_out, diff_qn_g, diff_kn_g, diff_lq1, diff_lk1, diff_lq2, diff_lk2, diff_subln_g, router_w, router_b, moe_w_gu, moe_b_gu, moe_w_down, moe_b_down):
    nb, seq, d = x.shape
    n_ctx = ctx.shape[1]
    depth = mod_w.shape[0]
    l = n_ctx + seq
    tpb = l // TM
    assert n_ctx == TM and seq % TQ == 0 and seq % GRID_W == 0

    xa = jnp.concatenate([x, ctx], axis=1).reshape(nb * l, d)
    cvec = jnp.concatenate([c, c_ctx[None, :], jnp.zeros((SUBLANES - nb - 1, d), F32)], axis=0)
    router_w_p = jnp.concatenate([router_w, jnp.zeros((depth, d, LANES - N_EXPERTS), F32)], axis=-1)
    router_b_p = jnp.concatenate([router_b, jnp.zeros((depth, LANES - N_EXPERTS), F32)], axis=-1)

    for layer in range(depth):
        last = layer == depth - 1
        i = layer // 2
        mod = _adaln(cvec, mod_w[layer], mod_b[layer])[:nb + 1]
        sh1, sc1, g1, sh2, sc2, g2 = (mod[:, k * d:(k + 1) * d].reshape(nb + 1, 1, d) for k in range(6))
        n1g = norm1_g[layer].reshape(1, d)
        n2g = norm2_g[layer].reshape(1, d)

        if layer % 2 == 0:
            w_in = ev_w_in[i].astype(BF16)
            o = 2 * LRU_WIDTH
            splits = [w_in[:, :LRU_WIDTH], w_in[:, LRU_WIDTH:o], w_in[:, o:o + MLA_Q_RANK],
                      w_in[:, o + MLA_Q_RANK:o + MLA_Q_RANK + MLA_KV_RANK],
                      _pad_cols(w_in[:, o + MLA_Q_RANK + MLA_KV_RANK:], LANES)]
            gl, rl, qc, kvc, kr = _even_proj(xa, n1g, sc1, sh1, splits, tpb, nb)

            w_gates, b_gates = _lru_gate_weights(lru_wa[i], lru_ba[i], lru_wx[i], lru_bx[i])
            lru = _lru(gl, rl, lru_conv_w[i], lru_conv_b[i].reshape(1, LRU_WIDTH), w_gates, b_gates,
                       lru_lambda[i], nb, n_ctx, seq)

            w_uq = mla_w_uq[i].astype(BF16).reshape(MLA_Q_RANK, MLA_HEADS, MLA_QK)
            w_q_nope = w_uq[:, :, :MLA_NOPE].reshape(MLA_Q_RANK, MLA_HEADS * MLA_NOPE)
            w_q_rope = jnp.concatenate(
                [w_uq[:, :, MLA_NOPE:], jnp.zeros((MLA_Q_RANK, MLA_HEADS, LANES - MLA_ROPE), BF16)],
                axis=-1).reshape(MLA_Q_RANK, MLA_HEADS * LANES)
            pad_g = lambda g: jnp.concatenate([g, jnp.zeros((LANES - MLA_ROPE,), F32)]).reshape(1, LANES)
            rope = _rope_tables_128(seq, n_ctx, MLA_ROPE, tile_groups=False)
            q, k, v = _mla_prep(qc, kvc, kr, mla_q_norm_g[i].reshape(1, -1), mla_kv_norm_g[i].reshape(1, -1),
                                w_q_nope, w_q_rope, mla_w_ukv[i].astype(BF16),
                                mla_qn_g[i][:MLA_NOPE].reshape(1, LANES), pad_g(mla_qn_g[i][MLA_NOPE:]),
                                mla_kn_g[i][:MLA_NOPE].reshape(1, LANES), pad_g(mla_kn_g[i][MLA_NOPE:]),
                                rope, tpb)
            att = _attend_all(_mla_attn_kernel, "mla_attention", 1, q, k, v, nb, n_ctx, seq, MLA_HEADS,
                              2 * LANES, MLA_V, need_ctx=not last)
            w_out = ev_w_out[i].astype(BF16)
            if last:
                lru = lru.reshape(nb, l, LRU_WIDTH)[:, :seq].reshape(nb * seq, LRU_WIDTH)
            mixes = [lru, att]
            weights = [w_out[:LRU_WIDTH], w_out[LRU_WIDTH:]]
        else:
            lam_init = 0.8 - 0.6 * math.exp(-0.3 * layer)
            w_in = od_w_in[i].astype(BF16)
            n_qk = DIFF_HEADS * 2 * DIFF_HEAD_DIM
            gidx = jnp.arange(LANES) // DIFF_HEAD_DIM
            gmat = jnp.where(gidx[:, None] == gidx[None, :], 1.0 / DIFF_HEAD_DIM, 0.0).astype(BF16)
            rope = _rope_tables_128(seq, n_ctx, DIFF_HEAD_DIM, tile_groups=True)
            tile_g = lambda g: jnp.tile(g, LANES // DIFF_HEAD_DIM).reshape(1, LANES)
            q, k, v = _odd_proj(xa, n1g, sc1, sh1, w_in[:, :n_qk], w_in[:, n_qk:2 * n_qk], w_in[:, 2 * n_qk:],
                                tile_g(diff_qn_g[i]), tile_g(diff_kn_g[i]), gmat, rope, tpb, nb)
            lam = (jnp.exp(jnp.sum(diff_lq1[i] * diff_lk1[i])) - jnp.exp(jnp.sum(diff_lq2[i] * diff_lk2[i]))
                   + lam_init).reshape(1).astype(F32)
            att = _attend_all(
                functools.partial(_diff_attn_kernel, out_scale=1.0 - lam_init), "diff_attention", 2,
                q, k, v, nb, n_ctx, seq, DIFF_HEADS, 2 * DIFF_HEAD_DIM, DIFF_V, need_ctx=not last,
                extra_in=(diff_subln_g[i].reshape(1, DIFF_V), lam),
                extra_specs=(pl.BlockSpec((1, DIFF_V), lambda b, h: (0, 0)),
                             pl.BlockSpec(memory_space=pltpu.SMEM)))
            mixes = [att]
            weights = [od_w_out[i].astype(BF16)]

        xo, h2, te, tw, tr, counts = _post_mix(mixes, weights, xa, g1, n2g, sc2, sh2, router_w_p[layer],
                                               router_b_p[layer].reshape(1, LANES), tpb, nb, last)
        midx = (lambda t: t // (tpb - 1)) if last else _mod_index(tpb, nb)
        xa = _moe(xo, g2, midx, h2, te, tw, tr, counts, moe_w_gu, moe_b_gu, moe_w_down, moe_b_down, layer)
    return xa.reshape(nb, seq, d)
```

```python
import functools
import math

import jax
import jax.numpy as jnp
from jax import lax
from jax.experimental import pallas as pl
from jax.experimental.pallas import tpu as pltpu

F32 = jnp.float32
BF16 = jnp.bfloat16
HIGHEST = lax.Precision.HIGHEST

GRID_W = 64
NORM_EPS = 1e-6
ROPE_BASE = 10000.0
LRU_WIDTH = 512
LRU_BLOCKS = 8
LRU_BLOCK_W = LRU_WIDTH // LRU_BLOCKS
LRU_C = 8.0
CONV_W = 4
MLA_HEADS = 4
MLA_Q_RANK = 384
MLA_KV_RANK = 256
MLA_NOPE = 128
MLA_ROPE = 64
MLA_V = 128
MLA_QK = MLA_NOPE + MLA_ROPE
DIFF_HEADS = 8
DIFF_HEAD_DIM = 64
DIFF_V = 2 * DIFF_HEAD_DIM
N_EXPERTS = 32
TOP_K = 4
SWIGLU_LIMIT = 7.0
SWIGLU_ALPHA = 1.702

LANES = 128
SUBLANES = 8
VMEM_LIMIT = 52 * 1024 * 1024

TM = 256
TQ = 512
LRU_CT = 128
LRU_CHUNK = 128
LRU_SEG = LRU_CHUNK // SUBLANES
LRU_PAD = 8
MOE_BM = 256
NEG_BIG = -1e30
LOG2_E = math.log2(math.e)


def _cparams(sem, vmem=VMEM_LIMIT):
    return pltpu.CompilerParams(dimension_semantics=sem, vmem_limit_bytes=vmem)


def _adaln_kernel(c_ref, w_ref, b_ref, o_ref):
    c = c_ref[...]
    s = c * jax.nn.sigmoid(c)
    o_ref[...] = jnp.dot(s, w_ref[...], preferred_element_type=F32, precision=HIGHEST) + b_ref[...]


def _adaln(cvec, w, b):
    rows, d = cvec.shape
    n = w.shape[1]
    tn = 1536
    return pl.pallas_call(
        _adaln_kernel,
        grid=(n // tn,),
        in_specs=[pl.BlockSpec((rows, d), lambda j: (0, 0)),
                  pl.BlockSpec((d, tn), lambda j: (0, j)),
                  pl.BlockSpec((1, tn), lambda j: (0, j))],
        out_specs=pl.BlockSpec((rows, tn), lambda j: (0, j)),
        out_shape=jax.ShapeDtypeStruct((rows, n), F32),
        compiler_params=_cparams(("arbitrary",)),
        name="adaln_mod",
    )(cvec, w, b.reshape(1, n))


def _norm_mod(x, g, sc, sh):
    var = jnp.mean(x * x, axis=-1, keepdims=True)
    y = x * lax.rsqrt(var + NORM_EPS) * g
    return y * (1.0 + sc) + sh


def _ones_column(shape):
    lane = lax.broadcasted_iota(jnp.int32, shape, 1)
    return jnp.where(lane == 0, 1.0, 0.0).astype(BF16)


def _rope_tile(x, cos, sin_a, sin_b):
    up = pltpu.roll(x, LANES - 16, axis=1)
    dn = pltpu.roll(x, 16, axis=1)
    return x * cos + up * sin_a + dn * sin_b


def _rope_tables(seq, rot_dim):
    n_rows = seq // GRID_W
    rows = jnp.repeat(jnp.arange(n_rows, dtype=F32), GRID_W)
    cols = jnp.tile(jnp.arange(GRID_W, dtype=F32), n_rows)
    axis_dim = rot_dim // 2
    inv_freq = ROPE_BASE ** (-jnp.arange(0, axis_dim, 2, dtype=F32) / axis_dim)
    ang_r = rows[:, None] * inv_freq
    ang_c = cols[:, None] * inv_freq
    ang = jnp.concatenate([ang_r, ang_r, ang_c, ang_c], axis=-1)
    cos, sin = jnp.cos(ang), jnp.sin(ang)
    quarter = rot_dim // 4
    first = (jnp.arange(rot_dim) % (2 * quarter)) < quarter
    sin_a = jnp.where(first, -sin, 0.0)
    sin_b = jnp.where(first, 0.0, sin)
    return cos, sin_a, sin_b


def _rope_tables_128(seq, ctx, rot_dim, tile_groups):
    cos, sin_a, sin_b = _rope_tables(seq, rot_dim)
    if tile_groups:
        reps = LANES // rot_dim
        cos, sin_a, sin_b = (jnp.tile(t, (1, reps)) for t in (cos, sin_a, sin_b))
    else:
        pad = LANES - rot_dim
        cos = jnp.concatenate([cos, jnp.ones((seq, pad), F32)], axis=-1)
        sin_a = jnp.concatenate([sin_a, jnp.zeros((seq, pad), F32)], axis=-1)
        sin_b = jnp.concatenate([sin_b, jnp.zeros((seq, pad), F32)], axis=-1)
    cos = jnp.concatenate([cos, jnp.ones((ctx, LANES), F32)], axis=0)
    sin_a = jnp.concatenate([sin_a, jnp.zeros((ctx, LANES), F32)], axis=0)
    sin_b = jnp.concatenate([sin_b, jnp.zeros((ctx, LANES), F32)], axis=0)
    return cos, sin_a, sin_b


def _mod_index(tpb, nb):
    def idx(i):
        return jnp.where(i % tpb == tpb - 1, nb, i // tpb)
    return idx


def _even_proj_kernel(x_ref, g_ref, sc_ref, sh_ref, wg_ref, wr_ref, wq_ref, wkv_ref, wkr_ref,
                      og_ref, or_ref, oq_ref, okv_ref, okr_ref):
    h = _norm_mod(x_ref[...], g_ref[...], sc_ref[...], sh_ref[...]).astype(BF16)
    for w_ref, o_ref in ((wg_ref, og_ref), (wr_ref, or_ref), (wq_ref, oq_ref),
                         (wkv_ref, okv_ref), (wkr_ref, okr_ref)):
        o_ref[...] = jnp.dot(h, w_ref[...], preferred_element_type=F32)


def _even_proj(x, gain, sc, sh, weights, tpb, nb):
    t, d = x.shape
    midx = _mod_index(tpb, nb)
    w_specs = [pl.BlockSpec(w.shape, lambda i: (0, 0)) for w in weights]
    return pl.pallas_call(
        _even_proj_kernel,
        grid=(t // TM,),
        in_specs=[pl.BlockSpec((TM, d), lambda i: (i, 0)),
                  pl.BlockSpec((1, d), lambda i: (0, 0)),
                  pl.BlockSpec((None, 1, d), lambda i: (midx(i), 0, 0)),
                  pl.BlockSpec((None, 1, d), lambda i: (midx(i), 0, 0))] + w_specs,
        out_specs=[pl.BlockSpec((TM, w.shape[1]), lambda i: (i, 0)) for w in weights],
        out_shape=[jax.ShapeDtypeStruct((t, w.shape[1]), F32) for w in weights],
        compiler_params=_cparams(("parallel",)),
        name="even_in_proj",
    )(x, gain, sc, sh, *weights)


def _mla_prep_kernel(qc_ref, kvc_ref, kr_ref, qng_ref, kvng_ref, wqn_ref, wqr_ref, wkv_ref,
                     qgn_ref, qgr_ref, kgn_ref, kgr_ref, cos_ref, sa_ref, sb_ref,
                     q_ref, k_ref, v_ref):
    cos, sa, sb = cos_ref[...], sa_ref[...], sb_ref[...]
    inv_qk = 1.0 / MLA_QK

    qc = qc_ref[...]
    hq = (qc * lax.rsqrt(jnp.mean(qc * qc, axis=-1, keepdims=True) + NORM_EPS) * qng_ref[...]).astype(BF16)
    q_nope = jnp.dot(hq, wqn_ref[...], preferred_element_type=F32)
    q_rope = jnp.dot(hq, wqr_ref[...], preferred_element_type=F32)
    q_scale = MLA_QK ** -0.5 * LOG2_E
    for h in range(MLA_HEADS):
        qn = q_nope[:, h * LANES:(h + 1) * LANES]
        qr = q_rope[:, h * LANES:(h + 1) * LANES]
        ms = (jnp.sum(qn * qn, axis=-1, keepdims=True) + jnp.sum(qr * qr, axis=-1, keepdims=True)) * inv_qk
        rs = lax.rsqrt(ms + NORM_EPS) * q_scale
        q_ref[:, (2 * h) * LANES:(2 * h + 1) * LANES] = (qn * rs * qgn_ref[...]).astype(BF16)
        q_ref[:, (2 * h + 1) * LANES:(2 * h + 2) * LANES] = _rope_tile(qr * rs * qgr_ref[...], cos, sa, sb).astype(BF16)

    kvc = kvc_ref[...]
    hkv = (kvc * lax.rsqrt(jnp.mean(kvc * kvc, axis=-1, keepdims=True) + NORM_EPS) * kvng_ref[...]).astype(BF16)
    kv = jnp.dot(hkv, wkv_ref[...], preferred_element_type=F32)
    kr = kr_ref[...]
    kr_ss = jnp.sum(kr * kr, axis=-1, keepdims=True)
    kr_rot = _rope_tile(kr * kgr_ref[...], cos, sa, sb)
    for h in range(MLA_HEADS):
        kn = kv[:, (2 * h) * LANES:(2 * h + 1) * LANES]
        ms = (jnp.sum(kn * kn, axis=-1, keepdims=True) + kr_ss) * inv_qk
        rs = lax.rsqrt(ms + NORM_EPS)
        k_ref[:, (2 * h) * LANES:(2 * h + 1) * LANES] = (kn * rs * kgn_ref[...]).astype(BF16)
        k_ref[:, (2 * h + 1) * LANES:(2 * h + 2) * LANES] = (kr_rot * rs).astype(BF16)
        v_ref[:, (2 * h) * LANES:(2 * h + 1) * LANES] = kv[:, (2 * h + 1) * LANES:(2 * h + 2) * LANES].astype(BF16)
        v_ref[:, (2 * h + 1) * LANES:(2 * h + 2) * LANES] = _ones_column((kv.shape[0], LANES))


def _mla_prep(qc, kvc, kr, q_norm_g, kv_norm_g, w_q_nope, w_q_rope, w_ukv, q_gn, q_gr, k_gn, k_gr,
              rope, tpb):
    t = qc.shape[0]
    cos, sa, sb = rope

    def const(a):
        return pl.BlockSpec(a.shape, lambda i: (0, 0))

    def rows(a):
        return pl.BlockSpec((TM, a.shape[1]), lambda i: (i, 0))

    tab = pl.BlockSpec((TM, LANES), lambda i: (i % tpb, 0))
    hq, hv = MLA_HEADS * 2 * LANES, MLA_HEADS * 2 * MLA_V
    return pl.pallas_call(
        _mla_prep_kernel,
        grid=(t // TM,),
        in_specs=[rows(qc), rows(kvc), rows(kr), const(q_norm_g), const(kv_norm_g),
                  const(w_q_nope), const(w_q_rope), const(w_ukv),
                  const(q_gn), const(q_gr), const(k_gn), const(k_gr), tab, tab, tab],
        out_specs=[pl.BlockSpec((TM, hq), lambda i: (i, 0)),
                   pl.BlockSpec((TM, hq), lambda i: (i, 0)),
                   pl.BlockSpec((TM, hv), lambda i: (i, 0))],
        out_shape=[jax.ShapeDtypeStruct((t, hq), BF16),
                   jax.ShapeDtypeStruct((t, hq), BF16),
                   jax.ShapeDtypeStruct((t, hv), BF16)],
        compiler_params=_cparams(("parallel",)),
        name="mla_prep",
    )(qc, kvc, kr, q_norm_g, kv_norm_g, w_q_nope, w_q_rope, w_ukv, q_gn, q_gr, k_gn, k_gr, cos, sa, sb)


def _sublane_iota():
    return lax.broadcasted_iota(jnp.int32, (SUBLANES, LANES), 0)


def _scan_chunk(a_chunk, u_chunk, h_chunk, carry, reverse):
    steps = range(LRU_SEG - 1, -1, -1) if reverse else range(LRU_SEG)
    h_loc, p_loc = [None] * LRU_SEG, [None] * LRU_SEG
    h = p = None
    for j in steps:
        a = a_chunk[pl.ds(j, SUBLANES, stride=LRU_SEG), :]
        u = u_chunk[pl.ds(j, SUBLANES, stride=LRU_SEG), :]
        if h is None:
            h, p = u, a
        else:
            h, p = a * h + u, a * p
        h_loc[j], p_loc[j] = h, p
    sub = _sublane_iota()
    seg_p, seg_h = p, h
    for d in (1, 2, 4):
        shift = SUBLANES - d if reverse else d
        prev_p = pltpu.roll(seg_p, shift, axis=0)
        prev_h = pltpu.roll(seg_h, shift, axis=0)
        valid = (sub < SUBLANES - d) if reverse else (sub >= d)
        seg_h = jnp.where(valid, seg_p * prev_h + seg_h, seg_h)
        seg_p = jnp.where(valid, seg_p * prev_p, seg_p)
    h_end = seg_h + seg_p * carry
    if reverse:
        h_in = jnp.where(sub == SUBLANES - 1, carry, pltpu.roll(h_end, SUBLANES - 1, axis=0))
        new_carry = h_end[0:1, :]
    else:
        h_in = jnp.where(sub == 0, carry, pltpu.roll(h_end, 1, axis=0))
        new_carry = h_end[SUBLANES - 1:SUBLANES, :]
    for j in range(LRU_SEG):
        h_chunk[pl.ds(j, SUBLANES, stride=LRU_SEG), :] = h_loc[j] + p_loc[j] * h_in
    return jnp.broadcast_to(new_carry, (SUBLANES, LANES))


def _lru_kernel(g_ref, r_ref, cw_ref, cb_ref, wg_ref, bg_ref, lam_ref, o_ref,
                rp_ref, af_ref, uf_ref, ab_ref, ub_ref, hf_ref, hb_ref, *, ctx, seq):
    n_ctx, n_lat = ctx // LRU_CHUNK, seq // LRU_CHUNK
    zeros_pad = jnp.zeros((LRU_PAD, LRU_CT), F32)
    ctx0 = seq + 2 * LRU_PAD
    rp_ref[0:LRU_PAD, :] = zeros_pad
    rp_ref[LRU_PAD:LRU_PAD + seq, :] = r_ref[0:seq, :]
    rp_ref[LRU_PAD + seq:ctx0, :] = zeros_pad
    rp_ref[ctx0:ctx0 + ctx, :] = r_ref[seq:seq + ctx, :]
    rp_ref[ctx0 + ctx:ctx0 + ctx + LRU_PAD, :] = zeros_pad

    cw = cw_ref[...]
    cb = cb_ref[...]
    wg = wg_ref[...]
    bg = bg_ref[...]
    lam = lam_ref[...]
    sp = jnp.maximum(-lam, 0.0) + jnp.log1p(jnp.exp(-jnp.abs(lam)))

    def coeff_chunk(c, pad_off, row_off):
        start = pl.multiple_of(pad_off + c * LRU_CHUNK, SUBLANES)
        ext = rp_ref[pl.ds(start, LRU_CHUNK + 2 * LRU_PAD), :]
        x = cb
        for tap in range(CONV_W):
            lo = LRU_PAD - 2 + tap
            x = x + ext[lo:lo + LRU_CHUNK, :] * cw[tap:tap + 1, :]
        gates = jnp.dot(x.astype(BF16), wg, preferred_element_type=F32) + bg
        out_row = pl.multiple_of(row_off + c * LRU_CHUNK, SUBLANES)
        for d, (a_ref, u_ref) in enumerate(((af_ref, uf_ref), (ab_ref, ub_ref))):
            r = jax.nn.sigmoid(gates[:, (2 * d) * LRU_CT:(2 * d + 1) * LRU_CT])
            i = jax.nn.sigmoid(gates[:, (2 * d + 1) * LRU_CT:(2 * d + 2) * LRU_CT])
            log_a = -LRU_C * r * sp[d:d + 1, :]
            a_ref[pl.ds(out_row, LRU_CHUNK), :] = jnp.exp(log_a)
            th = jnp.tanh(log_a)
            u_ref[pl.ds(out_row, LRU_CHUNK), :] = jnp.sqrt(-2.0 * th / (1.0 - th)) * i * x
        return None

    def coeff_ctx(c, _):
        coeff_chunk(c, seq + LRU_PAD, seq)
        return 0

    def coeff_lat(c, _):
        coeff_chunk(c, 0, 0)
        return 0

    lax.fori_loop(0, n_ctx, coeff_ctx, 0)
    lax.fori_loop(0, n_lat, coeff_lat, 0)

    def scan_pair(n, row_off):
        def body(c, carry):
            cf, cb_ = carry
            f_row = pl.multiple_of(row_off + c * LRU_CHUNK, SUBLANES)
            b_row = pl.multiple_of(row_off + (n - 1 - c) * LRU_CHUNK, SUBLANES)
            cf = _scan_chunk(af_ref.at[pl.ds(f_row, LRU_CHUNK), :], uf_ref.at[pl.ds(f_row, LRU_CHUNK), :],
                             hf_ref.at[pl.ds(f_row, LRU_CHUNK), :], cf, False)
            cb_ = _scan_chunk(ab_ref.at[pl.ds(b_row, LRU_CHUNK), :], ub_ref.at[pl.ds(b_row, LRU_CHUNK), :],
                              hb_ref.at[pl.ds(b_row, LRU_CHUNK), :], cb_, True)
            return cf, cb_
        return body

    zero = jnp.zeros((SUBLANES, LANES), F32)
    carry = lax.fori_loop(0, n_ctx, scan_pair(n_ctx, seq), (zero, zero))
    lax.fori_loop(0, n_lat, scan_pair(n_lat, 0), carry)

    g = g_ref[...]
    gelu = 0.5 * g * (1.0 + jnp.tanh(math.sqrt(2.0 / math.pi) * (g + 0.044715 * (g * g * g))))
    o_ref[...] = (gelu * (hf_ref[...] + hb_ref[...])).astype(BF16)


def _lru(g, r, conv_w, conv_b, w_gates, b_gates, lam, nb, ctx, seq):
    l = ctx + seq
    width = g.shape[1]
    n_ct = width // LRU_CT
    g3 = g.reshape(nb, l, width)
    r3 = r.reshape(nb, l, width)
    seq_spec = pl.BlockSpec((None, l, LRU_CT), lambda b, c: (b, 0, c))
    scratch = [pltpu.VMEM((l + 3 * LRU_PAD, LRU_CT), F32)] + [pltpu.VMEM((l, LRU_CT), F32)] * 6
    out = pl.pallas_call(
        functools.partial(_lru_kernel, ctx=ctx, seq=seq),
        grid=(nb, n_ct),
        in_specs=[seq_spec, seq_spec,
                  pl.BlockSpec((CONV_W, LRU_CT), lambda b, c: (0, c)),
                  pl.BlockSpec((1, LRU_CT), lambda b, c: (0, c)),
                  pl.BlockSpec((None, LRU_CT, 4 * LRU_CT), lambda b, c: (c, 0, 0)),
                  pl.BlockSpec((None, 1, 4 * LRU_CT), lambda b, c: (c, 0, 0)),
                  pl.BlockSpec((2, LRU_CT), lambda b, c: (0, c))],
        out_specs=seq_spec,
        out_shape=jax.ShapeDtypeStruct((nb, l, width), BF16),
        scratch_shapes=scratch,
        compiler_params=_cparams(("parallel", "parallel")),
        name="rglru",
    )(g3, r3, conv_w, conv_b, w_gates, b_gates, lam)
    return out.reshape(nb * l, width)


def _lru_gate_weights(wa, ba, wx, bx):
    per = LRU_CT // LRU_BLOCK_W
    n_ct = LRU_BLOCKS // per
    eye = jnp.eye(per, dtype=F32)

    def dense(w):
        w4 = w.reshape(n_ct, per, LRU_BLOCK_W, LRU_BLOCK_W)
        return jnp.einsum('cide,ij->cidje', w4, eye).reshape(n_ct, LRU_CT, LRU_CT)

    w = jnp.concatenate([dense(wa[0]), dense(wx[0]), dense(wa[1]), dense(wx[1])], axis=-1)
    b = jnp.concatenate([v.reshape(n_ct, 1, LRU_CT) for v in (ba[0], bx[0], ba[1], bx[1])], axis=-1)
    return w.astype(BF16), b


def _key_spans(n_keys):
    half = n_keys // 2 if n_keys % (2 * LANES) == 0 else n_keys
    return [(lo, lo + half) for lo in range(0, n_keys, half)]


def _attn_scores(q_ops, kt_ref, s_ref, m_ref):
    for t, q in enumerate(q_ops):
        for lo, hi in _key_spans(kt_ref.shape[-1]):
            s_ref[t, :, lo:hi] = jnp.dot(q, kt_ref[:, lo:hi], preferred_element_type=F32)
        m_ref[t] = jnp.max(s_ref[t], axis=-1, keepdims=True)


def _attn_values(v_ref, s_ref, m_ref, p_ref):
    accs = []
    for t in range(s_ref.shape[0]):
        m = m_ref[t]
        acc = None
        for lo, hi in _key_spans(s_ref.shape[-1]):
            p_ref[t, :, lo:hi] = jnp.exp2(s_ref[t, :, lo:hi] - m).astype(BF16)
            part = jnp.dot(p_ref[t, :, lo:hi], v_ref[lo:hi, :], preferred_element_type=F32)
            acc = part if acc is None else acc + part
        accs.append(acc)
    return accs


def _attn_pipeline(q_ref, kt_ref, v_ref, o_ref, bufs, p_ref, q_ops_fn, values_fn):
    s0, m0 = bufs[0]
    tq = s0.shape[1]
    n = q_ref.shape[0] // tq

    def scores(t, s_ref, m_ref):
        row = t * tq if isinstance(t, int) else pl.multiple_of(t * tq, tq)
        _attn_scores(q_ops_fn(q_ref[pl.ds(row, tq), :]), kt_ref, s_ref, m_ref)

    def values(t, s_ref, m_ref):
        row = t * tq if isinstance(t, int) else pl.multiple_of(t * tq, tq)
        o_ref[pl.ds(row, tq), :] = values_fn(v_ref, s_ref, m_ref, p_ref).astype(o_ref.dtype)

    if len(bufs) == 1:
        def one(t, _):
            scores(t, s0, m0)
            values(t, s0, m0)
            return 0

        if n == 1:
            one(0, 0)
        else:
            lax.fori_loop(0, n, one, 0)
        return

    s1, m1 = bufs[1]
    scores(0, s0, m0)
    if n == 1:
        values(0, s0, m0)
        return
    assert n % 2 == 0

    def pair(k, _):
        t = 2 * k
        scores(t + 1, s1, m1)
        values(t, s0, m0)
        scores(t + 2, s0, m0)
        values(t + 1, s1, m1)
        return 0

    lax.fori_loop(0, n // 2 - 1, pair, 0)
    scores(n - 1, s1, m1)
    values(n - 2, s0, m0)
    values(n - 1, s1, m1)


def _score_bufs(scratch):
    return tuple(zip(scratch[0:-1:2], scratch[1:-1:2])), scratch[-1]


def _mla_attn_kernel(q_ref, kt_ref, v_ref, o_ref, *scratch):
    dv = o_ref.shape[-1]
    bufs, p_ref = _score_bufs(scratch)

    def values(*refs):
        acc = _attn_values(*refs)[0]
        return acc[:, :dv] / acc[:, dv:dv + 1]

    _attn_pipeline(q_ref, kt_ref, v_ref, o_ref, bufs, p_ref, lambda q: [q], values)


def _diff_attn_kernel(q_ref, kt_ref, v_ref, g_ref, lam_ref, o_ref, *scratch, out_scale):
    dv = o_ref.shape[-1]
    bufs, p_ref = _score_bufs(scratch)

    def q_ops(q):
        lane = lax.broadcasted_iota(jnp.int32, q.shape, 1)
        zero = jnp.zeros_like(q)
        return [jnp.where(lane < DIFF_HEAD_DIM, q, zero), jnp.where(lane < DIFF_HEAD_DIM, zero, q)]

    def values(*refs):
        a1, a2 = _attn_values(*refs)
        o = a1[:, :dv] / a1[:, dv:dv + 1] - lam_ref[0] * (a2[:, :dv] / a2[:, dv:dv + 1])
        y = o * lax.rsqrt(jnp.mean(o * o, axis=-1, keepdims=True) + NORM_EPS) * g_ref[...]
        return y * out_scale

    _attn_pipeline(q_ref, kt_ref, v_ref, o_ref, bufs, p_ref, q_ops, values)


def _attention(kernel_fn, name, n_softmax, q, kt, v, nb, ctx, seq, heads, dk, dv, ctx_queries,
               extra_in=(), extra_specs=()):
    l = ctx + seq
    q3 = q.reshape(nb, l, heads * dk)
    v3 = v.reshape(nb, l, heads * 2 * dv)
    if ctx_queries:
        tq, n_keys, rows = ctx, ctx, ctx
        blk = seq // ctx
        in_specs = [pl.BlockSpec((None, rows, dk), lambda b, h: (b, blk, h)),
                    pl.BlockSpec((None, None, dk, ctx), lambda b, h: (b, h, 0, blk)),
                    pl.BlockSpec((None, ctx, 2 * dv), lambda b, h: (b, blk, h))]
    else:
        tq, n_keys, rows = TQ, l, seq
        in_specs = [pl.BlockSpec((None, rows, dk), lambda b, h: (b, 0, h)),
                    pl.BlockSpec((None, None, dk, l), lambda b, h: (b, h, 0, 0)),
                    pl.BlockSpec((None, l, 2 * dv), lambda b, h: (b, 0, h))]
    score_buf = [pltpu.VMEM((n_softmax, tq, n_keys), F32), pltpu.VMEM((n_softmax, tq, 1), F32)]
    out = pl.pallas_call(
        kernel_fn,
        grid=(nb, heads),
        in_specs=in_specs + list(extra_specs),
        out_specs=pl.BlockSpec((None, rows, dv), lambda b, h: (b, 0, h)),
        out_shape=jax.ShapeDtypeStruct((nb, rows, heads * dv), BF16),
        scratch_shapes=score_buf * (2 if n_softmax == 1 else 1) + [pltpu.VMEM((n_softmax, tq, n_keys), BF16)],
        compiler_params=_cparams(("parallel", "parallel")),
        name=name,
    )(q3, kt, v3, *extra_in)
    return out


def _key_transpose(k, nb, l, heads, dk):
    return jnp.transpose(k.reshape(nb, l, heads, dk), (0, 2, 3, 1))


def _attend_all(kernel_fn, name, n_softmax, q, k, v, nb, ctx, seq, heads, dk, dv, need_ctx, **extra):
    kt = _key_transpose(k, nb, ctx + seq, heads, dk)
    lat = _attention(kernel_fn, name, n_softmax, q, kt, v, nb, ctx, seq, heads, dk, dv, False, **extra)
    if not need_ctx:
        return lat.reshape(nb * seq, heads * dv)
    cx = _attention(kernel_fn, name + "_ctx", n_softmax, q, kt, v, nb, ctx, seq, heads, dk, dv, True, **extra)
    return jnp.concatenate([lat, cx], axis=1).reshape(nb * (ctx + seq), heads * dv)


def _group_ms(x, gmat):
    return jnp.dot((x * x).astype(BF16), gmat, preferred_element_type=F32)


def _odd_proj_kernel(x_ref, g_ref, sc_ref, sh_ref, wq_ref, wk_ref, wv_ref, qg_ref, kg_ref, gm_ref,
                     cos_ref, sa_ref, sb_ref, q_ref, k_ref, v_ref):
    h = _norm_mod(x_ref[...], g_ref[...], sc_ref[...], sh_ref[...]).astype(BF16)
    cos, sa, sb = cos_ref[...], sa_ref[...], sb_ref[...]
    gmat = gm_ref[...]
    q_scale = DIFF_HEAD_DIM ** -0.5 * LOG2_E
    for w_ref, gain_ref, o_ref, scale in ((wq_ref, qg_ref, q_ref, q_scale), (wk_ref, kg_ref, k_ref, 1.0)):
        raw = jnp.dot(h, w_ref[...], preferred_element_type=F32)
        gain = gain_ref[...] * scale
        for c in range(raw.shape[1] // LANES):
            t = raw[:, c * LANES:(c + 1) * LANES]
            t = t * lax.rsqrt(_group_ms(t, gmat) + NORM_EPS) * gain
            o_ref[:, c * LANES:(c + 1) * LANES] = _rope_tile(t, cos, sa, sb).astype(BF16)
    v = jnp.dot(h, wv_ref[...], preferred_element_type=F32).astype(BF16)
    for c in range(v.shape[1] // DIFF_V):
        v_ref[:, (2 * c) * DIFF_V:(2 * c + 1) * DIFF_V] = v[:, c * DIFF_V:(c + 1) * DIFF_V]
        v_ref[:, (2 * c + 1) * DIFF_V:(2 * c + 2) * DIFF_V] = _ones_column((v.shape[0], DIFF_V))


def _odd_proj(x, gain, sc, sh, wq, wk, wv, q_gain, k_gain, gmat, rope, tpb, nb):
    t, d = x.shape
    midx = _mod_index(tpb, nb)
    cos, sa, sb = rope

    def const(a):
        return pl.BlockSpec(a.shape, lambda i: (0, 0))

    tab = pl.BlockSpec((TM, LANES), lambda i: (i % tpb, 0))
    n = wq.shape[1]
    return pl.pallas_call(
        _odd_proj_kernel,
        grid=(t // TM,),
        in_specs=[pl.BlockSpec((TM, d), lambda i: (i, 0)), const(gain),
                  pl.BlockSpec((None, 1, d), lambda i: (midx(i), 0, 0)),
                  pl.BlockSpec((None, 1, d), lambda i: (midx(i), 0, 0)),
                  const(wq), const(wk), const(wv), const(q_gain), const(k_gain), const(gmat),
                  tab, tab, tab],
        out_specs=[pl.BlockSpec((TM, n), lambda i: (i, 0))] * 2 + [pl.BlockSpec((TM, 2 * n), lambda i: (i, 0))],
        out_shape=[jax.ShapeDtypeStruct((t, n), BF16)] * 2 + [jax.ShapeDtypeStruct((t, 2 * n), BF16)],
        compiler_params=_cparams(("parallel",)),
        name="odd_in_proj",
    )(x, gain, sc, sh, wq, wk, wv, q_gain, k_gain, gmat, cos, sa, sb)


def _post_mix_kernel(*refs, n_mix):
    mix_refs = refs[:n_mix]
    w_refs = refs[n_mix:2 * n_mix]
    (x_ref, g1_ref, n2_ref, sc_ref, sh_ref, rwh_ref, rwl_ref, rb_ref,
     xo_ref, h2_ref, te_ref, tw_ref, tr_ref, cnt_ref, base_ref) = refs[2 * n_mix:]
    i = pl.program_id(0)

    @pl.when(i == 0)
    def _():
        base_ref[...] = jnp.zeros_like(base_ref)

    d = x_ref.shape[-1]
    halves = []
    for lo in (0, d // 2):
        m = jnp.dot(mix_refs[0][...], w_refs[0][:, lo:lo + d // 2], preferred_element_type=F32)
        for a_ref, w_ref in zip(mix_refs[1:], w_refs[1:]):
            m = m + jnp.dot(a_ref[...], w_ref[:, lo:lo + d // 2], preferred_element_type=F32)
        halves.append(m)
    x = x_ref[...] + g1_ref[...] * jnp.concatenate(halves, axis=-1)
    xo_ref[...] = x
    h2 = _norm_mod(x, n2_ref[...], sc_ref[...], sh_ref[...])
    h2_hi = h2.astype(BF16)
    h2_ref[...] = h2_hi

    h2_lo = (h2 - h2_hi.astype(F32)).astype(BF16)
    logits = (jnp.dot(h2_hi, rwh_ref[...], preferred_element_type=F32)
              + jnp.dot(h2_lo, rwh_ref[...], preferred_element_type=F32)
              + jnp.dot(h2_hi, rwl_ref[...], preferred_element_type=F32)) + rb_ref[...]
    lane = lax.broadcasted_iota(jnp.int32, logits.shape, 1)
    lane_f = lane.astype(F32)
    vals = jnp.where(lane < N_EXPERTS, logits, NEG_BIG)
    tops, hots = [], []
    for _ in range(TOP_K):
        top = jnp.max(vals, axis=-1, keepdims=True)
        first = jnp.min(jnp.where(vals == top, lane_f, float(LANES)), axis=-1, keepdims=True)
        hot = lane_f == first
        tops.append(top)
        hots.append(hot)
        vals = jnp.where(hot, 2.0 * NEG_BIG, vals)
    exps = [jnp.exp(t - tops[0]) for t in tops]
    denom = exps[0] + exps[1] + exps[2] + exps[3]

    picked = (hots[0] | hots[1]) | (hots[2] | hots[3])
    cnt = jnp.where(picked, 1.0, 0.0)
    r_io = lax.broadcasted_iota(jnp.int32, (TM, TM), 0)
    c_io = lax.broadcasted_iota(jnp.int32, (TM, TM), 1)
    tri = jnp.where(c_io < r_io, 1.0, 0.0).astype(BF16)
    before = jnp.dot(tri, cnt.astype(BF16), preferred_element_type=F32) + base_ref[...]
    te = jnp.zeros(logits.shape, F32)
    tw = jnp.zeros(logits.shape, F32)
    tr = jnp.zeros(logits.shape, F32)
    for k in range(TOP_K):
        e_k = jnp.sum(jnp.where(hots[k], lane_f, 0.0), axis=-1, keepdims=True)
        r_k = jnp.sum(jnp.where(hots[k], before, 0.0), axis=-1, keepdims=True)
        te = jnp.where(lane == k, e_k, te)
        tw = jnp.where(lane == k, exps[k] / denom, tw)
        tr = jnp.where(lane == k, r_k, tr)
    te_ref[...] = te.astype(jnp.int32)
    tw_ref[...] = tw
    tr_ref[...] = tr.astype(jnp.int32)
    base_ref[...] += jnp.sum(cnt, axis=0, keepdims=True)
    cnt_ref[...] = base_ref[...]


def _post_mix(mixes, weights, x, g1, n2g, sc2, sh2, router_w, router_b, tpb, nb, latent_only):
    d = x.shape[1]
    tiles_in = tpb
    if latent_only:
        tiles_out = tpb - 1
        n_tiles = nb * tiles_out

        def in_row(i):
            return (i // tiles_out) * tiles_in + i % tiles_out

        def midx(i):
            return i // tiles_out
    else:
        n_tiles = nb * tpb

        def in_row(i):
            return i

        midx = _mod_index(tpb, nb)
    t_out = n_tiles * TM

    def const(a):
        return pl.BlockSpec(a.shape, lambda i: (0, 0))

    def mod(a):
        return pl.BlockSpec((None, 1, d), lambda i: (midx(i), 0, 0))

    in_specs = [pl.BlockSpec((TM, a.shape[1]), lambda i: (i, 0)) for a in mixes]
    in_specs += [const(w) for w in weights]
    in_specs += [pl.BlockSpec((TM, d), lambda i: (in_row(i), 0)), mod(g1), const(n2g), mod(sc2), mod(sh2),
                 const(router_w), const(router_w), const(router_b)]
    rw_hi = router_w.astype(BF16)
    rw_lo = (router_w - rw_hi.astype(F32)).astype(BF16)
    row_out = lambda width: pl.BlockSpec((TM, width), lambda i: (i, 0))
    return pl.pallas_call(
        functools.partial(_post_mix_kernel, n_mix=len(mixes)),
        grid=(n_tiles,),
        in_specs=in_specs,
        out_specs=[row_out(d), row_out(d), row_out(LANES), row_out(LANES), row_out(LANES),
                   pl.BlockSpec((1, LANES), lambda i: (0, 0))],
        out_shape=[jax.ShapeDtypeStruct((t_out, d), F32), jax.ShapeDtypeStruct((t_out, d), BF16),
                   jax.ShapeDtypeStruct((t_out, LANES), jnp.int32), jax.ShapeDtypeStruct((t_out, LANES), F32),
                   jax.ShapeDtypeStruct((t_out, LANES), jnp.int32), jax.ShapeDtypeStruct((1, LANES), F32)],
        scratch_shapes=[pltpu.VMEM((1, LANES), F32)],
        compiler_params=_cparams(("arbitrary",)),
        name="post_mix_router",
    )(*mixes, *weights, x, g1, n2g, sc2, sh2, rw_hi, rw_lo, router_b)


def _moe_ffn_kernel(be_ref, nu_ref, x_ref, wgu_ref, bgu_ref, wd_ref, bd_ref, o_ref, wgu_bf, wd_bf):
    i = pl.program_id(0)
    prev = be_ref[jnp.maximum(i - 1, 0)]
    fresh = jnp.logical_or(i == 0, be_ref[i] != prev)

    @pl.when(jnp.logical_and(fresh, i < nu_ref[0]))
    def _():
        wgu_bf[...] = wgu_ref[...].astype(BF16)
        wd_bf[...] = wd_ref[...].astype(BF16)

    @pl.when(i < nu_ref[0])
    def _():
        d_ff = wd_bf.shape[0]
        gu = jnp.dot(x_ref[...], wgu_bf[...], preferred_element_type=F32) + bgu_ref[...]
        g = jnp.minimum(gu[:, :d_ff], SWIGLU_LIMIT)
        u = jnp.clip(gu[:, d_ff:], -SWIGLU_LIMIT, SWIGLU_LIMIT)
        act = (u + 1.0) * g * jax.nn.sigmoid(SWIGLU_ALPHA * g)
        y = jnp.dot(act.astype(BF16), wd_bf[...], preferred_element_type=F32) + bd_ref[...]
        o_ref[...] = y.astype(o_ref.dtype)

    @pl.when(i >= nu_ref[0])
    def _():
        o_ref[...] = jnp.zeros_like(o_ref)


def _moe_ffn(block_e, n_used, xs, w_gu, b_gu, w_down, b_down, layer):
    n_rows, d = xs.shape
    depth, n_e, _, two_ff = w_gu.shape
    d_ff = two_ff // 2
    n_blocks = n_rows // MOE_BM
    grid_spec = pltpu.PrefetchScalarGridSpec(
        num_scalar_prefetch=2,
        grid=(n_blocks,),
        in_specs=[pl.BlockSpec((MOE_BM, d), lambda i, be, nu: (i, 0)),
                  pl.BlockSpec((None, None, d, two_ff), lambda i, be, nu: (layer, be[i], 0, 0)),
                  pl.BlockSpec((None, None, 1, two_ff), lambda i, be, nu: (layer, be[i], 0, 0)),
                  pl.BlockSpec((None, None, d_ff, d), lambda i, be, nu: (layer, be[i], 0, 0)),
                  pl.BlockSpec((None, None, 1, d), lambda i, be, nu: (layer, be[i], 0, 0))],
        out_specs=pl.BlockSpec((MOE_BM, d), lambda i, be, nu: (i, 0)),
        scratch_shapes=[pltpu.VMEM((d, two_ff), BF16), pltpu.VMEM((d_ff, d), BF16)],
    )
    return pl.pallas_call(
        _moe_ffn_kernel,
        grid_spec=grid_spec,
        out_shape=jax.ShapeDtypeStruct((n_rows, d), BF16),
        compiler_params=_cparams(("arbitrary",)),
        name="moe_ffn",
    )(block_e, n_used, xs, w_gu, b_gu.reshape(depth, n_e, 1, two_ff), w_down, b_down.reshape(depth, n_e, 1, d))


def _moe_combine_kernel(x_ref, g2_ref, w_ref, y_ref, o_ref):
    w = w_ref[...]
    f = w[:, 0:1] * y_ref[0].astype(F32)
    for k in range(1, TOP_K):
        f = f + w[:, k:k + 1] * y_ref[k].astype(F32)
    o_ref[...] = x_ref[...] + g2_ref[...] * f


def _moe_combine(x, g2, top_w, picked, midx):
    t, d = x.shape
    return pl.pallas_call(
        _moe_combine_kernel,
        grid=(t // TM,),
        in_specs=[pl.BlockSpec((TM, d), lambda i: (i, 0)),
                  pl.BlockSpec((None, 1, d), lambda i: (midx(i), 0, 0)),
                  pl.BlockSpec((TM, LANES), lambda i: (i, 0)),
                  pl.BlockSpec((TOP_K, TM, d), lambda i: (0, i, 0))],
        out_specs=pl.BlockSpec((TM, d), lambda i: (i, 0)),
        out_shape=jax.ShapeDtypeStruct((t, d), F32),
        compiler_params=_cparams(("parallel",)),
        name="moe_combine",
    )(x, g2, top_w, picked)


def _moe(x, g2, midx, h2, top_e, top_w, top_r, counts, w_gu, b_gu, w_down, b_down, layer):
    t, d = h2.shape
    n_assign = t * TOP_K
    n_blocks = -(-n_assign // MOE_BM) + N_EXPERTS
    n_rows = n_blocks * MOE_BM
    cnt = counts[0, :N_EXPERTS].astype(jnp.int32)
    padded = (cnt + MOE_BM - 1) // MOE_BM * MOE_BM
    pends = jnp.cumsum(padded)
    pstarts = pends - padded
    e = top_e[:, :TOP_K]
    expert_ids = jnp.arange(N_EXPERTS, dtype=jnp.int32)
    pstart_of_pick = jnp.sum(jnp.where(e[:, :, None] == expert_ids, pstarts, 0), axis=-1)
    dest = pstart_of_pick + top_r[:, :TOP_K]
    block_start = jnp.arange(n_blocks, dtype=jnp.int32) * MOE_BM
    block_e = jnp.sum((block_start[:, None] >= pends[None, :]).astype(jnp.int32), axis=1)
    block_e = jnp.minimum(block_e, N_EXPERTS - 1)
    n_used = (pends[-1] // MOE_BM).astype(jnp.int32).reshape(1)

    tok = jnp.broadcast_to(jnp.arange(t, dtype=jnp.int32)[:, None], (t, TOP_K))
    row_tok = jnp.arange(n_rows, dtype=jnp.int32).at[dest.reshape(-1)].set(
        tok.reshape(-1), unique_indices=True, mode='promise_in_bounds')
    h2_big = jnp.concatenate([h2, jnp.zeros((n_rows - t, d), h2.dtype)], axis=0)
    xs = h2_big[row_tok]
    ys = _moe_ffn(block_e, n_used, xs, w_gu, b_gu, w_down, b_down, layer)
    picked = ys[dest.T.reshape(-1)].reshape(TOP_K, t, d)
    return _moe_combine(x, g2, top_w, picked, midx)


def _pad_cols(w, width):
    return jnp.concatenate([w, jnp.zeros((w.shape[0], width - w.shape[1]), w.dtype)], axis=1)


def kernel(x, c, ctx, c_ctx, mod_w, mod_b, norm1_g, norm2_g, ev_w_in, ev_w_out, lru_conv_w, lru_conv_b, lru_wa, lru_ba, lru_wx, lru_bx, lru_lambda, mla_q_norm_g, mla_w_uq, mla_kv_norm_g, mla_w_ukv, mla_qn_g, mla_kn_g, od_w_in, od_w_out, diff_qn_g, diff_kn_g, diff_lq1, diff_lk1, diff_lq2, diff_lk2, diff_subln_g, router_w, router_b, moe_w_gu, moe_b_gu, moe_w_down, moe_b_down):
    nb, seq, d = x.shape
    n_ctx = ctx.shape[1]
    depth = mod_w.shape[0]
    l = n_ctx + seq
    tpb = l // TM
    assert n_ctx == TM and seq % TQ == 0 and seq % GRID_W == 0

    xa = jnp.concatenate([x, ctx], axis=1).reshape(nb * l, d)
    cvec = jnp.concatenate([c, c_ctx[None, :], jnp.zeros((SUBLANES - nb - 1, d), F32)], axis=0)
    router_w_p = jnp.concatenate([router_w, jnp.zeros((depth, d, LANES - N_EXPERTS), F32)], axis=-1)
    router_b_p = jnp.concatenate([router_b, jnp.zeros((depth, LANES - N_EXPERTS), F32)], axis=-1)

    for layer in range(depth):
        last = layer == depth - 1
        i = layer // 2
        mod = _adaln(cvec, mod_w[layer], mod_b[layer])[:nb + 1]
        sh1, sc1, g1, sh2, sc2, g2 = (mod[:, k * d:(k + 1) * d].reshape(nb + 1, 1, d) for k in range(6))
        n1g = norm1_g[layer].reshape(1, d)
        n2g = norm2_g[layer].reshape(1, d)

        if layer % 2 == 0:
            w_in = ev_w_in[i].astype(BF16)
            o = 2 * LRU_WIDTH
            splits = [w_in[:, :LRU_WIDTH], w_in[:, LRU_WIDTH:o], w_in[:, o:o + MLA_Q_RANK],
                      w_in[:, o + MLA_Q_RANK:o + MLA_Q_RANK + MLA_KV_RANK],
                      _pad_cols(w_in[:, o + MLA_Q_RANK + MLA_KV_RANK:], LANES)]
            gl, rl, qc, kvc, kr = _even_proj(xa, n1g, sc1, sh1, splits, tpb, nb)

            w_gates, b_gates = _lru_gate_weights(lru_wa[i], lru_ba[i], lru_wx[i], lru_bx[i])
            lru = _lru(gl, rl, lru_conv_w[i], lru_conv_b[i].reshape(1, LRU_WIDTH), w_gates, b_gates,
                       lru_lambda[i], nb, n_ctx, seq)

            w_uq = mla_w_uq[i].astype(BF16).reshape(MLA_Q_RANK, MLA_HEADS, MLA_QK)
            w_q_nope = w_uq[:, :, :MLA_NOPE].reshape(MLA_Q_RANK, MLA_HEADS * MLA_NOPE)
            w_q_rope = jnp.concatenate(
                [w_uq[:, :, MLA_NOPE:], jnp.zeros((MLA_Q_RANK, MLA_HEADS, LANES - MLA_ROPE), BF16)],
                axis=-1).reshape(MLA_Q_RANK, MLA_HEADS * LANES)
            pad_g = lambda g: jnp.concatenate([g, jnp.zeros((LANES - MLA_ROPE,), F32)]).reshape(1, LANES)
            rope = _rope_tables_128(seq, n_ctx, MLA_ROPE, tile_groups=False)
            q, k, v = _mla_prep(qc, kvc, kr, mla_q_norm_g[i].reshape(1, -1), mla_kv_norm_g[i].reshape(1, -1),
                                w_q_nope, w_q_rope, mla_w_ukv[i].astype(BF16),
                                mla_qn_g[i][:MLA_NOPE].reshape(1, LANES), pad_g(mla_qn_g[i][MLA_NOPE:]),
                                mla_kn_g[i][:MLA_NOPE].reshape(1, LANES), pad_g(mla_kn_g[i][MLA_NOPE:]),
                                rope, tpb)
            att = _attend_all(_mla_attn_kernel, "mla_attention", 1, q, k, v, nb, n_ctx, seq, MLA_HEADS,
                              2 * LANES, MLA_V, need_ctx=not last)
            w_out = ev_w_out[i].astype(BF16)
            if last:
                lru = lru.reshape(nb, l, LRU_WIDTH)[:, :seq].reshape(nb * seq, LRU_WIDTH)
            mixes = [lru, att]
            weights = [w_out[:LRU_WIDTH], w_out[LRU_WIDTH:]]
        else:
            lam_init = 0.8 - 0.6 * math.exp(-0.3 * layer)
            w_in = od_w_in[i].astype(BF16)
            n_qk = DIFF_HEADS * 2 * DIFF_HEAD_DIM
            gidx = jnp.arange(LANES) // DIFF_HEAD_DIM
            gmat = jnp.where(gidx[:, None] == gidx[None, :], 1.0 / DIFF_HEAD_DIM, 0.0).astype(BF16)
            rope = _rope_tables_128(seq, n_ctx, DIFF_HEAD_DIM, tile_groups=True)
            tile_g = lambda g: jnp.tile(g, LANES // DIFF_HEAD_DIM).reshape(1, LANES)
            q, k, v = _odd_proj(xa, n1g, sc1, sh1, w_in[:, :n_qk], w_in[:, n_qk:2 * n_qk], w_in[:, 2 * n_qk:],
                                tile_g(diff_qn_g[i]), tile_g(diff_kn_g[i]), gmat, rope, tpb, nb)
            lam = (jnp.exp(jnp.sum(diff_lq1[i] * diff_lk1[i])) - jnp.exp(jnp.sum(diff_lq2[i] * diff_lk2[i]))
                   + lam_init).reshape(1).astype(F32)
            att = _attend_all(
                functools.partial(_diff_attn_kernel, out_scale=1.0 - lam_init), "diff_attention", 2,
                q, k, v, nb, n_ctx, seq, DIFF_HEADS, 2 * DIFF_HEAD_DIM, DIFF_V, need_ctx=not last,
                extra_in=(diff_subln_g[i].reshape(1, DIFF_V), lam),
                extra_specs=(pl.BlockSpec((1, DIFF_V), lambda b, h: (0, 0)),
                             pl.BlockSpec(memory_space=pltpu.SMEM)))
            mixes = [att]
            weights = [od_w_out[i].astype(BF16)]

        xo, h2, te, tw, tr, counts = _post_mix(mixes, weights, xa, g1, n2g, sc2, sh2, router_w_p[layer],
                                               router_b_p[layer].reshape(1, LANES), tpb, nb, last)
        midx = (lambda t: t // (tpb - 1)) if last else _mod_index(tpb, nb)
        xa = _moe(xo, g2, midx, h2, te, tw, tr, counts, moe_w_gu, moe_b_gu, moe_w_down, moe_b_down, layer)
    return xa.reshape(nb, seq, d)
```

```python
import functools
import math

import jax
import jax.numpy as jnp
from jax import lax
from jax.experimental import pallas as pl
from jax.experimental.pallas import tpu as pltpu

F32 = jnp.float32
BF16 = jnp.bfloat16
HIGHEST = lax.Precision.HIGHEST

GRID_W = 64
NORM_EPS = 1e-6
ROPE_BASE = 10000.0
LRU_WIDTH = 512
LRU_BLOCKS = 8
LRU_BLOCK_W = LRU_WIDTH // LRU_BLOCKS
LRU_C = 8.0
CONV_W = 4
MLA_HEADS = 4
MLA_Q_RANK = 384
MLA_KV_RANK = 256
MLA_NOPE = 128
MLA_ROPE = 64
MLA_V = 128
MLA_QK = MLA_NOPE + MLA_ROPE
DIFF_HEADS = 8
DIFF_HEAD_DIM = 64
DIFF_V = 2 * DIFF_HEAD_DIM
N_EXPERTS = 32
TOP_K = 4
SWIGLU_LIMIT = 7.0
SWIGLU_ALPHA = 1.702

LANES = 128
SUBLANES = 8
VMEM_LIMIT = 52 * 1024 * 1024

TM = 256
TQ = 512
LRU_CT = 128
LRU_CHUNK = 128
LRU_SEG = LRU_CHUNK // SUBLANES
LRU_PAD = 8
MOE_BM = 256
NEG_BIG = -1e30
LOG2_E = math.log2(math.e)


def _cparams(sem, vmem=VMEM_LIMIT):
    return pltpu.CompilerParams(dimension_semantics=sem, vmem_limit_bytes=vmem)


def _adaln_kernel(c_ref, w_ref, b_ref, o_ref):
    c = c_ref[...]
    s = c * jax.nn.sigmoid(c)
    o_ref[...] = jnp.dot(s, w_ref[...], preferred_element_type=F32, precision=HIGHEST) + b_ref[...]


def _adaln(cvec, w, b):
    rows, d = cvec.shape
    n = w.shape[1]
    tn = 1536
    return pl.pallas_call(
        _adaln_kernel,
        grid=(n // tn,),
        in_specs=[pl.BlockSpec((rows, d), lambda j: (0, 0)),
                  pl.BlockSpec((d, tn), lambda j: (0, j)),
                  pl.BlockSpec((1, tn), lambda j: (0, j))],
        out_specs=pl.BlockSpec((rows, tn), lambda j: (0, j)),
        out_shape=jax.ShapeDtypeStruct((rows, n), F32),
        compiler_params=_cparams(("arbitrary",)),
        name="adaln_mod",
    )(cvec, w, b.reshape(1, n))


def _norm_mod(x, g, sc, sh):
    var = jnp.mean(x * x, axis=-1, keepdims=True)
    y = x * lax.rsqrt(var + NORM_EPS) * g
    return y * (1.0 + sc) + sh


def _ones_column(shape):
    lane = lax.broadcasted_iota(jnp.int32, shape, 1)
    return jnp.where(lane == 0, 1.0, 0.0).astype(BF16)


def _rope_tile(x, cos, sin_a, sin_b):
    up = pltpu.roll(x, LANES - 16, axis=1)
    dn = pltpu.roll(x, 16, axis=1)
    return x * cos + up * sin_a + dn * sin_b


def _rope_tables(seq, rot_dim):
    n_rows = seq // GRID_W
    rows = jnp.repeat(jnp.arange(n_rows, dtype=F32), GRID_W)
    cols = jnp.tile(jnp.arange(GRID_W, dtype=F32), n_rows)
    axis_dim = rot_dim // 2
    inv_freq = ROPE_BASE ** (-jnp.arange(0, axis_dim, 2, dtype=F32) / axis_dim)
    ang_r = rows[:, None] * inv_freq
    ang_c = cols[:, None] * inv_freq
    ang = jnp.concatenate([ang_r, ang_r, ang_c, ang_c], axis=-1)
    cos, sin = jnp.cos(ang), jnp.sin(ang)
    quarter = rot_dim // 4
    first = (jnp.arange(rot_dim) % (2 * quarter)) < quarter
    sin_a = jnp.where(first, -sin, 0.0)
    sin_b = jnp.where(first, 0.0, sin)
    return cos, sin_a, sin_b


def _rope_tables_128(seq, ctx, rot_dim, tile_groups):
    cos, sin_a, sin_b = _rope_tables(seq, rot_dim)
    if tile_groups:
        reps = LANES // rot_dim
        cos, sin_a, sin_b = (jnp.tile(t, (1, reps)) for t in (cos, sin_a, sin_b))
    else:
        pad = LANES - rot_dim
        cos = jnp.concatenate([cos, jnp.ones((seq, pad), F32)], axis=-1)
        sin_a = jnp.concatenate([sin_a, jnp.zeros((seq, pad), F32)], axis=-1)
        sin_b = jnp.concatenate([sin_b, jnp.zeros((seq, pad), F32)], axis=-1)
    cos = jnp.concatenate([cos, jnp.ones((ctx, LANES), F32)], axis=0)
    sin_a = jnp.concatenate([sin_a, jnp.zeros((ctx, LANES), F32)], axis=0)
    sin_b = jnp.concatenate([sin_b, jnp.zeros((ctx, LANES), F32)], axis=0)
    return cos, sin_a, sin_b


def _mod_index(tpb, nb):
    def idx(i):
        return jnp.where(i % tpb == tpb - 1, nb, i // tpb)
    return idx


def _even_proj_kernel(x_ref, g_ref, sc_ref, sh_ref, wg_ref, wr_ref, wq_ref, wkv_ref, wkr_ref,
                      og_ref, or_ref, oq_ref, okv_ref, okr_ref):
    h = _norm_mod(x_ref[...], g_ref[...], sc_ref[...], sh_ref[...]).astype(BF16)
    for w_ref, o_ref in ((wg_ref, og_ref), (wr_ref, or_ref), (wq_ref, oq_ref),
                         (wkv_ref, okv_ref), (wkr_ref, okr_ref)):
        o_ref[...] = jnp.dot(h, w_ref[...], preferred_element_type=F32)


def _even_proj(x, gain, sc, sh, weights, tpb, nb):
    t, d = x.shape
    midx = _mod_index(tpb, nb)
    w_specs = [pl.BlockSpec(w.shape, lambda i: (0, 0)) for w in weights]
    return pl.pallas_call(
        _even_proj_kernel,
        grid=(t // TM,),
        in_specs=[pl.BlockSpec((TM, d), lambda i: (i, 0)),
                  pl.BlockSpec((1, d), lambda i: (0, 0)),
                  pl.BlockSpec((None, 1, d), lambda i: (midx(i), 0, 0)),
                  pl.BlockSpec((None, 1, d), lambda i: (midx(i), 0, 0))] + w_specs,
        out_specs=[pl.BlockSpec((TM, w.shape[1]), lambda i: (i, 0)) for w in weights],
        out_shape=[jax.ShapeDtypeStruct((t, w.shape[1]), F32) for w in weights],
        compiler_params=_cparams(("parallel",)),
        name="even_in_proj",
    )(x, gain, sc, sh, *weights)


def _mla_prep_kernel(qc_ref, kvc_ref, kr_ref, qng_ref, kvng_ref, wqn_ref, wqr_ref, wkv_ref,
                     qgn_ref, qgr_ref, kgn_ref, kgr_ref, cos_ref, sa_ref, sb_ref,
                     q_ref, k_ref, v_ref):
    cos, sa, sb = cos_ref[...], sa_ref[...], sb_ref[...]
    inv_qk = 1.0 / MLA_QK

    qc = qc_ref[...]
    hq = (qc * lax.rsqrt(jnp.mean(qc * qc, axis=-1, keepdims=True) + NORM_EPS) * qng_ref[...]).astype(BF16)
    q_nope = jnp.dot(hq, wqn_ref[...], preferred_element_type=F32)
    q_rope = jnp.dot(hq, wqr_ref[...], preferred_element_type=F32)
    q_scale = MLA_QK ** -0.5 * LOG2_E
    for h in range(MLA_HEADS):
        qn = q_nope[:, h * LANES:(h + 1) * LANES]
        qr = q_rope[:, h * LANES:(h + 1) * LANES]
        ms = (jnp.sum(qn * qn, axis=-1, keepdims=True) + jnp.sum(qr * qr, axis=-1, keepdims=True)) * inv_qk
        rs = lax.rsqrt(ms + NORM_EPS) * q_scale
        q_ref[:, (2 * h) * LANES:(2 * h + 1) * LANES] = (qn * rs * qgn_ref[...]).astype(BF16)
        q_ref[:, (2 * h + 1) * LANES:(2 * h + 2) * LANES] = _rope_tile(qr * rs * qgr_ref[...], cos, sa, sb).astype(BF16)

    kvc = kvc_ref[...]
    hkv = (kvc * lax.rsqrt(jnp.mean(kvc * kvc, axis=-1, keepdims=True) + NORM_EPS) * kvng_ref[...]).astype(BF16)
    kv = jnp.dot(hkv, wkv_ref[...], preferred_element_type=F32)
    kr = kr_ref[...]
    kr_ss = jnp.sum(kr * kr, axis=-1, keepdims=True)
    kr_rot = _rope_tile(kr * kgr_ref[...], cos, sa, sb)
    for h in range(MLA_HEADS):
        kn = kv[:, (2 * h) * LANES:(2 * h + 1) * LANES]
        ms = (jnp.sum(kn * kn, axis=-1, keepdims=True) + kr_ss) * inv_qk
        rs = lax.rsqrt(ms + NORM_EPS)
        k_ref[:, (2 * h) * LANES:(2 * h + 1) * LANES] = (kn * rs * kgn_ref[...]).astype(BF16)
        k_ref[:, (2 * h + 1) * LANES:(2 * h + 2) * LANES] = (kr_rot * rs).astype(BF16)
        v_ref[:, (2 * h) * LANES:(2 * h + 1) * LANES] = kv[:, (2 * h + 1) * LANES:(2 * h + 2) * LANES].astype(BF16)
        v_ref[:, (2 * h + 1) * LANES:(2 * h + 2) * LANES] = _ones_column((kv.shape[0], LANES))


def _mla_prep(qc, kvc, kr, q_norm_g, kv_norm_g, w_q_nope, w_q_rope, w_ukv, q_gn, q_gr, k_gn, k_gr,
              rope, tpb):
    t = qc.shape[0]
    cos, sa, sb = rope

    def const(a):
        return pl.BlockSpec(a.shape, lambda i: (0, 0))

    def rows(a):
        return pl.BlockSpec((TM, a.shape[1]), lambda i: (i, 0))

    tab = pl.BlockSpec((TM, LANES), lambda i: (i % tpb, 0))
    hq, hv = MLA_HEADS * 2 * LANES, MLA_HEADS * 2 * MLA_V
    return pl.pallas_call(
        _mla_prep_kernel,
        grid=(t // TM,),
        in_specs=[rows(qc), rows(kvc), rows(kr), const(q_norm_g), const(kv_norm_g),
                  const(w_q_nope), const(w_q_rope), const(w_ukv),
                  const(q_gn), const(q_gr), const(k_gn), const(k_gr), tab, tab, tab],
        out_specs=[pl.BlockSpec((TM, hq), lambda i: (i, 0)),
                   pl.BlockSpec((TM, hq), lambda i: (i, 0)),
                   pl.BlockSpec((TM, hv), lambda i: (i, 0))],
        out_shape=[jax.ShapeDtypeStruct((t, hq), BF16),
                   jax.ShapeDtypeStruct((t, hq), BF16),
                   jax.ShapeDtypeStruct((t, hv), BF16)],
        compiler_params=_cparams(("parallel",)),
        name="mla_prep",
    )(qc, kvc, kr, q_norm_g, kv_norm_g, w_q_nope, w_q_rope, w_ukv, q_gn, q_gr, k_gn, k_gr, cos, sa, sb)


def _sublane_iota():
    return lax.broadcasted_iota(jnp.int32, (SUBLANES, LANES), 0)


def _scan_chunk(a_chunk, u_chunk, h_chunk, carry, reverse):
    steps = range(LRU_SEG - 1, -1, -1) if reverse else range(LRU_SEG)
    h_loc, p_loc = [None] * LRU_SEG, [None] * LRU_SEG
    h = p = None
    for j in steps:
        a = a_chunk[pl.ds(j, SUBLANES, stride=LRU_SEG), :]
        u = u_chunk[pl.ds(j, SUBLANES, stride=LRU_SEG), :]
        if h is None:
            h, p = u, a
        else:
            h, p = a * h + u, a * p
        h_loc[j], p_loc[j] = h, p
    sub = _sublane_iota()
    seg_p, seg_h = p, h
    for d in (1, 2, 4):
        shift = SUBLANES - d if reverse else d
        prev_p = pltpu.roll(seg_p, shift, axis=0)
        prev_h = pltpu.roll(seg_h, shift, axis=0)
        valid = (sub < SUBLANES - d) if reverse else (sub >= d)
        seg_h = jnp.where(valid, seg_p * prev_h + seg_h, seg_h)
        seg_p = jnp.where(valid, seg_p * prev_p, seg_p)
    h_end = seg_h + seg_p * carry
    if reverse:
        h_in = jnp.where(sub == SUBLANES - 1, carry, pltpu.roll(h_end, SUBLANES - 1, axis=0))
        new_carry = h_end[0:1, :]
    else:
        h_in = jnp.where(sub == 0, carry, pltpu.roll(h_end, 1, axis=0))
        new_carry = h_end[SUBLANES - 1:SUBLANES, :]
    for j in range(LRU_SEG):
        h_chunk[pl.ds(j, SUBLANES, stride=LRU_SEG), :] = h_loc[j] + p_loc[j] * h_in
    return jnp.broadcast_to(new_carry, (SUBLANES, LANES))


def _lru_kernel(g_ref, r_ref, cw_ref, cb_ref, wg_ref, bg_ref, lam_ref, o_ref,
                rp_ref, af_ref, uf_ref, ab_ref, ub_ref, hf_ref, hb_ref, *, ctx, seq):
    n_ctx, n_lat = ctx // LRU_CHUNK, seq // LRU_CHUNK
    zeros_pad = jnp.zeros((LRU_PAD, LRU_CT), F32)
    ctx0 = seq + 2 * LRU_PAD
    rp_ref[0:LRU_PAD, :] = zeros_pad
    rp_ref[LRU_PAD:LRU_PAD + seq, :] = r_ref[0:seq, :]
    rp_ref[LRU_PAD + seq:ctx0, :] = zeros_pad
    rp_ref[ctx0:ctx0 + ctx, :] = r_ref[seq:seq + ctx, :]
    rp_ref[ctx0 + ctx:ctx0 + ctx + LRU_PAD, :] = zeros_pad

    cw = cw_ref[...]
    cb = cb_ref[...]
    wg = wg_ref[...]
    bg = bg_ref[...]
    lam = lam_ref[...]
    sp = jnp.maximum(-lam, 0.0) + jnp.log1p(jnp.exp(-jnp.abs(lam)))

    def coeff_chunk(c, pad_off, row_off):
        start = pl.multiple_of(pad_off + c * LRU_CHUNK, SUBLANES)
        ext = rp_ref[pl.ds(start, LRU_CHUNK + 2 * LRU_PAD), :]
        x = cb
        for tap in range(CONV_W):
            lo = LRU_PAD - 2 + tap
            x = x + ext[lo:lo + LRU_CHUNK, :] * cw[tap:tap + 1, :]
        gates = jnp.dot(x.astype(BF16), wg, preferred_element_type=F32) + bg
        out_row = pl.multiple_of(row_off + c * LRU_CHUNK, SUBLANES)
        for d, (a_ref, u_ref) in enumerate(((af_ref, uf_ref), (ab_ref, ub_ref))):
            r = jax.nn.sigmoid(gates[:, (2 * d) * LRU_CT:(2 * d + 1) * LRU_CT])
            i = jax.nn.sigmoid(gates[:, (2 * d + 1) * LRU_CT:(2 * d + 2) * LRU_CT])
            log_a = -LRU_C * r * sp[d:d + 1, :]
            a_ref[pl.ds(out_row, LRU_CHUNK), :] = jnp.exp(log_a)
            th = jnp.tanh(log_a)
            u_ref[pl.ds(out_row, LRU_CHUNK), :] = jnp.sqrt(-2.0 * th / (1.0 - th)) * i * x
        return None

    def coeff_ctx(c, _):
        coeff_chunk(c, seq + LRU_PAD, seq)
        return 0

    def coeff_lat(c, _):
        coeff_chunk(c, 0, 0)
        return 0

    lax.fori_loop(0, n_ctx, coeff_ctx, 0)
    lax.fori_loop(0, n_lat, coeff_lat, 0)

    def scan_pair(n, row_off):
        def body(c, carry):
            cf, cb_ = carry
            f_row = pl.multiple_of(row_off + c * LRU_CHUNK, SUBLANES)
            b_row = pl.multiple_of(row_off + (n - 1 - c) * LRU_CHUNK, SUBLANES)
            cf = _scan_chunk(af_ref.at[pl.ds(f_row, LRU_CHUNK), :], uf_ref.at[pl.ds(f_row, LRU_CHUNK), :],
                             hf_ref.at[pl.ds(f_row, LRU_CHUNK), :], cf, False)
            cb_ = _scan_chunk(ab_ref.at[pl.ds(b_row, LRU_CHUNK), :], ub_ref.at[pl.ds(b_row, LRU_CHUNK), :],
                              hb_ref.at[pl.ds(b_row, LRU_CHUNK), :], cb_, True)
            return cf, cb_
        return body

    zero = jnp.zeros((SUBLANES, LANES), F32)
    carry = lax.fori_loop(0, n_ctx, scan_pair(n_ctx, seq), (zero, zero))
    lax.fori_loop(0, n_lat, scan_pair(n_lat, 0), carry)

    g = g_ref[...]
    gelu = 0.5 * g * (1.0 + jnp.tanh(math.sqrt(2.0 / math.pi) * (g + 0.044715 * (g * g * g))))
    o_ref[...] = (gelu * (hf_ref[...] + hb_ref[...])).astype(BF16)


def _lru(g, r, conv_w, conv_b, w_gates, b_gates, lam, nb, ctx, seq):
    l = ctx + seq
    width = g.shape[1]
    n_ct = width // LRU_CT
    g3 = g.reshape(nb, l, width)
    r3 = r.reshape(nb, l, width)
    seq_spec = pl.BlockSpec((None, l, LRU_CT), lambda b, c: (b, 0, c))
    scratch = [pltpu.VMEM((l + 3 * LRU_PAD, LRU_CT), F32)] + [pltpu.VMEM((l, LRU_CT), F32)] * 6
    out = pl.pallas_call(
        functools.partial(_lru_kernel, ctx=ctx, seq=seq),
        grid=(nb, n_ct),
        in_specs=[seq_spec, seq_spec,
                  pl.BlockSpec((CONV_W, LRU_CT), lambda b, c: (0, c)),
                  pl.BlockSpec((1, LRU_CT), lambda b, c: (0, c)),
                  pl.BlockSpec((None, LRU_CT, 4 * LRU_CT), lambda b, c: (c, 0, 0)),
                  pl.BlockSpec((None, 1, 4 * LRU_CT), lambda b, c: (c, 0, 0)),
                  pl.BlockSpec((2, LRU_CT), lambda b, c: (0, c))],
        out_specs=seq_spec,
        out_shape=jax.ShapeDtypeStruct((nb, l, width), BF16),
        scratch_shapes=scratch,
        compiler_params=_cparams(("parallel", "parallel")),
        name="rglru",
    )(g3, r3, conv_w, conv_b, w_gates, b_gates, lam)
    return out.reshape(nb * l, width)


def _lru_gate_weights(wa, ba, wx, bx):
    per = LRU_CT // LRU_BLOCK_W
    n_ct = LRU_BLOCKS // per
    eye = jnp.eye(per, dtype=F32)

    def dense(w):
        w4 = w.reshape(n_ct, per, LRU_BLOCK_W, LRU_BLOCK_W)
        return jnp.einsum('cide,ij->cidje', w4, eye).reshape(n_ct, LRU_CT, LRU_CT)

    w = jnp.concatenate([dense(wa[0]), dense(wx[0]), dense(wa[1]), dense(wx[1])], axis=-1)
    b = jnp.concatenate([v.reshape(n_ct, 1, LRU_CT) for v in (ba[0], bx[0], ba[1], bx[1])], axis=-1)
    return w.astype(BF16), b


def _key_spans(n_keys):
    half = n_keys // 2 if n_keys % (2 * LANES) == 0 else n_keys
    return [(lo, lo + half) for lo in range(0, n_keys, half)]


def _attn_scores(q_ops, kt_ref, s_ref, m_ref):
    for t, q in enumerate(q_ops):
        for lo, hi in _key_spans(kt_ref.shape[-1]):
            s_ref[t, :, lo:hi] = jnp.dot(q, kt_ref[:, lo:hi], preferred_element_type=F32)
        m_ref[t] = jnp.max(s_ref[t], axis=-1, keepdims=True)


def _attn_values(v_ref, s_ref, m_ref, p_ref):
    accs = []
    for t in range(s_ref.shape[0]):
        m = m_ref[t]
        acc = None
        for lo, hi in _key_spans(s_ref.shape[-1]):
            p_ref[t, :, lo:hi] = jnp.exp2(s_ref[t, :, lo:hi] - m).astype(BF16)
            part = jnp.dot(p_ref[t, :, lo:hi], v_ref[lo:hi, :], preferred_element_type=F32)
            acc = part if acc is None else acc + part
        accs.append(acc)
    return accs


def _attn_pipeline(q_ref, kt_ref, v_ref, o_ref, bufs, p_ref, q_ops_fn, values_fn):
    s0, m0 = bufs[0]
    tq = s0.shape[1]
    n = q_ref.shape[0] // tq

    def scores(t, s_ref, m_ref):
        row = t * tq if isinstance(t, int) else pl.multiple_of(t * tq, tq)
        _attn_scores(q_ops_fn(q_ref[pl.ds(row, tq), :]), kt_ref, s_ref, m_ref)

    def values(t, s_ref, m_ref):
        row = t * tq if isinstance(t, int) else pl.multiple_of(t * tq, tq)
        o_ref[pl.ds(row, tq), :] = values_fn(v_ref, s_ref, m_ref, p_ref).astype(o_ref.dtype)

    if len(bufs) == 1:
        def one(t, _):
            scores(t, s0, m0)
            values(t, s0, m0)
            return 0

        if n == 1:
            one(0, 0)
        else:
            lax.fori_loop(0, n, one, 0)
        return

    s1, m1 = bufs[1]
    scores(0, s0, m0)
    if n == 1:
        values(0, s0, m0)
        return
    assert n % 2 == 0

    def pair(k, _):
        t = 2 * k
        scores(t + 1, s1, m1)
        values(t, s0, m0)
        scores(t + 2, s0, m0)
        values(t + 1, s1, m1)
        return 0

    lax.fori_loop(0, n // 2 - 1, pair, 0)
    scores(n - 1, s1, m1)
    values(n - 2, s0, m0)
    values(n - 1, s1, m1)


def _score_bufs(scratch):
    return tuple(zip(scratch[0:-1:2], scratch[1:-1:2])), scratch[-1]


def _mla_attn_kernel(q_ref, kt_ref, v_ref, o_ref, *scratch):
    dv = o_ref.shape[-1]
    bufs, p_ref = _score_bufs(scratch)

    def values(*refs):
        acc = _attn_values(*refs)[0]
        return acc[:, :dv] / acc[:, dv:dv + 1]

    _attn_pipeline(q_ref, kt_ref, v_ref, o_ref, bufs, p_ref, lambda q: [q], values)


def _diff_attn_kernel(q_ref, kt_ref, v_ref, g_ref, lam_ref, o_ref, *scratch, out_scale):
    dv = o_ref.shape[-1]
    bufs, p_ref = _score_bufs(scratch)

    def q_ops(q):
        lane = lax.broadcasted_iota(jnp.int32, q.shape, 1)
        zero = jnp.zeros_like(q)
        return [jnp.where(lane < DIFF_HEAD_DIM, q, zero), jnp.where(lane < DIFF_HEAD_DIM, zero, q)]

    def values(*refs):
        a1, a2 = _attn_values(*refs)
        o = a1[:, :dv] / a1[:, dv:dv + 1] - lam_ref[0] * (a2[:, :dv] / a2[:, dv:dv + 1])
        y = o * lax.rsqrt(jnp.mean(o * o, axis=-1, keepdims=True) + NORM_EPS) * g_ref[...]
        return y * out_scale

    _attn_pipeline(q_ref, kt_ref, v_ref, o_ref, bufs, p_ref, q_ops, values)


def _attention(kernel_fn, name, n_softmax, q, kt, v, nb, ctx, seq, heads, dk, dv, ctx_queries,
               extra_in=(), extra_specs=()):
    l = ctx + seq
    q3 = q.reshape(nb, l, heads * dk)
    v3 = v.reshape(nb, l, heads * 2 * dv)
    if ctx_queries:
        tq, n_keys, rows = ctx, ctx, ctx
        blk = seq // ctx
        in_specs = [pl.BlockSpec((None, rows, dk), lambda b, h: (b, blk, h)),
                    pl.BlockSpec((None, None, dk, ctx), lambda b, h: (b, h, 0, blk)),
                    pl.BlockSpec((None, ctx, 2 * dv), lambda b, h: (b, blk, h))]
    else:
        tq, n_keys, rows = TQ, l, seq
        in_specs = [pl.BlockSpec((None, rows, dk), lambda b, h: (b, 0, h)),
                    pl.BlockSpec((None, None, dk, l), lambda b, h: (b, h, 0, 0)),
                    pl.BlockSpec((None, l, 2 * dv), lambda b, h: (b, 0, h))]
    score_buf = [pltpu.VMEM((n_softmax, tq, n_keys), F32), pltpu.VMEM((n_softmax, tq, 1), F32)]
    out = pl.pallas_call(
        kernel_fn,
        grid=(nb, heads),
        in_specs=in_specs + list(extra_specs),
        out_specs=pl.BlockSpec((None, rows, dv), lambda b, h: (b, 0, h)),
        out_shape=jax.ShapeDtypeStruct((nb, rows, heads * dv), BF16),
        scratch_shapes=score_buf * (2 if n_softmax == 1 else 1) + [pltpu.VMEM((n_softmax, tq, n_keys), BF16)],
        compiler_params=_cparams(("parallel", "parallel")),
        name=name,
    )(q3, kt, v3, *extra_in)
    return out


def _key_transpose(k, nb, l, heads, dk):
    return jnp.transpose(k.reshape(nb, l, heads, dk), (0, 2, 3, 1))


def _attend_all(kernel_fn, name, n_softmax, q, k, v, nb, ctx, seq, heads, dk, dv, need_ctx, **extra):
    kt = _key_transpose(k, nb, ctx + seq, heads, dk)
    lat = _attention(kernel_fn, name, n_softmax, q, kt, v, nb, ctx, seq, heads, dk, dv, False, **extra)
    if not need_ctx:
        return lat.reshape(nb * seq, heads * dv)
    cx = _attention(kernel_fn, name + "_ctx", n_softmax, q, kt, v, nb, ctx, seq, heads, dk, dv, True, **extra)
    return jnp.concatenate([lat, cx], axis=1).reshape(nb * (ctx + seq), heads * dv)


def _group_ms(x, gmat):
    return jnp.dot((x * x).astype(BF16), gmat, preferred_element_type=F32)


def _odd_proj_kernel(x_ref, g_ref, sc_ref, sh_ref, wq_ref, wk_ref, wv_ref, qg_ref, kg_ref, gm_ref,
                     cos_ref, sa_ref, sb_ref, q_ref, k_ref, v_ref):
    h = _norm_mod(x_ref[...], g_ref[...], sc_ref[...], sh_ref[...]).astype(BF16)
    cos, sa, sb = cos_ref[...], sa_ref[...], sb_ref[...]
    gmat = gm_ref[...]
    q_scale = DIFF_HEAD_DIM ** -0.5 * LOG2_E
    for w_ref, gain_ref, o_ref, scale in ((wq_ref, qg_ref, q_ref, q_scale), (wk_ref, kg_ref, k_ref, 1.0)):
        raw = jnp.dot(h, w_ref[...], preferred_element_type=F32)
        gain = gain_ref[...] * scale
        for c in range(raw.shape[1] // LANES):
            t = raw[:, c * LANES:(c + 1) * LANES]
            t = t * lax.rsqrt(_group_ms(t, gmat) + NORM_EPS) * gain
            o_ref[:, c * LANES:(c + 1) * LANES] = _rope_tile(t, cos, sa, sb).astype(BF16)
    v = jnp.dot(h, wv_ref[...], preferred_element_type=F32).astype(BF16)
    for c in range(v.shape[1] // DIFF_V):
        v_ref[:, (2 * c) * DIFF_V:(2 * c + 1) * DIFF_V] = v[:, c * DIFF_V:(c + 1) * DIFF_V]
        v_ref[:, (2 * c + 1) * DIFF_V:(2 * c + 2) * DIFF_V] = _ones_column((v.shape[0], DIFF_V))


def _odd_proj(x, gain, sc, sh, wq, wk, wv, q_gain, k_gain, gmat, rope, tpb, nb):
    t, d = x.shape
    midx = _mod_index(tpb, nb)
    cos, sa, sb = rope

    def const(a):
        return pl.BlockSpec(a.shape, lambda i: (0, 0))

    tab = pl.BlockSpec((TM, LANES), lambda i: (i % tpb, 0))
    n = wq.shape[1]
    return pl.pallas_call(
        _odd_proj_kernel,
        grid=(t // TM,),
        in_specs=[pl.BlockSpec((TM, d), lambda i: (i, 0)), const(gain),
                  pl.BlockSpec((None, 1, d), lambda i: (midx(i), 0, 0)),
                  pl.BlockSpec((None, 1, d), lambda i: (midx(i), 0, 0)),
                  const(wq), const(wk), const(wv), const(q_gain), const(k_gain), const(gmat),
                  tab, tab, tab],
        out_specs=[pl.BlockSpec((TM, n), lambda i: (i, 0))] * 2 + [pl.BlockSpec((TM, 2 * n), lambda i: (i, 0))],
        out_shape=[jax.ShapeDtypeStruct((t, n), BF16)] * 2 + [jax.ShapeDtypeStruct((t, 2 * n), BF16)],
        compiler_params=_cparams(("parallel",)),
        name="odd_in_proj",
    )(x, gain, sc, sh, wq, wk, wv, q_gain, k_gain, gmat, cos, sa, sb)


def _post_mix_kernel(*refs, n_mix):
    mix_refs = refs[:n_mix]
    w_refs = refs[n_mix:2 * n_mix]
    (x_ref, g1_ref, n2_ref, sc_ref, sh_ref, rwh_ref, rwl_ref, rb_ref,
     xo_ref, h2_ref, te_ref, tw_ref, tr_ref, cnt_ref, base_ref) = refs[2 * n_mix:]
    i = pl.program_id(0)

    @pl.when(i == 0)
    def _():
        base_ref[...] = jnp.zeros_like(base_ref)

    d = x_ref.shape[-1]
    halves = []
    for lo in (0, d // 2):
        m = jnp.dot(mix_refs[0][...], w_refs[0][:, lo:lo + d // 2], preferred_element_type=F32)
        for a_ref, w_ref in zip(mix_refs[1:], w_refs[1:]):
            m = m + jnp.dot(a_ref[...], w_ref[:, lo:lo + d // 2], preferred_element_type=F32)
        halves.append(m)
    x = x_ref[...] + g1_ref[...] * jnp.concatenate(halves, axis=-1)
    xo_ref[...] = x
    h2 = _norm_mod(x, n2_ref[...], sc_ref[...], sh_ref[...])
    h2_hi = h2.astype(BF16)
    h2_ref[...] = h2_hi

    h2_lo = (h2 - h2_hi.astype(F32)).astype(BF16)
    logits = (jnp.dot(h2_hi, rwh_ref[...], preferred_element_type=F32)
              + jnp.dot(h2_lo, rwh_ref[...], preferred_element_type=F32)
              + jnp.dot(h2_hi, rwl_ref[...], preferred_element_type=F32)) + rb_ref[...]
    lane = lax.broadcasted_iota(jnp.int32, logits.shape, 1)
    lane_f = lane.astype(F32)
    vals = jnp.where(lane < N_EXPERTS, logits, NEG_BIG)
    tops, hots = [], []
    for _ in range(TOP_K):
        top = jnp.max(vals, axis=-1, keepdims=True)
        first = jnp.min(jnp.where(vals == top, lane_f, float(LANES)), axis=-1, keepdims=True)
        hot = lane_f == first
        tops.append(top)
        hots.append(hot)
        vals = jnp.where(hot, 2.0 * NEG_BIG, vals)
    exps = [jnp.exp(t - tops[0]) for t in tops]
    denom = exps[0] + exps[1] + exps[2] + exps[3]

    picked = (hots[0] | hots[1]) | (hots[2] | hots[3])
    cnt = jnp.where(picked, 1.0, 0.0)
    r_io = lax.broadcasted_iota(jnp.int32, (TM, TM), 0)
    c_io = lax.broadcasted_iota(jnp.int32, (TM, TM), 1)
    tri = jnp.where(c_io < r_io, 1.0, 0.0).astype(BF16)
    before = jnp.dot(tri, cnt.astype(BF16), preferred_element_type=F32) + base_ref[...]
    te = jnp.zeros(logits.shape, F32)
    tw = jnp.zeros(logits.shape, F32)
    tr = jnp.zeros(logits.shape, F32)
    for k in range(TOP_K):
        e_k = jnp.sum(jnp.where(hots[k], lane_f, 0.0), axis=-1, keepdims=True)
        r_k = jnp.sum(jnp.where(hots[k], before, 0.0), axis=-1, keepdims=True)
        te = jnp.where(lane == k, e_k, te)
        tw = jnp.where(lane == k, exps[k] / denom, tw)
        tr = jnp.where(lane == k, r_k, tr)
    te_ref[...] = te.astype(jnp.int32)
    tw_ref[...] = tw
    tr_ref[...] = tr.astype(jnp.int32)
    base_ref[...] += jnp.sum(cnt, axis=0, keepdims=True)
    cnt_ref[...] = base_ref[...]


def _post_mix(mixes, weights, x, g1, n2g, sc2, sh2, router_w, router_b, tpb, nb, latent_only):
    d = x.shape[1]
    tiles_in = tpb
    if latent_only:
        tiles_out = tpb - 1
        n_tiles = nb * tiles_out

        def in_row(i):
            return (i // tiles_out) * tiles_in + i % tiles_out

        def midx(i):
            return i // tiles_out
    else:
        n_tiles = nb * tpb

        def in_row(i):
            return i

        midx = _mod_index(tpb, nb)
    t_out = n_tiles * TM

    def const(a):
        return pl.BlockSpec(a.shape, lambda i: (0, 0))

    def mod(a):
        return pl.BlockSpec((None, 1, d), lambda i: (midx(i), 0, 0))

    in_specs = [pl.BlockSpec((TM, a.shape[1]), lambda i: (i, 0)) for a in mixes]
    in_specs += [const(w) for w in weights]
    in_specs += [pl.BlockSpec((TM, d), lambda i: (in_row(i), 0)), mod(g1), const(n2g), mod(sc2), mod(sh2),
                 const(router_w), const(router_w), const(router_b)]
    rw_hi = router_w.astype(BF16)
    rw_lo = (router_w - rw_hi.astype(F32)).astype(BF16)
    row_out = lambda width: pl.BlockSpec((TM, width), lambda i: (i, 0))
    return pl.pallas_call(
        functools.partial(_post_mix_kernel, n_mix=len(mixes)),
        grid=(n_tiles,),
        in_specs=in_specs,
        out_specs=[row_out(d), row_out(d), row_out(LANES), row_out(LANES), row_out(LANES),
                   pl.BlockSpec((1, LANES), lambda i: (0, 0))],
        out_shape=[jax.ShapeDtypeStruct((t_out, d), F32), jax.ShapeDtypeStruct((t_out, d), BF16),
                   jax.ShapeDtypeStruct((t_out, LANES), jnp.int32), jax.ShapeDtypeStruct((t_out, LANES), F32),
                   jax.ShapeDtypeStruct((t_out, LANES), jnp.int32), jax.ShapeDtypeStruct((1, LANES), F32)],
        scratch_shapes=[pltpu.VMEM((1, LANES), F32)],
        compiler_params=_cparams(("arbitrary",)),
        name="post_mix_router",
    )(*mixes, *weights, x, g1, n2g, sc2, sh2, rw_hi, rw_lo, router_b)


def _moe_ffn_kernel(be_ref, nu_ref, x_ref, wgu_ref, bgu_ref, wd_ref, bd_ref, o_ref, wgu_bf, wd_bf):
    i = pl.program_id(0)
    prev = be_ref[jnp.maximum(i - 1, 0)]
    fresh = jnp.logical_or(i == 0, be_ref[i] != prev)

    @pl.when(jnp.logical_and(fresh, i < nu_ref[0]))
    def _():
        wgu_bf[...] = wgu_ref[...].astype(BF16)
        wd_bf[...] = wd_ref[...].astype(BF16)

    @pl.when(i < nu_ref[0])
    def _():
        d_ff = wd_bf.shape[0]
        gu = jnp.dot(x_ref[...], wgu_bf[...], preferred_element_type=F32) + bgu_ref[...]
        g = jnp.minimum(gu[:, :d_ff], SWIGLU_LIMIT)
        u = jnp.clip(gu[:, d_ff:], -SWIGLU_LIMIT, SWIGLU_LIMIT)
        act = (u + 1.0) * g * jax.nn.sigmoid(SWIGLU_ALPHA * g)
        y = jnp.dot(act.astype(BF16), wd_bf[...], preferred_element_type=F32) + bd_ref[...]
        o_ref[...] = y.astype(o_ref.dtype)

    @pl.when(i >= nu_ref[0])
    def _():
        o_ref[...] = jnp.zeros_like(o_ref)


def _moe_ffn(block_e, n_used, xs, w_gu, b_gu, w_down, b_down, layer):
    n_rows, d = xs.shape
    depth, n_e, _, two_ff = w_gu.shape
    d_ff = two_ff // 2
    n_blocks = n_rows // MOE_BM
    grid_spec = pltpu.PrefetchScalarGridSpec(
        num_scalar_prefetch=2,
        grid=(n_blocks,),
        in_specs=[pl.BlockSpec((MOE_BM, d), lambda i, be, nu: (i, 0)),
                  pl.BlockSpec((None, None, d, two_ff), lambda i, be, nu: (layer, be[i], 0, 0)),
                  pl.BlockSpec((None, None, 1, two_ff), lambda i, be, nu: (layer, be[i], 0, 0)),
                  pl.BlockSpec((None, None, d_ff, d), lambda i, be, nu: (layer, be[i], 0, 0)),
                  pl.BlockSpec((None, None, 1, d), lambda i, be, nu: (layer, be[i], 0, 0))],
        out_specs=pl.BlockSpec((MOE_BM, d), lambda i, be, nu: (i, 0)),
        scratch_shapes=[pltpu.VMEM((d, two_ff), BF16), pltpu.VMEM((d_ff, d), BF16)],
    )
    return pl.pallas_call(
        _moe_ffn_kernel,
        grid_spec=grid_spec,
        out_shape=jax.ShapeDtypeStruct((n_rows, d), BF16),
        compiler_params=_cparams(("arbitrary",)),
        name="moe_ffn",
    )(block_e, n_used, xs, w_gu, b_gu.reshape(depth, n_e, 1, two_ff), w_down, b_down.reshape(depth, n_e, 1, d))


def _moe_combine_kernel(x_ref, g2_ref, w_ref, y_ref, o_ref):
    w = w_ref[...]
    f = w[:, 0:1] * y_ref[0].astype(F32)
    for k in range(1, TOP_K):
        f = f + w[:, k:k + 1] * y_ref[k].astype(F32)
    o_ref[...] = x_ref[...] + g2_ref[...] * f


def _moe_combine(x, g2, top_w, picked, midx):
    t, d = x.shape
    return pl.pallas_call(
        _moe_combine_kernel,
        grid=(t // TM,),
        in_specs=[pl.BlockSpec((TM, d), lambda i: (i, 0)),
                  pl.BlockSpec((None, 1, d), lambda i: (midx(i), 0, 0)),
                  pl.BlockSpec((TM, LANES), lambda i: (i, 0)),
                  pl.BlockSpec((TOP_K, TM, d), lambda i: (0, i, 0))],
        out_specs=pl.BlockSpec((TM, d), lambda i: (i, 0)),
        out_shape=jax.ShapeDtypeStruct((t, d), F32),
        compiler_params=_cparams(("parallel",)),
        name="moe_combine",
    )(x, g2, top_w, picked)


SMEM_TILE = 1024


def _row_tokens_kernel(lo_ref, hi_ref, dest_ref, rt_ref):
    i = pl.program_id(0)
    chunk = dest_ref.shape[0]

    @pl.when(i == 0)
    def _():
        def fill_gap(g, _):
            def fill(r, _):
                rt_ref[r] = r
                return 0
            lax.fori_loop(lo_ref[g], hi_ref[g], fill, 0)
            return 0
        lax.fori_loop(0, lo_ref.shape[0], fill_gap, 0)

    base = i * (chunk // TOP_K)

    shift = TOP_K.bit_length() - 1
    assert 1 << shift == TOP_K

    def body(j, _):
        rt_ref[dest_ref[j]] = base + lax.shift_right_logical(j, shift)
        return 0

    lax.fori_loop(0, chunk, body, 0, unroll=8)


def _row_tokens(gap_lo, gap_hi, dest_flat, n_rows):
    n_assign = dest_flat.shape[0]
    chunk = 4 * SMEM_TILE if n_assign % (4 * SMEM_TILE) == 0 else SMEM_TILE
    assert n_assign % chunk == 0
    grid_spec = pltpu.PrefetchScalarGridSpec(
        num_scalar_prefetch=2,
        grid=(n_assign // chunk,),
        in_specs=[pl.BlockSpec((chunk,), lambda i, lo, hi: (i,), memory_space=pltpu.SMEM)],
        out_specs=pl.BlockSpec(memory_space=pltpu.SMEM),
    )
    return pl.pallas_call(
        _row_tokens_kernel,
        grid_spec=grid_spec,
        out_shape=jax.ShapeDtypeStruct((n_rows,), jnp.int32),
        compiler_params=_cparams(("arbitrary",)),
        name="moe_row_tokens",
    )(gap_lo, gap_hi, dest_flat)


def _moe(x, g2, midx, h2, top_e, top_w, top_r, counts, w_gu, b_gu, w_down, b_down, layer):
    t, d = h2.shape
    n_assign = t * TOP_K
    n_blocks = -(-n_assign // MOE_BM) + N_EXPERTS
    n_rows = n_blocks * MOE_BM
    cnt = counts[0, :N_EXPERTS].astype(jnp.int32)
    padded = (cnt + MOE_BM - 1) // MOE_BM * MOE_BM
    pends = jnp.cumsum(padded)
    pstarts = pends - padded
    e = top_e[:, :TOP_K]
    expert_ids = jnp.arange(N_EXPERTS, dtype=jnp.int32)
    pstart_of_pick = jnp.sum(jnp.where(e[:, :, None] == expert_ids, pstarts, 0), axis=-1)
    dest = pstart_of_pick + top_r[:, :TOP_K]
    block_start = jnp.arange(n_blocks, dtype=jnp.int32) * MOE_BM
    block_e = jnp.sum((block_start[:, None] >= pends[None, :]).astype(jnp.int32), axis=1)
    block_e = jnp.minimum(block_e, N_EXPERTS - 1)
    n_used = (pends[-1] // MOE_BM).astype(jnp.int32).reshape(1)

    gap_lo = jnp.concatenate([pstarts + cnt, pends[-1:]])
    gap_hi = jnp.concatenate([pends, jnp.full((1,), n_rows, jnp.int32)])
    row_tok = _row_tokens(gap_lo, gap_hi, dest.reshape(-1), n_rows)
    h2_big = jnp.concatenate([h2, jnp.zeros((n_rows - t, d), h2.dtype)], axis=0)
    xs = h2_big[row_tok]
    ys = _moe_ffn(block_e, n_used, xs, w_gu, b_gu, w_down, b_down, layer)
    picked = ys[dest.T.reshape(-1)].reshape(TOP_K, t, d)
    return _moe_combine(x, g2, top_w, picked, midx)


def _pad_cols(w, width):
    return jnp.concatenate([w, jnp.zeros((w.shape[0], width - w.shape[1]), w.dtype)], axis=1)


def kernel(x, c, ctx, c_ctx, mod_w, mod_b, norm1_g, norm2_g, ev_w_in, ev_w_out, lru_conv_w, lru_conv_b, lru_wa, lru_ba, lru_wx, lru_bx, lru_lambda, mla_q_norm_g, mla_w_uq, mla_kv_norm_g, mla_w_ukv, mla_qn_g, mla_kn_g, od_w_in, od_w_out, diff_qn_g, diff_kn_g, diff_lq1, diff_lk1, diff_lq2, diff_lk2, diff_subln_g, router_w, router_b, moe_w_gu, moe_b_gu, moe_w_down, moe_b_down):
    nb, seq, d = x.shape
    n_ctx = ctx.shape[1]
    depth = mod_w.shape[0]
    l = n_ctx + seq
    tpb = l // TM
    assert n_ctx == TM and seq % TQ == 0 and seq % GRID_W == 0

    xa = jnp.concatenate([x, ctx], axis=1).reshape(nb * l, d)
    cvec = jnp.concatenate([c, c_ctx[None, :], jnp.zeros((SUBLANES - nb - 1, d), F32)], axis=0)
    router_w_p = jnp.concatenate([router_w, jnp.zeros((depth, d, LANES - N_EXPERTS), F32)], axis=-1)
    router_b_p = jnp.concatenate([router_b, jnp.zeros((depth, LANES - N_EXPERTS), F32)], axis=-1)

    for layer in range(depth):
        last = layer == depth - 1
        i = layer // 2
        mod = _adaln(cvec, mod_w[layer], mod_b[layer])[:nb + 1]
        sh1, sc1, g1, sh2, sc2, g2 = (mod[:, k * d:(k + 1) * d].reshape(nb + 1, 1, d) for k in range(6))
        n1g = norm1_g[layer].reshape(1, d)
        n2g = norm2_g[layer].reshape(1, d)

        if layer % 2 == 0:
            w_in = ev_w_in[i].astype(BF16)
            o = 2 * LRU_WIDTH
            splits = [w_in[:, :LRU_WIDTH], w_in[:, LRU_WIDTH:o], w_in[:, o:o + MLA_Q_RANK],
                      w_in[:, o + MLA_Q_RANK:o + MLA_Q_RANK + MLA_KV_RANK],
                      _pad_cols(w_in[:, o + MLA_Q_RANK + MLA_KV_RANK:], LANES)]
            gl, rl, qc, kvc, kr = _even_proj(xa, n1g, sc1, sh1, splits, tpb, nb)

            w_gates, b_gates = _lru_gate_weights(lru_wa[i], lru_ba[i], lru_wx[i], lru_bx[i])
            lru = _lru(gl, rl, lru_conv_w[i], lru_conv_b[i].reshape(1, LRU_WIDTH), w_gates, b_gates,
                       lru_lambda[i], nb, n_ctx, seq)

            w_uq = mla_w_uq[i].astype(BF16).reshape(MLA_Q_RANK, MLA_HEADS, MLA_QK)
            w_q_nope = w_uq[:, :, :MLA_NOPE].reshape(MLA_Q_RANK, MLA_HEADS * MLA_NOPE)
            w_q_rope = jnp.concatenate(
                [w_uq[:, :, MLA_NOPE:], jnp.zeros((MLA_Q_RANK, MLA_HEADS, LANES - MLA_ROPE), BF16)],
                axis=-1).reshape(MLA_Q_RANK, MLA_HEADS * LANES)
            pad_g = lambda g: jnp.concatenate([g, jnp.zeros((LANES - MLA_ROPE,), F32)]).reshape(1, LANES)
            rope = _rope_tables_128(seq, n_ctx, MLA_ROPE, tile_groups=False)
            q, k, v = _mla_prep(qc, kvc, kr, mla_q_norm_g[i].reshape(1, -1), mla_kv_norm_g[i].reshape(1, -1),
                                w_q_nope, w_q_rope, mla_w_ukv[i].astype(BF16),
                                mla_qn_g[i][:MLA_NOPE].reshape(1, LANES), pad_g(mla_qn_g[i][MLA_NOPE:]),
                                mla_kn_g[i][:MLA_NOPE].reshape(1, LANES), pad_g(mla_kn_g[i][MLA_NOPE:]),
                                rope, tpb)
            att = _attend_all(_mla_attn_kernel, "mla_attention", 1, q, k, v, nb, n_ctx, seq, MLA_HEADS,
                              2 * LANES, MLA_V, need_ctx=not last)
            w_out = ev_w_out[i].astype(BF16)
            if last:
                lru = lru.reshape(nb, l, LRU_WIDTH)[:, :seq].reshape(nb * seq, LRU_WIDTH)
            mixes = [lru, att]
            weights = [w_out[:LRU_WIDTH], w_out[LRU_WIDTH:]]
        else:
            lam_init = 0.8 - 0.6 * math.exp(-0.3 * layer)
            w_in = od_w_in[i].astype(BF16)
            n_qk = DIFF_HEADS * 2 * DIFF_HEAD_DIM
            gidx = jnp.arange(LANES) // DIFF_HEAD_DIM
            gmat = jnp.where(gidx[:, None] == gidx[None, :], 1.0 / DIFF_HEAD_DIM, 0.0).astype(BF16)
            rope = _rope_tables_128(seq, n_ctx, DIFF_HEAD_DIM, tile_groups=True)
            tile_g = lambda g: jnp.tile(g, LANES // DIFF_HEAD_DIM).reshape(1, LANES)
            q, k, v = _odd_proj(xa, n1g, sc1, sh1, w_in[:, :n_qk], w_in[:, n_qk:2 * n_qk], w_in[:, 2 * n_qk:],
                                tile_g(diff_qn_g[i]), tile_g(diff_kn_g[i]), gmat, rope, tpb, nb)
            lam = (jnp.exp(jnp.sum(diff_lq1[i] * diff_lk1[i])) - jnp.exp(jnp.sum(diff_lq2[i] * diff_lk2[i]))
                   + lam_init).reshape(1).astype(F32)
            att = _attend_all(
                functools.partial(_diff_attn_kernel, out_scale=1.0 - lam_init), "diff_attention", 2,
                q, k, v, nb, n_ctx, seq, DIFF_HEADS, 2 * DIFF_HEAD_DIM, DIFF_V, need_ctx=not last,
                extra_in=(diff_subln_g[i].reshape(1, DIFF_V), lam),
                extra_specs=(pl.BlockSpec((1, DIFF_V), lambda b, h: (0, 0)),
                             pl.BlockSpec(memory_space=pltpu.SMEM)))
            mixes = [att]
            weights = [od_w_out[i].astype(BF16)]

        xo, h2, te, tw, tr, counts = _post_mix(mixes, weights, xa, g1, n2g, sc2, sh2, router_w_p[layer],
                                               router_b_p[layer].reshape(1, LANES), tpb, nb, last)
        midx = (lambda t: t // (tpb - 1)) if last else _mod_index(tpb, nb)
        xa = _moe(xo, g2, midx, h2, te, tw, tr, counts, moe_w_gu, moe_b_gu, moe_w_down, moe_b_down, layer)
    return xa.reshape(nb, seq, d)
```

```python
import functools
import math

import jax
import jax.numpy as jnp
from jax import lax
from jax.experimental import pallas as pl
from jax.experimental.pallas import tpu as pltpu

F32 = jnp.float32
BF16 = jnp.bfloat16
HIGHEST = lax.Precision.HIGHEST

GRID_W = 64
NORM_EPS = 1e-6
ROPE_BASE = 10000.0
LRU_WIDTH = 512
LRU_BLOCKS = 8
LRU_BLOCK_W = LRU_WIDTH // LRU_BLOCKS
LRU_C = 8.0
CONV_W = 4
MLA_HEADS = 4
MLA_Q_RANK = 384
MLA_KV_RANK = 256
MLA_NOPE = 128
MLA_ROPE = 64
MLA_V = 128
MLA_QK = MLA_NOPE + MLA_ROPE
DIFF_HEADS = 8
DIFF_HEAD_DIM = 64
DIFF_V = 2 * DIFF_HEAD_DIM
N_EXPERTS = 32
TOP_K = 4
SWIGLU_LIMIT = 7.0
SWIGLU_ALPHA = 1.702

LANES = 128
SUBLANES = 8
VMEM_LIMIT = 52 * 1024 * 1024

TM = 256
TQ = 512
LRU_CT = 128
LRU_CHUNK = 128
LRU_SEG = LRU_CHUNK // SUBLANES
LRU_PAD = 8
MOE_BM = 256
NEG_BIG = -1e30
LOG2_E = math.log2(math.e)


def _cparams(sem, vmem=VMEM_LIMIT):
    return pltpu.CompilerParams(dimension_semantics=sem, vmem_limit_bytes=vmem)


def _adaln_kernel(c_ref, w_ref, b_ref, o_ref):
    c = c_ref[...]
    s = c * jax.nn.sigmoid(c)
    o_ref[...] = jnp.dot(s, w_ref[...], preferred_element_type=F32, precision=HIGHEST) + b_ref[...]


def _adaln(cvec, w, b):
    rows, d = cvec.shape
    n = w.shape[1]
    tn = 1536
    return pl.pallas_call(
        _adaln_kernel,
        grid=(n // tn,),
        in_specs=[pl.BlockSpec((rows, d), lambda j: (0, 0)),
                  pl.BlockSpec((d, tn), lambda j: (0, j)),
                  pl.BlockSpec((1, tn), lambda j: (0, j))],
        out_specs=pl.BlockSpec((rows, tn), lambda j: (0, j)),
        out_shape=jax.ShapeDtypeStruct((rows, n), F32),
        compiler_params=_cparams(("arbitrary",)),
        name="adaln_mod",
    )(cvec, w, b.reshape(1, n))


def _norm_mod(x, g, sc, sh):
    var = jnp.mean(x * x, axis=-1, keepdims=True)
    y = x * lax.rsqrt(var + NORM_EPS) * g
    return y * (1.0 + sc) + sh


def _ones_column(shape):
    lane = lax.broadcasted_iota(jnp.int32, shape, 1)
    return jnp.where(lane == 0, 1.0, 0.0).astype(BF16)


def _rope_tile(x, cos, sin_a, sin_b):
    up = pltpu.roll(x, LANES - 16, axis=1)
    dn = pltpu.roll(x, 16, axis=1)
    return x * cos + up * sin_a + dn * sin_b


def _rope_tables(seq, rot_dim):
    n_rows = seq // GRID_W
    rows = jnp.repeat(jnp.arange(n_rows, dtype=F32), GRID_W)
    cols = jnp.tile(jnp.arange(GRID_W, dtype=F32), n_rows)
    axis_dim = rot_dim // 2
    inv_freq = ROPE_BASE ** (-jnp.arange(0, axis_dim, 2, dtype=F32) / axis_dim)
    ang_r = rows[:, None] * inv_freq
    ang_c = cols[:, None] * inv_freq
    ang = jnp.concatenate([ang_r, ang_r, ang_c, ang_c], axis=-1)
    cos, sin = jnp.cos(ang), jnp.sin(ang)
    quarter = rot_dim // 4
    first = (jnp.arange(rot_dim) % (2 * quarter)) < quarter
    sin_a = jnp.where(first, -sin, 0.0)
    sin_b = jnp.where(first, 0.0, sin)
    return cos, sin_a, sin_b


def _rope_tables_128(seq, ctx, rot_dim, tile_groups):
    cos, sin_a, sin_b = _rope_tables(seq, rot_dim)
    if tile_groups:
        reps = LANES // rot_dim
        cos, sin_a, sin_b = (jnp.tile(t, (1, reps)) for t in (cos, sin_a, sin_b))
    else:
        pad = LANES - rot_dim
        cos = jnp.concatenate([cos, jnp.ones((seq, pad), F32)], axis=-1)
        sin_a = jnp.concatenate([sin_a, jnp.zeros((seq, pad), F32)], axis=-1)
        sin_b = jnp.concatenate([sin_b, jnp.zeros((seq, pad), F32)], axis=-1)
    cos = jnp.concatenate([cos, jnp.ones((ctx, LANES), F32)], axis=0)
    sin_a = jnp.concatenate([sin_a, jnp.zeros((ctx, LANES), F32)], axis=0)
    sin_b = jnp.concatenate([sin_b, jnp.zeros((ctx, LANES), F32)], axis=0)
    return cos, sin_a, sin_b


def _mod_index(tpb, nb):
    def idx(i):
        return jnp.where(i % tpb == tpb - 1, nb, i // tpb)
    return idx


def _even_proj_kernel(x_ref, g_ref, sc_ref, sh_ref, wg_ref, wr_ref, wq_ref, wkv_ref, wkr_ref,
                      og_ref, or_ref, oq_ref, okv_ref, okr_ref):
    h = _norm_mod(x_ref[...], g_ref[...], sc_ref[...], sh_ref[...]).astype(BF16)
    for w_ref, o_ref in ((wg_ref, og_ref), (wr_ref, or_ref), (wq_ref, oq_ref),
                         (wkv_ref, okv_ref), (wkr_ref, okr_ref)):
        o_ref[...] = jnp.dot(h, w_ref[...], preferred_element_type=F32)


def _even_proj(x, gain, sc, sh, weights, tpb, nb):
    t, d = x.shape
    midx = _mod_index(tpb, nb)
    w_specs = [pl.BlockSpec(w.shape, lambda i: (0, 0)) for w in weights]
    return pl.pallas_call(
        _even_proj_kernel,
        grid=(t // TM,),
        in_specs=[pl.BlockSpec((TM, d), lambda i: (i, 0)),
                  pl.BlockSpec((1, d), lambda i: (0, 0)),
                  pl.BlockSpec((None, 1, d), lambda i: (midx(i), 0, 0)),
                  pl.BlockSpec((None, 1, d), lambda i: (midx(i), 0, 0))] + w_specs,
        out_specs=[pl.BlockSpec((TM, w.shape[1]), lambda i: (i, 0)) for w in weights],
        out_shape=[jax.ShapeDtypeStruct((t, w.shape[1]), F32) for w in weights],
        compiler_params=_cparams(("parallel",)),
        name="even_in_proj",
    )(x, gain, sc, sh, *weights)


def _mla_prep_kernel(qc_ref, kvc_ref, kr_ref, qng_ref, kvng_ref, wqn_ref, wqr_ref, wkv_ref,
                     qgn_ref, qgr_ref, kgn_ref, kgr_ref, cos_ref, sa_ref, sb_ref,
                     q_ref, k_ref, v_ref):
    cos, sa, sb = cos_ref[...], sa_ref[...], sb_ref[...]
    inv_qk = 1.0 / MLA_QK

    qc = qc_ref[...]
    hq = (qc * lax.rsqrt(jnp.mean(qc * qc, axis=-1, keepdims=True) + NORM_EPS) * qng_ref[...]).astype(BF16)
    q_nope = jnp.dot(hq, wqn_ref[...], preferred_element_type=F32)
    q_rope = jnp.dot(hq, wqr_ref[...], preferred_element_type=F32)
    q_scale = MLA_QK ** -0.5 * LOG2_E
    for h in range(MLA_HEADS):
        qn = q_nope[:, h * LANES:(h + 1) * LANES]
        qr = q_rope[:, h * LANES:(h + 1) * LANES]
        ms = (jnp.sum(qn * qn, axis=-1, keepdims=True) + jnp.sum(qr * qr, axis=-1, keepdims=True)) * inv_qk
        rs = lax.rsqrt(ms + NORM_EPS) * q_scale
        q_ref[:, (2 * h) * LANES:(2 * h + 1) * LANES] = (qn * rs * qgn_ref[...]).astype(BF16)
        q_ref[:, (2 * h + 1) * LANES:(2 * h + 2) * LANES] = _rope_tile(qr * rs * qgr_ref[...], cos, sa, sb).astype(BF16)

    kvc = kvc_ref[...]
    hkv = (kvc * lax.rsqrt(jnp.mean(kvc * kvc, axis=-1, keepdims=True) + NORM_EPS) * kvng_ref[...]).astype(BF16)
    kv = jnp.dot(hkv, wkv_ref[...], preferred_element_type=F32)
    kr = kr_ref[...]
    kr_ss = jnp.sum(kr * kr, axis=-1, keepdims=True)
    kr_rot = _rope_tile(kr * kgr_ref[...], cos, sa, sb)
    for h in range(MLA_HEADS):
        kn = kv[:, (2 * h) * LANES:(2 * h + 1) * LANES]
        ms = (jnp.sum(kn * kn, axis=-1, keepdims=True) + kr_ss) * inv_qk
        rs = lax.rsqrt(ms + NORM_EPS)
        k_ref[:, (2 * h) * LANES:(2 * h + 1) * LANES] = (kn * rs * kgn_ref[...]).astype(BF16)
        k_ref[:, (2 * h + 1) * LANES:(2 * h + 2) * LANES] = (kr_rot * rs).astype(BF16)
        v_ref[:, (2 * h) * LANES:(2 * h + 1) * LANES] = kv[:, (2 * h + 1) * LANES:(2 * h + 2) * LANES].astype(BF16)
        v_ref[:, (2 * h + 1) * LANES:(2 * h + 2) * LANES] = _ones_column((kv.shape[0], LANES))


def _mla_prep(qc, kvc, kr, q_norm_g, kv_norm_g, w_q_nope, w_q_rope, w_ukv, q_gn, q_gr, k_gn, k_gr,
              rope, tpb):
    t = qc.shape[0]
    cos, sa, sb = rope

    def const(a):
        return pl.BlockSpec(a.shape, lambda i: (0, 0))

    def rows(a):
        return pl.BlockSpec((TM, a.shape[1]), lambda i: (i, 0))

    tab = pl.BlockSpec((TM, LANES), lambda i: (i % tpb, 0))
    hq, hv = MLA_HEADS * 2 * LANES, MLA_HEADS * 2 * MLA_V
    return pl.pallas_call(
        _mla_prep_kernel,
        grid=(t // TM,),
        in_specs=[rows(qc), rows(kvc), rows(kr), const(q_norm_g), const(kv_norm_g),
                  const(w_q_nope), const(w_q_rope), const(w_ukv),
                  const(q_gn), const(q_gr), const(k_gn), const(k_gr), tab, tab, tab],
        out_specs=[pl.BlockSpec((TM, hq), lambda i: (i, 0)),
                   pl.BlockSpec((TM, hq), lambda i: (i, 0)),
                   pl.BlockSpec((TM, hv), lambda i: (i, 0))],
        out_shape=[jax.ShapeDtypeStruct((t, hq), BF16),
                   jax.ShapeDtypeStruct((t, hq), BF16),
                   jax.ShapeDtypeStruct((t, hv), BF16)],
        compiler_params=_cparams(("parallel",)),
        name="mla_prep",
    )(qc, kvc, kr, q_norm_g, kv_norm_g, w_q_nope, w_q_rope, w_ukv, q_gn, q_gr, k_gn, k_gr, cos, sa, sb)


def _sublane_iota():
    return lax.broadcasted_iota(jnp.int32, (SUBLANES, LANES), 0)


def _scan_chunk(a_chunk, u_chunk, h_chunk, carry, reverse):
    steps = range(LRU_SEG - 1, -1, -1) if reverse else range(LRU_SEG)
    h_loc, p_loc = [None] * LRU_SEG, [None] * LRU_SEG
    h = p = None
    for j in steps:
        a = a_chunk[pl.ds(j, SUBLANES, stride=LRU_SEG), :]
        u = u_chunk[pl.ds(j, SUBLANES, stride=LRU_SEG), :]
        if h is None:
            h, p = u, a
        else:
            h, p = a * h + u, a * p
        h_loc[j], p_loc[j] = h, p
    sub = _sublane_iota()
    seg_p, seg_h = p, h
    for d in (1, 2, 4):
        shift = SUBLANES - d if reverse else d
        prev_p = pltpu.roll(seg_p, shift, axis=0)
        prev_h = pltpu.roll(seg_h, shift, axis=0)
        valid = (sub < SUBLANES - d) if reverse else (sub >= d)
        seg_h = jnp.where(valid, seg_p * prev_h + seg_h, seg_h)
        seg_p = jnp.where(valid, seg_p * prev_p, seg_p)
    h_end = seg_h + seg_p * carry
    if reverse:
        h_in = jnp.where(sub == SUBLANES - 1, carry, pltpu.roll(h_end, SUBLANES - 1, axis=0))
        new_carry = h_end[0:1, :]
    else:
        h_in = jnp.where(sub == 0, carry, pltpu.roll(h_end, 1, axis=0))
        new_carry = h_end[SUBLANES - 1:SUBLANES, :]
    for j in range(LRU_SEG):
        h_chunk[pl.ds(j, SUBLANES, stride=LRU_SEG), :] = h_loc[j] + p_loc[j] * h_in
    return jnp.broadcast_to(new_carry, (SUBLANES, LANES))


def _lru_kernel(g_ref, r_ref, cw_ref, cb_ref, wg_ref, bg_ref, lam_ref, o_ref,
                rp_ref, af_ref, uf_ref, ab_ref, ub_ref, hf_ref, hb_ref, *, ctx, seq):
    n_ctx, n_lat = ctx // LRU_CHUNK, seq // LRU_CHUNK
    zeros_pad = jnp.zeros((LRU_PAD, LRU_CT), F32)
    ctx0 = seq + 2 * LRU_PAD
    rp_ref[0:LRU_PAD, :] = zeros_pad
    rp_ref[LRU_PAD:LRU_PAD + seq, :] = r_ref[0:seq, :]
    rp_ref[LRU_PAD + seq:ctx0, :] = zeros_pad
    rp_ref[ctx0:ctx0 + ctx, :] = r_ref[seq:seq + ctx, :]
    rp_ref[ctx0 + ctx:ctx0 + ctx + LRU_PAD, :] = zeros_pad

    cw = cw_ref[...]
    cb = cb_ref[...]
    wg = wg_ref[...]
    bg = bg_ref[...]
    lam = lam_ref[...]
    sp = jnp.maximum(-lam, 0.0) + jnp.log1p(jnp.exp(-jnp.abs(lam)))

    def coeff_chunk(c, pad_off, row_off):
        start = pl.multiple_of(pad_off + c * LRU_CHUNK, SUBLANES)
        ext = rp_ref[pl.ds(start, LRU_CHUNK + 2 * LRU_PAD), :]
        x = cb
        for tap in range(CONV_W):
            lo = LRU_PAD - 2 + tap
            x = x + ext[lo:lo + LRU_CHUNK, :] * cw[tap:tap + 1, :]
        gates = jnp.dot(x.astype(BF16), wg, preferred_element_type=F32) + bg
        out_row = pl.multiple_of(row_off + c * LRU_CHUNK, SUBLANES)
        for d, (a_ref, u_ref) in enumerate(((af_ref, uf_ref), (ab_ref, ub_ref))):
            r = jax.nn.sigmoid(gates[:, (2 * d) * LRU_CT:(2 * d + 1) * LRU_CT])
            i = jax.nn.sigmoid(gates[:, (2 * d + 1) * LRU_CT:(2 * d + 2) * LRU_CT])
            log_a = -LRU_C * r * sp[d:d + 1, :]
            a_ref[pl.ds(out_row, LRU_CHUNK), :] = jnp.exp(log_a)
            th = jnp.tanh(log_a)
            u_ref[pl.ds(out_row, LRU_CHUNK), :] = jnp.sqrt(-2.0 * th / (1.0 - th)) * i * x
        return None

    def coeff_ctx(c, _):
        coeff_chunk(c, seq + LRU_PAD, seq)
        return 0

    def coeff_lat(c, _):
        coeff_chunk(c, 0, 0)
        return 0

    lax.fori_loop(0, n_ctx, coeff_ctx, 0)
    lax.fori_loop(0, n_lat, coeff_lat, 0)

    def scan_pair(n, row_off):
        def body(c, carry):
            cf, cb_ = carry
            f_row = pl.multiple_of(row_off + c * LRU_CHUNK, SUBLANES)
            b_row = pl.multiple_of(row_off + (n - 1 - c) * LRU_CHUNK, SUBLANES)
            cf = _scan_chunk(af_ref.at[pl.ds(f_row, LRU_CHUNK), :], uf_ref.at[pl.ds(f_row, LRU_CHUNK), :],
                             hf_ref.at[pl.ds(f_row, LRU_CHUNK), :], cf, False)
            cb_ = _scan_chunk(ab_ref.at[pl.ds(b_row, LRU_CHUNK), :], ub_ref.at[pl.ds(b_row, LRU_CHUNK), :],
                              hb_ref.at[pl.ds(b_row, LRU_CHUNK), :], cb_, True)
            return cf, cb_
        return body

    zero = jnp.zeros((SUBLANES, LANES), F32)
    carry = lax.fori_loop(0, n_ctx, scan_pair(n_ctx, seq), (zero, zero))
    lax.fori_loop(0, n_lat, scan_pair(n_lat, 0), carry)

    g = g_ref[...]
    gelu = 0.5 * g * (1.0 + jnp.tanh(math.sqrt(2.0 / math.pi) * (g + 0.044715 * (g * g * g))))
    o_ref[...] = (gelu * (hf_ref[...] + hb_ref[...])).astype(BF16)


def _lru(g, r, conv_w, conv_b, w_gates, b_gates, lam, nb, ctx, seq):
    l = ctx + seq
    width = g.shape[1]
    n_ct = width // LRU_CT
    g3 = g.reshape(nb, l, width)
    r3 = r.reshape(nb, l, width)
    seq_spec = pl.BlockSpec((None, l, LRU_CT), lambda b, c: (b, 0, c))
    scratch = [pltpu.VMEM((l + 3 * LRU_PAD, LRU_CT), F32)] + [pltpu.VMEM((l, LRU_CT), F32)] * 6
    out = pl.pallas_call(
        functools.partial(_lru_kernel, ctx=ctx, seq=seq),
        grid=(nb, n_ct),
        in_specs=[seq_spec, seq_spec,
                  pl.BlockSpec((CONV_W, LRU_CT), lambda b, c: (0, c)),
                  pl.BlockSpec((1, LRU_CT), lambda b, c: (0, c)),
                  pl.BlockSpec((None, LRU_CT, 4 * LRU_CT), lambda b, c: (c, 0, 0)),
                  pl.BlockSpec((None, 1, 4 * LRU_CT), lambda b, c: (c, 0, 0)),
                  pl.BlockSpec((2, LRU_CT), lambda b, c: (0, c))],
        out_specs=seq_spec,
        out_shape=jax.ShapeDtypeStruct((nb, l, width), BF16),
        scratch_shapes=scratch,
        compiler_params=_cparams(("parallel", "parallel")),
        name="rglru",
    )(g3, r3, conv_w, conv_b, w_gates, b_gates, lam)
    return out.reshape(nb * l, width)


def _lru_gate_weights(wa, ba, wx, bx):
    per = LRU_CT // LRU_BLOCK_W
    n_ct = LRU_BLOCKS // per
    eye = jnp.eye(per, dtype=F32)

    def dense(w):
        w4 = w.reshape(n_ct, per, LRU_BLOCK_W, LRU_BLOCK_W)
        return jnp.einsum('cide,ij->cidje', w4, eye).reshape(n_ct, LRU_CT, LRU_CT)

    w = jnp.concatenate([dense(wa[0]), dense(wx[0]), dense(wa[1]), dense(wx[1])], axis=-1)
    b = jnp.concatenate([v.reshape(n_ct, 1, LRU_CT) for v in (ba[0], bx[0], ba[1], bx[1])], axis=-1)
    return w.astype(BF16), b


def _key_spans(n_keys):
    half = n_keys // 2 if n_keys % (2 * LANES) == 0 else n_keys
    return [(lo, lo + half) for lo in range(0, n_keys, half)]


def _attn_scores(q_ops, kt_ref, s_ref, m_ref):
    for t, q in enumerate(q_ops):
        for lo, hi in _key_spans(kt_ref.shape[-1]):
            s_ref[t, :, lo:hi] = jnp.dot(q, kt_ref[:, lo:hi], preferred_element_type=F32)
        m_ref[t] = jnp.max(s_ref[t], axis=-1, keepdims=True)


def _attn_values(v_ref, s_ref, m_ref, p_ref):
    accs = []
    for t in range(s_ref.shape[0]):
        m = m_ref[t]
        acc = None
        for lo, hi in _key_spans(s_ref.shape[-1]):
            p_ref[t, :, lo:hi] = jnp.exp2(s_ref[t, :, lo:hi] - m).astype(BF16)
            part = jnp.dot(p_ref[t, :, lo:hi], v_ref[lo:hi, :], preferred_element_type=F32)
            acc = part if acc is None else acc + part
        accs.append(acc)
    return accs


def _attn_pipeline(q_ref, kt_ref, v_ref, o_ref, bufs, p_ref, q_ops_fn, values_fn):
    s0, m0 = bufs[0]
    tq = s0.shape[1]
    n = q_ref.shape[0] // tq

    def scores(t, s_ref, m_ref):
        row = t * tq if isinstance(t, int) else pl.multiple_of(t * tq, tq)
        _attn_scores(q_ops_fn(q_ref[pl.ds(row, tq), :]), kt_ref, s_ref, m_ref)

    def values(t, s_ref, m_ref):
        row = t * tq if isinstance(t, int) else pl.multiple_of(t * tq, tq)
        o_ref[pl.ds(row, tq), :] = values_fn(v_ref, s_ref, m_ref, p_ref).astype(o_ref.dtype)

    if len(bufs) == 1:
        def one(t, _):
            scores(t, s0, m0)
            values(t, s0, m0)
            return 0

        if n == 1:
            one(0, 0)
        else:
            lax.fori_loop(0, n, one, 0)
        return

    s1, m1 = bufs[1]
    scores(0, s0, m0)
    if n == 1:
        values(0, s0, m0)
        return
    assert n % 2 == 0

    def pair(k, _):
        t = 2 * k
        scores(t + 1, s1, m1)
        values(t, s0, m0)
        scores(t + 2, s0, m0)
        values(t + 1, s1, m1)
        return 0

    lax.fori_loop(0, n // 2 - 1, pair, 0)
    scores(n - 1, s1, m1)
    values(n - 2, s0, m0)
    values(n - 1, s1, m1)


def _score_bufs(scratch):
    return tuple(zip(scratch[0:-1:2], scratch[1:-1:2])), scratch[-1]


def _mla_attn_kernel(q_ref, kt_ref, v_ref, o_ref, *scratch):
    dv = o_ref.shape[-1]
    bufs, p_ref = _score_bufs(scratch)

    def values(*refs):
        acc = _attn_values(*refs)[0]
        return acc[:, :dv] / acc[:, dv:dv + 1]

    _attn_pipeline(q_ref, kt_ref, v_ref, o_ref, bufs, p_ref, lambda q: [q], values)


def _diff_attn_kernel(q_ref, kt_ref, v_ref, g_ref, lam_ref, o_ref, *scratch, out_scale):
    dv = o_ref.shape[-1]
    bufs, p_ref = _score_bufs(scratch)

    def q_ops(q):
        lane = lax.broadcasted_iota(jnp.int32, q.shape, 1)
        zero = jnp.zeros_like(q)
        return [jnp.where(lane < DIFF_HEAD_DIM, q, zero), jnp.where(lane < DIFF_HEAD_DIM, zero, q)]

    def values(*refs):
        a1, a2 = _attn_values(*refs)
        o = a1[:, :dv] / a1[:, dv:dv + 1] - lam_ref[0] * (a2[:, :dv] / a2[:, dv:dv + 1])
        y = o * lax.rsqrt(jnp.mean(o * o, axis=-1, keepdims=True) + NORM_EPS) * g_ref[...]
        return y * out_scale

    _attn_pipeline(q_ref, kt_ref, v_ref, o_ref, bufs, p_ref, q_ops, values)


def _attention(kernel_fn, name, n_softmax, q, kt, v, nb, ctx, seq, heads, dk, dv, ctx_queries,
               extra_in=(), extra_specs=()):
    l = ctx + seq
    q3 = q.reshape(nb, l, heads * dk)
    v3 = v.reshape(nb, l, heads * 2 * dv)
    if ctx_queries:
        tq, n_keys, rows = ctx, ctx, ctx
        blk = seq // ctx
        in_specs = [pl.BlockSpec((None, rows, dk), lambda b, h: (b, blk, h)),
                    pl.BlockSpec((None, None, dk, ctx), lambda b, h: (b, h, 0, blk)),
                    pl.BlockSpec((None, ctx, 2 * dv), lambda b, h: (b, blk, h))]
    else:
        tq, n_keys, rows = TQ, l, seq
        in_specs = [pl.BlockSpec((None, rows, dk), lambda b, h: (b, 0, h)),
                    pl.BlockSpec((None, None, dk, l), lambda b, h: (b, h, 0, 0)),
                    pl.BlockSpec((None, l, 2 * dv), lambda b, h: (b, 0, h))]
    score_buf = [pltpu.VMEM((n_softmax, tq, n_keys), F32), pltpu.VMEM((n_softmax, tq, 1), F32)]
    out = pl.pallas_call(
        kernel_fn,
        grid=(nb, heads),
        in_specs=in_specs + list(extra_specs),
        out_specs=pl.BlockSpec((None, rows, dv), lambda b, h: (b, 0, h)),
        out_shape=jax.ShapeDtypeStruct((nb, rows, heads * dv), BF16),
        scratch_shapes=score_buf * (2 if n_softmax == 1 else 1) + [pltpu.VMEM((n_softmax, tq, n_keys), BF16)],
        compiler_params=_cparams(("parallel", "parallel")),
        name=name,
    )(q3, kt, v3, *extra_in)
    return out


def _key_transpose(k, nb, l, heads, dk):
    return jnp.transpose(k.reshape(nb, l, heads, dk), (0, 2, 3, 1))


def _attend_all(kernel_fn, name, n_softmax, q, k, v, nb, ctx, seq, heads, dk, dv, need_ctx, **extra):
    kt = _key_transpose(k, nb, ctx + seq, heads, dk)
    lat = _attention(kernel_fn, name, n_softmax, q, kt, v, nb, ctx, seq, heads, dk, dv, False, **extra)
    if not need_ctx:
        return lat.reshape(nb * seq, heads * dv)
    cx = _attention(kernel_fn, name + "_ctx", n_softmax, q, kt, v, nb, ctx, seq, heads, dk, dv, True, **extra)
    return jnp.concatenate([lat, cx], axis=1).reshape(nb * (ctx + seq), heads * dv)


def _group_ms(x, gmat):
    return jnp.dot((x * x).astype(BF16), gmat, preferred_element_type=F32)


def _odd_proj_kernel(x_ref, g_ref, sc_ref, sh_ref, wq_ref, wk_ref, wv_ref, qg_ref, kg_ref, gm_ref,
                     cos_ref, sa_ref, sb_ref, q_ref, k_ref, v_ref):
    h = _norm_mod(x_ref[...], g_ref[...], sc_ref[...], sh_ref[...]).astype(BF16)
    cos, sa, sb = cos_ref[...], sa_ref[...], sb_ref[...]
    gmat = gm_ref[...]
    q_scale = DIFF_HEAD_DIM ** -0.5 * LOG2_E
    for w_ref, gain_ref, o_ref, scale in ((wq_ref, qg_ref, q_ref, q_scale), (wk_ref, kg_ref, k_ref, 1.0)):
        raw = jnp.dot(h, w_ref[...], preferred_element_type=F32)
        gain = gain_ref[...] * scale
        for c in range(raw.shape[1] // LANES):
            t = raw[:, c * LANES:(c + 1) * LANES]
            t = t * lax.rsqrt(_group_ms(t, gmat) + NORM_EPS) * gain
            o_ref[:, c * LANES:(c + 1) * LANES] = _rope_tile(t, cos, sa, sb).astype(BF16)
    v = jnp.dot(h, wv_ref[...], preferred_element_type=F32).astype(BF16)
    for c in range(v.shape[1] // DIFF_V):
        v_ref[:, (2 * c) * DIFF_V:(2 * c + 1) * DIFF_V] = v[:, c * DIFF_V:(c + 1) * DIFF_V]
        v_ref[:, (2 * c + 1) * DIFF_V:(2 * c + 2) * DIFF_V] = _ones_column((v.shape[0], DIFF_V))


def _odd_proj(x, gain, sc, sh, wq, wk, wv, q_gain, k_gain, gmat, rope, tpb, nb):
    t, d = x.shape
    midx = _mod_index(tpb, nb)
    cos, sa, sb = rope

    def const(a):
        return pl.BlockSpec(a.shape, lambda i: (0, 0))

    tab = pl.BlockSpec((TM, LANES), lambda i: (i % tpb, 0))
    n = wq.shape[1]
    return pl.pallas_call(
        _odd_proj_kernel,
        grid=(t // TM,),
        in_specs=[pl.BlockSpec((TM, d), lambda i: (i, 0)), const(gain),
                  pl.BlockSpec((None, 1, d), lambda i: (midx(i), 0, 0)),
                  pl.BlockSpec((None, 1, d), lambda i: (midx(i), 0, 0)),
                  const(wq), const(wk), const(wv), const(q_gain), const(k_gain), const(gmat),
                  tab, tab, tab],
        out_specs=[pl.BlockSpec((TM, n), lambda i: (i, 0))] * 2 + [pl.BlockSpec((TM, 2 * n), lambda i: (i, 0))],
        out_shape=[jax.ShapeDtypeStruct((t, n), BF16)] * 2 + [jax.ShapeDtypeStruct((t, 2 * n), BF16)],
        compiler_params=_cparams(("parallel",)),
        name="odd_in_proj",
    )(x, gain, sc, sh, wq, wk, wv, q_gain, k_gain, gmat, cos, sa, sb)


def _post_mix_kernel(*refs, n_mix):
    mix_refs = refs[:n_mix]
    w_refs = refs[n_mix:2 * n_mix]
    (x_ref, g1_ref, n2_ref, sc_ref, sh_ref, rwh_ref, rwl_ref, rb_ref,
     xo_ref, h2_ref, te_ref, tw_ref, tr_ref, cnt_ref, base_ref) = refs[2 * n_mix:]
    i = pl.program_id(0)

    @pl.when(i == 0)
    def _():
        base_ref[...] = jnp.zeros_like(base_ref)

    d = x_ref.shape[-1]
    halves = []
    for lo in (0, d // 2):
        m = jnp.dot(mix_refs[0][...], w_refs[0][:, lo:lo + d // 2], preferred_element_type=F32)
        for a_ref, w_ref in zip(mix_refs[1:], w_refs[1:]):
            m = m + jnp.dot(a_ref[...], w_ref[:, lo:lo + d // 2], preferred_element_type=F32)
        halves.append(m)
    x = x_ref[...] + g1_ref[...] * jnp.concatenate(halves, axis=-1)
    xo_ref[...] = x
    h2 = _norm_mod(x, n2_ref[...], sc_ref[...], sh_ref[...])
    h2_hi = h2.astype(BF16)
    h2_ref[...] = h2_hi

    h2_lo = (h2 - h2_hi.astype(F32)).astype(BF16)
    logits = (jnp.dot(h2_hi, rwh_ref[...], preferred_element_type=F32)
              + jnp.dot(h2_lo, rwh_ref[...], preferred_element_type=F32)
              + jnp.dot(h2_hi, rwl_ref[...], preferred_element_type=F32)) + rb_ref[...]
    lane = lax.broadcasted_iota(jnp.int32, logits.shape, 1)
    lane_f = lane.astype(F32)
    vals = jnp.where(lane < N_EXPERTS, logits, NEG_BIG)
    tops, firsts, hots = [], [], []
    for _ in range(TOP_K):
        top = jnp.max(vals, axis=-1, keepdims=True)
        first = jnp.min(jnp.where(vals == top, lane_f, float(LANES)), axis=-1, keepdims=True)
        hot = lane_f == first
        tops.append(top)
        firsts.append(first)
        hots.append(hot)
        vals = jnp.where(hot, 2.0 * NEG_BIG, vals)
    exps = [jnp.exp(t - tops[0]) for t in tops]
    denom = exps[0] + exps[1] + exps[2] + exps[3]

    picked = (hots[0] | hots[1]) | (hots[2] | hots[3])
    cnt = jnp.where(picked, 1.0, 0.0)
    r_io = lax.broadcasted_iota(jnp.int32, (TM, TM), 0)
    c_io = lax.broadcasted_iota(jnp.int32, (TM, TM), 1)
    tri = jnp.where(c_io < r_io, 1.0, 0.0).astype(BF16)
    before = jnp.dot(tri, cnt.astype(BF16), preferred_element_type=F32) + base_ref[...]
    te = jnp.zeros(logits.shape, F32)
    tw = jnp.zeros(logits.shape, F32)
    tr = jnp.zeros(logits.shape, F32)
    for k in range(TOP_K):
        r_k = jnp.sum(jnp.where(hots[k], before, 0.0), axis=-1, keepdims=True)
        te = jnp.where(lane == k, firsts[k], te)
        tw = jnp.where(lane == k, exps[k] / denom, tw)
        tr = jnp.where(lane == k, r_k, tr)
    te_ref[...] = te.astype(jnp.int32)
    tw_ref[...] = tw
    tr_ref[...] = tr.astype(jnp.int32)
    base_ref[...] += jnp.sum(cnt, axis=0, keepdims=True)
    cnt_ref[...] = base_ref[...]


def _post_mix(mixes, weights, x, g1, n2g, sc2, sh2, router_w, router_b, tpb, nb, latent_only):
    d = x.shape[1]
    tiles_in = tpb
    if latent_only:
        tiles_out = tpb - 1
        n_tiles = nb * tiles_out

        def in_row(i):
            return (i // tiles_out) * tiles_in + i % tiles_out

        def midx(i):
            return i // tiles_out
    else:
        n_tiles = nb * tpb

        def in_row(i):
            return i

        midx = _mod_index(tpb, nb)
    t_out = n_tiles * TM

    def const(a):
        return pl.BlockSpec(a.shape, lambda i: (0, 0))

    def mod(a):
        return pl.BlockSpec((None, 1, d), lambda i: (midx(i), 0, 0))

    in_specs = [pl.BlockSpec((TM, a.shape[1]), lambda i: (i, 0)) for a in mixes]
    in_specs += [const(w) for w in weights]
    in_specs += [pl.BlockSpec((TM, d), lambda i: (in_row(i), 0)), mod(g1), const(n2g), mod(sc2), mod(sh2),
                 const(router_w), const(router_w), const(router_b)]
    rw_hi = router_w.astype(BF16)
    rw_lo = (router_w - rw_hi.astype(F32)).astype(BF16)
    row_out = lambda width: pl.BlockSpec((TM, width), lambda i: (i, 0))
    return pl.pallas_call(
        functools.partial(_post_mix_kernel, n_mix=len(mixes)),
        grid=(n_tiles,),
        in_specs=in_specs,
        out_specs=[row_out(d), row_out(d), row_out(LANES), row_out(LANES), row_out(LANES),
                   pl.BlockSpec((1, LANES), lambda i: (0, 0))],
        out_shape=[jax.ShapeDtypeStruct((t_out, d), F32), jax.ShapeDtypeStruct((t_out, d), BF16),
                   jax.ShapeDtypeStruct((t_out, LANES), jnp.int32), jax.ShapeDtypeStruct((t_out, LANES), F32),
                   jax.ShapeDtypeStruct((t_out, LANES), jnp.int32), jax.ShapeDtypeStruct((1, LANES), F32)],
        scratch_shapes=[pltpu.VMEM((1, LANES), F32)],
        compiler_params=_cparams(("arbitrary",)),
        name="post_mix_router",
    )(*mixes, *weights, x, g1, n2g, sc2, sh2, rw_hi, rw_lo, router_b)


def _moe_ffn_kernel(be_ref, nu_ref, nxt_ref, x_ref, wgu_hbm, bgu_ref, wd_hbm, bd_ref, o_ref,
                    wgu_f32, wd_f32, wgu_bf, wd_bf, sem, slot_ref, *, layer):
    i = pl.program_id(0)
    prev = be_ref[jnp.maximum(i - 1, 0)]
    fresh = jnp.logical_or(i == 0, be_ref[i] != prev)

    def weight_copies(expert, slot):
        return (pltpu.make_async_copy(wgu_hbm.at[layer, expert], wgu_f32.at[slot], sem.at[0, slot]),
                pltpu.make_async_copy(wd_hbm.at[layer, expert], wd_f32.at[slot], sem.at[1, slot]))

    @pl.when(i == 0)
    def _():
        slot_ref[0] = 0
        for cp in weight_copies(be_ref[0], 0):
            cp.start()

    @pl.when(jnp.logical_and(fresh, i < nu_ref[0]))
    def _():
        slot = slot_ref[0]
        for cp in weight_copies(be_ref[i], slot):
            cp.wait()
        wgu_bf[...] = wgu_f32[slot].astype(BF16)
        wd_bf[...] = wd_f32[slot].astype(BF16)
        slot_ref[0] = 1 - slot

        @pl.when(nxt_ref[i] >= 0)
        def _():
            for cp in weight_copies(nxt_ref[i], 1 - slot):
                cp.start()

    @pl.when(i < nu_ref[0])
    def _():
        d_ff = wd_bf.shape[0]
        gu = jnp.dot(x_ref[...], wgu_bf[...], preferred_element_type=F32) + bgu_ref[...]
        g = jnp.minimum(gu[:, :d_ff], SWIGLU_LIMIT)
        u = jnp.clip(gu[:, d_ff:], -SWIGLU_LIMIT, SWIGLU_LIMIT)
        act = (u + 1.0) * g * jax.nn.sigmoid(SWIGLU_ALPHA * g)
        y = jnp.dot(act.astype(BF16), wd_bf[...], preferred_element_type=F32) + bd_ref[...]
        o_ref[...] = y.astype(o_ref.dtype)

    @pl.when(i >= nu_ref[0])
    def _():
        o_ref[...] = jnp.zeros_like(o_ref)


def _moe_ffn(block_e, n_used, next_e, xs, w_gu, b_gu, w_down, b_down, layer):
    n_rows, d = xs.shape
    depth, n_e, _, two_ff = w_gu.shape
    d_ff = two_ff // 2
    n_blocks = n_rows // MOE_BM
    grid_spec = pltpu.PrefetchScalarGridSpec(
        num_scalar_prefetch=3,
        grid=(n_blocks,),
        in_specs=[pl.BlockSpec((MOE_BM, d), lambda i, be, nu, nx: (i, 0)),
                  pl.BlockSpec(memory_space=pl.ANY),
                  pl.BlockSpec((None, None, 1, two_ff), lambda i, be, nu, nx: (layer, be[i], 0, 0)),
                  pl.BlockSpec(memory_space=pl.ANY),
                  pl.BlockSpec((None, None, 1, d), lambda i, be, nu, nx: (layer, be[i], 0, 0))],
        out_specs=pl.BlockSpec((MOE_BM, d), lambda i, be, nu, nx: (i, 0)),
        scratch_shapes=[pltpu.VMEM((2, d, two_ff), F32), pltpu.VMEM((2, d_ff, d), F32),
                        pltpu.VMEM((d, two_ff), BF16), pltpu.VMEM((d_ff, d), BF16),
                        pltpu.SemaphoreType.DMA((2, 2)), pltpu.SMEM((1,), jnp.int32)],
    )
    return pl.pallas_call(
        functools.partial(_moe_ffn_kernel, layer=layer),
        grid_spec=grid_spec,
        out_shape=jax.ShapeDtypeStruct((n_rows, d), BF16),
        compiler_params=_cparams(("arbitrary",)),
        name="moe_ffn",
    )(block_e, n_used, next_e, xs, w_gu, b_gu.reshape(depth, n_e, 1, two_ff), w_down,
      b_down.reshape(depth, n_e, 1, d))


def _moe_combine_kernel(x_ref, g2_ref, w_ref, y_ref, o_ref):
    w = w_ref[...]
    f = w[:, 0:1] * y_ref[0].astype(F32)
    for k in range(1, TOP_K):
        f = f + w[:, k:k + 1] * y_ref[k].astype(F32)
    o_ref[...] = x_ref[...] + g2_ref[...] * f


def _moe_combine(x, g2, top_w, picked, midx):
    t, d = x.shape
    return pl.pallas_call(
        _moe_combine_kernel,
        grid=(t // TM,),
        in_specs=[pl.BlockSpec((TM, d), lambda i: (i, 0)),
                  pl.BlockSpec((None, 1, d), lambda i: (midx(i), 0, 0)),
                  pl.BlockSpec((TM, LANES), lambda i: (i, 0)),
                  pl.BlockSpec((TOP_K, TM, d), lambda i: (0, i, 0))],
        out_specs=pl.BlockSpec((TM, d), lambda i: (i, 0)),
        out_shape=jax.ShapeDtypeStruct((t, d), F32),
        compiler_params=_cparams(("parallel",)),
        name="moe_combine",
    )(x, g2, top_w, picked)


SMEM_TILE = 1024


def _row_tokens_kernel(lo_ref, hi_ref, dest_ref, rt_ref):
    i = pl.program_id(0)
    chunk = dest_ref.shape[0]

    @pl.when(i == 0)
    def _():
        def fill_gap(g, _):
            def fill(r, _):
                rt_ref[r] = r
                return 0
            lax.fori_loop(lo_ref[g], hi_ref[g], fill, 0)
            return 0
        lax.fori_loop(0, lo_ref.shape[0], fill_gap, 0)

    base = i * (chunk // TOP_K)

    def body(tok, _):
        for k in range(TOP_K):
            rt_ref[dest_ref[tok * TOP_K + k]] = base + tok
        return 0

    lax.fori_loop(0, chunk // TOP_K, body, 0, unroll=4)


def _row_tokens(gap_lo, gap_hi, dest_flat, n_rows):
    n_assign = dest_flat.shape[0]
    chunk = 4 * SMEM_TILE if n_assign % (4 * SMEM_TILE) == 0 else SMEM_TILE
    assert n_assign % chunk == 0
    grid_spec = pltpu.PrefetchScalarGridSpec(
        num_scalar_prefetch=2,
        grid=(n_assign // chunk,),
        in_specs=[pl.BlockSpec((chunk,), lambda i, lo, hi: (i,), memory_space=pltpu.SMEM)],
        out_specs=pl.BlockSpec(memory_space=pltpu.SMEM),
    )
    return pl.pallas_call(
        _row_tokens_kernel,
        grid_spec=grid_spec,
        out_shape=jax.ShapeDtypeStruct((n_rows,), jnp.int32),
        compiler_params=_cparams(("arbitrary",)),
        name="moe_row_tokens",
    )(gap_lo, gap_hi, dest_flat)


def _moe(x, g2, midx, h2, top_e, top_w, top_r, counts, w_gu, b_gu, w_down, b_down, layer):
    t, d = h2.shape
    n_assign = t * TOP_K
    n_blocks = -(-n_assign // MOE_BM) + N_EXPERTS
    n_rows = n_blocks * MOE_BM
    cnt = counts[0, :N_EXPERTS].astype(jnp.int32)
    padded = (cnt + MOE_BM - 1) // MOE_BM * MOE_BM
    pends = jnp.cumsum(padded)
    pstarts = pends - padded
    e = top_e[:, :TOP_K]
    expert_ids = jnp.arange(N_EXPERTS, dtype=jnp.int32)
    pstart_of_pick = jnp.sum(jnp.where(e[:, :, None] == expert_ids, pstarts, 0), axis=-1)
    dest = pstart_of_pick + top_r[:, :TOP_K]
    block_start = jnp.arange(n_blocks, dtype=jnp.int32) * MOE_BM
    block_e = jnp.sum((block_start[:, None] >= pends[None, :]).astype(jnp.int32), axis=1)
    block_e = jnp.minimum(block_e, N_EXPERTS - 1)
    n_used = (pends[-1] // MOE_BM).astype(jnp.int32).reshape(1)
    later_nonempty = (expert_ids[None, :] > block_e[:, None]) & (cnt[None, :] > 0)
    next_e = jnp.min(jnp.where(later_nonempty, expert_ids[None, :], N_EXPERTS), axis=1)
    next_e = jnp.where(next_e == N_EXPERTS, -1, next_e).astype(jnp.int32)

    gap_lo = jnp.concatenate([pstarts + cnt, pends[-1:]])
    gap_hi = jnp.concatenate([pends, jnp.full((1,), n_rows, jnp.int32)])
    row_tok = _row_tokens(gap_lo, gap_hi, dest.reshape(-1), n_rows)
    h2_big = jnp.concatenate([h2, jnp.zeros((n_rows - t, d), h2.dtype)], axis=0)
    xs = h2_big[row_tok]
    ys = _moe_ffn(block_e, n_used, next_e, xs, w_gu, b_gu, w_down, b_down, layer)
    picked = ys[dest.T.reshape(-1)].reshape(TOP_K, t, d)
    return _moe_combine(x, g2, top_w, picked, midx)


def _pad_cols(w, width):
    return jnp.concatenate([w, jnp.zeros((w.shape[0], width - w.shape[1]), w.dtype)], axis=1)


def kernel(x, c, ctx, c_ctx, mod_w, mod_b, norm1_g, norm2_g, ev_w_in, ev_w_out, lru_conv_w, lru_conv_b, lru_wa, lru_ba, lru_wx, lru_bx, lru_lambda, mla_q_norm_g, mla_w_uq, mla_kv_norm_g, mla_w_ukv, mla_qn_g, mla_kn_g, od_w_in, od_w_out, diff_qn_g, diff_kn_g, diff_lq1, diff_lk1, diff_lq2, diff_lk2, diff_subln_g, router_w, router_b, moe_w_gu, moe_b_gu, moe_w_down, moe_b_down):
    nb, seq, d = x.shape
    n_ctx = ctx.shape[1]
    depth = mod_w.shape[0]
    l = n_ctx + seq
    tpb = l // TM
    assert n_ctx == TM and seq % TQ == 0 and seq % GRID_W == 0

    xa = jnp.concatenate([x, ctx], axis=1).reshape(nb * l, d)
    cvec = jnp.concatenate([c, c_ctx[None, :], jnp.zeros((SUBLANES - nb - 1, d), F32)], axis=0)
    router_w_p = jnp.concatenate([router_w, jnp.zeros((depth, d, LANES - N_EXPERTS), F32)], axis=-1)
    router_b_p = jnp.concatenate([router_b, jnp.zeros((depth, LANES - N_EXPERTS), F32)], axis=-1)

    for layer in range(depth):
        last = layer == depth - 1
        i = layer // 2
        mod = _adaln(cvec, mod_w[layer], mod_b[layer])[:nb + 1]
        sh1, sc1, g1, sh2, sc2, g2 = (mod[:, k * d:(k + 1) * d].reshape(nb + 1, 1, d) for k in range(6))
        n1g = norm1_g[layer].reshape(1, d)
        n2g = norm2_g[layer].reshape(1, d)

        if layer % 2 == 0:
            w_in = ev_w_in[i].astype(BF16)
            o = 2 * LRU_WIDTH
            splits = [w_in[:, :LRU_WIDTH], w_in[:, LRU_WIDTH:o], w_in[:, o:o + MLA_Q_RANK],
                      w_in[:, o + MLA_Q_RANK:o + MLA_Q_RANK + MLA_KV_RANK],
                      _pad_cols(w_in[:, o + MLA_Q_RANK + MLA_KV_RANK:], LANES)]
            gl, rl, qc, kvc, kr = _even_proj(xa, n1g, sc1, sh1, splits, tpb, nb)

            w_gates, b_gates = _lru_gate_weights(lru_wa[i], lru_ba[i], lru_wx[i], lru_bx[i])
            lru = _lru(gl, rl, lru_conv_w[i], lru_conv_b[i].reshape(1, LRU_WIDTH), w_gates, b_gates,
                       lru_lambda[i], nb, n_ctx, seq)

            w_uq = mla_w_uq[i].astype(BF16).reshape(MLA_Q_RANK, MLA_HEADS, MLA_QK)
            w_q_nope = w_uq[:, :, :MLA_NOPE].reshape(MLA_Q_RANK, MLA_HEADS * MLA_NOPE)
            w_q_rope = jnp.concatenate(
                [w_uq[:, :, MLA_NOPE:], jnp.zeros((MLA_Q_RANK, MLA_HEADS, LANES - MLA_ROPE), BF16)],
                axis=-1).reshape(MLA_Q_RANK, MLA_HEADS * LANES)
            pad_g = lambda g: jnp.concatenate([g, jnp.zeros((LANES - MLA_ROPE,), F32)]).reshape(1, LANES)
            rope = _rope_tables_128(seq, n_ctx, MLA_ROPE, tile_groups=False)
            q, k, v = _mla_prep(qc, kvc, kr, mla_q_norm_g[i].reshape(1, -1), mla_kv_norm_g[i].reshape(1, -1),
                                w_q_nope, w_q_rope, mla_w_ukv[i].astype(BF16),
                                mla_qn_g[i][:MLA_NOPE].reshape(1, LANES), pad_g(mla_qn_g[i][MLA_NOPE:]),
                                mla_kn_g[i][:MLA_NOPE].reshape(1, LANES), pad_g(mla_kn_g[i][MLA_NOPE:]),
                                rope, tpb)
            att = _attend_all(_mla_attn_kernel, "mla_attention", 1, q, k, v, nb, n_ctx, seq, MLA_HEADS,
                              2 * LANES, MLA_V, need_ctx=not last)
            w_out = ev_w_out[i].astype(BF16)
            if last:
                lru = lru.reshape(nb, l, LRU_WIDTH)[:, :seq].reshape(nb * seq, LRU_WIDTH)
            mixes = [lru, att]
            weights = [w_out[:LRU_WIDTH], w_out[LRU_WIDTH:]]
        else:
            lam_init = 0.8 - 0.6 * math.exp(-0.3 * layer)
            w_in = od_w_in[i].astype(BF16)
            n_qk = DIFF_HEADS * 2 * DIFF_HEAD_DIM
            gidx = jnp.arange(LANES) // DIFF_HEAD_DIM
            gmat = jnp.where(gidx[:, None] == gidx[None, :], 1.0 / DIFF_HEAD_DIM, 0.0).astype(BF16)
            rope = _rope_tables_128(seq, n_ctx, DIFF_HEAD_DIM, tile_groups=True)
            tile_g = lambda g: jnp.tile(g, LANES // DIFF_HEAD_DIM).reshape(1, LANES)
            q, k, v = _odd_proj(xa, n1g, sc1, sh1, w_in[:, :n_qk], w_in[:, n_qk:2 * n_qk], w_in[:, 2 * n_qk:],
                                tile_g(diff_qn_g[i]), tile_g(diff_kn_g[i]), gmat, rope, tpb, nb)
            lam = (jnp.exp(jnp.sum(diff_lq1[i] * diff_lk1[i])) - jnp.exp(jnp.sum(diff_lq2[i] * diff_lk2[i]))
                   + lam_init).reshape(1).astype(F32)
            att = _attend_all(
                functools.partial(_diff_attn_kernel, out_scale=1.0 - lam_init), "diff_attention", 2,
                q, k, v, nb, n_ctx, seq, DIFF_HEADS, 2 * DIFF_HEAD_DIM, DIFF_V, need_ctx=not last,
                extra_in=(diff_subln_g[i].reshape(1, DIFF_V), lam),
                extra_specs=(pl.BlockSpec((1, DIFF_V), lambda b, h: (0, 0)),
                             pl.BlockSpec(memory_space=pltpu.SMEM)))
            mixes = [att]
            weights = [od_w_out[i].astype(BF16)]

        xo, h2, te, tw, tr, counts = _post_mix(mixes, weights, xa, g1, n2g, sc2, sh2, router_w_p[layer],
                                               router_b_p[layer].reshape(1, LANES), tpb, nb, last)
        midx = (lambda t: t // (tpb - 1)) if last else _mod_index(tpb, nb)
        xa = _moe(xo, g2, midx, h2, te, tw, tr, counts, moe_w_gu, moe_b_gu, moe_w_down, moe_b_down, layer)
    return xa.reshape(nb, seq, d)
```

```python
import functools
import math

import jax
import jax.numpy as jnp
from jax import lax
from jax.experimental import pallas as pl
from jax.experimental.pallas import tpu as pltpu

F32 = jnp.float32
BF16 = jnp.bfloat16
HIGHEST = lax.Precision.HIGHEST

GRID_W = 64
NORM_EPS = 1e-6
ROPE_BASE = 10000.0
LRU_WIDTH = 512
LRU_BLOCKS = 8
LRU_BLOCK_W = LRU_WIDTH // LRU_BLOCKS
LRU_C = 8.0
CONV_W = 4
MLA_HEADS = 4
MLA_Q_RANK = 384
MLA_KV_RANK = 256
MLA_NOPE = 128
MLA_ROPE = 64
MLA_V = 128
MLA_QK = MLA_NOPE + MLA_ROPE
DIFF_HEADS = 8
DIFF_HEAD_DIM = 64
DIFF_V = 2 * DIFF_HEAD_DIM
N_EXPERTS = 32
TOP_K = 4
SWIGLU_LIMIT = 7.0
SWIGLU_ALPHA = 1.702

LANES = 128
SUBLANES = 8
VMEM_LIMIT = 52 * 1024 * 1024

TM = 256
TQ = 512
LRU_CT = 128
LRU_CHUNK = 128
LRU_SEG = LRU_CHUNK // SUBLANES
LRU_PAD = 8
MOE_BM = 256
NEG_BIG = -1e30
LOG2_E = math.log2(math.e)


def _cparams(sem, vmem=VMEM_LIMIT):
    return pltpu.CompilerParams(dimension_semantics=sem, vmem_limit_bytes=vmem)


def _adaln_kernel(c_ref, w_ref, b_ref, o_ref):
    c = c_ref[...]
    s = c * jax.nn.sigmoid(c)
    o_ref[...] = jnp.dot(s, w_ref[...], preferred_element_type=F32, precision=HIGHEST) + b_ref[...]


def _adaln(cvec, w, b):
    rows, d = cvec.shape
    n = w.shape[1]
    tn = 1536
    return pl.pallas_call(
        _adaln_kernel,
        grid=(n // tn,),
        in_specs=[pl.BlockSpec((rows, d), lambda j: (0, 0)),
                  pl.BlockSpec((d, tn), lambda j: (0, j)),
                  pl.BlockSpec((1, tn), lambda j: (0, j))],
        out_specs=pl.BlockSpec((rows, tn), lambda j: (0, j)),
        out_shape=jax.ShapeDtypeStruct((rows, n), F32),
        compiler_params=_cparams(("arbitrary",)),
        name="adaln_mod",
    )(cvec, w, b.reshape(1, n))


def _norm_mod(x, g, sc, sh):
    var = jnp.mean(x * x, axis=-1, keepdims=True)
    y = x * lax.rsqrt(var + NORM_EPS) * g
    return y * (1.0 + sc) + sh


def _ones_column(shape):
    lane = lax.broadcasted_iota(jnp.int32, shape, 1)
    return jnp.where(lane == 0, 1.0, 0.0).astype(BF16)


def _rope_tile(x, cos, sin_a, sin_b):
    up = pltpu.roll(x, LANES - 16, axis=1)
    dn = pltpu.roll(x, 16, axis=1)
    return x * cos + up * sin_a + dn * sin_b


def _rope_tables(seq, rot_dim):
    n_rows = seq // GRID_W
    rows = jnp.repeat(jnp.arange(n_rows, dtype=F32), GRID_W)
    cols = jnp.tile(jnp.arange(GRID_W, dtype=F32), n_rows)
    axis_dim = rot_dim // 2
    inv_freq = ROPE_BASE ** (-jnp.arange(0, axis_dim, 2, dtype=F32) / axis_dim)
    ang_r = rows[:, None] * inv_freq
    ang_c = cols[:, None] * inv_freq
    ang = jnp.concatenate([ang_r, ang_r, ang_c, ang_c], axis=-1)
    cos, sin = jnp.cos(ang), jnp.sin(ang)
    quarter = rot_dim // 4
    first = (jnp.arange(rot_dim) % (2 * quarter)) < quarter
    sin_a = jnp.where(first, -sin, 0.0)
    sin_b = jnp.where(first, 0.0, sin)
    return cos, sin_a, sin_b


def _rope_tables_128(seq, ctx, rot_dim, tile_groups):
    cos, sin_a, sin_b = _rope_tables(seq, rot_dim)
    if tile_groups:
        reps = LANES // rot_dim
        cos, sin_a, sin_b = (jnp.tile(t, (1, reps)) for t in (cos, sin_a, sin_b))
    else:
        pad = LANES - rot_dim
        cos = jnp.concatenate([cos, jnp.ones((seq, pad), F32)], axis=-1)
        sin_a = jnp.concatenate([sin_a, jnp.zeros((seq, pad), F32)], axis=-1)
        sin_b = jnp.concatenate([sin_b, jnp.zeros((seq, pad), F32)], axis=-1)
    cos = jnp.concatenate([cos, jnp.ones((ctx, LANES), F32)], axis=0)
    sin_a = jnp.concatenate([sin_a, jnp.zeros((ctx, LANES), F32)], axis=0)
    sin_b = jnp.concatenate([sin_b, jnp.zeros((ctx, LANES), F32)], axis=0)
    return cos, sin_a, sin_b


def _mod_index(tpb, nb):
    def idx(i):
        return jnp.where(i % tpb == tpb - 1, nb, i // tpb)
    return idx


def _even_proj_kernel(x_ref, g_ref, sc_ref, sh_ref, wg_ref, wr_ref, wq_ref, wkv_ref, wkr_ref,
                      og_ref, or_ref, oq_ref, okv_ref, okr_ref):
    h = _norm_mod(x_ref[...], g_ref[...], sc_ref[...], sh_ref[...]).astype(BF16)
    for w_ref, o_ref in ((wg_ref, og_ref), (wr_ref, or_ref), (wq_ref, oq_ref),
                         (wkv_ref, okv_ref), (wkr_ref, okr_ref)):
        o_ref[...] = jnp.dot(h, w_ref[...], preferred_element_type=F32)


def _even_proj(x, gain, sc, sh, weights, tpb, nb):
    t, d = x.shape
    midx = _mod_index(tpb, nb)
    w_specs = [pl.BlockSpec(w.shape, lambda i: (0, 0)) for w in weights]
    return pl.pallas_call(
        _even_proj_kernel,
        grid=(t // TM,),
        in_specs=[pl.BlockSpec((TM, d), lambda i: (i, 0)),
                  pl.BlockSpec((1, d), lambda i: (0, 0)),
                  pl.BlockSpec((None, 1, d), lambda i: (midx(i), 0, 0)),
                  pl.BlockSpec((None, 1, d), lambda i: (midx(i), 0, 0))] + w_specs,
        out_specs=[pl.BlockSpec((TM, w.shape[1]), lambda i: (i, 0)) for w in weights],
        out_shape=[jax.ShapeDtypeStruct((t, w.shape[1]), F32) for w in weights],
        compiler_params=_cparams(("parallel",)),
        name="even_in_proj",
    )(x, gain, sc, sh, *weights)


def _mla_prep_kernel(qc_ref, kvc_ref, kr_ref, qng_ref, kvng_ref, wqn_ref, wqr_ref, wkv_ref,
                     qgn_ref, qgr_ref, kgn_ref, kgr_ref, cos_ref, sa_ref, sb_ref,
                     q_ref, k_ref, v_ref):
    cos, sa, sb = cos_ref[...], sa_ref[...], sb_ref[...]
    inv_qk = 1.0 / MLA_QK

    qc = qc_ref[...]
    hq = (qc * lax.rsqrt(jnp.mean(qc * qc, axis=-1, keepdims=True) + NORM_EPS) * qng_ref[...]).astype(BF16)
    q_nope = jnp.dot(hq, wqn_ref[...], preferred_element_type=F32)
    q_rope = jnp.dot(hq, wqr_ref[...], preferred_element_type=F32)
    q_scale = MLA_QK ** -0.5 * LOG2_E
    for h in range(MLA_HEADS):
        qn = q_nope[:, h * LANES:(h + 1) * LANES]
        qr = q_rope[:, h * LANES:(h + 1) * LANES]
        ms = (jnp.sum(qn * qn, axis=-1, keepdims=True) + jnp.sum(qr * qr, axis=-1, keepdims=True)) * inv_qk
        rs = lax.rsqrt(ms + NORM_EPS) * q_scale
        q_ref[:, (2 * h) * LANES:(2 * h + 1) * LANES] = (qn * rs * qgn_ref[...]).astype(BF16)
        q_ref[:, (2 * h + 1) * LANES:(2 * h + 2) * LANES] = _rope_tile(qr * rs * qgr_ref[...], cos, sa, sb).astype(BF16)

    kvc = kvc_ref[...]
    hkv = (kvc * lax.rsqrt(jnp.mean(kvc * kvc, axis=-1, keepdims=True) + NORM_EPS) * kvng_ref[...]).astype(BF16)
    kv = jnp.dot(hkv, wkv_ref[...], preferred_element_type=F32)
    kr = kr_ref[...]
    kr_ss = jnp.sum(kr * kr, axis=-1, keepdims=True)
    kr_rot = _rope_tile(kr * kgr_ref[...], cos, sa, sb)
    for h in range(MLA_HEADS):
        kn = kv[:, (2 * h) * LANES:(2 * h + 1) * LANES]
        ms = (jnp.sum(kn * kn, axis=-1, keepdims=True) + kr_ss) * inv_qk
        rs = lax.rsqrt(ms + NORM_EPS)
        k_ref[:, (2 * h) * LANES:(2 * h + 1) * LANES] = (kn * rs * kgn_ref[...]).astype(BF16)
        k_ref[:, (2 * h + 1) * LANES:(2 * h + 2) * LANES] = (kr_rot * rs).astype(BF16)
        v_ref[:, (2 * h) * LANES:(2 * h + 1) * LANES] = kv[:, (2 * h + 1) * LANES:(2 * h + 2) * LANES].astype(BF16)
        v_ref[:, (2 * h + 1) * LANES:(2 * h + 2) * LANES] = _ones_column((kv.shape[0], LANES))


def _mla_prep(qc, kvc, kr, q_norm_g, kv_norm_g, w_q_nope, w_q_rope, w_ukv, q_gn, q_gr, k_gn, k_gr,
              rope, tpb):
    t = qc.shape[0]
    cos, sa, sb = rope

    def const(a):
        return pl.BlockSpec(a.shape, lambda i: (0, 0))

    def rows(a):
        return pl.BlockSpec((TM, a.shape[1]), lambda i: (i, 0))

    tab = pl.BlockSpec((TM, LANES), lambda i: (i % tpb, 0))
    hq, hv = MLA_HEADS * 2 * LANES, MLA_HEADS * 2 * MLA_V
    return pl.pallas_call(
        _mla_prep_kernel,
        grid=(t // TM,),
        in_specs=[rows(qc), rows(kvc), rows(kr), const(q_norm_g), const(kv_norm_g),
                  const(w_q_nope), const(w_q_rope), const(w_ukv),
                  const(q_gn), const(q_gr), const(k_gn), const(k_gr), tab, tab, tab],
        out_specs=[pl.BlockSpec((TM, hq), lambda i: (i, 0)),
                   pl.BlockSpec((TM, hq), lambda i: (i, 0)),
                   pl.BlockSpec((TM, hv), lambda i: (i, 0))],
        out_shape=[jax.ShapeDtypeStruct((t, hq), BF16),
                   jax.ShapeDtypeStruct((t, hq), BF16),
                   jax.ShapeDtypeStruct((t, hv), BF16)],
        compiler_params=_cparams(("parallel",)),
        name="mla_prep",
    )(qc, kvc, kr, q_norm_g, kv_norm_g, w_q_nope, w_q_rope, w_ukv, q_gn, q_gr, k_gn, k_gr, cos, sa, sb)


def _sublane_iota():
    return lax.broadcasted_iota(jnp.int32, (SUBLANES, LANES), 0)


def _scan_chunk(a_chunk, u_chunk, h_chunk, carry, reverse):
    steps = range(LRU_SEG - 1, -1, -1) if reverse else range(LRU_SEG)
    h_loc, p_loc = [None] * LRU_SEG, [None] * LRU_SEG
    h = p = None
    for j in steps:
        a = a_chunk[pl.ds(j, SUBLANES, stride=LRU_SEG), :]
        u = u_chunk[pl.ds(j, SUBLANES, stride=LRU_SEG), :]
        if h is None:
            h, p = u, a
        else:
            h, p = a * h + u, a * p
        h_loc[j], p_loc[j] = h, p
    sub = _sublane_iota()
    seg_p, seg_h = p, h
    for d in (1, 2, 4):
        shift = SUBLANES - d if reverse else d
        prev_p = pltpu.roll(seg_p, shift, axis=0)
        prev_h = pltpu.roll(seg_h, shift, axis=0)
        valid = (sub < SUBLANES - d) if reverse else (sub >= d)
        seg_h = jnp.where(valid, seg_p * prev_h + seg_h, seg_h)
        seg_p = jnp.where(valid, seg_p * prev_p, seg_p)
    h_end = seg_h + seg_p * carry
    if reverse:
        h_in = jnp.where(sub == SUBLANES - 1, carry, pltpu.roll(h_end, SUBLANES - 1, axis=0))
        new_carry = h_end[0:1, :]
    else:
        h_in = jnp.where(sub == 0, carry, pltpu.roll(h_end, 1, axis=0))
        new_carry = h_end[SUBLANES - 1:SUBLANES, :]
    for j in range(LRU_SEG):
        h_chunk[pl.ds(j, SUBLANES, stride=LRU_SEG), :] = h_loc[j] + p_loc[j] * h_in
    return jnp.broadcast_to(new_carry, (SUBLANES, LANES))


def _lru_kernel(g_ref, r_ref, cw_ref, cb_ref, wg_ref, bg_ref, lam_ref, o_ref,
                rp_ref, af_ref, uf_ref, ab_ref, ub_ref, hf_ref, hb_ref, *, ctx, seq):
    n_ctx, n_lat = ctx // LRU_CHUNK, seq // LRU_CHUNK
    zeros_pad = jnp.zeros((LRU_PAD, LRU_CT), F32)
    ctx0 = seq + 2 * LRU_PAD
    rp_ref[0:LRU_PAD, :] = zeros_pad
    rp_ref[LRU_PAD:LRU_PAD + seq, :] = r_ref[0:seq, :]
    rp_ref[LRU_PAD + seq:ctx0, :] = zeros_pad
    rp_ref[ctx0:ctx0 + ctx, :] = r_ref[seq:seq + ctx, :]
    rp_ref[ctx0 + ctx:ctx0 + ctx + LRU_PAD, :] = zeros_pad

    cw = cw_ref[...]
    cb = cb_ref[...]
    wg = wg_ref[...]
    bg = bg_ref[...]
    lam = lam_ref[...]
    sp = jnp.maximum(-lam, 0.0) + jnp.log1p(jnp.exp(-jnp.abs(lam)))

    def coeff_chunk(c, pad_off, row_off):
        start = pl.multiple_of(pad_off + c * LRU_CHUNK, SUBLANES)
        ext = rp_ref[pl.ds(start, LRU_CHUNK + 2 * LRU_PAD), :]
        x = cb
        for tap in range(CONV_W):
            lo = LRU_PAD - 2 + tap
            x = x + ext[lo:lo + LRU_CHUNK, :] * cw[tap:tap + 1, :]
        gates = jnp.dot(x.astype(BF16), wg, preferred_element_type=F32) + bg
        out_row = pl.multiple_of(row_off + c * LRU_CHUNK, SUBLANES)
        for d, (a_ref, u_ref) in enumerate(((af_ref, uf_ref), (ab_ref, ub_ref))):
            r = jax.nn.sigmoid(gates[:, (2 * d) * LRU_CT:(2 * d + 1) * LRU_CT])
            i = jax.nn.sigmoid(gates[:, (2 * d + 1) * LRU_CT:(2 * d + 2) * LRU_CT])
            log_a = -LRU_C * r * sp[d:d + 1, :]
            a_ref[pl.ds(out_row, LRU_CHUNK), :] = jnp.exp(log_a)
            th = jnp.tanh(log_a)
            u_ref[pl.ds(out_row, LRU_CHUNK), :] = jnp.sqrt(-2.0 * th / (1.0 - th)) * i * x
        return None

    def coeff_ctx(c, _):
        coeff_chunk(c, seq + LRU_PAD, seq)
        return 0

    def coeff_lat(c, _):
        coeff_chunk(c, 0, 0)
        return 0

    lax.fori_loop(0, n_ctx, coeff_ctx, 0, unroll=2)
    lax.fori_loop(0, n_lat, coeff_lat, 0, unroll=2)

    def scan_pair(n, row_off):
        def body(c, carry):
            cf, cb_ = carry
            f_row = pl.multiple_of(row_off + c * LRU_CHUNK, SUBLANES)
            b_row = pl.multiple_of(row_off + (n - 1 - c) * LRU_CHUNK, SUBLANES)
            cf = _scan_chunk(af_ref.at[pl.ds(f_row, LRU_CHUNK), :], uf_ref.at[pl.ds(f_row, LRU_CHUNK), :],
                             hf_ref.at[pl.ds(f_row, LRU_CHUNK), :], cf, False)
            cb_ = _scan_chunk(ab_ref.at[pl.ds(b_row, LRU_CHUNK), :], ub_ref.at[pl.ds(b_row, LRU_CHUNK), :],
                              hb_ref.at[pl.ds(b_row, LRU_CHUNK), :], cb_, True)
            return cf, cb_
        return body

    zero = jnp.zeros((SUBLANES, LANES), F32)
    carry = lax.fori_loop(0, n_ctx, scan_pair(n_ctx, seq), (zero, zero), unroll=2)
    lax.fori_loop(0, n_lat, scan_pair(n_lat, 0), carry, unroll=2)

    g = g_ref[...]
    gelu = 0.5 * g * (1.0 + jnp.tanh(math.sqrt(2.0 / math.pi) * (g + 0.044715 * (g * g * g))))
    o_ref[...] = (gelu * (hf_ref[...] + hb_ref[...])).astype(BF16)


def _lru(g, r, conv_w, conv_b, w_gates, b_gates, lam, nb, ctx, seq):
    l = ctx + seq
    width = g.shape[1]
    n_ct = width // LRU_CT
    g3 = g.reshape(nb, l, width)
    r3 = r.reshape(nb, l, width)
    seq_spec = pl.BlockSpec((None, l, LRU_CT), lambda b, c: (b, 0, c))
    scratch = [pltpu.VMEM((l + 3 * LRU_PAD, LRU_CT), F32)] + [pltpu.VMEM((l, LRU_CT), F32)] * 6
    out = pl.pallas_call(
        functools.partial(_lru_kernel, ctx=ctx, seq=seq),
        grid=(nb, n_ct),
        in_specs=[seq_spec, seq_spec,
                  pl.BlockSpec((CONV_W, LRU_CT), lambda b, c: (0, c)),
                  pl.BlockSpec((1, LRU_CT), lambda b, c: (0, c)),
                  pl.BlockSpec((None, LRU_CT, 4 * LRU_CT), lambda b, c: (c, 0, 0)),
                  pl.BlockSpec((None, 1, 4 * LRU_CT), lambda b, c: (c, 0, 0)),
                  pl.BlockSpec((2, LRU_CT), lambda b, c: (0, c))],
        out_specs=seq_spec,
        out_shape=jax.ShapeDtypeStruct((nb, l, width), BF16),
        scratch_shapes=scratch,
        compiler_params=_cparams(("parallel", "parallel")),
        name="rglru",
    )(g3, r3, conv_w, conv_b, w_gates, b_gates, lam)
    return out.reshape(nb * l, width)


def _lru_gate_weights(wa, ba, wx, bx):
    per = LRU_CT // LRU_BLOCK_W
    n_ct = LRU_BLOCKS // per
    eye = jnp.eye(per, dtype=F32)

    def dense(w):
        w4 = w.reshape(n_ct, per, LRU_BLOCK_W, LRU_BLOCK_W)
        return jnp.einsum('cide,ij->cidje', w4, eye).reshape(n_ct, LRU_CT, LRU_CT)

    w = jnp.concatenate([dense(wa[0]), dense(wx[0]), dense(wa[1]), dense(wx[1])], axis=-1)
    b = jnp.concatenate([v.reshape(n_ct, 1, LRU_CT) for v in (ba[0], bx[0], ba[1], bx[1])], axis=-1)
    return w.astype(BF16), b


def _key_spans(n_keys):
    half = n_keys // 2 if n_keys % (2 * LANES) == 0 else n_keys
    return [(lo, lo + half) for lo in range(0, n_keys, half)]


def _attn_scores(q_ops, kt_ref, s_ref, m_ref):
    for t, q in enumerate(q_ops):
        for lo, hi in _key_spans(kt_ref.shape[-1]):
            s_ref[t, :, lo:hi] = jnp.dot(q, kt_ref[:, lo:hi], preferred_element_type=F32)
        m_ref[t] = jnp.max(s_ref[t], axis=-1, keepdims=True)


def _attn_values(v_ref, s_ref, m_ref, p_ref):
    accs = []
    for t in range(s_ref.shape[0]):
        m = m_ref[t]
        acc = None
        for lo, hi in _key_spans(s_ref.shape[-1]):
            p_ref[t, :, lo:hi] = jnp.exp2(s_ref[t, :, lo:hi] - m).astype(BF16)
            part = jnp.dot(p_ref[t, :, lo:hi], v_ref[lo:hi, :], preferred_element_type=F32)
            acc = part if acc is None else acc + part
        accs.append(acc)
    return accs


def _attn_pipeline(q_ref, kt_ref, v_ref, o_ref, bufs, p_ref, q_ops_fn, values_fn):
    s0, m0 = bufs[0]
    tq = s0.shape[1]
    n = q_ref.shape[0] // tq

    def scores(t, s_ref, m_ref):
        row = t * tq if isinstance(t, int) else pl.multiple_of(t * tq, tq)
        _attn_scores(q_ops_fn(q_ref[pl.ds(row, tq), :]), kt_ref, s_ref, m_ref)

    def values(t, s_ref, m_ref):
        row = t * tq if isinstance(t, int) else pl.multiple_of(t * tq, tq)
        o_ref[pl.ds(row, tq), :] = values_fn(v_ref, s_ref, m_ref, p_ref).astype(o_ref.dtype)

    if len(bufs) == 1:
        def one(t, _):
            scores(t, s0, m0)
            values(t, s0, m0)
            return 0

        if n == 1:
            one(0, 0)
        else:
            lax.fori_loop(0, n, one, 0)
        return

    s1, m1 = bufs[1]
    scores(0, s0, m0)
    if n == 1:
        values(0, s0, m0)
        return
    assert n % 2 == 0

    def pair(k, _):
        t = 2 * k
        scores(t + 1, s1, m1)
        values(t, s0, m0)
        scores(t + 2, s0, m0)
        values(t + 1, s1, m1)
        return 0

    lax.fori_loop(0, n // 2 - 1, pair, 0)
    scores(n - 1, s1, m1)
    values(n - 2, s0, m0)
    values(n - 1, s1, m1)


def _score_bufs(scratch):
    return tuple(zip(scratch[0:-1:2], scratch[1:-1:2])), scratch[-1]


def _mla_attn_kernel(q_ref, kt_ref, v_ref, o_ref, *scratch):
    dv = o_ref.shape[-1]
    bufs, p_ref = _score_bufs(scratch)

    def values(*refs):
        acc = _attn_values(*refs)[0]
        return acc[:, :dv] / acc[:, dv:dv + 1]

    _attn_pipeline(q_ref, kt_ref, v_ref, o_ref, bufs, p_ref, lambda q: [q], values)


def _diff_attn_kernel(q_ref, kt_ref, v_ref, g_ref, lam_ref, o_ref, *scratch, out_scale):
    dv = o_ref.shape[-1]
    bufs, p_ref = _score_bufs(scratch)

    def q_ops(q):
        lane = lax.broadcasted_iota(jnp.int32, q.shape, 1)
        zero = jnp.zeros_like(q)
        return [jnp.where(lane < DIFF_HEAD_DIM, q, zero), jnp.where(lane < DIFF_HEAD_DIM, zero, q)]

    def values(*refs):
        a1, a2 = _attn_values(*refs)
        o = a1[:, :dv] / a1[:, dv:dv + 1] - lam_ref[0] * (a2[:, :dv] / a2[:, dv:dv + 1])
        y = o * lax.rsqrt(jnp.mean(o * o, axis=-1, keepdims=True) + NORM_EPS) * g_ref[...]
        return y * out_scale

    _attn_pipeline(q_ref, kt_ref, v_ref, o_ref, bufs, p_ref, q_ops, values)


def _attention(kernel_fn, name, n_softmax, q, kt, v, nb, ctx, seq, heads, dk, dv, ctx_queries,
               extra_in=(), extra_specs=()):
    l = ctx + seq
    q3 = q.reshape(nb, l, heads * dk)
    v3 = v.reshape(nb, l, heads * 2 * dv)
    if ctx_queries:
        tq, n_keys, rows = ctx, ctx, ctx
        blk = seq // ctx
        in_specs = [pl.BlockSpec((None, rows, dk), lambda b, h: (b, blk, h)),
                    pl.BlockSpec((None, None, dk, ctx), lambda b, h: (b, h, 0, blk)),
                    pl.BlockSpec((None, ctx, 2 * dv), lambda b, h: (b, blk, h))]
    else:
        tq, n_keys, rows = TQ, l, seq
        in_specs = [pl.BlockSpec((None, rows, dk), lambda b, h: (b, 0, h)),
                    pl.BlockSpec((None, None, dk, l), lambda b, h: (b, h, 0, 0)),
                    pl.BlockSpec((None, l, 2 * dv), lambda b, h: (b, 0, h))]
    score_buf = [pltpu.VMEM((n_softmax, tq, n_keys), F32), pltpu.VMEM((n_softmax, tq, 1), F32)]
    out = pl.pallas_call(
        kernel_fn,
        grid=(nb, heads),
        in_specs=in_specs + list(extra_specs),
        out_specs=pl.BlockSpec((None, rows, dv), lambda b, h: (b, 0, h)),
        out_shape=jax.ShapeDtypeStruct((nb, rows, heads * dv), BF16),
        scratch_shapes=score_buf * (2 if n_softmax == 1 else 1) + [pltpu.VMEM((n_softmax, tq, n_keys), BF16)],
        compiler_params=_cparams(("parallel", "parallel")),
        name=name,
    )(q3, kt, v3, *extra_in)
    return out


def _key_transpose(k, nb, l, heads, dk):
    return jnp.transpose(k.reshape(nb, l, heads, dk), (0, 2, 3, 1))


def _attend_all(kernel_fn, name, n_softmax, q, k, v, nb, ctx, seq, heads, dk, dv, need_ctx, **extra):
    kt = _key_transpose(k, nb, ctx + seq, heads, dk)
    lat = _attention(kernel_fn, name, n_softmax, q, kt, v, nb, ctx, seq, heads, dk, dv, False, **extra)
    if not need_ctx:
        return lat.reshape(nb * seq, heads * dv)
    cx = _attention(kernel_fn, name + "_ctx", n_softmax, q, kt, v, nb, ctx, seq, heads, dk, dv, True, **extra)
    return jnp.concatenate([lat, cx], axis=1).reshape(nb * (ctx + seq), heads * dv)


def _group_ms(x, gmat):
    return jnp.dot((x * x).astype(BF16), gmat, preferred_element_type=F32)


def _odd_proj_kernel(x_ref, g_ref, sc_ref, sh_ref, wq_ref, wk_ref, wv_ref, qg_ref, kg_ref, gm_ref,
                     cos_ref, sa_ref, sb_ref, q_ref, k_ref, v_ref):
    h = _norm_mod(x_ref[...], g_ref[...], sc_ref[...], sh_ref[...]).astype(BF16)
    cos, sa, sb = cos_ref[...], sa_ref[...], sb_ref[...]
    gmat = gm_ref[...]
    q_scale = DIFF_HEAD_DIM ** -0.5 * LOG2_E
    for w_ref, gain_ref, o_ref, scale in ((wq_ref, qg_ref, q_ref, q_scale), (wk_ref, kg_ref, k_ref, 1.0)):
        raw = jnp.dot(h, w_ref[...], preferred_element_type=F32)
        gain = gain_ref[...] * scale
        for c in range(raw.shape[1] // LANES):
            t = raw[:, c * LANES:(c + 1) * LANES]
            t = t * lax.rsqrt(_group_ms(t, gmat) + NORM_EPS) * gain
            o_ref[:, c * LANES:(c + 1) * LANES] = _rope_tile(t, cos, sa, sb).astype(BF16)
    v = jnp.dot(h, wv_ref[...], preferred_element_type=F32).astype(BF16)
    for c in range(v.shape[1] // DIFF_V):
        v_ref[:, (2 * c) * DIFF_V:(2 * c + 1) * DIFF_V] = v[:, c * DIFF_V:(c + 1) * DIFF_V]
        v_ref[:, (2 * c + 1) * DIFF_V:(2 * c + 2) * DIFF_V] = _ones_column((v.shape[0], DIFF_V))


def _odd_proj(x, gain, sc, sh, wq, wk, wv, q_gain, k_gain, gmat, rope, tpb, nb):
    t, d = x.shape
    midx = _mod_index(tpb, nb)
    cos, sa, sb = rope

    def const(a):
        return pl.BlockSpec(a.shape, lambda i: (0, 0))

    tab = pl.BlockSpec((TM, LANES), lambda i: (i % tpb, 0))
    n = wq.shape[1]
    return pl.pallas_call(
        _odd_proj_kernel,
        grid=(t // TM,),
        in_specs=[pl.BlockSpec((TM, d), lambda i: (i, 0)), const(gain),
                  pl.BlockSpec((None, 1, d), lambda i: (midx(i), 0, 0)),
                  pl.BlockSpec((None, 1, d), lambda i: (midx(i), 0, 0)),
                  const(wq), const(wk), const(wv), const(q_gain), const(k_gain), const(gmat),
                  tab, tab, tab],
        out_specs=[pl.BlockSpec((TM, n), lambda i: (i, 0))] * 2 + [pl.BlockSpec((TM, 2 * n), lambda i: (i, 0))],
        out_shape=[jax.ShapeDtypeStruct((t, n), BF16)] * 2 + [jax.ShapeDtypeStruct((t, 2 * n), BF16)],
        compiler_params=_cparams(("parallel",)),
        name="odd_in_proj",
    )(x, gain, sc, sh, wq, wk, wv, q_gain, k_gain, gmat, cos, sa, sb)


def _post_mix_kernel(*refs, n_mix):
    mix_refs = refs[:n_mix]
    w_refs = refs[n_mix:2 * n_mix]
    (x_ref, g1_ref, n2_ref, sc_ref, sh_ref, rwh_ref, rwl_ref, rb_ref,
     xo_ref, h2_ref, te_ref, tw_ref, tr_ref, cnt_ref, base_ref) = refs[2 * n_mix:]
    i = pl.program_id(0)

    @pl.when(i == 0)
    def _():
        base_ref[...] = jnp.zeros_like(base_ref)

    d = x_ref.shape[-1]
    halves = []
    for lo in (0, d // 2):
        m = jnp.dot(mix_refs[0][...], w_refs[0][:, lo:lo + d // 2], preferred_element_type=F32)
        for a_ref, w_ref in zip(mix_refs[1:], w_refs[1:]):
            m = m + jnp.dot(a_ref[...], w_ref[:, lo:lo + d // 2], preferred_element_type=F32)
        halves.append(m)
    x = x_ref[...] + g1_ref[...] * jnp.concatenate(halves, axis=-1)
    xo_ref[...] = x
    h2 = _norm_mod(x, n2_ref[...], sc_ref[...], sh_ref[...])
    h2_hi = h2.astype(BF16)
    h2_ref[...] = h2_hi

    h2_lo = (h2 - h2_hi.astype(F32)).astype(BF16)
    logits = (jnp.dot(h2_hi, rwh_ref[...], preferred_element_type=F32)
              + jnp.dot(h2_lo, rwh_ref[...], preferred_element_type=F32)
              + jnp.dot(h2_hi, rwl_ref[...], preferred_element_type=F32)) + rb_ref[...]
    lane = lax.broadcasted_iota(jnp.int32, logits.shape, 1)
    lane_f = lane.astype(F32)
    vals = jnp.where(lane < N_EXPERTS, logits, NEG_BIG)
    tops, firsts, hots = [], [], []
    for _ in range(TOP_K):
        top = jnp.max(vals, axis=-1, keepdims=True)
        first = jnp.min(jnp.where(vals == top, lane_f, float(LANES)), axis=-1, keepdims=True)
        hot = lane_f == first
        tops.append(top)
        firsts.append(first)
        hots.append(hot)
        vals = jnp.where(hot, 2.0 * NEG_BIG, vals)
    exps = [jnp.exp(t - tops[0]) for t in tops]
    denom = exps[0] + exps[1] + exps[2] + exps[3]

    picked = (hots[0] | hots[1]) | (hots[2] | hots[3])
    cnt = jnp.where(picked, 1.0, 0.0)
    r_io = lax.broadcasted_iota(jnp.int32, (TM, TM), 0)
    c_io = lax.broadcasted_iota(jnp.int32, (TM, TM), 1)
    tri = jnp.where(c_io < r_io, 1.0, 0.0).astype(BF16)
    before = jnp.dot(tri, cnt.astype(BF16), preferred_element_type=F32) + base_ref[...]
    te = jnp.zeros(logits.shape, F32)
    tw = jnp.zeros(logits.shape, F32)
    tr = jnp.zeros(logits.shape, F32)
    for k in range(TOP_K):
        r_k = jnp.sum(jnp.where(hots[k], before, 0.0), axis=-1, keepdims=True)
        te = jnp.where(lane == k, firsts[k], te)
        tw = jnp.where(lane == k, exps[k] / denom, tw)
        tr = jnp.where(lane == k, r_k, tr)
    te_ref[...] = te.astype(jnp.int32)
    tw_ref[...] = tw
    tr_ref[...] = tr.astype(jnp.int32)
    base_ref[...] += jnp.sum(cnt, axis=0, keepdims=True)
    cnt_ref[...] = base_ref[...]


def _post_mix(mixes, weights, x, g1, n2g, sc2, sh2, router_w, router_b, tpb, nb, latent_only):
    d = x.shape[1]
    tiles_in = tpb
    if latent_only:
        tiles_out = tpb - 1
        n_tiles = nb * tiles_out

        def in_row(i):
            return (i // tiles_out) * tiles_in + i % tiles_out

        def midx(i):
            return i // tiles_out
    else:
        n_tiles = nb * tpb

        def in_row(i):
            return i

        midx = _mod_index(tpb, nb)
    t_out = n_tiles * TM

    def const(a):
        return pl.BlockSpec(a.shape, lambda i: (0, 0))

    def mod(a):
        return pl.BlockSpec((None, 1, d), lambda i: (midx(i), 0, 0))

    in_specs = [pl.BlockSpec((TM, a.shape[1]), lambda i: (i, 0)) for a in mixes]
    in_specs += [const(w) for w in weights]
    in_specs += [pl.BlockSpec((TM, d), lambda i: (in_row(i), 0)), mod(g1), const(n2g), mod(sc2), mod(sh2),
                 const(router_w), const(router_w), const(router_b)]
    rw_hi = router_w.astype(BF16)
    rw_lo = (router_w - rw_hi.astype(F32)).astype(BF16)
    row_out = lambda width: pl.BlockSpec((TM, width), lambda i: (i, 0))
    return pl.pallas_call(
        functools.partial(_post_mix_kernel, n_mix=len(mixes)),
        grid=(n_tiles,),
        in_specs=in_specs,
        out_specs=[row_out(d), row_out(d), row_out(LANES), row_out(LANES), row_out(LANES),
                   pl.BlockSpec((1, LANES), lambda i: (0, 0))],
        out_shape=[jax.ShapeDtypeStruct((t_out, d), F32), jax.ShapeDtypeStruct((t_out, d), BF16),
                   jax.ShapeDtypeStruct((t_out, LANES), jnp.int32), jax.ShapeDtypeStruct((t_out, LANES), F32),
                   jax.ShapeDtypeStruct((t_out, LANES), jnp.int32), jax.ShapeDtypeStruct((1, LANES), F32)],
        scratch_shapes=[pltpu.VMEM((1, LANES), F32)],
        compiler_params=_cparams(("arbitrary",)),
        name="post_mix_router",
    )(*mixes, *weights, x, g1, n2g, sc2, sh2, rw_hi, rw_lo, router_b)


def _moe_ffn_kernel(be_ref, nu_ref, nxt_ref, x_ref, wgu_hbm, bgu_ref, wd_hbm, bd_ref, *rest, layer):
    o_ref, wgu_f32, wd_f32, wgu_bf, wd_bf, sem, slot_ref = rest[-7:]
    i = pl.program_id(0)
    prev = be_ref[jnp.maximum(i - 1, 0)]
    fresh = jnp.logical_or(i == 0, be_ref[i] != prev)

    def weight_copies(expert, slot):
        return (pltpu.make_async_copy(wgu_hbm.at[layer, expert], wgu_f32.at[slot], sem.at[0, slot]),
                pltpu.make_async_copy(wd_hbm.at[layer, expert], wd_f32.at[slot], sem.at[1, slot]))

    @pl.when(jnp.logical_and(i == 0, nu_ref[0] > 0))
    def _():
        slot_ref[0] = 0
        for cp in weight_copies(be_ref[0], 0):
            cp.start()

    @pl.when(jnp.logical_and(fresh, i < nu_ref[0]))
    def _():
        slot = slot_ref[0]
        for cp in weight_copies(be_ref[i], slot):
            cp.wait()
        wgu_bf[...] = wgu_f32[slot].astype(BF16)
        wd_bf[...] = wd_f32[slot].astype(BF16)
        slot_ref[0] = 1 - slot

        @pl.when(nxt_ref[i] >= 0)
        def _():
            for cp in weight_copies(nxt_ref[i], 1 - slot):
                cp.start()

    @pl.when(i < nu_ref[0])
    def _():
        d_ff = wd_bf.shape[0]
        gu = jnp.dot(x_ref[...], wgu_bf[...], preferred_element_type=F32) + bgu_ref[...]
        g = jnp.minimum(gu[:, :d_ff], SWIGLU_LIMIT)
        u = jnp.clip(gu[:, d_ff:], -SWIGLU_LIMIT, SWIGLU_LIMIT)
        act = (u + 1.0) * g * jax.nn.sigmoid(SWIGLU_ALPHA * g)
        y = jnp.dot(act.astype(BF16), wd_bf[...], preferred_element_type=F32) + bd_ref[...]
        o_ref[...] = y.astype(o_ref.dtype)

    @pl.when(i >= nu_ref[0])
    def _():
        o_ref[...] = jnp.zeros_like(o_ref)


def _moe_ffn(block_e, n_used, next_e, xs, w_gu, b_gu, w_down, b_down, layer, n_rows, blk0, done):
    d = xs.shape[1]
    depth, n_e, _, two_ff = w_gu.shape
    d_ff = two_ff // 2
    in_specs = [pl.BlockSpec((MOE_BM, d), lambda i, be, nu, nx: (i, 0)),
                pl.BlockSpec(memory_space=pl.ANY),
                pl.BlockSpec((None, None, 1, two_ff), lambda i, be, nu, nx: (layer, be[i], 0, 0)),
                pl.BlockSpec(memory_space=pl.ANY),
                pl.BlockSpec((None, None, 1, d), lambda i, be, nu, nx: (layer, be[i], 0, 0))]
    args = [block_e, n_used, next_e, xs, w_gu, b_gu.reshape(depth, n_e, 1, two_ff), w_down,
            b_down.reshape(depth, n_e, 1, d)]
    in_specs.append(pl.BlockSpec(memory_space=pl.ANY))
    aliases = {len(args): 0}
    args.append(done)
    grid_spec = pltpu.PrefetchScalarGridSpec(
        num_scalar_prefetch=3,
        grid=(xs.shape[0] // MOE_BM,),
        in_specs=in_specs,
        out_specs=pl.BlockSpec((MOE_BM, d), lambda i, be, nu, nx: (i + blk0, 0)),
        scratch_shapes=[pltpu.VMEM((2, d, two_ff), F32), pltpu.VMEM((2, d_ff, d), F32),
                        pltpu.VMEM((d, two_ff), BF16), pltpu.VMEM((d_ff, d), BF16),
                        pltpu.SemaphoreType.DMA((2, 2)), pltpu.SMEM((1,), jnp.int32)],
    )
    return pl.pallas_call(
        functools.partial(_moe_ffn_kernel, layer=layer),
        grid_spec=grid_spec,
        out_shape=jax.ShapeDtypeStruct((n_rows, d), BF16),
        input_output_aliases=aliases,
        compiler_params=_cparams(("arbitrary",)),
        name="moe_ffn",
    )(*args)


def _next_experts(block_e, n_used):
    idx = jnp.arange(block_e.shape[0], dtype=jnp.int32)
    later = (idx[None, :] > idx[:, None]) & (idx[None, :] < n_used) & (block_e[None, :] != block_e[:, None])
    nxt = jnp.min(jnp.where(later, block_e[None, :], N_EXPERTS), axis=1)
    return jnp.where(nxt == N_EXPERTS, -1, nxt).astype(jnp.int32)


def _moe_combine_kernel(x_ref, g2_ref, w_ref, y_ref, o_ref):
    w = w_ref[...]
    f = w[:, 0:1] * y_ref[0].astype(F32)
    for k in range(1, TOP_K):
        f = f + w[:, k:k + 1] * y_ref[k].astype(F32)
    o_ref[...] = x_ref[...] + g2_ref[...] * f


def _moe_combine(x, g2, top_w, picked, midx):
    t, d = x.shape
    return pl.pallas_call(
        _moe_combine_kernel,
        grid=(t // TM,),
        in_specs=[pl.BlockSpec((TM, d), lambda i: (i, 0)),
                  pl.BlockSpec((None, 1, d), lambda i: (midx(i), 0, 0)),
                  pl.BlockSpec((TM, LANES), lambda i: (i, 0)),
                  pl.BlockSpec((TOP_K, TM, d), lambda i: (0, i, 0))],
        out_specs=pl.BlockSpec((TM, d), lambda i: (i, 0)),
        out_shape=jax.ShapeDtypeStruct((t, d), F32),
        compiler_params=_cparams(("parallel",)),
        name="moe_combine",
    )(x, g2, top_w, picked)


SMEM_TILE = 1024


def _row_tokens_kernel(lo_ref, hi_ref, dest_ref, rt_ref):
    i = pl.program_id(0)
    chunk = dest_ref.shape[0]

    @pl.when(i == 0)
    def _():
        def fill_gap(g, _):
            def fill(r, _):
                rt_ref[r] = r
                return 0
            lax.fori_loop(lo_ref[g], hi_ref[g], fill, 0)
            return 0
        lax.fori_loop(0, lo_ref.shape[0], fill_gap, 0)

    base = i * (chunk // TOP_K)

    def body(tok, _):
        for k in range(TOP_K):
            rt_ref[dest_ref[tok * TOP_K + k]] = base + tok
        return 0

    lax.fori_loop(0, chunk // TOP_K, body, 0, unroll=4)


def _row_tokens(gap_lo, gap_hi, dest_flat, n_rows):
    n_assign = dest_flat.shape[0]
    chunk = 4 * SMEM_TILE if n_assign % (4 * SMEM_TILE) == 0 else SMEM_TILE
    assert n_assign % chunk == 0
    grid_spec = pltpu.PrefetchScalarGridSpec(
        num_scalar_prefetch=2,
        grid=(n_assign // chunk,),
        in_specs=[pl.BlockSpec((chunk,), lambda i, lo, hi: (i,), memory_space=pltpu.SMEM)],
        out_specs=pl.BlockSpec(memory_space=pltpu.SMEM),
    )
    return pl.pallas_call(
        _row_tokens_kernel,
        grid_spec=grid_spec,
        out_shape=jax.ShapeDtypeStruct((n_rows,), jnp.int32),
        compiler_params=_cparams(("arbitrary",)),
        name="moe_row_tokens",
    )(gap_lo, gap_hi, dest_flat)


def _moe_rows(n_tokens):
    return (-(-n_tokens * TOP_K // MOE_BM) + N_EXPERTS) * MOE_BM


def _moe(x, g2, midx, h2, top_e, top_w, top_r, counts, w_gu, b_gu, w_down, b_down, layer, ys_buf):
    t, d = h2.shape
    n_rows = ys_buf.shape[0]
    n_blocks = n_rows // MOE_BM
    assert n_rows >= _moe_rows(t)
    cnt = counts[0, :N_EXPERTS].astype(jnp.int32)
    padded = (cnt + MOE_BM - 1) // MOE_BM * MOE_BM
    pends = jnp.cumsum(padded)
    pstarts = pends - padded
    e = top_e[:, :TOP_K]
    expert_ids = jnp.arange(N_EXPERTS, dtype=jnp.int32)
    pstart_of_pick = jnp.sum(jnp.where(e[:, :, None] == expert_ids, pstarts, 0), axis=-1)
    dest = pstart_of_pick + top_r[:, :TOP_K]
    block_start = jnp.arange(n_blocks, dtype=jnp.int32) * MOE_BM
    block_e = jnp.sum((block_start[:, None] >= pends[None, :]).astype(jnp.int32), axis=1)
    block_e = jnp.minimum(block_e, N_EXPERTS - 1)
    n_used = (pends[-1] // MOE_BM).astype(jnp.int32).reshape(1)

    gap_lo = jnp.concatenate([pstarts + cnt, pends[-1:]])
    gap_hi = jnp.concatenate([pends, jnp.full((1,), n_rows, jnp.int32)])
    row_tok = _row_tokens(gap_lo, gap_hi, dest.reshape(-1), n_rows)
    h2_big = jnp.concatenate([h2, jnp.zeros((n_rows - t, d), h2.dtype)], axis=0)
    nb1 = n_blocks // 4
    r1 = nb1 * MOE_BM
    ys = ys_buf
    for blk0, be, rows in ((0, block_e[:nb1], row_tok[:r1]), (nb1, block_e[nb1:], row_tok[r1:])):
        used = n_used - blk0
        ys = _moe_ffn(be, used, _next_experts(be, used[0]), h2_big[rows], w_gu, b_gu, w_down, b_down, layer,
                      n_rows, blk0=blk0, done=ys)
    picked = ys[dest.T.reshape(-1)].reshape(TOP_K, t, d)
    return _moe_combine(x, g2, top_w, picked, midx), ys


def _pad_cols(w, width):
    return jnp.concatenate([w, jnp.zeros((w.shape[0], width - w.shape[1]), w.dtype)], axis=1)


def kernel(x, c, ctx, c_ctx, mod_w, mod_b, norm1_g, norm2_g, ev_w_in, ev_w_out, lru_conv_w, lru_conv_b, lru_wa, lru_ba, lru_wx, lru_bx, lru_lambda, mla_q_norm_g, mla_w_uq, mla_kv_norm_g, mla_w_ukv, mla_qn_g, mla_kn_g, od_w_in, od_w_out, diff_qn_g, diff_kn_g, diff_lq1, diff_lk1, diff_lq2, diff_lk2, diff_subln_g, router_w, router_b, moe_w_gu, moe_b_gu, moe_w_down, moe_b_down):
    nb, seq, d = x.shape
    n_ctx = ctx.shape[1]
    depth = mod_w.shape[0]
    l = n_ctx + seq
    tpb = l // TM
    assert n_ctx == TM and seq % TQ == 0 and seq % GRID_W == 0

    xa = jnp.concatenate([x, ctx], axis=1).reshape(nb * l, d)
    cvec = jnp.concatenate([c, c_ctx[None, :], jnp.zeros((SUBLANES - nb - 1, d), F32)], axis=0)
    router_w_p = jnp.concatenate([router_w, jnp.zeros((depth, d, LANES - N_EXPERTS), F32)], axis=-1)
    router_b_p = jnp.concatenate([router_b, jnp.zeros((depth, LANES - N_EXPERTS), F32)], axis=-1)
    ys_buf = jnp.zeros((_moe_rows(nb * l), d), BF16)

    for layer in range(depth):
        last = layer == depth - 1
        i = layer // 2
        mod = _adaln(cvec, mod_w[layer], mod_b[layer])[:nb + 1]
        sh1, sc1, g1, sh2, sc2, g2 = (mod[:, k * d:(k + 1) * d].reshape(nb + 1, 1, d) for k in range(6))
        n1g = norm1_g[layer].reshape(1, d)
        n2g = norm2_g[layer].reshape(1, d)

        if layer % 2 == 0:
            w_in = ev_w_in[i].astype(BF16)
            o = 2 * LRU_WIDTH
            splits = [w_in[:, :LRU_WIDTH], w_in[:, LRU_WIDTH:o], w_in[:, o:o + MLA_Q_RANK],
                      w_in[:, o + MLA_Q_RANK:o + MLA_Q_RANK + MLA_KV_RANK],
                      _pad_cols(w_in[:, o + MLA_Q_RANK + MLA_KV_RANK:], LANES)]
            gl, rl, qc, kvc, kr = _even_proj(xa, n1g, sc1, sh1, splits, tpb, nb)

            w_gates, b_gates = _lru_gate_weights(lru_wa[i], lru_ba[i], lru_wx[i], lru_bx[i])
            lru = _lru(gl, rl, lru_conv_w[i], lru_conv_b[i].reshape(1, LRU_WIDTH), w_gates, b_gates,
                       lru_lambda[i], nb, n_ctx, seq)

            w_uq = mla_w_uq[i].astype(BF16).reshape(MLA_Q_RANK, MLA_HEADS, MLA_QK)
            w_q_nope = w_uq[:, :, :MLA_NOPE].reshape(MLA_Q_RANK, MLA_HEADS * MLA_NOPE)
            w_q_rope = jnp.concatenate(
                [w_uq[:, :, MLA_NOPE:], jnp.zeros((MLA_Q_RANK, MLA_HEADS, LANES - MLA_ROPE), BF16)],
                axis=-1).reshape(MLA_Q_RANK, MLA_HEADS * LANES)
            pad_g = lambda g: jnp.concatenate([g, jnp.zeros((LANES - MLA_ROPE,), F32)]).reshape(1, LANES)
            rope = _rope_tables_128(seq, n_ctx, MLA_ROPE, tile_groups=False)
            q, k, v = _mla_prep(qc, kvc, kr, mla_q_norm_g[i].reshape(1, -1), mla_kv_norm_g[i].reshape(1, -1),
                                w_q_nope, w_q_rope, mla_w_ukv[i].astype(BF16),
                                mla_qn_g[i][:MLA_NOPE].reshape(1, LANES), pad_g(mla_qn_g[i][MLA_NOPE:]),
                                mla_kn_g[i][:MLA_NOPE].reshape(1, LANES), pad_g(mla_kn_g[i][MLA_NOPE:]),
                                rope, tpb)
            att = _attend_all(_mla_attn_kernel, "mla_attention", 1, q, k, v, nb, n_ctx, seq, MLA_HEADS,
                              2 * LANES, MLA_V, need_ctx=not last)
            w_out = ev_w_out[i].astype(BF16)
            if last:
                lru = lru.reshape(nb, l, LRU_WIDTH)[:, :seq].reshape(nb * seq, LRU_WIDTH)
            mixes = [lru, att]
            weights = [w_out[:LRU_WIDTH], w_out[LRU_WIDTH:]]
        else:
            lam_init = 0.8 - 0.6 * math.exp(-0.3 * layer)
            w_in = od_w_in[i].astype(BF16)
            n_qk = DIFF_HEADS * 2 * DIFF_HEAD_DIM
            gidx = jnp.arange(LANES) // DIFF_HEAD_DIM
            gmat = jnp.where(gidx[:, None] == gidx[None, :], 1.0 / DIFF_HEAD_DIM, 0.0).astype(BF16)
            rope = _rope_tables_128(seq, n_ctx, DIFF_HEAD_DIM, tile_groups=True)
            tile_g = lambda g: jnp.tile(g, LANES // DIFF_HEAD_DIM).reshape(1, LANES)
            q, k, v = _odd_proj(xa, n1g, sc1, sh1, w_in[:, :n_qk], w_in[:, n_qk:2 * n_qk], w_in[:, 2 * n_qk:],
                                tile_g(diff_qn_g[i]), tile_g(diff_kn_g[i]), gmat, rope, tpb, nb)
            lam = (jnp.exp(jnp.sum(diff_lq1[i] * diff_lk1[i])) - jnp.exp(jnp.sum(diff_lq2[i] * diff_lk2[i]))
                   + lam_init).reshape(1).astype(F32)
            att = _attend_all(
                functools.partial(_diff_attn_kernel, out_scale=1.0 - lam_init), "diff_attention", 2,
                q, k, v, nb, n_ctx, seq, DIFF_HEADS, 2 * DIFF_HEAD_DIM, DIFF_V, need_ctx=not last,
                extra_in=(diff_subln_g[i].reshape(1, DIFF_V), lam),
                extra_specs=(pl.BlockSpec((1, DIFF_V), lambda b, h: (0, 0)),
                             pl.BlockSpec(memory_space=pltpu.SMEM)))
            mixes = [att]
            weights = [od_w_out[i].astype(BF16)]

        xo, h2, te, tw, tr, counts = _post_mix(mixes, weights, xa, g1, n2g, sc2, sh2, router_w_p[layer],
                                               router_b_p[layer].reshape(1, LANES), tpb, nb, last)
        midx = (lambda t: t // (tpb - 1)) if last else _mod_index(tpb, nb)
        xa, ys_buf = _moe(xo, g2, midx, h2, te, tw, tr, counts, moe_w_gu, moe_b_gu, moe_w_down, moe_b_down,
                          layer, ys_buf)
    return xa.reshape(nb, seq, d)
```

```python
import functools
import math

import jax
import jax.numpy as jnp
from jax import lax
from jax.experimental import pallas as pl
from jax.experimental.pallas import tpu as pltpu

F32 = jnp.float32
BF16 = jnp.bfloat16
HIGHEST = lax.Precision.HIGHEST

GRID_W = 64
NORM_EPS = 1e-6
ROPE_BASE = 10000.0
LRU_WIDTH = 512
LRU_BLOCKS = 8
LRU_BLOCK_W = LRU_WIDTH // LRU_BLOCKS
LRU_C = 8.0
CONV_W = 4
MLA_HEADS = 4
MLA_Q_RANK = 384
MLA_KV_RANK = 256
MLA_NOPE = 128
MLA_ROPE = 64
MLA_V = 128
MLA_QK = MLA_NOPE + MLA_ROPE
DIFF_HEADS = 8
DIFF_HEAD_DIM = 64
DIFF_V = 2 * DIFF_HEAD_DIM
N_EXPERTS = 32
TOP_K = 4
SWIGLU_LIMIT = 7.0
SWIGLU_ALPHA = 1.702

LANES = 128
SUBLANES = 8
VMEM_LIMIT = 52 * 1024 * 1024

TM = 256
TQ = 512
LRU_CT = 128
LRU_CHUNK = 128
LRU_SEG = LRU_CHUNK // SUBLANES
LRU_PAD = 8
MOE_BM = 256
NEG_BIG = -1e30
LOG2_E = math.log2(math.e)


def _cparams(sem, vmem=VMEM_LIMIT):
    return pltpu.CompilerParams(dimension_semantics=sem, vmem_limit_bytes=vmem)


def _adaln_kernel(c_ref, w_ref, b_ref, o_ref):
    c = c_ref[...]
    s = c * jax.nn.sigmoid(c)
    o_ref[...] = jnp.dot(s, w_ref[...], preferred_element_type=F32, precision=HIGHEST) + b_ref[...]


def _adaln(cvec, w, b):
    rows, d = cvec.shape
    n = w.shape[1]
    tn = 1536
    return pl.pallas_call(
        _adaln_kernel,
        grid=(n // tn,),
        in_specs=[pl.BlockSpec((rows, d), lambda j: (0, 0)),
                  pl.BlockSpec((d, tn), lambda j: (0, j)),
                  pl.BlockSpec((1, tn), lambda j: (0, j))],
        out_specs=pl.BlockSpec((rows, tn), lambda j: (0, j)),
        out_shape=jax.ShapeDtypeStruct((rows, n), F32),
        compiler_params=_cparams(("arbitrary",)),
        name="adaln_mod",
    )(cvec, w, b.reshape(1, n))


def _norm_mod(x, g, sc, sh):
    var = jnp.mean(x * x, axis=-1, keepdims=True)
    y = x * lax.rsqrt(var + NORM_EPS) * g
    return y * (1.0 + sc) + sh


def _ones_column(shape):
    lane = lax.broadcasted_iota(jnp.int32, shape, 1)
    return jnp.where(lane == 0, 1.0, 0.0).astype(BF16)


def _rope_tile(x, cos, sin_a, sin_b):
    up = pltpu.roll(x, LANES - 16, axis=1)
    dn = pltpu.roll(x, 16, axis=1)
    return x * cos + up * sin_a + dn * sin_b


def _rope_tables(seq, rot_dim):
    n_rows = seq // GRID_W
    rows = jnp.repeat(jnp.arange(n_rows, dtype=F32), GRID_W)
    cols = jnp.tile(jnp.arange(GRID_W, dtype=F32), n_rows)
    axis_dim = rot_dim // 2
    inv_freq = ROPE_BASE ** (-jnp.arange(0, axis_dim, 2, dtype=F32) / axis_dim)
    ang_r = rows[:, None] * inv_freq
    ang_c = cols[:, None] * inv_freq
    ang = jnp.concatenate([ang_r, ang_r, ang_c, ang_c], axis=-1)
    cos, sin = jnp.cos(ang), jnp.sin(ang)
    quarter = rot_dim // 4
    first = (jnp.arange(rot_dim) % (2 * quarter)) < quarter
    sin_a = jnp.where(first, -sin, 0.0)
    sin_b = jnp.where(first, 0.0, sin)
    return cos, sin_a, sin_b


def _rope_tables_128(seq, ctx, rot_dim, tile_groups):
    cos, sin_a, sin_b = _rope_tables(seq, rot_dim)
    if tile_groups:
        reps = LANES // rot_dim
        cos, sin_a, sin_b = (jnp.tile(t, (1, reps)) for t in (cos, sin_a, sin_b))
    else:
        pad = LANES - rot_dim
        cos = jnp.concatenate([cos, jnp.ones((seq, pad), F32)], axis=-1)
        sin_a = jnp.concatenate([sin_a, jnp.zeros((seq, pad), F32)], axis=-1)
        sin_b = jnp.concatenate([sin_b, jnp.zeros((seq, pad), F32)], axis=-1)
    cos = jnp.concatenate([cos, jnp.ones((ctx, LANES), F32)], axis=0)
    sin_a = jnp.concatenate([sin_a, jnp.zeros((ctx, LANES), F32)], axis=0)
    sin_b = jnp.concatenate([sin_b, jnp.zeros((ctx, LANES), F32)], axis=0)
    return cos, sin_a, sin_b


def _mod_index(tpb, nb):
    def idx(i):
        return jnp.where(i % tpb == tpb - 1, nb, i // tpb)
    return idx


def _even_proj_kernel(x_ref, g_ref, sc_ref, sh_ref, wg_ref, wr_ref, wq_ref, wkv_ref, wkr_ref,
                      og_ref, or_ref, oq_ref, okv_ref, okr_ref):
    h = _norm_mod(x_ref[...], g_ref[...], sc_ref[...], sh_ref[...]).astype(BF16)
    for w_ref, o_ref in ((wg_ref, og_ref), (wr_ref, or_ref), (wq_ref, oq_ref),
                         (wkv_ref, okv_ref), (wkr_ref, okr_ref)):
        o_ref[...] = jnp.dot(h, w_ref[...], preferred_element_type=F32)


def _even_proj(x, gain, sc, sh, weights, tpb, nb):
    t, d = x.shape
    midx = _mod_index(tpb, nb)
    w_specs = [pl.BlockSpec(w.shape, lambda i: (0, 0)) for w in weights]
    return pl.pallas_call(
        _even_proj_kernel,
        grid=(t // TM,),
        in_specs=[pl.BlockSpec((TM, d), lambda i: (i, 0)),
                  pl.BlockSpec((1, d), lambda i: (0, 0)),
                  pl.BlockSpec((None, 1, d), lambda i: (midx(i), 0, 0)),
                  pl.BlockSpec((None, 1, d), lambda i: (midx(i), 0, 0))] + w_specs,
        out_specs=[pl.BlockSpec((TM, w.shape[1]), lambda i: (i, 0)) for w in weights],
        out_shape=[jax.ShapeDtypeStruct((t, w.shape[1]), F32) for w in weights],
        compiler_params=_cparams(("parallel",)),
        name="even_in_proj",
    )(x, gain, sc, sh, *weights)


def _mla_prep_kernel(qc_ref, kvc_ref, kr_ref, qng_ref, kvng_ref, wqn_ref, wqr_ref, wkv_ref,
                     qgn_ref, qgr_ref, kgn_ref, kgr_ref, cos_ref, sa_ref, sb_ref,
                     q_ref, k_ref, v_ref):
    cos, sa, sb = cos_ref[...], sa_ref[...], sb_ref[...]
    inv_qk = 1.0 / MLA_QK

    qc = qc_ref[...]
    hq = (qc * lax.rsqrt(jnp.mean(qc * qc, axis=-1, keepdims=True) + NORM_EPS) * qng_ref[...]).astype(BF16)
    q_nope = jnp.dot(hq, wqn_ref[...], preferred_element_type=F32)
    q_rope = jnp.dot(hq, wqr_ref[...], preferred_element_type=F32)
    q_scale = MLA_QK ** -0.5 * LOG2_E
    for h in range(MLA_HEADS):
        qn = q_nope[:, h * LANES:(h + 1) * LANES]
        qr = q_rope[:, h * LANES:(h + 1) * LANES]
        ms = (jnp.sum(qn * qn, axis=-1, keepdims=True) + jnp.sum(qr * qr, axis=-1, keepdims=True)) * inv_qk
        rs = lax.rsqrt(ms + NORM_EPS) * q_scale
        q_ref[:, (2 * h) * LANES:(2 * h + 1) * LANES] = (qn * rs * qgn_ref[...]).astype(BF16)
        q_ref[:, (2 * h + 1) * LANES:(2 * h + 2) * LANES] = _rope_tile(qr * rs * qgr_ref[...], cos, sa, sb).astype(BF16)

    kvc = kvc_ref[...]
    hkv = (kvc * lax.rsqrt(jnp.mean(kvc * kvc, axis=-1, keepdims=True) + NORM_EPS) * kvng_ref[...]).astype(BF16)
    kv = jnp.dot(hkv, wkv_ref[...], preferred_element_type=F32)
    kr = kr_ref[...]
    kr_ss = jnp.sum(kr * kr, axis=-1, keepdims=True)
    kr_rot = _rope_tile(kr * kgr_ref[...], cos, sa, sb)
    for h in range(MLA_HEADS):
        kn = kv[:, (2 * h) * LANES:(2 * h + 1) * LANES]
        ms = (jnp.sum(kn * kn, axis=-1, keepdims=True) + kr_ss) * inv_qk
        rs = lax.rsqrt(ms + NORM_EPS)
        k_ref[:, (2 * h) * LANES:(2 * h + 1) * LANES] = (kn * rs * kgn_ref[...]).astype(BF16)
        k_ref[:, (2 * h + 1) * LANES:(2 * h + 2) * LANES] = (kr_rot * rs).astype(BF16)
        v_ref[:, (2 * h) * LANES:(2 * h + 1) * LANES] = kv[:, (2 * h + 1) * LANES:(2 * h + 2) * LANES].astype(BF16)
        v_ref[:, (2 * h + 1) * LANES:(2 * h + 2) * LANES] = _ones_column((kv.shape[0], LANES))


def _mla_prep(qc, kvc, kr, q_norm_g, kv_norm_g, w_q_nope, w_q_rope, w_ukv, q_gn, q_gr, k_gn, k_gr,
              rope, tpb):
    t = qc.shape[0]
    cos, sa, sb = rope

    def const(a):
        return pl.BlockSpec(a.shape, lambda i: (0, 0))

    def rows(a):
        return pl.BlockSpec((TM, a.shape[1]), lambda i: (i, 0))

    tab = pl.BlockSpec((TM, LANES), lambda i: (i % tpb, 0))
    hq, hv = MLA_HEADS * 2 * LANES, MLA_HEADS * 2 * MLA_V
    return pl.pallas_call(
        _mla_prep_kernel,
        grid=(t // TM,),
        in_specs=[rows(qc), rows(kvc), rows(kr), const(q_norm_g), const(kv_norm_g),
                  const(w_q_nope), const(w_q_rope), const(w_ukv),
                  const(q_gn), const(q_gr), const(k_gn), const(k_gr), tab, tab, tab],
        out_specs=[pl.BlockSpec((TM, hq), lambda i: (i, 0)),
                   pl.BlockSpec((TM, hq), lambda i: (i, 0)),
                   pl.BlockSpec((TM, hv), lambda i: (i, 0))],
        out_shape=[jax.ShapeDtypeStruct((t, hq), BF16),
                   jax.ShapeDtypeStruct((t, hq), BF16),
                   jax.ShapeDtypeStruct((t, hv), BF16)],
        compiler_params=_cparams(("parallel",)),
        name="mla_prep",
    )(qc, kvc, kr, q_norm_g, kv_norm_g, w_q_nope, w_q_rope, w_ukv, q_gn, q_gr, k_gn, k_gr, cos, sa, sb)


def _sublane_iota():
    return lax.broadcasted_iota(jnp.int32, (SUBLANES, LANES), 0)


def _scan_chunk(a_chunk, u_chunk, h_chunk, carry, reverse):
    steps = range(LRU_SEG - 1, -1, -1) if reverse else range(LRU_SEG)
    h_loc, p_loc = [None] * LRU_SEG, [None] * LRU_SEG
    h = p = None
    for j in steps:
        a = a_chunk[pl.ds(j, SUBLANES, stride=LRU_SEG), :]
        u = u_chunk[pl.ds(j, SUBLANES, stride=LRU_SEG), :]
        if h is None:
            h, p = u, a
        else:
            h, p = a * h + u, a * p
        h_loc[j], p_loc[j] = h, p
    sub = _sublane_iota()
    seg_p, seg_h = p, h
    for d in (1, 2, 4):
        shift = SUBLANES - d if reverse else d
        prev_p = pltpu.roll(seg_p, shift, axis=0)
        prev_h = pltpu.roll(seg_h, shift, axis=0)
        valid = (sub < SUBLANES - d) if reverse else (sub >= d)
        seg_h = jnp.where(valid, seg_p * prev_h + seg_h, seg_h)
        seg_p = jnp.where(valid, seg_p * prev_p, seg_p)
    h_end = seg_h + seg_p * carry
    if reverse:
        h_in = jnp.where(sub == SUBLANES - 1, carry, pltpu.roll(h_end, SUBLANES - 1, axis=0))
        new_carry = h_end[0:1, :]
    else:
        h_in = jnp.where(sub == 0, carry, pltpu.roll(h_end, 1, axis=0))
        new_carry = h_end[SUBLANES - 1:SUBLANES, :]
    for j in range(LRU_SEG):
        h_chunk[pl.ds(j, SUBLANES, stride=LRU_SEG), :] = h_loc[j] + p_loc[j] * h_in
    return jnp.broadcast_to(new_carry, (SUBLANES, LANES))


def _lru_kernel(g_ref, r_ref, cw_ref, cb_ref, wg_ref, bg_ref, lam_ref, o_ref,
                rp_ref, af_ref, uf_ref, ab_ref, ub_ref, hf_ref, hb_ref, *, ctx, seq):
    n_ctx, n_lat = ctx // LRU_CHUNK, seq // LRU_CHUNK
    zeros_pad = jnp.zeros((LRU_PAD, LRU_CT), F32)
    ctx0 = seq + 2 * LRU_PAD
    rp_ref[0:LRU_PAD, :] = zeros_pad
    rp_ref[LRU_PAD:LRU_PAD + seq, :] = r_ref[0:seq, :]
    rp_ref[LRU_PAD + seq:ctx0, :] = zeros_pad
    rp_ref[ctx0:ctx0 + ctx, :] = r_ref[seq:seq + ctx, :]
    rp_ref[ctx0 + ctx:ctx0 + ctx + LRU_PAD, :] = zeros_pad

    cw = cw_ref[...]
    cb = cb_ref[...]
    wg = wg_ref[...]
    bg = bg_ref[...]
    lam = lam_ref[...]
    sp = jnp.maximum(-lam, 0.0) + jnp.log1p(jnp.exp(-jnp.abs(lam)))

    def coeff_chunk(c, pad_off, row_off):
        start = pl.multiple_of(pad_off + c * LRU_CHUNK, SUBLANES)
        ext = rp_ref[pl.ds(start, LRU_CHUNK + 2 * LRU_PAD), :]
        x = cb
        for tap in range(CONV_W):
            lo = LRU_PAD - 2 + tap
            x = x + ext[lo:lo + LRU_CHUNK, :] * cw[tap:tap + 1, :]
        gates = jnp.dot(x.astype(BF16), wg, preferred_element_type=F32) + bg
        out_row = pl.multiple_of(row_off + c * LRU_CHUNK, SUBLANES)
        for d, (a_ref, u_ref) in enumerate(((af_ref, uf_ref), (ab_ref, ub_ref))):
            r = jax.nn.sigmoid(gates[:, (2 * d) * LRU_CT:(2 * d + 1) * LRU_CT])
            i = jax.nn.sigmoid(gates[:, (2 * d + 1) * LRU_CT:(2 * d + 2) * LRU_CT])
            log_a = -LRU_C * r * sp[d:d + 1, :]
            a_ref[pl.ds(out_row, LRU_CHUNK), :] = jnp.exp(log_a)
            th = jnp.tanh(log_a)
            u_ref[pl.ds(out_row, LRU_CHUNK), :] = jnp.sqrt(-2.0 * th / (1.0 - th)) * i * x
        return None

    def coeff_ctx(c, _):
        coeff_chunk(c, seq + LRU_PAD, seq)
        return 0

    def coeff_lat(c, _):
        coeff_chunk(c, 0, 0)
        return 0

    lax.fori_loop(0, n_ctx, coeff_ctx, 0, unroll=2)
    lax.fori_loop(0, n_lat, coeff_lat, 0, unroll=2)

    def scan_pair(n, row_off):
        def body(c, carry):
            cf, cb_ = carry
            f_row = pl.multiple_of(row_off + c * LRU_CHUNK, SUBLANES)
            b_row = pl.multiple_of(row_off + (n - 1 - c) * LRU_CHUNK, SUBLANES)
            cf = _scan_chunk(af_ref.at[pl.ds(f_row, LRU_CHUNK), :], uf_ref.at[pl.ds(f_row, LRU_CHUNK), :],
                             hf_ref.at[pl.ds(f_row, LRU_CHUNK), :], cf, False)
            cb_ = _scan_chunk(ab_ref.at[pl.ds(b_row, LRU_CHUNK), :], ub_ref.at[pl.ds(b_row, LRU_CHUNK), :],
                              hb_ref.at[pl.ds(b_row, LRU_CHUNK), :], cb_, True)
            return cf, cb_
        return body

    zero = jnp.zeros((SUBLANES, LANES), F32)
    carry = lax.fori_loop(0, n_ctx, scan_pair(n_ctx, seq), (zero, zero), unroll=2)
    lax.fori_loop(0, n_lat, scan_pair(n_lat, 0), carry, unroll=2)

    g = g_ref[...]
    gelu = 0.5 * g * (1.0 + jnp.tanh(math.sqrt(2.0 / math.pi) * (g + 0.044715 * (g * g * g))))
    o_ref[...] = (gelu * (hf_ref[...] + hb_ref[...])).astype(BF16)


def _lru(g, r, conv_w, conv_b, w_gates, b_gates, lam, nb, ctx, seq):
    l = ctx + seq
    width = g.shape[1]
    n_ct = width // LRU_CT
    g3 = g.reshape(nb, l, width)
    r3 = r.reshape(nb, l, width)
    seq_spec = pl.BlockSpec((None, l, LRU_CT), lambda b, c: (b, 0, c))
    scratch = [pltpu.VMEM((l + 3 * LRU_PAD, LRU_CT), F32)] + [pltpu.VMEM((l, LRU_CT), F32)] * 6
    out = pl.pallas_call(
        functools.partial(_lru_kernel, ctx=ctx, seq=seq),
        grid=(nb, n_ct),
        in_specs=[seq_spec, seq_spec,
                  pl.BlockSpec((CONV_W, LRU_CT), lambda b, c: (0, c)),
                  pl.BlockSpec((1, LRU_CT), lambda b, c: (0, c)),
                  pl.BlockSpec((None, LRU_CT, 4 * LRU_CT), lambda b, c: (c, 0, 0)),
                  pl.BlockSpec((None, 1, 4 * LRU_CT), lambda b, c: (c, 0, 0)),
                  pl.BlockSpec((2, LRU_CT), lambda b, c: (0, c))],
        out_specs=seq_spec,
        out_shape=jax.ShapeDtypeStruct((nb, l, width), BF16),
        scratch_shapes=scratch,
        compiler_params=_cparams(("parallel", "parallel")),
        name="rglru",
    )(g3, r3, conv_w, conv_b, w_gates, b_gates, lam)
    return out.reshape(nb * l, width)


def _lru_gate_weights(wa, ba, wx, bx):
    per = LRU_CT // LRU_BLOCK_W
    n_ct = LRU_BLOCKS // per
    eye = jnp.eye(per, dtype=F32)

    def dense(w):
        w4 = w.reshape(n_ct, per, LRU_BLOCK_W, LRU_BLOCK_W)
        return jnp.einsum('cide,ij->cidje', w4, eye).reshape(n_ct, LRU_CT, LRU_CT)

    w = jnp.concatenate([dense(wa[0]), dense(wx[0]), dense(wa[1]), dense(wx[1])], axis=-1)
    b = jnp.concatenate([v.reshape(n_ct, 1, LRU_CT) for v in (ba[0], bx[0], ba[1], bx[1])], axis=-1)
    return w.astype(BF16), b


def _key_spans(n_keys):
    half = n_keys // 2 if n_keys % (2 * LANES) == 0 else n_keys
    return [(lo, lo + half) for lo in range(0, n_keys, half)]


def _attn_scores(q_ops, kt_ref, s_ref, m_ref):
    for t, q in enumerate(q_ops):
        for lo, hi in _key_spans(kt_ref.shape[-1]):
            s_ref[t, :, lo:hi] = jnp.dot(q, kt_ref[:, lo:hi], preferred_element_type=F32)
        m_ref[t] = jnp.max(s_ref[t], axis=-1, keepdims=True)


def _attn_values(v_ref, s_ref, m_ref, p_ref):
    accs = []
    for t in range(s_ref.shape[0]):
        m = m_ref[t]
        acc = None
        for lo, hi in _key_spans(s_ref.shape[-1]):
            p_ref[t, :, lo:hi] = jnp.exp2(s_ref[t, :, lo:hi] - m).astype(BF16)
            part = jnp.dot(p_ref[t, :, lo:hi], v_ref[lo:hi, :], preferred_element_type=F32)
            acc = part if acc is None else acc + part
        accs.append(acc)
    return accs


def _attn_pipeline(q_ref, kt_ref, v_ref, o_ref, bufs, p_ref, q_ops_fn, values_fn):
    s0, m0 = bufs[0]
    tq = s0.shape[1]
    n = q_ref.shape[0] // tq

    def scores(t, s_ref, m_ref):
        row = t * tq if isinstance(t, int) else pl.multiple_of(t * tq, tq)
        _attn_scores(q_ops_fn(q_ref[pl.ds(row, tq), :]), kt_ref, s_ref, m_ref)

    def values(t, s_ref, m_ref):
        row = t * tq if isinstance(t, int) else pl.multiple_of(t * tq, tq)
        o_ref[pl.ds(row, tq), :] = values_fn(v_ref, s_ref, m_ref, p_ref).astype(o_ref.dtype)

    if len(bufs) == 1:
        def one(t, _):
            scores(t, s0, m0)
            values(t, s0, m0)
            return 0

        if n == 1:
            one(0, 0)
        else:
            lax.fori_loop(0, n, one, 0)
        return

    s1, m1 = bufs[1]
    scores(0, s0, m0)
    if n == 1:
        values(0, s0, m0)
        return
    assert n % 2 == 0

    def pair(k, _):
        t = 2 * k
        scores(t + 1, s1, m1)
        values(t, s0, m0)
        scores(t + 2, s0, m0)
        values(t + 1, s1, m1)
        return 0

    lax.fori_loop(0, n // 2 - 1, pair, 0)
    scores(n - 1, s1, m1)
    values(n - 2, s0, m0)
    values(n - 1, s1, m1)


def _score_bufs(scratch):
    return tuple(zip(scratch[0:-1:2], scratch[1:-1:2])), scratch[-1]


def _mla_attn_kernel(q_ref, kt_ref, v_ref, o_ref, *scratch):
    dv = o_ref.shape[-1]
    bufs, p_ref = _score_bufs(scratch)

    def values(*refs):
        acc = _attn_values(*refs)[0]
        return acc[:, :dv] / acc[:, dv:dv + 1]

    _attn_pipeline(q_ref, kt_ref, v_ref, o_ref, bufs, p_ref, lambda q: [q], values)


def _diff_attn_kernel(q_ref, kt_ref, v_ref, g_ref, lam_ref, o_ref, *scratch, out_scale):
    dv = o_ref.shape[-1]
    bufs, p_ref = _score_bufs(scratch)

    def q_ops(q):
        lane = lax.broadcasted_iota(jnp.int32, q.shape, 1)
        zero = jnp.zeros_like(q)
        return [jnp.where(lane < DIFF_HEAD_DIM, q, zero), jnp.where(lane < DIFF_HEAD_DIM, zero, q)]

    def values(*refs):
        a1, a2 = _attn_values(*refs)
        o = a1[:, :dv] / a1[:, dv:dv + 1] - lam_ref[0] * (a2[:, :dv] / a2[:, dv:dv + 1])
        y = o * lax.rsqrt(jnp.mean(o * o, axis=-1, keepdims=True) + NORM_EPS) * g_ref[...]
        return y * out_scale

    _attn_pipeline(q_ref, kt_ref, v_ref, o_ref, bufs, p_ref, q_ops, values)


def _attention(kernel_fn, name, n_softmax, q, kt, v, nb, ctx, seq, heads, dk, dv, ctx_queries,
               extra_in=(), extra_specs=()):
    l = ctx + seq
    q3 = q.reshape(nb, l, heads * dk)
    v3 = v.reshape(nb, l, heads * 2 * dv)
    if ctx_queries:
        tq, n_keys, rows = ctx, ctx, ctx
        blk = seq // ctx
        in_specs = [pl.BlockSpec((None, rows, dk), lambda b, h: (b, blk, h)),
                    pl.BlockSpec((None, None, dk, ctx), lambda b, h: (b, h, 0, blk)),
                    pl.BlockSpec((None, ctx, 2 * dv), lambda b, h: (b, blk, h))]
    else:
        tq, n_keys, rows = TQ, l, seq
        in_specs = [pl.BlockSpec((None, rows, dk), lambda b, h: (b, 0, h)),
                    pl.BlockSpec((None, None, dk, l), lambda b, h: (b, h, 0, 0)),
                    pl.BlockSpec((None, l, 2 * dv), lambda b, h: (b, 0, h))]
    score_buf = [pltpu.VMEM((n_softmax, tq, n_keys), F32), pltpu.VMEM((n_softmax, tq, 1), F32)]
    out = pl.pallas_call(
        kernel_fn,
        grid=(nb, heads),
        in_specs=in_specs + list(extra_specs),
        out_specs=pl.BlockSpec((None, rows, dv), lambda b, h: (b, 0, h)),
        out_shape=jax.ShapeDtypeStruct((nb, rows, heads * dv), BF16),
        scratch_shapes=score_buf * (2 if n_softmax == 1 else 1) + [pltpu.VMEM((n_softmax, tq, n_keys), BF16)],
        compiler_params=_cparams(("parallel", "parallel")),
        name=name,
    )(q3, kt, v3, *extra_in)
    return out


def _key_transpose(k, nb, l, heads, dk):
    return jnp.transpose(k.reshape(nb, l, heads, dk), (0, 2, 3, 1))


def _attend_all(kernel_fn, name, n_softmax, q, k, v, nb, ctx, seq, heads, dk, dv, need_ctx, **extra):
    kt = _key_transpose(k, nb, ctx + seq, heads, dk)
    lat = _attention(kernel_fn, name, n_softmax, q, kt, v, nb, ctx, seq, heads, dk, dv, False, **extra)
    if not need_ctx:
        return lat.reshape(nb * seq, heads * dv)
    cx = _attention(kernel_fn, name + "_ctx", n_softmax, q, kt, v, nb, ctx, seq, heads, dk, dv, True, **extra)
    return jnp.concatenate([lat, cx], axis=1).reshape(nb * (ctx + seq), heads * dv)


def _group_ms(x, gmat):
    return jnp.dot((x * x).astype(BF16), gmat, preferred_element_type=F32)


def _odd_proj_kernel(x_ref, g_ref, sc_ref, sh_ref, wq_ref, wk_ref, wv_ref, qg_ref, kg_ref, gm_ref,
                     cos_ref, sa_ref, sb_ref, q_ref, k_ref, v_ref):
    h = _norm_mod(x_ref[...], g_ref[...], sc_ref[...], sh_ref[...]).astype(BF16)
    cos, sa, sb = cos_ref[...], sa_ref[...], sb_ref[...]
    gmat = gm_ref[...]
    q_scale = DIFF_HEAD_DIM ** -0.5 * LOG2_E
    for w_ref, gain_ref, o_ref, scale in ((wq_ref, qg_ref, q_ref, q_scale), (wk_ref, kg_ref, k_ref, 1.0)):
        raw = jnp.dot(h, w_ref[...], preferred_element_type=F32)
        gain = gain_ref[...] * scale
        for c in range(raw.shape[1] // LANES):
            t = raw[:, c * LANES:(c + 1) * LANES]
            t = t * lax.rsqrt(_group_ms(t, gmat) + NORM_EPS) * gain
            o_ref[:, c * LANES:(c + 1) * LANES] = _rope_tile(t, cos, sa, sb).astype(BF16)
    v = jnp.dot(h, wv_ref[...], preferred_element_type=F32).astype(BF16)
    for c in range(v.shape[1] // DIFF_V):
        v_ref[:, (2 * c) * DIFF_V:(2 * c + 1) * DIFF_V] = v[:, c * DIFF_V:(c + 1) * DIFF_V]
        v_ref[:, (2 * c + 1) * DIFF_V:(2 * c + 2) * DIFF_V] = _ones_column((v.shape[0], DIFF_V))


def _odd_proj(x, gain, sc, sh, wq, wk, wv, q_gain, k_gain, gmat, rope, tpb, nb):
    t, d = x.shape
    midx = _mod_index(tpb, nb)
    cos, sa, sb = rope

    def const(a):
        return pl.BlockSpec(a.shape, lambda i: (0, 0))

    tab = pl.BlockSpec((TM, LANES), lambda i: (i % tpb, 0))
    n = wq.shape[1]
    return pl.pallas_call(
        _odd_proj_kernel,
        grid=(t // TM,),
        in_specs=[pl.BlockSpec((TM, d), lambda i: (i, 0)), const(gain),
                  pl.BlockSpec((None, 1, d), lambda i: (midx(i), 0, 0)),
                  pl.BlockSpec((None, 1, d), lambda i: (midx(i), 0, 0)),
                  const(wq), const(wk), const(wv), const(q_gain), const(k_gain), const(gmat),
                  tab, tab, tab],
        out_specs=[pl.BlockSpec((TM, n), lambda i: (i, 0))] * 2 + [pl.BlockSpec((TM, 2 * n), lambda i: (i, 0))],
        out_shape=[jax.ShapeDtypeStruct((t, n), BF16)] * 2 + [jax.ShapeDtypeStruct((t, 2 * n), BF16)],
        compiler_params=_cparams(("parallel",)),
        name="odd_in_proj",
    )(x, gain, sc, sh, wq, wk, wv, q_gain, k_gain, gmat, cos, sa, sb)


def _post_mix_kernel(*refs, n_mix):
    mix_refs = refs[:n_mix]
    w_refs = refs[n_mix:2 * n_mix]
    (x_ref, g1_ref, n2_ref, sc_ref, sh_ref, rwh_ref, rwl_ref, rb_ref,
     xo_ref, h2_ref, te_ref, tw_ref, tr_ref, cnt_ref, base_ref) = refs[2 * n_mix:]
    i = pl.program_id(0)

    @pl.when(i == 0)
    def _():
        base_ref[...] = jnp.zeros_like(base_ref)

    d = x_ref.shape[-1]
    halves = []
    for lo in (0, d // 2):
        m = jnp.dot(mix_refs[0][...], w_refs[0][:, lo:lo + d // 2], preferred_element_type=F32)
        for a_ref, w_ref in zip(mix_refs[1:], w_refs[1:]):
            m = m + jnp.dot(a_ref[...], w_ref[:, lo:lo + d // 2], preferred_element_type=F32)
        halves.append(m)
    x = x_ref[...] + g1_ref[...] * jnp.concatenate(halves, axis=-1)
    xo_ref[...] = x
    h2 = _norm_mod(x, n2_ref[...], sc_ref[...], sh_ref[...])
    h2_hi = h2.astype(BF16)
    h2_ref[...] = h2_hi

    h2_lo = (h2 - h2_hi.astype(F32)).astype(BF16)
    logits = (jnp.dot(h2_hi, rwh_ref[...], preferred_element_type=F32)
              + jnp.dot(h2_lo, rwh_ref[...], preferred_element_type=F32)
              + jnp.dot(h2_hi, rwl_ref[...], preferred_element_type=F32)) + rb_ref[...]
    lane = lax.broadcasted_iota(jnp.int32, logits.shape, 1)
    lane_f = lane.astype(F32)
    vals = jnp.where(lane < N_EXPERTS, logits, NEG_BIG)
    tops, firsts, hots = [], [], []
    for _ in range(TOP_K):
        top = jnp.max(vals, axis=-1, keepdims=True)
        first = jnp.min(jnp.where(vals == top, lane_f, float(LANES)), axis=-1, keepdims=True)
        hot = lane_f == first
        tops.append(top)
        firsts.append(first)
        hots.append(hot)
        vals = jnp.where(hot, 2.0 * NEG_BIG, vals)
    exps = [jnp.exp(t - tops[0]) for t in tops]
    denom = exps[0] + exps[1] + exps[2] + exps[3]

    picked = (hots[0] | hots[1]) | (hots[2] | hots[3])
    cnt = jnp.where(picked, 1.0, 0.0)
    r_io = lax.broadcasted_iota(jnp.int32, (TM, TM), 0)
    c_io = lax.broadcasted_iota(jnp.int32, (TM, TM), 1)
    tri = jnp.where(c_io < r_io, 1.0, 0.0).astype(BF16)
    before = jnp.dot(tri, cnt.astype(BF16), preferred_element_type=F32) + base_ref[...]
    te = jnp.zeros(logits.shape, F32)
    tw = jnp.zeros(logits.shape, F32)
    tr = jnp.zeros(logits.shape, F32)
    for k in range(TOP_K):
        r_k = jnp.sum(jnp.where(hots[k], before, 0.0), axis=-1, keepdims=True)
        te = jnp.where(lane == k, firsts[k], te)
        tw = jnp.where(lane == k, exps[k] / denom, tw)
        tr = jnp.where(lane == k, r_k, tr)
    te_ref[...] = te.astype(jnp.int32)
    tw_ref[...] = tw
    tr_ref[...] = tr.astype(jnp.int32)
    base_ref[...] += jnp.sum(cnt, axis=0, keepdims=True)
    cnt_ref[...] = base_ref[...]


def _post_mix(mixes, weights, x, g1, n2g, sc2, sh2, router_w, router_b, tpb, nb, latent_only):
    d = x.shape[1]
    tiles_in = tpb
    if latent_only:
        tiles_out = tpb - 1
        n_tiles = nb * tiles_out

        def in_row(i):
            return (i // tiles_out) * tiles_in + i % tiles_out

        def midx(i):
            return i // tiles_out
    else:
        n_tiles = nb * tpb

        def in_row(i):
            return i

        midx = _mod_index(tpb, nb)
    t_out = n_tiles * TM

    def const(a):
        return pl.BlockSpec(a.shape, lambda i: (0, 0))

    def mod(a):
        return pl.BlockSpec((None, 1, d), lambda i: (midx(i), 0, 0))

    in_specs = [pl.BlockSpec((TM, a.shape[1]), lambda i: (i, 0)) for a in mixes]
    in_specs += [const(w) for w in weights]
    in_specs += [pl.BlockSpec((TM, d), lambda i: (in_row(i), 0)), mod(g1), const(n2g), mod(sc2), mod(sh2),
                 const(router_w), const(router_w), const(router_b)]
    rw_hi = router_w.astype(BF16)
    rw_lo = (router_w - rw_hi.astype(F32)).astype(BF16)
    row_out = lambda width: pl.BlockSpec((TM, width), lambda i: (i, 0))
    return pl.pallas_call(
        functools.partial(_post_mix_kernel, n_mix=len(mixes)),
        grid=(n_tiles,),
        in_specs=in_specs,
        out_specs=[row_out(d), row_out(d), row_out(LANES), row_out(LANES), row_out(LANES),
                   pl.BlockSpec((1, LANES), lambda i: (0, 0))],
        out_shape=[jax.ShapeDtypeStruct((t_out, d), F32), jax.ShapeDtypeStruct((t_out, d), BF16),
                   jax.ShapeDtypeStruct((t_out, LANES), jnp.int32), jax.ShapeDtypeStruct((t_out, LANES), F32),
                   jax.ShapeDtypeStruct((t_out, LANES), jnp.int32), jax.ShapeDtypeStruct((1, LANES), F32)],
        scratch_shapes=[pltpu.VMEM((1, LANES), F32)],
        compiler_params=_cparams(("arbitrary",)),
        name="post_mix_router",
    )(*mixes, *weights, x, g1, n2g, sc2, sh2, rw_hi, rw_lo, router_b)


def _moe_ffn_kernel(be_ref, nu_ref, nxt_ref, x_ref, wgu_hbm, bgu_ref, wd_hbm, bd_ref, o_ref,
                    wgu_f32, wd_f32, wgu_bf, wd_bf, sem, slot_ref, *, layer):
    i = pl.program_id(0)
    prev = be_ref[jnp.maximum(i - 1, 0)]
    fresh = jnp.logical_or(i == 0, be_ref[i] != prev)

    def weight_copies(expert, slot):
        return (pltpu.make_async_copy(wgu_hbm.at[layer, expert], wgu_f32.at[slot], sem.at[0, slot]),
                pltpu.make_async_copy(wd_hbm.at[layer, expert], wd_f32.at[slot], sem.at[1, slot]))

    @pl.when(jnp.logical_and(i == 0, nu_ref[0] > 0))
    def _():
        slot_ref[0] = 0
        for cp in weight_copies(be_ref[0], 0):
            cp.start()

    @pl.when(jnp.logical_and(fresh, i < nu_ref[0]))
    def _():
        slot = slot_ref[0]
        for cp in weight_copies(be_ref[i], slot):
            cp.wait()
        wgu_bf[...] = wgu_f32[slot].astype(BF16)
        wd_bf[...] = wd_f32[slot].astype(BF16)
        slot_ref[0] = 1 - slot

        @pl.when(nxt_ref[i] >= 0)
        def _():
            for cp in weight_copies(nxt_ref[i], 1 - slot):
                cp.start()

    @pl.when(i < nu_ref[0])
    def _():
        d_ff = wd_bf.shape[0]
        gu = jnp.dot(x_ref[...], wgu_bf[...], preferred_element_type=F32) + bgu_ref[...]
        g = jnp.minimum(gu[:, :d_ff], SWIGLU_LIMIT)
        u = jnp.clip(gu[:, d_ff:], -SWIGLU_LIMIT, SWIGLU_LIMIT)
        act = (u + 1.0) * g * jax.nn.sigmoid(SWIGLU_ALPHA * g)
        y = jnp.dot(act.astype(BF16), wd_bf[...], preferred_element_type=F32) + bd_ref[...]
        o_ref[...] = y.astype(o_ref.dtype)

    @pl.when(i >= nu_ref[0])
    def _():
        o_ref[...] = jnp.zeros_like(o_ref)


def _moe_ffn(block_e, n_used, next_e, xs, w_gu, b_gu, w_down, b_down, layer):
    n_rows, d = xs.shape
    depth, n_e, _, two_ff = w_gu.shape
    d_ff = two_ff // 2
    in_specs = [pl.BlockSpec((MOE_BM, d), lambda i, be, nu, nx: (i, 0)),
                pl.BlockSpec(memory_space=pl.ANY),
                pl.BlockSpec((None, None, 1, two_ff), lambda i, be, nu, nx: (layer, be[i], 0, 0)),
                pl.BlockSpec(memory_space=pl.ANY),
                pl.BlockSpec((None, None, 1, d), lambda i, be, nu, nx: (layer, be[i], 0, 0))]
    args = [block_e, n_used, next_e, xs, w_gu, b_gu.reshape(depth, n_e, 1, two_ff), w_down,
            b_down.reshape(depth, n_e, 1, d)]
    grid_spec = pltpu.PrefetchScalarGridSpec(
        num_scalar_prefetch=3,
        grid=(n_rows // MOE_BM,),
        in_specs=in_specs,
        out_specs=pl.BlockSpec((MOE_BM, d), lambda i, be, nu, nx: (i, 0)),
        scratch_shapes=[pltpu.VMEM((2, d, two_ff), F32), pltpu.VMEM((2, d_ff, d), F32),
                        pltpu.VMEM((d, two_ff), BF16), pltpu.VMEM((d_ff, d), BF16),
                        pltpu.SemaphoreType.DMA((2, 2)), pltpu.SMEM((1,), jnp.int32)],
    )
    return pl.pallas_call(
        functools.partial(_moe_ffn_kernel, layer=layer),
        grid_spec=grid_spec,
        out_shape=jax.ShapeDtypeStruct((n_rows, d), BF16),
        compiler_params=_cparams(("arbitrary",)),
        name="moe_ffn",
    )(*args)


def _next_experts(block_e, n_used):
    idx = jnp.arange(block_e.shape[0], dtype=jnp.int32)
    later = (idx[None, :] > idx[:, None]) & (idx[None, :] < n_used) & (block_e[None, :] != block_e[:, None])
    nxt = jnp.min(jnp.where(later, block_e[None, :], N_EXPERTS), axis=1)
    return jnp.where(nxt == N_EXPERTS, -1, nxt).astype(jnp.int32)


def _moe_combine_kernel(x_ref, g2_ref, w_ref, y_ref, o_ref):
    w = w_ref[...]
    f = w[:, 0:1] * y_ref[0].astype(F32)
    for k in range(1, TOP_K):
        f = f + w[:, k:k + 1] * y_ref[k].astype(F32)
    o_ref[...] = x_ref[...] + g2_ref[...] * f


def _moe_combine(x, g2, top_w, picked, midx):
    t, d = x.shape
    return pl.pallas_call(
        _moe_combine_kernel,
        grid=(t // TM,),
        in_specs=[pl.BlockSpec((TM, d), lambda i: (i, 0)),
                  pl.BlockSpec((None, 1, d), lambda i: (midx(i), 0, 0)),
                  pl.BlockSpec((TM, LANES), lambda i: (i, 0)),
                  pl.BlockSpec((TOP_K, TM, d), lambda i: (0, i, 0))],
        out_specs=pl.BlockSpec((TM, d), lambda i: (i, 0)),
        out_shape=jax.ShapeDtypeStruct((t, d), F32),
        compiler_params=_cparams(("parallel",)),
        name="moe_combine",
    )(x, g2, top_w, picked)


SMEM_TILE = 1024


def _row_tokens_kernel(lo_ref, hi_ref, dest_ref, rt_ref):
    i = pl.program_id(0)
    chunk = dest_ref.shape[0]

    @pl.when(i == 0)
    def _():
        def fill_gap(g, _):
            def fill(r, _):
                rt_ref[r] = r
                return 0
            lax.fori_loop(lo_ref[g], hi_ref[g], fill, 0)
            return 0
        lax.fori_loop(0, lo_ref.shape[0], fill_gap, 0)

    base = i * (chunk // TOP_K)

    def body(tok, _):
        for k in range(TOP_K):
            rt_ref[dest_ref[tok * TOP_K + k]] = base + tok
        return 0

    lax.fori_loop(0, chunk // TOP_K, body, 0, unroll=4)


def _row_tokens(gap_lo, gap_hi, dest_flat, n_rows):
    n_assign = dest_flat.shape[0]
    chunk = 4 * SMEM_TILE if n_assign % (4 * SMEM_TILE) == 0 else SMEM_TILE
    assert n_assign % chunk == 0
    grid_spec = pltpu.PrefetchScalarGridSpec(
        num_scalar_prefetch=2,
        grid=(n_assign // chunk,),
        in_specs=[pl.BlockSpec((chunk,), lambda i, lo, hi: (i,), memory_space=pltpu.SMEM)],
        out_specs=pl.BlockSpec(memory_space=pltpu.SMEM),
    )
    return pl.pallas_call(
        _row_tokens_kernel,
        grid_spec=grid_spec,
        out_shape=jax.ShapeDtypeStruct((n_rows,), jnp.int32),
        compiler_params=_cparams(("arbitrary",)),
        name="moe_row_tokens",
    )(gap_lo, gap_hi, dest_flat)


def _moe(x, g2, midx, h2, top_e, top_w, top_r, counts, w_gu, b_gu, w_down, b_down, layer):
    t, d = h2.shape
    n_blocks = -(-t * TOP_K // MOE_BM) + N_EXPERTS
    n_rows = n_blocks * MOE_BM
    cnt = counts[0, :N_EXPERTS].astype(jnp.int32)
    padded = (cnt + MOE_BM - 1) // MOE_BM * MOE_BM
    pends = jnp.cumsum(padded)
    pstarts = pends - padded
    e = top_e[:, :TOP_K]
    expert_ids = jnp.arange(N_EXPERTS, dtype=jnp.int32)
    pstart_of_pick = jnp.sum(jnp.where(e[:, :, None] == expert_ids, pstarts, 0), axis=-1)
    dest = pstart_of_pick + top_r[:, :TOP_K]
    block_start = jnp.arange(n_blocks, dtype=jnp.int32) * MOE_BM
    block_e = jnp.sum((block_start[:, None] >= pends[None, :]).astype(jnp.int32), axis=1)
    block_e = jnp.minimum(block_e, N_EXPERTS - 1)
    n_used = (pends[-1] // MOE_BM).astype(jnp.int32).reshape(1)

    gap_lo = jnp.concatenate([pstarts + cnt, pends[-1:]])
    gap_hi = jnp.concatenate([pends, jnp.full((1,), n_rows, jnp.int32)])
    row_tok = _row_tokens(gap_lo, gap_hi, dest.reshape(-1), n_rows)
    h2_big = jnp.concatenate([h2, jnp.zeros((n_rows - t, d), h2.dtype)], axis=0)
    xs = h2_big[row_tok]
    ys = _moe_ffn(block_e, n_used, _next_experts(block_e, n_used[0]), xs, w_gu, b_gu, w_down, b_down, layer)
    picked = ys[dest.T.reshape(-1)].reshape(TOP_K, t, d)
    return _moe_combine(x, g2, top_w, picked, midx)


def _pad_cols(w, width):
    return jnp.concatenate([w, jnp.zeros((w.shape[0], width - w.shape[1]), w.dtype)], axis=1)


def kernel(x, c, ctx, c_ctx, mod_w, mod_b, norm1_g, norm2_g, ev_w_in, ev_w_out, lru_conv_w, lru_conv_b, lru_wa, lru_ba, lru_wx, lru_bx, lru_lambda, mla_q_norm_g, mla_w_uq, mla_kv_norm_g, mla_w_ukv, mla_qn_g, mla_kn_g, od_w_in, od_w_out, diff_qn_g, diff_kn_g, diff_lq1, diff_lk1, diff_lq2, diff_lk2, diff_subln_g, router_w, router_b, moe_w_gu, moe_b_gu, moe_w_down, moe_b_down):
    nb, seq, d = x.shape
    n_ctx = ctx.shape[1]
    depth = mod_w.shape[0]
    l = n_ctx + seq
    tpb = l // TM
    assert n_ctx == TM and seq % TQ == 0 and seq % GRID_W == 0

    xa = jnp.concatenate([x, ctx], axis=1).reshape(nb * l, d)
    cvec = jnp.concatenate([c, c_ctx[None, :], jnp.zeros((SUBLANES - nb - 1, d), F32)], axis=0)
    router_w_p = jnp.concatenate([router_w, jnp.zeros((depth, d, LANES - N_EXPERTS), F32)], axis=-1)
    router_b_p = jnp.concatenate([router_b, jnp.zeros((depth, LANES - N_EXPERTS), F32)], axis=-1)
    for layer in range(depth):
        last = layer == depth - 1
        i = layer // 2
        mod = _adaln(cvec, mod_w[layer], mod_b[layer])[:nb + 1]
        sh1, sc1, g1, sh2, sc2, g2 = (mod[:, k * d:(k + 1) * d].reshape(nb + 1, 1, d) for k in range(6))
        n1g = norm1_g[layer].reshape(1, d)
        n2g = norm2_g[layer].reshape(1, d)

        if layer % 2 == 0:
            w_in = ev_w_in[i].astype(BF16)
            o = 2 * LRU_WIDTH
            splits = [w_in[:, :LRU_WIDTH], w_in[:, LRU_WIDTH:o], w_in[:, o:o + MLA_Q_RANK],
                      w_in[:, o + MLA_Q_RANK:o + MLA_Q_RANK + MLA_KV_RANK],
                      _pad_cols(w_in[:, o + MLA_Q_RANK + MLA_KV_RANK:], LANES)]
            gl, rl, qc, kvc, kr = _even_proj(xa, n1g, sc1, sh1, splits, tpb, nb)

            w_gates, b_gates = _lru_gate_weights(lru_wa[i], lru_ba[i], lru_wx[i], lru_bx[i])
            lru = _lru(gl, rl, lru_conv_w[i], lru_conv_b[i].reshape(1, LRU_WIDTH), w_gates, b_gates,
                       lru_lambda[i], nb, n_ctx, seq)

            w_uq = mla_w_uq[i].astype(BF16).reshape(MLA_Q_RANK, MLA_HEADS, MLA_QK)
            w_q_nope = w_uq[:, :, :MLA_NOPE].reshape(MLA_Q_RANK, MLA_HEADS * MLA_NOPE)
            w_q_rope = jnp.concatenate(
                [w_uq[:, :, MLA_NOPE:], jnp.zeros((MLA_Q_RANK, MLA_HEADS, LANES - MLA_ROPE), BF16)],
                axis=-1).reshape(MLA_Q_RANK, MLA_HEADS * LANES)
            pad_g = lambda g: jnp.concatenate([g, jnp.zeros((LANES - MLA_ROPE,), F32)]).reshape(1, LANES)
            rope = _rope_tables_128(seq, n_ctx, MLA_ROPE, tile_groups=False)
            q, k, v = _mla_prep(qc, kvc, kr, mla_q_norm_g[i].reshape(1, -1), mla_kv_norm_g[i].reshape(1, -1),
                                w_q_nope, w_q_rope, mla_w_ukv[i].astype(BF16),
                                mla_qn_g[i][:MLA_NOPE].reshape(1, LANES), pad_g(mla_qn_g[i][MLA_NOPE:]),
                                mla_kn_g[i][:MLA_NOPE].reshape(1, LANES), pad_g(mla_kn_g[i][MLA_NOPE:]),
                                rope, tpb)
            att = _attend_all(_mla_attn_kernel, "mla_attention", 1, q, k, v, nb, n_ctx, seq, MLA_HEADS,
                              2 * LANES, MLA_V, need_ctx=not last)
            w_out = ev_w_out[i].astype(BF16)
            if last:
                lru = lru.reshape(nb, l, LRU_WIDTH)[:, :seq].reshape(nb * seq, LRU_WIDTH)
            mixes = [lru, att]
            weights = [w_out[:LRU_WIDTH], w_out[LRU_WIDTH:]]
        else:
            lam_init = 0.8 - 0.6 * math.exp(-0.3 * layer)
            w_in = od_w_in[i].astype(BF16)
            n_qk = DIFF_HEADS * 2 * DIFF_HEAD_DIM
            gidx = jnp.arange(LANES) // DIFF_HEAD_DIM
            gmat = jnp.where(gidx[:, None] == gidx[None, :], 1.0 / DIFF_HEAD_DIM, 0.0).astype(BF16)
            rope = _rope_tables_128(seq, n_ctx, DIFF_HEAD_DIM, tile_groups=True)
            tile_g = lambda g: jnp.tile(g, LANES // DIFF_HEAD_DIM).reshape(1, LANES)
            q, k, v = _odd_proj(xa, n1g, sc1, sh1, w_in[:, :n_qk], w_in[:, n_qk:2 * n_qk], w_in[:, 2 * n_qk:],
                                tile_g(diff_qn_g[i]), tile_g(diff_kn_g[i]), gmat, rope, tpb, nb)
            lam = (jnp.exp(jnp.sum(diff_lq1[i] * diff_lk1[i])) - jnp.exp(jnp.sum(diff_lq2[i] * diff_lk2[i]))
                   + lam_init).reshape(1).astype(F32)
            att = _attend_all(
                functools.partial(_diff_attn_kernel, out_scale=1.0 - lam_init), "diff_attention", 2,
                q, k, v, nb, n_ctx, seq, DIFF_HEADS, 2 * DIFF_HEAD_DIM, DIFF_V, need_ctx=not last,
                extra_in=(diff_subln_g[i].reshape(1, DIFF_V), lam),
                extra_specs=(pl.BlockSpec((1, DIFF_V), lambda b, h: (0, 0)),
                             pl.BlockSpec(memory_space=pltpu.SMEM)))
            mixes = [att]
            weights = [od_w_out[i].astype(BF16)]

        xo, h2, te, tw, tr, counts = _post_mix(mixes, weights, xa, g1, n2g, sc2, sh2, router_w_p[layer],
                                               router_b_p[layer].reshape(1, LANES), tpb, nb, last)
        midx = (lambda t: t // (tpb - 1)) if last else _mod_index(tpb, nb)
        xa = _moe(xo, g2, midx, h2, te, tw, tr, counts, moe_w_gu, moe_b_gu, moe_w_down, moe_b_down, layer)
    return xa.reshape(nb, seq, d)
```

```python
import functools
import math

import jax
import jax.numpy as jnp
from jax import lax
from jax.experimental import pallas as pl
from jax.experimental.pallas import tpu as pltpu

F32 = jnp.float32
BF16 = jnp.bfloat16
HIGHEST = lax.Precision.HIGHEST

GRID_W = 64
NORM_EPS = 1e-6
ROPE_BASE = 10000.0
LRU_WIDTH = 512
LRU_BLOCKS = 8
LRU_BLOCK_W = LRU_WIDTH // LRU_BLOCKS
LRU_C = 8.0
CONV_W = 4
MLA_HEADS = 4
MLA_Q_RANK = 384
MLA_KV_RANK = 256
MLA_NOPE = 128
MLA_ROPE = 64
MLA_V = 128
MLA_QK = MLA_NOPE + MLA_ROPE
DIFF_HEADS = 8
DIFF_HEAD_DIM = 64
DIFF_V = 2 * DIFF_HEAD_DIM
N_EXPERTS = 32
TOP_K = 4
SWIGLU_LIMIT = 7.0
SWIGLU_ALPHA = 1.702

LANES = 128
SUBLANES = 8
VMEM_LIMIT = 52 * 1024 * 1024

TM = 256
TQ = 512
LRU_CT = 128
LRU_CHUNK = 128
LRU_SEG = LRU_CHUNK // SUBLANES
LRU_PAD = 8
MOE_BM = 256
NEG_BIG = -1e30
LOG2_E = math.log2(math.e)


def _cparams(sem, vmem=VMEM_LIMIT):
    return pltpu.CompilerParams(dimension_semantics=sem, vmem_limit_bytes=vmem)


def _adaln_kernel(c_ref, w_ref, b_ref, o_ref):
    c = c_ref[...]
    s = c * jax.nn.sigmoid(c)
    o_ref[...] = jnp.dot(s, w_ref[...], preferred_element_type=F32, precision=HIGHEST) + b_ref[...]


def _adaln(cvec, w, b):
    rows, d = cvec.shape
    n = w.shape[1]
    tn = 1536
    return pl.pallas_call(
        _adaln_kernel,
        grid=(n // tn,),
        in_specs=[pl.BlockSpec((rows, d), lambda j: (0, 0)),
                  pl.BlockSpec((d, tn), lambda j: (0, j)),
                  pl.BlockSpec((1, tn), lambda j: (0, j))],
        out_specs=pl.BlockSpec((rows, tn), lambda j: (0, j)),
        out_shape=jax.ShapeDtypeStruct((rows, n), F32),
        compiler_params=_cparams(("arbitrary",)),
        name="adaln_mod",
    )(cvec, w, b.reshape(1, n))


def _norm_mod(x, g, sc, sh):
    var = jnp.mean(x * x, axis=-1, keepdims=True)
    y = x * lax.rsqrt(var + NORM_EPS) * g
    return y * (1.0 + sc) + sh


def _ones_column(shape):
    lane = lax.broadcasted_iota(jnp.int32, shape, 1)
    return jnp.where(lane == 0, 1.0, 0.0).astype(BF16)


def _rope_tile(x, cos, sin_a, sin_b):
    up = pltpu.roll(x, LANES - 16, axis=1)
    dn = pltpu.roll(x, 16, axis=1)
    return x * cos + up * sin_a + dn * sin_b


def _rope_tables(seq, rot_dim):
    n_rows = seq // GRID_W
    rows = jnp.repeat(jnp.arange(n_rows, dtype=F32), GRID_W)
    cols = jnp.tile(jnp.arange(GRID_W, dtype=F32), n_rows)
    axis_dim = rot_dim // 2
    inv_freq = ROPE_BASE ** (-jnp.arange(0, axis_dim, 2, dtype=F32) / axis_dim)
    ang_r = rows[:, None] * inv_freq
    ang_c = cols[:, None] * inv_freq
    ang = jnp.concatenate([ang_r, ang_r, ang_c, ang_c], axis=-1)
    cos, sin = jnp.cos(ang), jnp.sin(ang)
    quarter = rot_dim // 4
    first = (jnp.arange(rot_dim) % (2 * quarter)) < quarter
    sin_a = jnp.where(first, -sin, 0.0)
    sin_b = jnp.where(first, 0.0, sin)
    return cos, sin_a, sin_b


def _rope_tables_128(seq, ctx, rot_dim, tile_groups):
    cos, sin_a, sin_b = _rope_tables(seq, rot_dim)
    if tile_groups:
        reps = LANES // rot_dim
        cos, sin_a, sin_b = (jnp.tile(t, (1, reps)) for t in (cos, sin_a, sin_b))
    else:
        pad = LANES - rot_dim
        cos = jnp.concatenate([cos, jnp.ones((seq, pad), F32)], axis=-1)
        sin_a = jnp.concatenate([sin_a, jnp.zeros((seq, pad), F32)], axis=-1)
        sin_b = jnp.concatenate([sin_b, jnp.zeros((seq, pad), F32)], axis=-1)
    cos = jnp.concatenate([cos, jnp.ones((ctx, LANES), F32)], axis=0)
    sin_a = jnp.concatenate([sin_a, jnp.zeros((ctx, LANES), F32)], axis=0)
    sin_b = jnp.concatenate([sin_b, jnp.zeros((ctx, LANES), F32)], axis=0)
    return cos, sin_a, sin_b


def _mod_index(tpb, nb):
    def idx(i):
        return jnp.where(i % tpb == tpb - 1, nb, i // tpb)
    return idx


def _even_proj_kernel(x_ref, g_ref, sc_ref, sh_ref, wg_ref, wr_ref, wq_ref, wkv_ref, wkr_ref,
                      og_ref, or_ref, oq_ref, okv_ref, okr_ref):
    h = _norm_mod(x_ref[...], g_ref[...], sc_ref[...], sh_ref[...]).astype(BF16)
    for w_ref, o_ref in ((wg_ref, og_ref), (wr_ref, or_ref), (wq_ref, oq_ref),
                         (wkv_ref, okv_ref), (wkr_ref, okr_ref)):
        o_ref[...] = jnp.dot(h, w_ref[...], preferred_element_type=F32)


def _even_proj(x, gain, sc, sh, weights, tpb, nb):
    t, d = x.shape
    midx = _mod_index(tpb, nb)
    w_specs = [pl.BlockSpec(w.shape, lambda i: (0, 0)) for w in weights]
    return pl.pallas_call(
        _even_proj_kernel,
        grid=(t // TM,),
        in_specs=[pl.BlockSpec((TM, d), lambda i: (i, 0)),
                  pl.BlockSpec((1, d), lambda i: (0, 0)),
                  pl.BlockSpec((None, 1, d), lambda i: (midx(i), 0, 0)),
                  pl.BlockSpec((None, 1, d), lambda i: (midx(i), 0, 0))] + w_specs,
        out_specs=[pl.BlockSpec((TM, w.shape[1]), lambda i: (i, 0)) for w in weights],
        out_shape=[jax.ShapeDtypeStruct((t, w.shape[1]), F32) for w in weights],
        compiler_params=_cparams(("parallel",)),
        name="even_in_proj",
    )(x, gain, sc, sh, *weights)


def _mla_prep_kernel(qc_ref, kvc_ref, kr_ref, qng_ref, kvng_ref, wqn_ref, wqr_ref, wkv_ref,
                     qgn_ref, qgr_ref, kgn_ref, kgr_ref, cos_ref, sa_ref, sb_ref,
                     q_ref, k_ref, v_ref):
    cos, sa, sb = cos_ref[...], sa_ref[...], sb_ref[...]
    inv_qk = 1.0 / MLA_QK

    qc = qc_ref[...]
    hq = (qc * lax.rsqrt(jnp.mean(qc * qc, axis=-1, keepdims=True) + NORM_EPS) * qng_ref[...]).astype(BF16)
    q_nope = jnp.dot(hq, wqn_ref[...], preferred_element_type=F32)
    q_rope = jnp.dot(hq, wqr_ref[...], preferred_element_type=F32)
    q_scale = MLA_QK ** -0.5 * LOG2_E
    for h in range(MLA_HEADS):
        qn = q_nope[:, h * LANES:(h + 1) * LANES]
        qr = q_rope[:, h * LANES:(h + 1) * LANES]
        ms = (jnp.sum(qn * qn, axis=-1, keepdims=True) + jnp.sum(qr * qr, axis=-1, keepdims=True)) * inv_qk
        rs = lax.rsqrt(ms + NORM_EPS) * q_scale
        q_ref[:, (2 * h) * LANES:(2 * h + 1) * LANES] = (qn * rs * qgn_ref[...]).astype(BF16)
        q_ref[:, (2 * h + 1) * LANES:(2 * h + 2) * LANES] = _rope_tile(qr * rs * qgr_ref[...], cos, sa, sb).astype(BF16)

    kvc = kvc_ref[...]
    hkv = (kvc * lax.rsqrt(jnp.mean(kvc * kvc, axis=-1, keepdims=True) + NORM_EPS) * kvng_ref[...]).astype(BF16)
    kv = jnp.dot(hkv, wkv_ref[...], preferred_element_type=F32)
    kr = kr_ref[...]
    kr_ss = jnp.sum(kr * kr, axis=-1, keepdims=True)
    kr_rot = _rope_tile(kr * kgr_ref[...], cos, sa, sb)
    for h in range(MLA_HEADS):
        kn = kv[:, (2 * h) * LANES:(2 * h + 1) * LANES]
        ms = (jnp.sum(kn * kn, axis=-1, keepdims=True) + kr_ss) * inv_qk
        rs = lax.rsqrt(ms + NORM_EPS)
        k_ref[:, (2 * h) * LANES:(2 * h + 1) * LANES] = (kn * rs * kgn_ref[...]).astype(BF16)
        k_ref[:, (2 * h + 1) * LANES:(2 * h + 2) * LANES] = (kr_rot * rs).astype(BF16)
        v_ref[:, (2 * h) * LANES:(2 * h + 1) * LANES] = kv[:, (2 * h + 1) * LANES:(2 * h + 2) * LANES].astype(BF16)
        v_ref[:, (2 * h + 1) * LANES:(2 * h + 2) * LANES] = _ones_column((kv.shape[0], LANES))


def _mla_prep(qc, kvc, kr, q_norm_g, kv_norm_g, w_q_nope, w_q_rope, w_ukv, q_gn, q_gr, k_gn, k_gr,
              rope, tpb):
    t = qc.shape[0]
    cos, sa, sb = rope

    def const(a):
        return pl.BlockSpec(a.shape, lambda i: (0, 0))

    def rows(a):
        return pl.BlockSpec((TM, a.shape[1]), lambda i: (i, 0))

    tab = pl.BlockSpec((TM, LANES), lambda i: (i % tpb, 0))
    hq, hv = MLA_HEADS * 2 * LANES, MLA_HEADS * 2 * MLA_V
    return pl.pallas_call(
        _mla_prep_kernel,
        grid=(t // TM,),
        in_specs=[rows(qc), rows(kvc), rows(kr), const(q_norm_g), const(kv_norm_g),
                  const(w_q_nope), const(w_q_rope), const(w_ukv),
                  const(q_gn), const(q_gr), const(k_gn), const(k_gr), tab, tab, tab],
        out_specs=[pl.BlockSpec((TM, hq), lambda i: (i, 0)),
                   pl.BlockSpec((TM, hq), lambda i: (i, 0)),
                   pl.BlockSpec((TM, hv), lambda i: (i, 0))],
        out_shape=[jax.ShapeDtypeStruct((t, hq), BF16),
                   jax.ShapeDtypeStruct((t, hq), BF16),
                   jax.ShapeDtypeStruct((t, hv), BF16)],
        compiler_params=_cparams(("parallel",)),
        name="mla_prep",
    )(qc, kvc, kr, q_norm_g, kv_norm_g, w_q_nope, w_q_rope, w_ukv, q_gn, q_gr, k_gn, k_gr, cos, sa, sb)


def _sublane_iota():
    return lax.broadcasted_iota(jnp.int32, (SUBLANES, LANES), 0)


def _scan_chunk(a_chunk, u_chunk, h_chunk, carry, reverse):
    steps = range(LRU_SEG - 1, -1, -1) if reverse else range(LRU_SEG)
    h_loc, p_loc = [None] * LRU_SEG, [None] * LRU_SEG
    h = p = None
    for j in steps:
        a = a_chunk[pl.ds(j, SUBLANES, stride=LRU_SEG), :]
        u = u_chunk[pl.ds(j, SUBLANES, stride=LRU_SEG), :]
        if h is None:
            h, p = u, a
        else:
            h, p = a * h + u, a * p
        h_loc[j], p_loc[j] = h, p
    sub = _sublane_iota()
    seg_p, seg_h = p, h
    for d in (1, 2, 4):
        shift = SUBLANES - d if reverse else d
        prev_p = pltpu.roll(seg_p, shift, axis=0)
        prev_h = pltpu.roll(seg_h, shift, axis=0)
        valid = (sub < SUBLANES - d) if reverse else (sub >= d)
        seg_h = jnp.where(valid, seg_p * prev_h + seg_h, seg_h)
        seg_p = jnp.where(valid, seg_p * prev_p, seg_p)
    h_end = seg_h + seg_p * carry
    if reverse:
        h_in = jnp.where(sub == SUBLANES - 1, carry, pltpu.roll(h_end, SUBLANES - 1, axis=0))
        new_carry = h_end[0:1, :]
    else:
        h_in = jnp.where(sub == 0, carry, pltpu.roll(h_end, 1, axis=0))
        new_carry = h_end[SUBLANES - 1:SUBLANES, :]
    for j in range(LRU_SEG):
        h_chunk[pl.ds(j, SUBLANES, stride=LRU_SEG), :] = h_loc[j] + p_loc[j] * h_in
    return jnp.broadcast_to(new_carry, (SUBLANES, LANES))


def _lru_kernel(g_ref, r_ref, cw_ref, cb_ref, wg_ref, bg_ref, lam_ref, o_ref,
                rp_ref, af_ref, uf_ref, ab_ref, ub_ref, hf_ref, hb_ref, *, ctx, seq):
    n_ctx, n_lat = ctx // LRU_CHUNK, seq // LRU_CHUNK
    zeros_pad = jnp.zeros((LRU_PAD, LRU_CT), F32)
    ctx0 = seq + 2 * LRU_PAD
    rp_ref[0:LRU_PAD, :] = zeros_pad
    rp_ref[LRU_PAD:LRU_PAD + seq, :] = r_ref[0:seq, :]
    rp_ref[LRU_PAD + seq:ctx0, :] = zeros_pad
    rp_ref[ctx0:ctx0 + ctx, :] = r_ref[seq:seq + ctx, :]
    rp_ref[ctx0 + ctx:ctx0 + ctx + LRU_PAD, :] = zeros_pad

    cw = cw_ref[...]
    cb = cb_ref[...]
    wg = wg_ref[...]
    bg = bg_ref[...]
    lam = lam_ref[...]
    sp = jnp.maximum(-lam, 0.0) + jnp.log1p(jnp.exp(-jnp.abs(lam)))

    def coeff_chunk(c, pad_off, row_off):
        start = pl.multiple_of(pad_off + c * LRU_CHUNK, SUBLANES)
        ext = rp_ref[pl.ds(start, LRU_CHUNK + 2 * LRU_PAD), :]
        x = cb
        for tap in range(CONV_W):
            lo = LRU_PAD - 2 + tap
            x = x + ext[lo:lo + LRU_CHUNK, :] * cw[tap:tap + 1, :]
        gates = jnp.dot(x.astype(BF16), wg, preferred_element_type=F32) + bg
        out_row = pl.multiple_of(row_off + c * LRU_CHUNK, SUBLANES)
        for d, (a_ref, u_ref) in enumerate(((af_ref, uf_ref), (ab_ref, ub_ref))):
            r = jax.nn.sigmoid(gates[:, (2 * d) * LRU_CT:(2 * d + 1) * LRU_CT])
            i = jax.nn.sigmoid(gates[:, (2 * d + 1) * LRU_CT:(2 * d + 2) * LRU_CT])
            log_a = -LRU_C * r * sp[d:d + 1, :]
            a_ref[pl.ds(out_row, LRU_CHUNK), :] = jnp.exp(log_a)
            th = jnp.tanh(log_a)
            u_ref[pl.ds(out_row, LRU_CHUNK), :] = jnp.sqrt(-2.0 * th / (1.0 - th)) * i * x
        return None

    def coeff_ctx(c, _):
        coeff_chunk(c, seq + LRU_PAD, seq)
        return 0

    def coeff_lat(c, _):
        coeff_chunk(c, 0, 0)
        return 0

    lax.fori_loop(0, n_ctx, coeff_ctx, 0, unroll=2)
    lax.fori_loop(0, n_lat, coeff_lat, 0, unroll=2)

    def scan_pair(n, row_off):
        def body(c, carry):
            cf, cb_ = carry
            f_row = pl.multiple_of(row_off + c * LRU_CHUNK, SUBLANES)
            b_row = pl.multiple_of(row_off + (n - 1 - c) * LRU_CHUNK, SUBLANES)
            cf = _scan_chunk(af_ref.at[pl.ds(f_row, LRU_CHUNK), :], uf_ref.at[pl.ds(f_row, LRU_CHUNK), :],
                             hf_ref.at[pl.ds(f_row, LRU_CHUNK), :], cf, False)
            cb_ = _scan_chunk(ab_ref.at[pl.ds(b_row, LRU_CHUNK), :], ub_ref.at[pl.ds(b_row, LRU_CHUNK), :],
                              hb_ref.at[pl.ds(b_row, LRU_CHUNK), :], cb_, True)
            return cf, cb_
        return body

    zero = jnp.zeros((SUBLANES, LANES), F32)
    carry = lax.fori_loop(0, n_ctx, scan_pair(n_ctx, seq), (zero, zero), unroll=2)
    lax.fori_loop(0, n_lat, scan_pair(n_lat, 0), carry, unroll=2)

    g = g_ref[...]
    gelu = 0.5 * g * (1.0 + jnp.tanh(math.sqrt(2.0 / math.pi) * (g + 0.044715 * (g * g * g))))
    o_ref[...] = (gelu * (hf_ref[...] + hb_ref[...])).astype(BF16)


def _lru(g, r, conv_w, conv_b, w_gates, b_gates, lam, nb, ctx, seq):
    l = ctx + seq
    width = g.shape[1]
    n_ct = width // LRU_CT
    g3 = g.reshape(nb, l, width)
    r3 = r.reshape(nb, l, width)
    seq_spec = pl.BlockSpec((None, l, LRU_CT), lambda b, c: (b, 0, c))
    scratch = [pltpu.VMEM((l + 3 * LRU_PAD, LRU_CT), F32)] + [pltpu.VMEM((l, LRU_CT), F32)] * 6
    out = pl.pallas_call(
        functools.partial(_lru_kernel, ctx=ctx, seq=seq),
        grid=(nb, n_ct),
        in_specs=[seq_spec, seq_spec,
                  pl.BlockSpec((CONV_W, LRU_CT), lambda b, c: (0, c)),
                  pl.BlockSpec((1, LRU_CT), lambda b, c: (0, c)),
                  pl.BlockSpec((None, LRU_CT, 4 * LRU_CT), lambda b, c: (c, 0, 0)),
                  pl.BlockSpec((None, 1, 4 * LRU_CT), lambda b, c: (c, 0, 0)),
                  pl.BlockSpec((2, LRU_CT), lambda b, c: (0, c))],
        out_specs=seq_spec,
        out_shape=jax.ShapeDtypeStruct((nb, l, width), BF16),
        scratch_shapes=scratch,
        compiler_params=_cparams(("parallel", "parallel")),
        name="rglru",
    )(g3, r3, conv_w, conv_b, w_gates, b_gates, lam)
    return out.reshape(nb * l, width)


def _lru_gate_weights(wa, ba, wx, bx):
    per = LRU_CT // LRU_BLOCK_W
    n_ct = LRU_BLOCKS // per
    eye = jnp.eye(per, dtype=F32)

    def dense(w):
        w4 = w.reshape(n_ct, per, LRU_BLOCK_W, LRU_BLOCK_W)
        return jnp.einsum('cide,ij->cidje', w4, eye).reshape(n_ct, LRU_CT, LRU_CT)

    w = jnp.concatenate([dense(wa[0]), dense(wx[0]), dense(wa[1]), dense(wx[1])], axis=-1)
    b = jnp.concatenate([v.reshape(n_ct, 1, LRU_CT) for v in (ba[0], bx[0], ba[1], bx[1])], axis=-1)
    return w.astype(BF16), b


def _key_spans(n_keys):
    half = n_keys // 2 if n_keys % (2 * LANES) == 0 else n_keys
    return [(lo, lo + half) for lo in range(0, n_keys, half)]


def _attn_scores(q_ops, kt_ref, s_ref, m_ref):
    for t, q in enumerate(q_ops):
        for lo, hi in _key_spans(kt_ref.shape[-1]):
            s_ref[t, :, lo:hi] = jnp.dot(q, kt_ref[:, lo:hi], preferred_element_type=F32)
        m_ref[t] = jnp.max(s_ref[t], axis=-1, keepdims=True)


def _attn_values(v_ref, s_ref, m_ref, p_ref):
    accs = []
    for t in range(s_ref.shape[0]):
        m = m_ref[t]
        acc = None
        for lo, hi in _key_spans(s_ref.shape[-1]):
            p_ref[t, :, lo:hi] = jnp.exp2(s_ref[t, :, lo:hi] - m).astype(BF16)
            part = jnp.dot(p_ref[t, :, lo:hi], v_ref[lo:hi, :], preferred_element_type=F32)
            acc = part if acc is None else acc + part
        accs.append(acc)
    return accs


def _attn_pipeline(q_ref, kt_ref, v_ref, o_ref, bufs, p_ref, q_ops_fn, values_fn):
    s0, m0 = bufs[0]
    tq = s0.shape[1]
    n = q_ref.shape[0] // tq

    def scores(t, s_ref, m_ref):
        row = t * tq if isinstance(t, int) else pl.multiple_of(t * tq, tq)
        _attn_scores(q_ops_fn(q_ref[pl.ds(row, tq), :]), kt_ref, s_ref, m_ref)

    def values(t, s_ref, m_ref):
        row = t * tq if isinstance(t, int) else pl.multiple_of(t * tq, tq)
        o_ref[pl.ds(row, tq), :] = values_fn(v_ref, s_ref, m_ref, p_ref).astype(o_ref.dtype)

    if len(bufs) == 1:
        def one(t, _):
            scores(t, s0, m0)
            values(t, s0, m0)
            return 0

        if n == 1:
            one(0, 0)
        else:
            lax.fori_loop(0, n, one, 0)
        return

    s1, m1 = bufs[1]
    scores(0, s0, m0)
    if n == 1:
        values(0, s0, m0)
        return
    assert n % 2 == 0

    def pair(k, _):
        t = 2 * k
        scores(t + 1, s1, m1)
        values(t, s0, m0)
        scores(t + 2, s0, m0)
        values(t + 1, s1, m1)
        return 0

    lax.fori_loop(0, n // 2 - 1, pair, 0)
    scores(n - 1, s1, m1)
    values(n - 2, s0, m0)
    values(n - 1, s1, m1)


def _score_bufs(scratch):
    return tuple(zip(scratch[0:-1:2], scratch[1:-1:2])), scratch[-1]


def _mla_attn_kernel(q_ref, kt_ref, v_ref, o_ref, *scratch):
    dv = o_ref.shape[-1]
    bufs, p_ref = _score_bufs(scratch)

    def values(*refs):
        acc = _attn_values(*refs)[0]
        return acc[:, :dv] / acc[:, dv:dv + 1]

    _attn_pipeline(q_ref, kt_ref, v_ref, o_ref, bufs, p_ref, lambda q: [q], values)


def _diff_attn_kernel(q_ref, kt_ref, v_ref, g_ref, lam_ref, o_ref, *scratch, out_scale):
    dv = o_ref.shape[-1]
    bufs, p_ref = _score_bufs(scratch)

    def q_ops(q):
        lane = lax.broadcasted_iota(jnp.int32, q.shape, 1)
        zero = jnp.zeros_like(q)
        return [jnp.where(lane < DIFF_HEAD_DIM, q, zero), jnp.where(lane < DIFF_HEAD_DIM, zero, q)]

    def values(*refs):
        a1, a2 = _attn_values(*refs)
        o = a1[:, :dv] / a1[:, dv:dv + 1] - lam_ref[0] * (a2[:, :dv] / a2[:, dv:dv + 1])
        y = o * lax.rsqrt(jnp.mean(o * o, axis=-1, keepdims=True) + NORM_EPS) * g_ref[...]
        return y * out_scale

    _attn_pipeline(q_ref, kt_ref, v_ref, o_ref, bufs, p_ref, q_ops, values)


def _attention(kernel_fn, name, n_softmax, q, kt, v, nb, ctx, seq, heads, dk, dv, ctx_queries,
               extra_in=(), extra_specs=()):
    l = ctx + seq
    q3 = q.reshape(nb, l, heads * dk)
    v3 = v.reshape(nb, l, heads * 2 * dv)
    if ctx_queries:
        tq, n_keys, rows = ctx, ctx, ctx
        blk = seq // ctx
        in_specs = [pl.BlockSpec((None, rows, dk), lambda b, h: (b, blk, h)),
                    pl.BlockSpec((None, None, dk, ctx), lambda b, h: (b, h, 0, blk)),
                    pl.BlockSpec((None, ctx, 2 * dv), lambda b, h: (b, blk, h))]
    else:
        tq, n_keys, rows = TQ, l, seq
        in_specs = [pl.BlockSpec((None, rows, dk), lambda b, h: (b, 0, h)),
                    pl.BlockSpec((None, None, dk, l), lambda b, h: (b, h, 0, 0)),
                    pl.BlockSpec((None, l, 2 * dv), lambda b, h: (b, 0, h))]
    score_buf = [pltpu.VMEM((n_softmax, tq, n_keys), F32), pltpu.VMEM((n_softmax, tq, 1), F32)]
    out = pl.pallas_call(
        kernel_fn,
        grid=(nb, heads),
        in_specs=in_specs + list(extra_specs),
        out_specs=pl.BlockSpec((None, rows, dv), lambda b, h: (b, 0, h)),
        out_shape=jax.ShapeDtypeStruct((nb, rows, heads * dv), BF16),
        scratch_shapes=score_buf * (2 if n_softmax == 1 else 1) + [pltpu.VMEM((n_softmax, tq, n_keys), BF16)],
        compiler_params=_cparams(("parallel", "parallel")),
        name=name,
    )(q3, kt, v3, *extra_in)
    return out


def _key_transpose(k, nb, l, heads, dk):
    return jnp.transpose(k.reshape(nb, l, heads, dk), (0, 2, 3, 1))


def _attend_all(kernel_fn, name, n_softmax, q, k, v, nb, ctx, seq, heads, dk, dv, need_ctx, **extra):
    kt = _key_transpose(k, nb, ctx + seq, heads, dk)
    lat = _attention(kernel_fn, name, n_softmax, q, kt, v, nb, ctx, seq, heads, dk, dv, False, **extra)
    if not need_ctx:
        return lat.reshape(nb * seq, heads * dv)
    cx = _attention(kernel_fn, name + "_ctx", n_softmax, q, kt, v, nb, ctx, seq, heads, dk, dv, True, **extra)
    return jnp.concatenate([lat, cx], axis=1).reshape(nb * (ctx + seq), heads * dv)


def _group_ms(x, gmat):
    return jnp.dot((x * x).astype(BF16), gmat, preferred_element_type=F32)


def _odd_proj_kernel(x_ref, g_ref, sc_ref, sh_ref, wq_ref, wk_ref, wv_ref, qg_ref, kg_ref, gm_ref,
                     cos_ref, sa_ref, sb_ref, q_ref, k_ref, v_ref):
    h = _norm_mod(x_ref[...], g_ref[...], sc_ref[...], sh_ref[...]).astype(BF16)
    cos, sa, sb = cos_ref[...], sa_ref[...], sb_ref[...]
    gmat = gm_ref[...]
    q_scale = DIFF_HEAD_DIM ** -0.5 * LOG2_E
    for w_ref, gain_ref, o_ref, scale in ((wq_ref, qg_ref, q_ref, q_scale), (wk_ref, kg_ref, k_ref, 1.0)):
        raw = jnp.dot(h, w_ref[...], preferred_element_type=F32)
        gain = gain_ref[...] * scale
        for c in range(raw.shape[1] // LANES):
            t = raw[:, c * LANES:(c + 1) * LANES]
            t = t * lax.rsqrt(_group_ms(t, gmat) + NORM_EPS) * gain
            o_ref[:, c * LANES:(c + 1) * LANES] = _rope_tile(t, cos, sa, sb).astype(BF16)
    v = jnp.dot(h, wv_ref[...], preferred_element_type=F32).astype(BF16)
    for c in range(v.shape[1] // DIFF_V):
        v_ref[:, (2 * c) * DIFF_V:(2 * c + 1) * DIFF_V] = v[:, c * DIFF_V:(c + 1) * DIFF_V]
        v_ref[:, (2 * c + 1) * DIFF_V:(2 * c + 2) * DIFF_V] = _ones_column((v.shape[0], DIFF_V))


def _odd_proj(x, gain, sc, sh, wq, wk, wv, q_gain, k_gain, gmat, rope, tpb, nb):
    t, d = x.shape
    midx = _mod_index(tpb, nb)
    cos, sa, sb = rope

    def const(a):
        return pl.BlockSpec(a.shape, lambda i: (0, 0))

    tab = pl.BlockSpec((TM, LANES), lambda i: (i % tpb, 0))
    n = wq.shape[1]
    return pl.pallas_call(
        _odd_proj_kernel,
        grid=(t // TM,),
        in_specs=[pl.BlockSpec((TM, d), lambda i: (i, 0)), const(gain),
                  pl.BlockSpec((None, 1, d), lambda i: (midx(i), 0, 0)),
                  pl.BlockSpec((None, 1, d), lambda i: (midx(i), 0, 0)),
                  const(wq), const(wk), const(wv), const(q_gain), const(k_gain), const(gmat),
                  tab, tab, tab],
        out_specs=[pl.BlockSpec((TM, n), lambda i: (i, 0))] * 2 + [pl.BlockSpec((TM, 2 * n), lambda i: (i, 0))],
        out_shape=[jax.ShapeDtypeStruct((t, n), BF16)] * 2 + [jax.ShapeDtypeStruct((t, 2 * n), BF16)],
        compiler_params=_cparams(("parallel",)),
        name="odd_in_proj",
    )(x, gain, sc, sh, wq, wk, wv, q_gain, k_gain, gmat, cos, sa, sb)


def _post_mix_kernel(*refs, n_mix):
    mix_refs = refs[:n_mix]
    w_refs = refs[n_mix:2 * n_mix]
    (x_ref, g1_ref, n2_ref, sc_ref, sh_ref, rwh_ref, rwl_ref, rb_ref,
     xo_ref, h2_ref, te_ref, tw_ref, tr_ref, cnt_ref, base_ref) = refs[2 * n_mix:]
    i = pl.program_id(0)

    @pl.when(i == 0)
    def _():
        base_ref[...] = jnp.zeros_like(base_ref)

    d = x_ref.shape[-1]
    halves = []
    for lo in (0, d // 2):
        m = jnp.dot(mix_refs[0][...], w_refs[0][:, lo:lo + d // 2], preferred_element_type=F32)
        for a_ref, w_ref in zip(mix_refs[1:], w_refs[1:]):
            m = m + jnp.dot(a_ref[...], w_ref[:, lo:lo + d // 2], preferred_element_type=F32)
        halves.append(m)
    x = x_ref[...] + g1_ref[...] * jnp.concatenate(halves, axis=-1)
    xo_ref[...] = x
    h2 = _norm_mod(x, n2_ref[...], sc_ref[...], sh_ref[...])
    h2_hi = h2.astype(BF16)
    h2_ref[...] = h2_hi

    h2_lo = (h2 - h2_hi.astype(F32)).astype(BF16)
    logits = (jnp.dot(h2_hi, rwh_ref[...], preferred_element_type=F32)
              + jnp.dot(h2_lo, rwh_ref[...], preferred_element_type=F32)
              + jnp.dot(h2_hi, rwl_ref[...], preferred_element_type=F32)) + rb_ref[...]
    lane = lax.broadcasted_iota(jnp.int32, logits.shape, 1)
    lane_f = lane.astype(F32)
    vals = jnp.where(lane < N_EXPERTS, logits, NEG_BIG)
    tops, firsts, hots = [], [], []
    for _ in range(TOP_K):
        top = jnp.max(vals, axis=-1, keepdims=True)
        first = jnp.min(jnp.where(vals == top, lane_f, float(LANES)), axis=-1, keepdims=True)
        hot = lane_f == first
        tops.append(top)
        firsts.append(first)
        hots.append(hot)
        vals = jnp.where(hot, 2.0 * NEG_BIG, vals)
    exps = [jnp.exp(t - tops[0]) for t in tops]
    denom = exps[0] + exps[1] + exps[2] + exps[3]

    picked = (hots[0] | hots[1]) | (hots[2] | hots[3])
    cnt = jnp.where(picked, 1.0, 0.0)
    r_io = lax.broadcasted_iota(jnp.int32, (TM, TM), 0)
    c_io = lax.broadcasted_iota(jnp.int32, (TM, TM), 1)
    tri = jnp.where(c_io < r_io, 1.0, 0.0).astype(BF16)
    before = jnp.dot(tri, cnt.astype(BF16), preferred_element_type=F32) + base_ref[...]
    te = jnp.zeros(logits.shape, F32)
    tw = jnp.zeros(logits.shape, F32)
    tr = jnp.zeros(logits.shape, F32)
    for k in range(TOP_K):
        r_k = jnp.sum(jnp.where(hots[k], before, 0.0), axis=-1, keepdims=True)
        te = jnp.where(lane == k, firsts[k], te)
        tw = jnp.where(lane == k, exps[k] / denom, tw)
        tr = jnp.where(lane == k, r_k, tr)
    te_ref[...] = te.astype(jnp.int32)
    tw_ref[...] = tw
    tr_ref[...] = tr.astype(jnp.int32)
    base_ref[...] += jnp.sum(cnt, axis=0, keepdims=True)
    cnt_ref[...] = base_ref[...]


def _post_mix(mixes, weights, x, g1, n2g, sc2, sh2, router_w, router_b, tpb, nb, latent_only):
    d = x.shape[1]
    tiles_in = tpb
    if latent_only:
        tiles_out = tpb - 1
        n_tiles = nb * tiles_out

        def in_row(i):
            return (i // tiles_out) * tiles_in + i % tiles_out

        def midx(i):
            return i // tiles_out
    else:
        n_tiles = nb * tpb

        def in_row(i):
            return i

        midx = _mod_index(tpb, nb)
    t_out = n_tiles * TM

    def const(a):
        return pl.BlockSpec(a.shape, lambda i: (0, 0))

    def mod(a):
        return pl.BlockSpec((None, 1, d), lambda i: (midx(i), 0, 0))

    in_specs = [pl.BlockSpec((TM, a.shape[1]), lambda i: (i, 0)) for a in mixes]
    in_specs += [const(w) for w in weights]
    in_specs += [pl.BlockSpec((TM, d), lambda i: (in_row(i), 0)), mod(g1), const(n2g), mod(sc2), mod(sh2),
                 const(router_w), const(router_w), const(router_b)]
    rw_hi = router_w.astype(BF16)
    rw_lo = (router_w - rw_hi.astype(F32)).astype(BF16)
    row_out = lambda width: pl.BlockSpec((TM, width), lambda i: (i, 0))
    return pl.pallas_call(
        functools.partial(_post_mix_kernel, n_mix=len(mixes)),
        grid=(n_tiles,),
        in_specs=in_specs,
        out_specs=[row_out(d), row_out(d), row_out(LANES), row_out(LANES), row_out(LANES),
                   pl.BlockSpec((1, LANES), lambda i: (0, 0))],
        out_shape=[jax.ShapeDtypeStruct((t_out, d), F32), jax.ShapeDtypeStruct((t_out, d), BF16),
                   jax.ShapeDtypeStruct((t_out, LANES), jnp.int32), jax.ShapeDtypeStruct((t_out, LANES), F32),
                   jax.ShapeDtypeStruct((t_out, LANES), jnp.int32), jax.ShapeDtypeStruct((1, LANES), F32)],
        scratch_shapes=[pltpu.VMEM((1, LANES), F32)],
        compiler_params=_cparams(("arbitrary",)),
        name="post_mix_router",
    )(*mixes, *weights, x, g1, n2g, sc2, sh2, rw_hi, rw_lo, router_b)


def _moe_ffn_kernel(be_ref, nu_ref, nxt_ref, x_ref, wgu_hbm, bgu_ref, wd_hbm, bd_ref, o_ref,
                    wgu_f32, wd_f32, wgu_bf, wd_bf, sem, slot_ref, *, layer):
    i = pl.program_id(0)
    prev = be_ref[jnp.maximum(i - 1, 0)]
    fresh = jnp.logical_or(i == 0, be_ref[i] != prev)

    def weight_copies(expert, slot):
        return (pltpu.make_async_copy(wgu_hbm.at[layer, expert], wgu_f32.at[slot], sem.at[0, slot]),
                pltpu.make_async_copy(wd_hbm.at[layer, expert], wd_f32.at[slot], sem.at[1, slot]))

    @pl.when(jnp.logical_and(i == 0, nu_ref[0] > 0))
    def _():
        slot_ref[0] = 0
        for cp in weight_copies(be_ref[0], 0):
            cp.start()

    @pl.when(jnp.logical_and(fresh, i < nu_ref[0]))
    def _():
        slot = slot_ref[0]
        for cp in weight_copies(be_ref[i], slot):
            cp.wait()
        wgu_bf[...] = wgu_f32[slot].astype(BF16)
        wd_bf[...] = wd_f32[slot].astype(BF16)
        slot_ref[0] = 1 - slot

        @pl.when(nxt_ref[i] >= 0)
        def _():
            for cp in weight_copies(nxt_ref[i], 1 - slot):
                cp.start()

    @pl.when(i < nu_ref[0])
    def _():
        d_ff = wd_bf.shape[0]
        gu = jnp.dot(x_ref[...], wgu_bf[...], preferred_element_type=F32) + bgu_ref[...]
        g = jnp.minimum(gu[:, :d_ff], SWIGLU_LIMIT)
        u = jnp.clip(gu[:, d_ff:], -SWIGLU_LIMIT, SWIGLU_LIMIT)
        act = (u + 1.0) * g * jax.nn.sigmoid(SWIGLU_ALPHA * g)
        y = jnp.dot(act.astype(BF16), wd_bf[...], preferred_element_type=F32) + bd_ref[...]
        o_ref[...] = y.astype(o_ref.dtype)

    @pl.when(i >= nu_ref[0])
    def _():
        o_ref[...] = jnp.zeros_like(o_ref)


def _moe_ffn(block_e, n_used, next_e, xs, w_gu, b_gu, w_down, b_down, layer):
    n_rows, d = xs.shape
    depth, n_e, _, two_ff = w_gu.shape
    d_ff = two_ff // 2
    in_specs = [pl.BlockSpec((MOE_BM, d), lambda i, be, nu, nx: (i, 0)),
                pl.BlockSpec(memory_space=pl.ANY),
                pl.BlockSpec((None, None, 1, two_ff), lambda i, be, nu, nx: (layer, be[i], 0, 0)),
                pl.BlockSpec(memory_space=pl.ANY),
                pl.BlockSpec((None, None, 1, d), lambda i, be, nu, nx: (layer, be[i], 0, 0))]
    args = [block_e, n_used, next_e, xs, w_gu, b_gu.reshape(depth, n_e, 1, two_ff), w_down,
            b_down.reshape(depth, n_e, 1, d)]
    grid_spec = pltpu.PrefetchScalarGridSpec(
        num_scalar_prefetch=3,
        grid=(n_rows // MOE_BM,),
        in_specs=in_specs,
        out_specs=pl.BlockSpec((MOE_BM, d), lambda i, be, nu, nx: (i, 0)),
        scratch_shapes=[pltpu.VMEM((2, d, two_ff), F32), pltpu.VMEM((2, d_ff, d), F32),
                        pltpu.VMEM((d, two_ff), BF16), pltpu.VMEM((d_ff, d), BF16),
                        pltpu.SemaphoreType.DMA((2, 2)), pltpu.SMEM((1,), jnp.int32)],
    )
    return pl.pallas_call(
        functools.partial(_moe_ffn_kernel, layer=layer),
        grid_spec=grid_spec,
        out_shape=jax.ShapeDtypeStruct((n_rows, d), BF16),
        compiler_params=_cparams(("arbitrary",)),
        name="moe_ffn",
    )(*args)


def _next_experts(block_e, n_used):
    idx = jnp.arange(block_e.shape[0], dtype=jnp.int32)
    later = (idx[None, :] > idx[:, None]) & (idx[None, :] < n_used) & (block_e[None, :] != block_e[:, None])
    nxt = jnp.min(jnp.where(later, block_e[None, :], N_EXPERTS), axis=1)
    return jnp.where(nxt == N_EXPERTS, -1, nxt).astype(jnp.int32)


def _moe_combine_kernel(x_ref, g2_ref, w_ref, y_ref, o_ref):
    w = w_ref[...]
    f = w[:, 0:1] * y_ref[0].astype(F32)
    for k in range(1, TOP_K):
        f = f + w[:, k:k + 1] * y_ref[k].astype(F32)
    o_ref[...] = x_ref[...] + g2_ref[...] * f


def _moe_combine(x, g2, top_w, picked, midx):
    t, d = x.shape
    return pl.pallas_call(
        _moe_combine_kernel,
        grid=(t // TM,),
        in_specs=[pl.BlockSpec((TM, d), lambda i: (i, 0)),
                  pl.BlockSpec((None, 1, d), lambda i: (midx(i), 0, 0)),
                  pl.BlockSpec((TM, LANES), lambda i: (i, 0)),
                  pl.BlockSpec((TOP_K, TM, d), lambda i: (0, i, 0))],
        out_specs=pl.BlockSpec((TM, d), lambda i: (i, 0)),
        out_shape=jax.ShapeDtypeStruct((t, d), F32),
        compiler_params=_cparams(("parallel",)),
        name="moe_combine",
    )(x, g2, top_w, picked)


SMEM_TILE = 1024


def _row_tokens_kernel(lo_ref, hi_ref, dest_ref, rt_ref, *, table_rows):
    i = pl.program_id(0)
    chunk = dest_ref.shape[0]

    @pl.when(i == 0)
    def _():
        def fill_gap(g, _):
            def fill(r, _):
                rt_ref[r] = lax.rem(r, table_rows)
                return 0
            lax.fori_loop(lo_ref[g], hi_ref[g], fill, 0)
            return 0
        lax.fori_loop(0, lo_ref.shape[0], fill_gap, 0)

    base = i * (chunk // TOP_K)

    def body(tok, _):
        for k in range(TOP_K):
            rt_ref[dest_ref[tok * TOP_K + k]] = base + tok
        return 0

    lax.fori_loop(0, chunk // TOP_K, body, 0, unroll=4)


def _row_tokens(gap_lo, gap_hi, dest_flat, n_rows, table_rows):
    n_assign = dest_flat.shape[0]
    chunk = 4 * SMEM_TILE if n_assign % (4 * SMEM_TILE) == 0 else SMEM_TILE
    assert n_assign % chunk == 0
    grid_spec = pltpu.PrefetchScalarGridSpec(
        num_scalar_prefetch=2,
        grid=(n_assign // chunk,),
        in_specs=[pl.BlockSpec((chunk,), lambda i, lo, hi: (i,), memory_space=pltpu.SMEM)],
        out_specs=pl.BlockSpec(memory_space=pltpu.SMEM),
    )
    return pl.pallas_call(
        functools.partial(_row_tokens_kernel, table_rows=table_rows),
        grid_spec=grid_spec,
        out_shape=jax.ShapeDtypeStruct((n_rows,), jnp.int32),
        compiler_params=_cparams(("arbitrary",)),
        name="moe_row_tokens",
    )(gap_lo, gap_hi, dest_flat)


def _moe(x, g2, midx, h2, top_e, top_w, top_r, counts, w_gu, b_gu, w_down, b_down, layer):
    t, d = h2.shape
    n_blocks = -(-t * TOP_K // MOE_BM) + N_EXPERTS
    n_rows = n_blocks * MOE_BM
    cnt = counts[0, :N_EXPERTS].astype(jnp.int32)
    padded = (cnt + MOE_BM - 1) // MOE_BM * MOE_BM
    pends = jnp.cumsum(padded)
    pstarts = pends - padded
    e = top_e[:, :TOP_K]
    expert_ids = jnp.arange(N_EXPERTS, dtype=jnp.int32)
    pstart_of_pick = jnp.sum(jnp.where(e[:, :, None] == expert_ids, pstarts, 0), axis=-1)
    dest = pstart_of_pick + top_r[:, :TOP_K]
    block_start = jnp.arange(n_blocks, dtype=jnp.int32) * MOE_BM
    block_e = jnp.sum((block_start[:, None] >= pends[None, :]).astype(jnp.int32), axis=1)
    block_e = jnp.minimum(block_e, N_EXPERTS - 1)
    n_used = (pends[-1] // MOE_BM).astype(jnp.int32).reshape(1)

    gap_lo = jnp.concatenate([pstarts + cnt, pends[-1:]])
    gap_hi = jnp.concatenate([pends, jnp.full((1,), n_rows, jnp.int32)])
    table_rows = 2 * t
    row_tok = _row_tokens(gap_lo, gap_hi, dest.reshape(-1), n_rows, table_rows)
    h2_big = jnp.concatenate([h2, jnp.zeros((table_rows - t, d), h2.dtype)], axis=0)
    xs = h2_big[row_tok]
    ys = _moe_ffn(block_e, n_used, _next_experts(block_e, n_used[0]), xs, w_gu, b_gu, w_down, b_down, layer)
    picked = ys[dest.T.reshape(-1)].reshape(TOP_K, t, d)
    return _moe_combine(x, g2, top_w, picked, midx)


def _pad_cols(w, width):
    return jnp.concatenate([w, jnp.zeros((w.shape[0], width - w.shape[1]), w.dtype)], axis=1)


def kernel(x, c, ctx, c_ctx, mod_w, mod_b, norm1_g, norm2_g, ev_w_in, ev_w_out, lru_conv_w, lru_conv_b, lru_wa, lru_ba, lru_wx, lru_bx, lru_lambda, mla_q_norm_g, mla_w_uq, mla_kv_norm_g, mla_w_ukv, mla_qn_g, mla_kn_g, od_w_in, od_w_out, diff_qn_g, diff_kn_g, diff_lq1, diff_lk1, diff_lq2, diff_lk2, diff_subln_g, router_w, router_b, moe_w_gu, moe_b_gu, moe_w_down, moe_b_down):
    nb, seq, d = x.shape
    n_ctx = ctx.shape[1]
    depth = mod_w.shape[0]
    l = n_ctx + seq
    tpb = l // TM
    assert n_ctx == TM and seq % TQ == 0 and seq % GRID_W == 0

    xa = jnp.concatenate([x, ctx], axis=1).reshape(nb * l, d)
    cvec = jnp.concatenate([c, c_ctx[None, :], jnp.zeros((SUBLANES - nb - 1, d), F32)], axis=0)
    router_w_p = jnp.concatenate([router_w, jnp.zeros((depth, d, LANES - N_EXPERTS), F32)], axis=-1)
    router_b_p = jnp.concatenate([router_b, jnp.zeros((depth, LANES - N_EXPERTS), F32)], axis=-1)
    for layer in range(depth):
        last = layer == depth - 1
        i = layer // 2
        mod = _adaln(cvec, mod_w[layer], mod_b[layer])[:nb + 1]
        sh1, sc1, g1, sh2, sc2, g2 = (mod[:, k * d:(k + 1) * d].reshape(nb + 1, 1, d) for k in range(6))
        n1g = norm1_g[layer].reshape(1, d)
        n2g = norm2_g[layer].reshape(1, d)

        if layer % 2 == 0:
            w_in = ev_w_in[i].astype(BF16)
            o = 2 * LRU_WIDTH
            splits = [w_in[:, :LRU_WIDTH], w_in[:, LRU_WIDTH:o], w_in[:, o:o + MLA_Q_RANK],
                      w_in[:, o + MLA_Q_RANK:o + MLA_Q_RANK + MLA_KV_RANK],
                      _pad_cols(w_in[:, o + MLA_Q_RANK + MLA_KV_RANK:], LANES)]
            gl, rl, qc, kvc, kr = _even_proj(xa, n1g, sc1, sh1, splits, tpb, nb)

            w_gates, b_gates = _lru_gate_weights(lru_wa[i], lru_ba[i], lru_wx[i], lru_bx[i])
            lru = _lru(gl, rl, lru_conv_w[i], lru_conv_b[i].reshape(1, LRU_WIDTH), w_gates, b_gates,
                       lru_lambda[i], nb, n_ctx, seq)

            w_uq = mla_w_uq[i].astype(BF16).reshape(MLA_Q_RANK, MLA_HEADS, MLA_QK)
            w_q_nope = w_uq[:, :, :MLA_NOPE].reshape(MLA_Q_RANK, MLA_HEADS * MLA_NOPE)
            w_q_rope = jnp.concatenate(
                [w_uq[:, :, MLA_NOPE:], jnp.zeros((MLA_Q_RANK, MLA_HEADS, LANES - MLA_ROPE), BF16)],
                axis=-1).reshape(MLA_Q_RANK, MLA_HEADS * LANES)
            pad_g = lambda g: jnp.concatenate([g, jnp.zeros((LANES - MLA_ROPE,), F32)]).reshape(1, LANES)
            rope = _rope_tables_128(seq, n_ctx, MLA_ROPE, tile_groups=False)
            q, k, v = _mla_prep(qc, kvc, kr, mla_q_norm_g[i].reshape(1, -1), mla_kv_norm_g[i].reshape(1, -1),
                                w_q_nope, w_q_rope, mla_w_ukv[i].astype(BF16),
                                mla_qn_g[i][:MLA_NOPE].reshape(1, LANES), pad_g(mla_qn_g[i][MLA_NOPE:]),
                                mla_kn_g[i][:MLA_NOPE].reshape(1, LANES), pad_g(mla_kn_g[i][MLA_NOPE:]),
                                rope, tpb)
            att = _attend_all(_mla_attn_kernel, "mla_attention", 1, q, k, v, nb, n_ctx, seq, MLA_HEADS,
                              2 * LANES, MLA_V, need_ctx=not last)
            w_out = ev_w_out[i].astype(BF16)
            if last:
                lru = lru.reshape(nb, l, LRU_WIDTH)[:, :seq].reshape(nb * seq, LRU_WIDTH)
            mixes = [lru, att]
            weights = [w_out[:LRU_WIDTH], w_out[LRU_WIDTH:]]
        else:
            lam_init = 0.8 - 0.6 * math.exp(-0.3 * layer)
            w_in = od_w_in[i].astype(BF16)
            n_qk = DIFF_HEADS * 2 * DIFF_HEAD_DIM
            gidx = jnp.arange(LANES) // DIFF_HEAD_DIM
            gmat = jnp.where(gidx[:, None] == gidx[None, :], 1.0 / DIFF_HEAD_DIM, 0.0).astype(BF16)
            rope = _rope_tables_128(seq, n_ctx, DIFF_HEAD_DIM, tile_groups=True)
            tile_g = lambda g: jnp.tile(g, LANES // DIFF_HEAD_DIM).reshape(1, LANES)
            q, k, v = _odd_proj(xa, n1g, sc1, sh1, w_in[:, :n_qk], w_in[:, n_qk:2 * n_qk], w_in[:, 2 * n_qk:],
                                tile_g(diff_qn_g[i]), tile_g(diff_kn_g[i]), gmat, rope, tpb, nb)
            lam = (jnp.exp(jnp.sum(diff_lq1[i] * diff_lk1[i])) - jnp.exp(jnp.sum(diff_lq2[i] * diff_lk2[i]))
                   + lam_init).reshape(1).astype(F32)
            att = _attend_all(
                functools.partial(_diff_attn_kernel, out_scale=1.0 - lam_init), "diff_attention", 2,
                q, k, v, nb, n_ctx, seq, DIFF_HEADS, 2 * DIFF_HEAD_DIM, DIFF_V, need_ctx=not last,
                extra_in=(diff_subln_g[i].reshape(1, DIFF_V), lam),
                extra_specs=(pl.BlockSpec((1, DIFF_V), lambda b, h: (0, 0)),
                             pl.BlockSpec(memory_space=pltpu.SMEM)))
            mixes = [att]
            weights = [od_w_out[i].astype(BF16)]

        xo, h2, te, tw, tr, counts = _post_mix(mixes, weights, xa, g1, n2g, sc2, sh2, router_w_p[layer],
                                               router_b_p[layer].reshape(1, LANES), tpb, nb, last)
        midx = (lambda t: t // (tpb - 1)) if last else _mod_index(tpb, nb)
        xa = _moe(xo, g2, midx, h2, te, tw, tr, counts, moe_w_gu, moe_b_gu, moe_w_down, moe_b_down, layer)
    return xa.reshape(nb, seq, d)
```

```python
import functools
import math

import jax
import jax.numpy as jnp
from jax import lax
from jax.experimental import pallas as pl
from jax.experimental.pallas import tpu as pltpu

F32 = jnp.float32
BF16 = jnp.bfloat16
HIGHEST = lax.Precision.HIGHEST

GRID_W = 64
NORM_EPS = 1e-6
ROPE_BASE = 10000.0
LRU_WIDTH = 512
LRU_BLOCKS = 8
LRU_BLOCK_W = LRU_WIDTH // LRU_BLOCKS
LRU_C = 8.0
CONV_W = 4
MLA_HEADS = 4
MLA_Q_RANK = 384
MLA_KV_RANK = 256
MLA_NOPE = 128
MLA_ROPE = 64
MLA_V = 128
MLA_QK = MLA_NOPE + MLA_ROPE
DIFF_HEADS = 8
DIFF_HEAD_DIM = 64
DIFF_V = 2 * DIFF_HEAD_DIM
N_EXPERTS = 32
TOP_K = 4
SWIGLU_LIMIT = 7.0
SWIGLU_ALPHA = 1.702

LANES = 128
SUBLANES = 8
VMEM_LIMIT = 52 * 1024 * 1024

TM = 256
TQ = 512
LRU_CT = 128
LRU_CHUNK = 128
LRU_SEG = LRU_CHUNK // SUBLANES
LRU_PAD = 8
MOE_BM = 256
NEG_BIG = -1e30
LOG2_E = math.log2(math.e)


def _cparams(sem, vmem=VMEM_LIMIT):
    return pltpu.CompilerParams(dimension_semantics=sem, vmem_limit_bytes=vmem)


def _adaln_kernel(c_ref, w_ref, b_ref, o_ref):
    c = c_ref[...]
    s = c * jax.nn.sigmoid(c)
    o_ref[...] = jnp.dot(s, w_ref[...], preferred_element_type=F32, precision=HIGHEST) + b_ref[...]


def _adaln(cvec, w, b):
    rows, d = cvec.shape
    n = w.shape[1]
    tn = 1536
    return pl.pallas_call(
        _adaln_kernel,
        grid=(n // tn,),
        in_specs=[pl.BlockSpec((rows, d), lambda j: (0, 0)),
                  pl.BlockSpec((d, tn), lambda j: (0, j)),
                  pl.BlockSpec((1, tn), lambda j: (0, j))],
        out_specs=pl.BlockSpec((rows, tn), lambda j: (0, j)),
        out_shape=jax.ShapeDtypeStruct((rows, n), F32),
        compiler_params=_cparams(("arbitrary",)),
        name="adaln_mod",
    )(cvec, w, b.reshape(1, n))


def _norm_mod(x, g, sc, sh):
    var = jnp.mean(x * x, axis=-1, keepdims=True)
    y = x * lax.rsqrt(var + NORM_EPS) * g
    return y * (1.0 + sc) + sh


def _ones_column(shape):
    lane = lax.broadcasted_iota(jnp.int32, shape, 1)
    return jnp.where(lane == 0, 1.0, 0.0).astype(BF16)


def _rope_tile(x, cos, sin_a, sin_b):
    up = pltpu.roll(x, LANES - 16, axis=1)
    dn = pltpu.roll(x, 16, axis=1)
    return x * cos + up * sin_a + dn * sin_b


def _rope_tables(seq, rot_dim):
    n_rows = seq // GRID_W
    rows = jnp.repeat(jnp.arange(n_rows, dtype=F32), GRID_W)
    cols = jnp.tile(jnp.arange(GRID_W, dtype=F32), n_rows)
    axis_dim = rot_dim // 2
    inv_freq = ROPE_BASE ** (-jnp.arange(0, axis_dim, 2, dtype=F32) / axis_dim)
    ang_r = rows[:, None] * inv_freq
    ang_c = cols[:, None] * inv_freq
    ang = jnp.concatenate([ang_r, ang_r, ang_c, ang_c], axis=-1)
    cos, sin = jnp.cos(ang), jnp.sin(ang)
    quarter = rot_dim // 4
    first = (jnp.arange(rot_dim) % (2 * quarter)) < quarter
    sin_a = jnp.where(first, -sin, 0.0)
    sin_b = jnp.where(first, 0.0, sin)
    return cos, sin_a, sin_b


def _rope_tables_128(seq, ctx, rot_dim, tile_groups):
    cos, sin_a, sin_b = _rope_tables(seq, rot_dim)
    if tile_groups:
        reps = LANES // rot_dim
        cos, sin_a, sin_b = (jnp.tile(t, (1, reps)) for t in (cos, sin_a, sin_b))
    else:
        pad = LANES - rot_dim
        cos = jnp.concatenate([cos, jnp.ones((seq, pad), F32)], axis=-1)
        sin_a = jnp.concatenate([sin_a, jnp.zeros((seq, pad), F32)], axis=-1)
        sin_b = jnp.concatenate([sin_b, jnp.zeros((seq, pad), F32)], axis=-1)
    cos = jnp.concatenate([cos, jnp.ones((ctx, LANES), F32)], axis=0)
    sin_a = jnp.concatenate([sin_a, jnp.zeros((ctx, LANES), F32)], axis=0)
    sin_b = jnp.concatenate([sin_b, jnp.zeros((ctx, LANES), F32)], axis=0)
    return cos, sin_a, sin_b


def _mod_index(tpb, nb):
    def idx(i):
        return jnp.where(i % tpb == tpb - 1, nb, i // tpb)
    return idx


def _even_proj_kernel(x_ref, g_ref, sc_ref, sh_ref, wg_ref, wr_ref, wq_ref, wkv_ref, wkr_ref,
                      og_ref, or_ref, oq_ref, okv_ref, okr_ref):
    h = _norm_mod(x_ref[...], g_ref[...], sc_ref[...], sh_ref[...]).astype(BF16)
    for w_ref, o_ref in ((wg_ref, og_ref), (wr_ref, or_ref), (wq_ref, oq_ref),
                         (wkv_ref, okv_ref), (wkr_ref, okr_ref)):
        o_ref[...] = jnp.dot(h, w_ref[...], preferred_element_type=F32)


def _even_proj(x, gain, sc, sh, weights, tpb, nb):
    t, d = x.shape
    midx = _mod_index(tpb, nb)
    w_specs = [pl.BlockSpec(w.shape, lambda i: (0, 0)) for w in weights]
    return pl.pallas_call(
        _even_proj_kernel,
        grid=(t // TM,),
        in_specs=[pl.BlockSpec((TM, d), lambda i: (i, 0)),
                  pl.BlockSpec((1, d), lambda i: (0, 0)),
                  pl.BlockSpec((None, 1, d), lambda i: (midx(i), 0, 0)),
                  pl.BlockSpec((None, 1, d), lambda i: (midx(i), 0, 0))] + w_specs,
        out_specs=[pl.BlockSpec((TM, w.shape[1]), lambda i: (i, 0)) for w in weights],
        out_shape=[jax.ShapeDtypeStruct((t, w.shape[1]), F32) for w in weights],
        compiler_params=_cparams(("parallel",)),
        name="even_in_proj",
    )(x, gain, sc, sh, *weights)


def _mla_prep_kernel(qc_ref, kvc_ref, kr_ref, qng_ref, kvng_ref, wqn_ref, wqr_ref, wkv_ref,
                     qgn_ref, qgr_ref, kgn_ref, kgr_ref, cos_ref, sa_ref, sb_ref,
                     q_ref, k_ref, v_ref):
    cos, sa, sb = cos_ref[...], sa_ref[...], sb_ref[...]
    inv_qk = 1.0 / MLA_QK

    qc = qc_ref[...]
    hq = (qc * lax.rsqrt(jnp.mean(qc * qc, axis=-1, keepdims=True) + NORM_EPS) * qng_ref[...]).astype(BF16)
    q_nope = jnp.dot(hq, wqn_ref[...], preferred_element_type=F32)
    q_rope = jnp.dot(hq, wqr_ref[...], preferred_element_type=F32)
    q_scale = MLA_QK ** -0.5 * LOG2_E
    for h in range(MLA_HEADS):
        qn = q_nope[:, h * LANES:(h + 1) * LANES]
        qr = q_rope[:, h * LANES:(h + 1) * LANES]
        ms = (jnp.sum(qn * qn, axis=-1, keepdims=True) + jnp.sum(qr * qr, axis=-1, keepdims=True)) * inv_qk
        rs = lax.rsqrt(ms + NORM_EPS) * q_scale
        q_ref[:, (2 * h) * LANES:(2 * h + 1) * LANES] = (qn * rs * qgn_ref[...]).astype(BF16)
        q_ref[:, (2 * h + 1) * LANES:(2 * h + 2) * LANES] = _rope_tile(qr * rs * qgr_ref[...], cos, sa, sb).astype(BF16)

    kvc = kvc_ref[...]
    hkv = (kvc * lax.rsqrt(jnp.mean(kvc * kvc, axis=-1, keepdims=True) + NORM_EPS) * kvng_ref[...]).astype(BF16)
    kv = jnp.dot(hkv, wkv_ref[...], preferred_element_type=F32)
    kr = kr_ref[...]
    kr_ss = jnp.sum(kr * kr, axis=-1, keepdims=True)
    kr_rot = _rope_tile(kr * kgr_ref[...], cos, sa, sb)
    for h in range(MLA_HEADS):
        kn = kv[:, (2 * h) * LANES:(2 * h + 1) * LANES]
        ms = (jnp.sum(kn * kn, axis=-1, keepdims=True) + kr_ss) * inv_qk
        rs = lax.rsqrt(ms + NORM_EPS)
        k_ref[h, 0:LANES, :] = jnp.transpose(kn * rs * kgn_ref[...]).astype(BF16)
        k_ref[h, LANES:2 * LANES, :] = jnp.transpose(kr_rot * rs).astype(BF16)
        v_ref[:, (2 * h) * LANES:(2 * h + 1) * LANES] = kv[:, (2 * h + 1) * LANES:(2 * h + 2) * LANES].astype(BF16)
        v_ref[:, (2 * h + 1) * LANES:(2 * h + 2) * LANES] = _ones_column((kv.shape[0], LANES))


def _mla_prep(qc, kvc, kr, q_norm_g, kv_norm_g, w_q_nope, w_q_rope, w_ukv, q_gn, q_gr, k_gn, k_gr,
              rope, tpb):
    t = qc.shape[0]
    cos, sa, sb = rope

    def const(a):
        return pl.BlockSpec(a.shape, lambda i: (0, 0))

    def rows(a):
        return pl.BlockSpec((TM, a.shape[1]), lambda i: (i, 0))

    tab = pl.BlockSpec((TM, LANES), lambda i: (i % tpb, 0))
    hq, hv = MLA_HEADS * 2 * LANES, MLA_HEADS * 2 * MLA_V
    return pl.pallas_call(
        _mla_prep_kernel,
        grid=(t // TM,),
        in_specs=[rows(qc), rows(kvc), rows(kr), const(q_norm_g), const(kv_norm_g),
                  const(w_q_nope), const(w_q_rope), const(w_ukv),
                  const(q_gn), const(q_gr), const(k_gn), const(k_gr), tab, tab, tab],
        out_specs=[pl.BlockSpec((TM, hq), lambda i: (i, 0)),
                   pl.BlockSpec((None, MLA_HEADS, 2 * LANES, TM), lambda i: (i // tpb, 0, 0, i % tpb)),
                   pl.BlockSpec((TM, hv), lambda i: (i, 0))],
        out_shape=[jax.ShapeDtypeStruct((t, hq), BF16),
                   jax.ShapeDtypeStruct((t // (tpb * TM), MLA_HEADS, 2 * LANES, tpb * TM), BF16),
                   jax.ShapeDtypeStruct((t, hv), BF16)],
        compiler_params=_cparams(("parallel",)),
        name="mla_prep",
    )(qc, kvc, kr, q_norm_g, kv_norm_g, w_q_nope, w_q_rope, w_ukv, q_gn, q_gr, k_gn, k_gr, cos, sa, sb)


def _sublane_iota():
    return lax.broadcasted_iota(jnp.int32, (SUBLANES, LANES), 0)


def _scan_chunk(a_chunk, u_chunk, h_chunk, carry, reverse):
    steps = range(LRU_SEG - 1, -1, -1) if reverse else range(LRU_SEG)
    h_loc, p_loc = [None] * LRU_SEG, [None] * LRU_SEG
    h = p = None
    for j in steps:
        a = a_chunk[pl.ds(j, SUBLANES, stride=LRU_SEG), :]
        u = u_chunk[pl.ds(j, SUBLANES, stride=LRU_SEG), :]
        if h is None:
            h, p = u, a
        else:
            h, p = a * h + u, a * p
        h_loc[j], p_loc[j] = h, p
    sub = _sublane_iota()
    seg_p, seg_h = p, h
    for d in (1, 2, 4):
        shift = SUBLANES - d if reverse else d
        prev_p = pltpu.roll(seg_p, shift, axis=0)
        prev_h = pltpu.roll(seg_h, shift, axis=0)
        valid = (sub < SUBLANES - d) if reverse else (sub >= d)
        seg_h = jnp.where(valid, seg_p * prev_h + seg_h, seg_h)
        seg_p = jnp.where(valid, seg_p * prev_p, seg_p)
    h_end = seg_h + seg_p * carry
    if reverse:
        h_in = jnp.where(sub == SUBLANES - 1, carry, pltpu.roll(h_end, SUBLANES - 1, axis=0))
        new_carry = h_end[0:1, :]
    else:
        h_in = jnp.where(sub == 0, carry, pltpu.roll(h_end, 1, axis=0))
        new_carry = h_end[SUBLANES - 1:SUBLANES, :]
    for j in range(LRU_SEG):
        h_chunk[pl.ds(j, SUBLANES, stride=LRU_SEG), :] = h_loc[j] + p_loc[j] * h_in
    return jnp.broadcast_to(new_carry, (SUBLANES, LANES))


def _lru_kernel(g_ref, r_ref, cw_ref, cb_ref, wg_ref, bg_ref, lam_ref, o_ref,
                rp_ref, af_ref, uf_ref, ab_ref, ub_ref, hf_ref, hb_ref, *, ctx, seq):
    n_ctx, n_lat = ctx // LRU_CHUNK, seq // LRU_CHUNK
    zeros_pad = jnp.zeros((LRU_PAD, LRU_CT), F32)
    ctx0 = seq + 2 * LRU_PAD
    rp_ref[0:LRU_PAD, :] = zeros_pad
    rp_ref[LRU_PAD:LRU_PAD + seq, :] = r_ref[0:seq, :]
    rp_ref[LRU_PAD + seq:ctx0, :] = zeros_pad
    rp_ref[ctx0:ctx0 + ctx, :] = r_ref[seq:seq + ctx, :]
    rp_ref[ctx0 + ctx:ctx0 + ctx + LRU_PAD, :] = zeros_pad

    cw = cw_ref[...]
    cb = cb_ref[...]
    wg = wg_ref[...]
    bg = bg_ref[...]
    lam = lam_ref[...]
    sp = jnp.maximum(-lam, 0.0) + jnp.log1p(jnp.exp(-jnp.abs(lam)))

    def coeff_chunk(c, pad_off, row_off):
        start = pl.multiple_of(pad_off + c * LRU_CHUNK, SUBLANES)
        ext = rp_ref[pl.ds(start, LRU_CHUNK + 2 * LRU_PAD), :]
        x = cb
        for tap in range(CONV_W):
            lo = LRU_PAD - 2 + tap
            x = x + ext[lo:lo + LRU_CHUNK, :] * cw[tap:tap + 1, :]
        gates = jnp.dot(x.astype(BF16), wg, preferred_element_type=F32) + bg
        out_row = pl.multiple_of(row_off + c * LRU_CHUNK, SUBLANES)
        for d, (a_ref, u_ref) in enumerate(((af_ref, uf_ref), (ab_ref, ub_ref))):
            r = jax.nn.sigmoid(gates[:, (2 * d) * LRU_CT:(2 * d + 1) * LRU_CT])
            i = jax.nn.sigmoid(gates[:, (2 * d + 1) * LRU_CT:(2 * d + 2) * LRU_CT])
            log_a = -LRU_C * r * sp[d:d + 1, :]
            a_ref[pl.ds(out_row, LRU_CHUNK), :] = jnp.exp(log_a)
            th = jnp.tanh(log_a)
            u_ref[pl.ds(out_row, LRU_CHUNK), :] = jnp.sqrt(-2.0 * th / (1.0 - th)) * i * x
        return None

    def coeff_ctx(c, _):
        coeff_chunk(c, seq + LRU_PAD, seq)
        return 0

    def coeff_lat(c, _):
        coeff_chunk(c, 0, 0)
        return 0

    lax.fori_loop(0, n_ctx, coeff_ctx, 0, unroll=2)
    lax.fori_loop(0, n_lat, coeff_lat, 0, unroll=2)

    def scan_pair(n, row_off):
        def body(c, carry):
            cf, cb_ = carry
            f_row = pl.multiple_of(row_off + c * LRU_CHUNK, SUBLANES)
            b_row = pl.multiple_of(row_off + (n - 1 - c) * LRU_CHUNK, SUBLANES)
            cf = _scan_chunk(af_ref.at[pl.ds(f_row, LRU_CHUNK), :], uf_ref.at[pl.ds(f_row, LRU_CHUNK), :],
                             hf_ref.at[pl.ds(f_row, LRU_CHUNK), :], cf, False)
            cb_ = _scan_chunk(ab_ref.at[pl.ds(b_row, LRU_CHUNK), :], ub_ref.at[pl.ds(b_row, LRU_CHUNK), :],
                              hb_ref.at[pl.ds(b_row, LRU_CHUNK), :], cb_, True)
            return cf, cb_
        return body

    zero = jnp.zeros((SUBLANES, LANES), F32)
    carry = lax.fori_loop(0, n_ctx, scan_pair(n_ctx, seq), (zero, zero), unroll=2)
    lax.fori_loop(0, n_lat, scan_pair(n_lat, 0), carry, unroll=2)

    g = g_ref[...]
    gelu = 0.5 * g * (1.0 + jnp.tanh(math.sqrt(2.0 / math.pi) * (g + 0.044715 * (g * g * g))))
    o_ref[...] = (gelu * (hf_ref[...] + hb_ref[...])).astype(BF16)


def _lru(g, r, conv_w, conv_b, w_gates, b_gates, lam, nb, ctx, seq):
    l = ctx + seq
    width = g.shape[1]
    n_ct = width // LRU_CT
    g3 = g.reshape(nb, l, width)
    r3 = r.reshape(nb, l, width)
    seq_spec = pl.BlockSpec((None, l, LRU_CT), lambda b, c: (b, 0, c))
    scratch = [pltpu.VMEM((l + 3 * LRU_PAD, LRU_CT), F32)] + [pltpu.VMEM((l, LRU_CT), F32)] * 6
    out = pl.pallas_call(
        functools.partial(_lru_kernel, ctx=ctx, seq=seq),
        grid=(nb, n_ct),
        in_specs=[seq_spec, seq_spec,
                  pl.BlockSpec((CONV_W, LRU_CT), lambda b, c: (0, c)),
                  pl.BlockSpec((1, LRU_CT), lambda b, c: (0, c)),
                  pl.BlockSpec((None, LRU_CT, 4 * LRU_CT), lambda b, c: (c, 0, 0)),
                  pl.BlockSpec((None, 1, 4 * LRU_CT), lambda b, c: (c, 0, 0)),
                  pl.BlockSpec((2, LRU_CT), lambda b, c: (0, c))],
        out_specs=seq_spec,
        out_shape=jax.ShapeDtypeStruct((nb, l, width), BF16),
        scratch_shapes=scratch,
        compiler_params=_cparams(("parallel", "parallel")),
        name="rglru",
    )(g3, r3, conv_w, conv_b, w_gates, b_gates, lam)
    return out.reshape(nb * l, width)


def _lru_gate_weights(wa, ba, wx, bx):
    per = LRU_CT // LRU_BLOCK_W
    n_ct = LRU_BLOCKS // per
    eye = jnp.eye(per, dtype=F32)

    def dense(w):
        w4 = w.reshape(n_ct, per, LRU_BLOCK_W, LRU_BLOCK_W)
        return jnp.einsum('cide,ij->cidje', w4, eye).reshape(n_ct, LRU_CT, LRU_CT)

    w = jnp.concatenate([dense(wa[0]), dense(wx[0]), dense(wa[1]), dense(wx[1])], axis=-1)
    b = jnp.concatenate([v.reshape(n_ct, 1, LRU_CT) for v in (ba[0], bx[0], ba[1], bx[1])], axis=-1)
    return w.astype(BF16), b


def _key_spans(n_keys):
    half = n_keys // 2 if n_keys % (2 * LANES) == 0 else n_keys
    return [(lo, lo + half) for lo in range(0, n_keys, half)]


def _attn_scores(q_ops, kt_ref, s_ref, m_ref):
    for t, q in enumerate(q_ops):
        for lo, hi in _key_spans(kt_ref.shape[-1]):
            s_ref[t, :, lo:hi] = jnp.dot(q, kt_ref[:, lo:hi], preferred_element_type=F32)
        m_ref[t] = jnp.max(s_ref[t], axis=-1, keepdims=True)


def _attn_values(v_ref, s_ref, m_ref, p_ref):
    accs = []
    for t in range(s_ref.shape[0]):
        m = m_ref[t]
        acc = None
        for lo, hi in _key_spans(s_ref.shape[-1]):
            p_ref[t, :, lo:hi] = jnp.exp2(s_ref[t, :, lo:hi] - m).astype(BF16)
            part = jnp.dot(p_ref[t, :, lo:hi], v_ref[lo:hi, :], preferred_element_type=F32)
            acc = part if acc is None else acc + part
        accs.append(acc)
    return accs


def _attn_pipeline(q_ref, kt_ref, v_ref, o_ref, bufs, p_ref, q_ops_fn, values_fn):
    s0, m0 = bufs[0]
    tq = s0.shape[1]
    n = q_ref.shape[0] // tq

    def scores(t, s_ref, m_ref):
        row = t * tq if isinstance(t, int) else pl.multiple_of(t * tq, tq)
        _attn_scores(q_ops_fn(q_ref[pl.ds(row, tq), :]), kt_ref, s_ref, m_ref)

    def values(t, s_ref, m_ref):
        row = t * tq if isinstance(t, int) else pl.multiple_of(t * tq, tq)
        o_ref[pl.ds(row, tq), :] = values_fn(v_ref, s_ref, m_ref, p_ref).astype(o_ref.dtype)

    if len(bufs) == 1:
        def one(t, _):
            scores(t, s0, m0)
            values(t, s0, m0)
            return 0

        if n == 1:
            one(0, 0)
        else:
            lax.fori_loop(0, n, one, 0)
        return

    s1, m1 = bufs[1]
    scores(0, s0, m0)
    if n == 1:
        values(0, s0, m0)
        return
    assert n % 2 == 0

    def pair(k, _):
        t = 2 * k
        scores(t + 1, s1, m1)
        values(t, s0, m0)
        scores(t + 2, s0, m0)
        values(t + 1, s1, m1)
        return 0

    lax.fori_loop(0, n // 2 - 1, pair, 0)
    scores(n - 1, s1, m1)
    values(n - 2, s0, m0)
    values(n - 1, s1, m1)


def _score_bufs(scratch):
    return tuple(zip(scratch[0:-1:2], scratch[1:-1:2])), scratch[-1]


def _mla_attn_kernel(q_ref, kt_ref, v_ref, o_ref, *scratch):
    dv = o_ref.shape[-1]
    bufs, p_ref = _score_bufs(scratch)

    def values(*refs):
        acc = _attn_values(*refs)[0]
        return acc[:, :dv] / acc[:, dv:dv + 1]

    _attn_pipeline(q_ref, kt_ref, v_ref, o_ref, bufs, p_ref, lambda q: [q], values)


def _diff_attn_kernel(q_ref, kt_ref, v_ref, g_ref, lam_ref, o_ref, *scratch, out_scale):
    dv = o_ref.shape[-1]
    bufs, p_ref = _score_bufs(scratch)

    def q_ops(q):
        lane = lax.broadcasted_iota(jnp.int32, q.shape, 1)
        zero = jnp.zeros_like(q)
        return [jnp.where(lane < DIFF_HEAD_DIM, q, zero), jnp.where(lane < DIFF_HEAD_DIM, zero, q)]

    def values(*refs):
        a1, a2 = _attn_values(*refs)
        o = a1[:, :dv] / a1[:, dv:dv + 1] - lam_ref[0] * (a2[:, :dv] / a2[:, dv:dv + 1])
        y = o * lax.rsqrt(jnp.mean(o * o, axis=-1, keepdims=True) + NORM_EPS) * g_ref[...]
        return y * out_scale

    _attn_pipeline(q_ref, kt_ref, v_ref, o_ref, bufs, p_ref, q_ops, values)


def _attention(kernel_fn, name, n_softmax, q, kt, v, nb, ctx, seq, heads, dk, dv, ctx_queries,
               extra_in=(), extra_specs=()):
    l = ctx + seq
    q3 = q.reshape(nb, l, heads * dk)
    v3 = v.reshape(nb, l, heads * 2 * dv)
    if ctx_queries:
        tq, n_keys, rows = ctx, ctx, ctx
        blk = seq // ctx
        in_specs = [pl.BlockSpec((None, rows, dk), lambda b, h: (b, blk, h)),
                    pl.BlockSpec((None, None, dk, ctx), lambda b, h: (b, h, 0, blk)),
                    pl.BlockSpec((None, ctx, 2 * dv), lambda b, h: (b, blk, h))]
    else:
        tq, n_keys, rows = TQ, l, seq
        in_specs = [pl.BlockSpec((None, rows, dk), lambda b, h: (b, 0, h)),
                    pl.BlockSpec((None, None, dk, l), lambda b, h: (b, h, 0, 0)),
                    pl.BlockSpec((None, l, 2 * dv), lambda b, h: (b, 0, h))]
    score_buf = [pltpu.VMEM((n_softmax, tq, n_keys), F32), pltpu.VMEM((n_softmax, tq, 1), F32)]
    out = pl.pallas_call(
        kernel_fn,
        grid=(nb, heads),
        in_specs=in_specs + list(extra_specs),
        out_specs=pl.BlockSpec((None, rows, dv), lambda b, h: (b, 0, h)),
        out_shape=jax.ShapeDtypeStruct((nb, rows, heads * dv), BF16),
        scratch_shapes=score_buf * (2 if n_softmax == 1 else 1) + [pltpu.VMEM((n_softmax, tq, n_keys), BF16)],
        compiler_params=_cparams(("parallel", "parallel")),
        name=name,
    )(q3, kt, v3, *extra_in)
    return out


def _key_transpose(k, nb, l, heads, dk):
    return jnp.transpose(k.reshape(nb, l, heads, dk), (0, 2, 3, 1))


def _attend_all(kernel_fn, name, n_softmax, q, k, v, nb, ctx, seq, heads, dk, dv, need_ctx, **extra):
    kt = k if k.ndim == 4 else _key_transpose(k, nb, ctx + seq, heads, dk)
    lat = _attention(kernel_fn, name, n_softmax, q, kt, v, nb, ctx, seq, heads, dk, dv, False, **extra)
    if not need_ctx:
        return lat.reshape(nb * seq, heads * dv)
    cx = _attention(kernel_fn, name + "_ctx", n_softmax, q, kt, v, nb, ctx, seq, heads, dk, dv, True, **extra)
    return jnp.concatenate([lat, cx], axis=1).reshape(nb * (ctx + seq), heads * dv)


def _group_ms(x, gmat):
    return jnp.dot((x * x).astype(BF16), gmat, preferred_element_type=F32)


def _odd_proj_kernel(x_ref, g_ref, sc_ref, sh_ref, wq_ref, wk_ref, wv_ref, qg_ref, kg_ref, gm_ref,
                     cos_ref, sa_ref, sb_ref, q_ref, k_ref, v_ref):
    h = _norm_mod(x_ref[...], g_ref[...], sc_ref[...], sh_ref[...]).astype(BF16)
    cos, sa, sb = cos_ref[...], sa_ref[...], sb_ref[...]
    gmat = gm_ref[...]
    q_scale = DIFF_HEAD_DIM ** -0.5 * LOG2_E
    for w_ref, gain_ref, o_ref, scale in ((wq_ref, qg_ref, q_ref, q_scale), (wk_ref, kg_ref, k_ref, 1.0)):
        raw = jnp.dot(h, w_ref[...], preferred_element_type=F32)
        gain = gain_ref[...] * scale
        for c in range(raw.shape[1] // LANES):
            t = raw[:, c * LANES:(c + 1) * LANES]
            t = t * lax.rsqrt(_group_ms(t, gmat) + NORM_EPS) * gain
            o_ref[:, c * LANES:(c + 1) * LANES] = _rope_tile(t, cos, sa, sb).astype(BF16)
    v = jnp.dot(h, wv_ref[...], preferred_element_type=F32).astype(BF16)
    for c in range(v.shape[1] // DIFF_V):
        v_ref[:, (2 * c) * DIFF_V:(2 * c + 1) * DIFF_V] = v[:, c * DIFF_V:(c + 1) * DIFF_V]
        v_ref[:, (2 * c + 1) * DIFF_V:(2 * c + 2) * DIFF_V] = _ones_column((v.shape[0], DIFF_V))


def _odd_proj(x, gain, sc, sh, wq, wk, wv, q_gain, k_gain, gmat, rope, tpb, nb):
    t, d = x.shape
    midx = _mod_index(tpb, nb)
    cos, sa, sb = rope

    def const(a):
        return pl.BlockSpec(a.shape, lambda i: (0, 0))

    tab = pl.BlockSpec((TM, LANES), lambda i: (i % tpb, 0))
    n = wq.shape[1]
    return pl.pallas_call(
        _odd_proj_kernel,
        grid=(t // TM,),
        in_specs=[pl.BlockSpec((TM, d), lambda i: (i, 0)), const(gain),
                  pl.BlockSpec((None, 1, d), lambda i: (midx(i), 0, 0)),
                  pl.BlockSpec((None, 1, d), lambda i: (midx(i), 0, 0)),
                  const(wq), const(wk), const(wv), const(q_gain), const(k_gain), const(gmat),
                  tab, tab, tab],
        out_specs=[pl.BlockSpec((TM, n), lambda i: (i, 0))] * 2 + [pl.BlockSpec((TM, 2 * n), lambda i: (i, 0))],
        out_shape=[jax.ShapeDtypeStruct((t, n), BF16)] * 2 + [jax.ShapeDtypeStruct((t, 2 * n), BF16)],
        compiler_params=_cparams(("parallel",)),
        name="odd_in_proj",
    )(x, gain, sc, sh, wq, wk, wv, q_gain, k_gain, gmat, cos, sa, sb)


def _post_mix_kernel(*refs, n_mix):
    mix_refs = refs[:n_mix]
    w_refs = refs[n_mix:2 * n_mix]
    (x_ref, g1_ref, n2_ref, sc_ref, sh_ref, rwh_ref, rwl_ref, rb_ref,
     xo_ref, h2_ref, te_ref, tw_ref, tr_ref, cnt_ref, base_ref) = refs[2 * n_mix:]
    i = pl.program_id(0)

    @pl.when(i == 0)
    def _():
        base_ref[...] = jnp.zeros_like(base_ref)

    d = x_ref.shape[-1]
    halves = []
    for lo in (0, d // 2):
        m = jnp.dot(mix_refs[0][...], w_refs[0][:, lo:lo + d // 2], preferred_element_type=F32)
        for a_ref, w_ref in zip(mix_refs[1:], w_refs[1:]):
            m = m + jnp.dot(a_ref[...], w_ref[:, lo:lo + d // 2], preferred_element_type=F32)
        halves.append(m)
    x = x_ref[...] + g1_ref[...] * jnp.concatenate(halves, axis=-1)
    xo_ref[...] = x
    h2 = _norm_mod(x, n2_ref[...], sc_ref[...], sh_ref[...])
    h2_hi = h2.astype(BF16)
    h2_ref[...] = h2_hi

    h2_lo = (h2 - h2_hi.astype(F32)).astype(BF16)
    logits = (jnp.dot(h2_hi, rwh_ref[...], preferred_element_type=F32)
              + jnp.dot(h2_lo, rwh_ref[...], preferred_element_type=F32)
              + jnp.dot(h2_hi, rwl_ref[...], preferred_element_type=F32)) + rb_ref[...]
    lane = lax.broadcasted_iota(jnp.int32, logits.shape, 1)
    lane_f = lane.astype(F32)
    vals = jnp.where(lane < N_EXPERTS, logits, NEG_BIG)
    tops, firsts, hots = [], [], []
    for _ in range(TOP_K):
        top = jnp.max(vals, axis=-1, keepdims=True)
        first = jnp.min(jnp.where(vals == top, lane_f, float(LANES)), axis=-1, keepdims=True)
        hot = lane_f == first
        tops.append(top)
        firsts.append(first)
        hots.append(hot)
        vals = jnp.where(hot, 2.0 * NEG_BIG, vals)
    exps = [jnp.exp(t - tops[0]) for t in tops]
    denom = exps[0] + exps[1] + exps[2] + exps[3]

    picked = (hots[0] | hots[1]) | (hots[2] | hots[3])
    cnt = jnp.where(picked, 1.0, 0.0)
    r_io = lax.broadcasted_iota(jnp.int32, (TM, TM), 0)
    c_io = lax.broadcasted_iota(jnp.int32, (TM, TM), 1)
    tri = jnp.where(c_io < r_io, 1.0, 0.0).astype(BF16)
    before = jnp.dot(tri, cnt.astype(BF16), preferred_element_type=F32) + base_ref[...]
    te = jnp.zeros(logits.shape, F32)
    tw = jnp.zeros(logits.shape, F32)
    tr = jnp.zeros(logits.shape, F32)
    for k in range(TOP_K):
        r_k = jnp.sum(jnp.where(hots[k], before, 0.0), axis=-1, keepdims=True)
        te = jnp.where(lane == k, firsts[k], te)
        tw = jnp.where(lane == k, exps[k] / denom, tw)
        tr = jnp.where(lane == k, r_k, tr)
    te_ref[...] = te.astype(jnp.int32)
    tw_ref[...] = tw
    tr_ref[...] = tr.astype(jnp.int32)
    base_ref[...] += jnp.sum(cnt, axis=0, keepdims=True)
    cnt_ref[...] = base_ref[...]


def _post_mix(mixes, weights, x, g1, n2g, sc2, sh2, router_w, router_b, tpb, nb, latent_only):
    d = x.shape[1]
    tiles_in = tpb
    if latent_only:
        tiles_out = tpb - 1
        n_tiles = nb * tiles_out

        def in_row(i):
            return (i // tiles_out) * tiles_in + i % tiles_out

        def midx(i):
            return i // tiles_out
    else:
        n_tiles = nb * tpb

        def in_row(i):
            return i

        midx = _mod_index(tpb, nb)
    t_out = n_tiles * TM

    def const(a):
        return pl.BlockSpec(a.shape, lambda i: (0, 0))

    def mod(a):
        return pl.BlockSpec((None, 1, d), lambda i: (midx(i), 0, 0))

    in_specs = [pl.BlockSpec((TM, a.shape[1]), lambda i: (i, 0)) for a in mixes]
    in_specs += [const(w) for w in weights]
    in_specs += [pl.BlockSpec((TM, d), lambda i: (in_row(i), 0)), mod(g1), const(n2g), mod(sc2), mod(sh2),
                 const(router_w), const(router_w), const(router_b)]
    rw_hi = router_w.astype(BF16)
    rw_lo = (router_w - rw_hi.astype(F32)).astype(BF16)
    row_out = lambda width: pl.BlockSpec((TM, width), lambda i: (i, 0))
    return pl.pallas_call(
        functools.partial(_post_mix_kernel, n_mix=len(mixes)),
        grid=(n_tiles,),
        in_specs=in_specs,
        out_specs=[row_out(d), row_out(d), row_out(LANES), row_out(LANES), row_out(LANES),
                   pl.BlockSpec((1, LANES), lambda i: (0, 0))],
        out_shape=[jax.ShapeDtypeStruct((t_out, d), F32), jax.ShapeDtypeStruct((t_out, d), BF16),
                   jax.ShapeDtypeStruct((t_out, LANES), jnp.int32), jax.ShapeDtypeStruct((t_out, LANES), F32),
                   jax.ShapeDtypeStruct((t_out, LANES), jnp.int32), jax.ShapeDtypeStruct((1, LANES), F32)],
        scratch_shapes=[pltpu.VMEM((1, LANES), F32)],
        compiler_params=_cparams(("arbitrary",)),
        name="post_mix_router",
    )(*mixes, *weights, x, g1, n2g, sc2, sh2, rw_hi, rw_lo, router_b)


def _moe_ffn_kernel(be_ref, nu_ref, nxt_ref, x_ref, wgu_hbm, bgu_ref, wd_hbm, bd_ref, o_ref,
                    wgu_f32, wd_f32, wgu_bf, wd_bf, sem, slot_ref, *, layer):
    i = pl.program_id(0)
    prev = be_ref[jnp.maximum(i - 1, 0)]
    fresh = jnp.logical_or(i == 0, be_ref[i] != prev)

    def weight_copies(expert, slot):
        return (pltpu.make_async_copy(wgu_hbm.at[layer, expert], wgu_f32.at[slot], sem.at[0, slot]),
                pltpu.make_async_copy(wd_hbm.at[layer, expert], wd_f32.at[slot], sem.at[1, slot]))

    @pl.when(jnp.logical_and(i == 0, nu_ref[0] > 0))
    def _():
        slot_ref[0] = 0
        for cp in weight_copies(be_ref[0], 0):
            cp.start()

    @pl.when(jnp.logical_and(fresh, i < nu_ref[0]))
    def _():
        slot = slot_ref[0]
        for cp in weight_copies(be_ref[i], slot):
            cp.wait()
        wgu_bf[...] = wgu_f32[slot].astype(BF16)
        wd_bf[...] = wd_f32[slot].astype(BF16)
        slot_ref[0] = 1 - slot

        @pl.when(nxt_ref[i] >= 0)
        def _():
            for cp in weight_copies(nxt_ref[i], 1 - slot):
                cp.start()

    @pl.when(i < nu_ref[0])
    def _():
        d_ff = wd_bf.shape[0]
        gu = jnp.dot(x_ref[...], wgu_bf[...], preferred_element_type=F32) + bgu_ref[...]
        g = jnp.minimum(gu[:, :d_ff], SWIGLU_LIMIT)
        u = jnp.clip(gu[:, d_ff:], -SWIGLU_LIMIT, SWIGLU_LIMIT)
        act = (u + 1.0) * g * jax.nn.sigmoid(SWIGLU_ALPHA * g)
        y = jnp.dot(act.astype(BF16), wd_bf[...], preferred_element_type=F32) + bd_ref[...]
        o_ref[...] = y.astype(o_ref.dtype)

    @pl.when(i >= nu_ref[0])
    def _():
        o_ref[...] = jnp.zeros_like(o_ref)


def _moe_ffn(block_e, n_used, next_e, xs, w_gu, b_gu, w_down, b_down, layer):
    n_rows, d = xs.shape
    depth, n_e, _, two_ff = w_gu.shape
    d_ff = two_ff // 2
    in_specs = [pl.BlockSpec((MOE_BM, d), lambda i, be, nu, nx: (i, 0)),
                pl.BlockSpec(memory_space=pl.ANY),
                pl.BlockSpec((None, None, 1, two_ff), lambda i, be, nu, nx: (layer, be[i], 0, 0)),
                pl.BlockSpec(memory_space=pl.ANY),
                pl.BlockSpec((None, None, 1, d), lambda i, be, nu, nx: (layer, be[i], 0, 0))]
    args = [block_e, n_used, next_e, xs, w_gu, b_gu.reshape(depth, n_e, 1, two_ff), w_down,
            b_down.reshape(depth, n_e, 1, d)]
    grid_spec = pltpu.PrefetchScalarGridSpec(
        num_scalar_prefetch=3,
        grid=(n_rows // MOE_BM,),
        in_specs=in_specs,
        out_specs=pl.BlockSpec((MOE_BM, d), lambda i, be, nu, nx: (i, 0)),
        scratch_shapes=[pltpu.VMEM((2, d, two_ff), F32), pltpu.VMEM((2, d_ff, d), F32),
                        pltpu.VMEM((d, two_ff), BF16), pltpu.VMEM((d_ff, d), BF16),
                        pltpu.SemaphoreType.DMA((2, 2)), pltpu.SMEM((1,), jnp.int32)],
    )
    return pl.pallas_call(
        functools.partial(_moe_ffn_kernel, layer=layer),
        grid_spec=grid_spec,
        out_shape=jax.ShapeDtypeStruct((n_rows, d), BF16),
        compiler_params=_cparams(("arbitrary",)),
        name="moe_ffn",
    )(*args)


def _next_experts(block_e, n_used):
    idx = jnp.arange(block_e.shape[0], dtype=jnp.int32)
    later = (idx[None, :] > idx[:, None]) & (idx[None, :] < n_used) & (block_e[None, :] != block_e[:, None])
    nxt = jnp.min(jnp.where(later, block_e[None, :], N_EXPERTS), axis=1)
    return jnp.where(nxt == N_EXPERTS, -1, nxt).astype(jnp.int32)


def _moe_combine_kernel(x_ref, g2_ref, w_ref, y_ref, o_ref):
    w = w_ref[...]
    f = w[:, 0:1] * y_ref[0].astype(F32)
    for k in range(1, TOP_K):
        f = f + w[:, k:k + 1] * y_ref[k].astype(F32)
    o_ref[...] = x_ref[...] + g2_ref[...] * f


def _moe_combine(x, g2, top_w, picked, midx):
    t, d = x.shape
    return pl.pallas_call(
        _moe_combine_kernel,
        grid=(t // TM,),
        in_specs=[pl.BlockSpec((TM, d), lambda i: (i, 0)),
                  pl.BlockSpec((None, 1, d), lambda i: (midx(i), 0, 0)),
                  pl.BlockSpec((TM, LANES), lambda i: (i, 0)),
                  pl.BlockSpec((TOP_K, TM, d), lambda i: (0, i, 0))],
        out_specs=pl.BlockSpec((TM, d), lambda i: (i, 0)),
        out_shape=jax.ShapeDtypeStruct((t, d), F32),
        compiler_params=_cparams(("parallel",)),
        name="moe_combine",
    )(x, g2, top_w, picked)


SMEM_TILE = 1024


def _row_tokens_kernel(lo_ref, hi_ref, dest_ref, rt_ref, *, table_rows):
    i = pl.program_id(0)
    chunk = dest_ref.shape[0]

    @pl.when(i == 0)
    def _():
        def fill_gap(g, _):
            def fill(r, _):
                rt_ref[r] = lax.rem(r, table_rows)
                return 0
            lax.fori_loop(lo_ref[g], hi_ref[g], fill, 0)
            return 0
        lax.fori_loop(0, lo_ref.shape[0], fill_gap, 0)

    base = i * (chunk // TOP_K)

    def body(tok, _):
        for k in range(TOP_K):
            rt_ref[dest_ref[tok * TOP_K + k]] = base + tok
        return 0

    lax.fori_loop(0, chunk // TOP_K, body, 0, unroll=4)


def _row_tokens(gap_lo, gap_hi, dest_flat, n_rows, table_rows):
    n_assign = dest_flat.shape[0]
    chunk = 4 * SMEM_TILE if n_assign % (4 * SMEM_TILE) == 0 else SMEM_TILE
    assert n_assign % chunk == 0
    grid_spec = pltpu.PrefetchScalarGridSpec(
        num_scalar_prefetch=2,
        grid=(n_assign // chunk,),
        in_specs=[pl.BlockSpec((chunk,), lambda i, lo, hi: (i,), memory_space=pltpu.SMEM)],
        out_specs=pl.BlockSpec(memory_space=pltpu.SMEM),
    )
    return pl.pallas_call(
        functools.partial(_row_tokens_kernel, table_rows=table_rows),
        grid_spec=grid_spec,
        out_shape=jax.ShapeDtypeStruct((n_rows,), jnp.int32),
        compiler_params=_cparams(("arbitrary",)),
        name="moe_row_tokens",
    )(gap_lo, gap_hi, dest_flat)


def _moe(x, g2, midx, h2, top_e, top_w, top_r, counts, w_gu, b_gu, w_down, b_down, layer):
    t, d = h2.shape
    n_blocks = -(-t * TOP_K // MOE_BM) + N_EXPERTS
    n_rows = n_blocks * MOE_BM
    cnt = counts[0, :N_EXPERTS].astype(jnp.int32)
    padded = (cnt + MOE_BM - 1) // MOE_BM * MOE_BM
    pends = jnp.cumsum(padded)
    pstarts = pends - padded
    e = top_e[:, :TOP_K]
    expert_ids = jnp.arange(N_EXPERTS, dtype=jnp.int32)
    pstart_of_pick = jnp.sum(jnp.where(e[:, :, None] == expert_ids, pstarts, 0), axis=-1)
    dest = pstart_of_pick + top_r[:, :TOP_K]
    block_start = jnp.arange(n_blocks, dtype=jnp.int32) * MOE_BM
    block_e = jnp.sum((block_start[:, None] >= pends[None, :]).astype(jnp.int32), axis=1)
    block_e = jnp.minimum(block_e, N_EXPERTS - 1)
    n_used = (pends[-1] // MOE_BM).astype(jnp.int32).reshape(1)

    gap_lo = jnp.concatenate([pstarts + cnt, pends[-1:]])
    gap_hi = jnp.concatenate([pends, jnp.full((1,), n_rows, jnp.int32)])
    table_rows = 2 * t
    row_tok = _row_tokens(gap_lo, gap_hi, dest.reshape(-1), n_rows, table_rows)
    h2_big = jnp.concatenate([h2, jnp.zeros((table_rows - t, d), h2.dtype)], axis=0)
    xs = h2_big[row_tok]
    ys = _moe_ffn(block_e, n_used, _next_experts(block_e, n_used[0]), xs, w_gu, b_gu, w_down, b_down, layer)
    picked = ys[dest.T.reshape(-1)].reshape(TOP_K, t, d)
    return _moe_combine(x, g2, top_w, picked, midx)


def _pad_cols(w, width):
    return jnp.concatenate([w, jnp.zeros((w.shape[0], width - w.shape[1]), w.dtype)], axis=1)


def kernel(x, c, ctx, c_ctx, mod_w, mod_b, norm1_g, norm2_g, ev_w_in, ev_w_out, lru_conv_w, lru_conv_b, lru_wa, lru_ba, lru_wx, lru_bx, lru_lambda, mla_q_norm_g, mla_w_uq, mla_kv_norm_g, mla_w_ukv, mla_qn_g, mla_kn_g, od_w_in, od_w_out, diff_qn_g, diff_kn_g, diff_lq1, diff_lk1, diff_lq2, diff_lk2, diff_subln_g, router_w, router_b, moe_w_gu, moe_b_gu, moe_w_down, moe_b_down):
    nb, seq, d = x.shape
    n_ctx = ctx.shape[1]
    depth = mod_w.shape[0]
    l = n_ctx + seq
    tpb = l // TM
    assert n_ctx == TM and seq % TQ == 0 and seq % GRID_W == 0

    xa = jnp.concatenate([x, ctx], axis=1).reshape(nb * l, d)
    cvec = jnp.concatenate([c, c_ctx[None, :], jnp.zeros((SUBLANES - nb - 1, d), F32)], axis=0)
    router_w_p = jnp.concatenate([router_w, jnp.zeros((depth, d, LANES - N_EXPERTS), F32)], axis=-1)
    router_b_p = jnp.concatenate([router_b, jnp.zeros((depth, LANES - N_EXPERTS), F32)], axis=-1)
    for layer in range(depth):
        last = layer == depth - 1
        i = layer // 2
        mod = _adaln(cvec, mod_w[layer], mod_b[layer])[:nb + 1]
        sh1, sc1, g1, sh2, sc2, g2 = (mod[:, k * d:(k + 1) * d].reshape(nb + 1, 1, d) for k in range(6))
        n1g = norm1_g[layer].reshape(1, d)
        n2g = norm2_g[layer].reshape(1, d)

        if layer % 2 == 0:
            w_in = ev_w_in[i].astype(BF16)
            o = 2 * LRU_WIDTH
            splits = [w_in[:, :LRU_WIDTH], w_in[:, LRU_WIDTH:o], w_in[:, o:o + MLA_Q_RANK],
                      w_in[:, o + MLA_Q_RANK:o + MLA_Q_RANK + MLA_KV_RANK],
                      _pad_cols(w_in[:, o + MLA_Q_RANK + MLA_KV_RANK:], LANES)]
            gl, rl, qc, kvc, kr = _even_proj(xa, n1g, sc1, sh1, splits, tpb, nb)

            w_gates, b_gates = _lru_gate_weights(lru_wa[i], lru_ba[i], lru_wx[i], lru_bx[i])
            lru = _lru(gl, rl, lru_conv_w[i], lru_conv_b[i].reshape(1, LRU_WIDTH), w_gates, b_gates,
                       lru_lambda[i], nb, n_ctx, seq)

            w_uq = mla_w_uq[i].astype(BF16).reshape(MLA_Q_RANK, MLA_HEADS, MLA_QK)
            w_q_nope = w_uq[:, :, :MLA_NOPE].reshape(MLA_Q_RANK, MLA_HEADS * MLA_NOPE)
            w_q_rope = jnp.concatenate(
                [w_uq[:, :, MLA_NOPE:], jnp.zeros((MLA_Q_RANK, MLA_HEADS, LANES - MLA_ROPE), BF16)],
                axis=-1).reshape(MLA_Q_RANK, MLA_HEADS * LANES)
            pad_g = lambda g: jnp.concatenate([g, jnp.zeros((LANES - MLA_ROPE,), F32)]).reshape(1, LANES)
            rope = _rope_tables_128(seq, n_ctx, MLA_ROPE, tile_groups=False)
            q, k, v = _mla_prep(qc, kvc, kr, mla_q_norm_g[i].reshape(1, -1), mla_kv_norm_g[i].reshape(1, -1),
                                w_q_nope, w_q_rope, mla_w_ukv[i].astype(BF16),
                                mla_qn_g[i][:MLA_NOPE].reshape(1, LANES), pad_g(mla_qn_g[i][MLA_NOPE:]),
                                mla_kn_g[i][:MLA_NOPE].reshape(1, LANES), pad_g(mla_kn_g[i][MLA_NOPE:]),
                                rope, tpb)
            att = _attend_all(_mla_attn_kernel, "mla_attention", 1, q, k, v, nb, n_ctx, seq, MLA_HEADS,
                              2 * LANES, MLA_V, need_ctx=not last)
            w_out = ev_w_out[i].astype(BF16)
            if last:
                lru = lru.reshape(nb, l, LRU_WIDTH)[:, :seq].reshape(nb * seq, LRU_WIDTH)
            mixes = [lru, att]
            weights = [w_out[:LRU_WIDTH], w_out[LRU_WIDTH:]]
        else:
            lam_init = 0.8 - 0.6 * math.exp(-0.3 * layer)
            w_in = od_w_in[i].astype(BF16)
            n_qk = DIFF_HEADS * 2 * DIFF_HEAD_DIM
            gidx = jnp.arange(LANES) // DIFF_HEAD_DIM
            gmat = jnp.where(gidx[:, None] == gidx[None, :], 1.0 / DIFF_HEAD_DIM, 0.0).astype(BF16)
            rope = _rope_tables_128(seq, n_ctx, DIFF_HEAD_DIM, tile_groups=True)
            tile_g = lambda g: jnp.tile(g, LANES // DIFF_HEAD_DIM).reshape(1, LANES)
            q, k, v = _odd_proj(xa, n1g, sc1, sh1, w_in[:, :n_qk], w_in[:, n_qk:2 * n_qk], w_in[:, 2 * n_qk:],
                                tile_g(diff_qn_g[i]), tile_g(diff_kn_g[i]), gmat, rope, tpb, nb)
            lam = (jnp.exp(jnp.sum(diff_lq1[i] * diff_lk1[i])) - jnp.exp(jnp.sum(diff_lq2[i] * diff_lk2[i]))
                   + lam_init).reshape(1).astype(F32)
            att = _attend_all(
                functools.partial(_diff_attn_kernel, out_scale=1.0 - lam_init), "diff_attention", 2,
                q, k, v, nb, n_ctx, seq, DIFF_HEADS, 2 * DIFF_HEAD_DIM, DIFF_V, need_ctx=not last,
                extra_in=(diff_subln_g[i].reshape(1, DIFF_V), lam),
                extra_specs=(pl.BlockSpec((1, DIFF_V), lambda b, h: (0, 0)),
                             pl.BlockSpec(memory_space=pltpu.SMEM)))
            mixes = [att]
            weights = [od_w_out[i].astype(BF16)]

        xo, h2, te, tw, tr, counts = _post_mix(mixes, weights, xa, g1, n2g, sc2, sh2, router_w_p[layer],
                                               router_b_p[layer].reshape(1, LANES), tpb, nb, last)
        midx = (lambda t: t // (tpb - 1)) if last else _mod_index(tpb, nb)
        xa = _moe(xo, g2, midx, h2, te, tw, tr, counts, moe_w_gu, moe_b_gu, moe_w_down, moe_b_down, layer)
    return xa.reshape(nb, seq, d)
```

```python
import functools
import math

import jax
import jax.numpy as jnp
from jax import lax
from jax.experimental import pallas as pl
from jax.experimental.pallas import tpu as pltpu

F32 = jnp.float32
BF16 = jnp.bfloat16
HIGHEST = lax.Precision.HIGHEST

GRID_W = 64
NORM_EPS = 1e-6
ROPE_BASE = 10000.0
LRU_WIDTH = 512
LRU_BLOCKS = 8
LRU_BLOCK_W = LRU_WIDTH // LRU_BLOCKS
LRU_C = 8.0
CONV_W = 4
MLA_HEADS = 4
MLA_Q_RANK = 384
MLA_KV_RANK = 256
MLA_NOPE = 128
MLA_ROPE = 64
MLA_V = 128
MLA_QK = MLA_NOPE + MLA_ROPE
DIFF_HEADS = 8
DIFF_HEAD_DIM = 64
DIFF_V = 2 * DIFF_HEAD_DIM
N_EXPERTS = 32
TOP_K = 4
SWIGLU_LIMIT = 7.0
SWIGLU_ALPHA = 1.702

LANES = 128
SUBLANES = 8
VMEM_LIMIT = 52 * 1024 * 1024

TM = 256
TQ = 512
LRU_CT = 128
LRU_CHUNK = 128
LRU_SEG = LRU_CHUNK // SUBLANES
LRU_PAD = 8
MOE_BM = 256
NEG_BIG = -1e30
LOG2_E = math.log2(math.e)


def _cparams(sem, vmem=VMEM_LIMIT):
    return pltpu.CompilerParams(dimension_semantics=sem, vmem_limit_bytes=vmem)


def _adaln_kernel(c_ref, w_ref, b_ref, o_ref):
    c = c_ref[...]
    s = c * jax.nn.sigmoid(c)
    o_ref[...] = jnp.dot(s, w_ref[...], preferred_element_type=F32, precision=HIGHEST) + b_ref[...]


def _adaln(cvec, w, b):
    rows, d = cvec.shape
    n = w.shape[1]
    tn = 1536
    return pl.pallas_call(
        _adaln_kernel,
        grid=(n // tn,),
        in_specs=[pl.BlockSpec((rows, d), lambda j: (0, 0)),
                  pl.BlockSpec((d, tn), lambda j: (0, j)),
                  pl.BlockSpec((1, tn), lambda j: (0, j))],
        out_specs=pl.BlockSpec((rows, tn), lambda j: (0, j)),
        out_shape=jax.ShapeDtypeStruct((rows, n), F32),
        compiler_params=_cparams(("arbitrary",)),
        name="adaln_mod",
    )(cvec, w, b.reshape(1, n))


def _norm_mod(x, g, sc, sh):
    var = jnp.mean(x * x, axis=-1, keepdims=True)
    y = x * lax.rsqrt(var + NORM_EPS) * g
    return y * (1.0 + sc) + sh


def _ones_column(shape):
    lane = lax.broadcasted_iota(jnp.int32, shape, 1)
    return jnp.where(lane == 0, 1.0, 0.0).astype(BF16)


def _rope_tile(x, cos, sin_a, sin_b):
    up = pltpu.roll(x, LANES - 16, axis=1)
    dn = pltpu.roll(x, 16, axis=1)
    return x * cos + up * sin_a + dn * sin_b


def _rope_tables(seq, rot_dim):
    n_rows = seq // GRID_W
    rows = jnp.repeat(jnp.arange(n_rows, dtype=F32), GRID_W)
    cols = jnp.tile(jnp.arange(GRID_W, dtype=F32), n_rows)
    axis_dim = rot_dim // 2
    inv_freq = ROPE_BASE ** (-jnp.arange(0, axis_dim, 2, dtype=F32) / axis_dim)
    ang_r = rows[:, None] * inv_freq
    ang_c = cols[:, None] * inv_freq
    ang = jnp.concatenate([ang_r, ang_r, ang_c, ang_c], axis=-1)
    cos, sin = jnp.cos(ang), jnp.sin(ang)
    quarter = rot_dim // 4
    first = (jnp.arange(rot_dim) % (2 * quarter)) < quarter
    sin_a = jnp.where(first, -sin, 0.0)
    sin_b = jnp.where(first, 0.0, sin)
    return cos, sin_a, sin_b


def _rope_tables_128(seq, ctx, rot_dim, tile_groups):
    cos, sin_a, sin_b = _rope_tables(seq, rot_dim)
    if tile_groups:
        reps = LANES // rot_dim
        cos, sin_a, sin_b = (jnp.tile(t, (1, reps)) for t in (cos, sin_a, sin_b))
    else:
        pad = LANES - rot_dim
        cos = jnp.concatenate([cos, jnp.ones((seq, pad), F32)], axis=-1)
        sin_a = jnp.concatenate([sin_a, jnp.zeros((seq, pad), F32)], axis=-1)
        sin_b = jnp.concatenate([sin_b, jnp.zeros((seq, pad), F32)], axis=-1)
    cos = jnp.concatenate([cos, jnp.ones((ctx, LANES), F32)], axis=0)
    sin_a = jnp.concatenate([sin_a, jnp.zeros((ctx, LANES), F32)], axis=0)
    sin_b = jnp.concatenate([sin_b, jnp.zeros((ctx, LANES), F32)], axis=0)
    return cos, sin_a, sin_b


def _mod_index(tpb, nb):
    def idx(i):
        return jnp.where(i % tpb == tpb - 1, nb, i // tpb)
    return idx


def _even_proj_kernel(x_ref, g_ref, sc_ref, sh_ref, wg_ref, wr_ref, wq_ref, wkv_ref, wkr_ref,
                      og_ref, or_ref, oq_ref, okv_ref, okr_ref):
    h = _norm_mod(x_ref[...], g_ref[...], sc_ref[...], sh_ref[...]).astype(BF16)
    for w_ref, o_ref in ((wg_ref, og_ref), (wr_ref, or_ref), (wq_ref, oq_ref),
                         (wkv_ref, okv_ref), (wkr_ref, okr_ref)):
        o_ref[...] = jnp.dot(h, w_ref[...], preferred_element_type=F32)


def _even_proj(x, gain, sc, sh, weights, tpb, nb):
    t, d = x.shape
    midx = _mod_index(tpb, nb)
    w_specs = [pl.BlockSpec(w.shape, lambda i: (0, 0)) for w in weights]
    return pl.pallas_call(
        _even_proj_kernel,
        grid=(t // TM,),
        in_specs=[pl.BlockSpec((TM, d), lambda i: (i, 0)),
                  pl.BlockSpec((1, d), lambda i: (0, 0)),
                  pl.BlockSpec((None, 1, d), lambda i: (midx(i), 0, 0)),
                  pl.BlockSpec((None, 1, d), lambda i: (midx(i), 0, 0))] + w_specs,
        out_specs=[pl.BlockSpec((TM, w.shape[1]), lambda i: (i, 0)) for w in weights],
        out_shape=[jax.ShapeDtypeStruct((t, w.shape[1]), F32) for w in weights],
        compiler_params=_cparams(("parallel",)),
        name="even_in_proj",
    )(x, gain, sc, sh, *weights)


def _mla_prep_kernel(qc_ref, kvc_ref, kr_ref, qng_ref, kvng_ref, wqn_ref, wqr_ref, wkv_ref,
                     qgn_ref, qgr_ref, kgn_ref, kgr_ref, cos_ref, sa_ref, sb_ref,
                     q_ref, k_ref, v_ref):
    cos, sa, sb = cos_ref[...], sa_ref[...], sb_ref[...]
    inv_qk = 1.0 / MLA_QK

    qc = qc_ref[...]
    hq = (qc * lax.rsqrt(jnp.mean(qc * qc, axis=-1, keepdims=True) + NORM_EPS) * qng_ref[...]).astype(BF16)
    q_nope = jnp.dot(hq, wqn_ref[...], preferred_element_type=F32)
    q_rope = jnp.dot(hq, wqr_ref[...], preferred_element_type=F32)
    q_scale = MLA_QK ** -0.5 * LOG2_E
    for h in range(MLA_HEADS):
        qn = q_nope[:, h * LANES:(h + 1) * LANES]
        qr = q_rope[:, h * LANES:(h + 1) * LANES]
        ms = (jnp.sum(qn * qn, axis=-1, keepdims=True) + jnp.sum(qr * qr, axis=-1, keepdims=True)) * inv_qk
        rs = lax.rsqrt(ms + NORM_EPS) * q_scale
        q_ref[:, (2 * h) * LANES:(2 * h + 1) * LANES] = (qn * rs * qgn_ref[...]).astype(BF16)
        q_ref[:, (2 * h + 1) * LANES:(2 * h + 2) * LANES] = _rope_tile(qr * rs * qgr_ref[...], cos, sa, sb).astype(BF16)

    kvc = kvc_ref[...]
    hkv = (kvc * lax.rsqrt(jnp.mean(kvc * kvc, axis=-1, keepdims=True) + NORM_EPS) * kvng_ref[...]).astype(BF16)
    kv = jnp.dot(hkv, wkv_ref[...], preferred_element_type=F32)
    kr = kr_ref[...]
    kr_ss = jnp.sum(kr * kr, axis=-1, keepdims=True)
    kr_rot = _rope_tile(kr * kgr_ref[...], cos, sa, sb)
    for h in range(MLA_HEADS):
        kn = kv[:, (2 * h) * LANES:(2 * h + 1) * LANES]
        ms = (jnp.sum(kn * kn, axis=-1, keepdims=True) + kr_ss) * inv_qk
        rs = lax.rsqrt(ms + NORM_EPS)
        k_ref[h, 0:LANES, :] = jnp.transpose(kn * rs * kgn_ref[...]).astype(BF16)
        k_ref[h, LANES:2 * LANES, :] = jnp.transpose(kr_rot * rs).astype(BF16)
        v_ref[:, (2 * h) * LANES:(2 * h + 1) * LANES] = kv[:, (2 * h + 1) * LANES:(2 * h + 2) * LANES].astype(BF16)
        v_ref[:, (2 * h + 1) * LANES:(2 * h + 2) * LANES] = _ones_column((kv.shape[0], LANES))


def _mla_prep(qc, kvc, kr, q_norm_g, kv_norm_g, w_q_nope, w_q_rope, w_ukv, q_gn, q_gr, k_gn, k_gr,
              rope, tpb):
    t = qc.shape[0]
    cos, sa, sb = rope

    def const(a):
        return pl.BlockSpec(a.shape, lambda i: (0, 0))

    def rows(a):
        return pl.BlockSpec((TM, a.shape[1]), lambda i: (i, 0))

    tab = pl.BlockSpec((TM, LANES), lambda i: (i % tpb, 0))
    hq, hv = MLA_HEADS * 2 * LANES, MLA_HEADS * 2 * MLA_V
    return pl.pallas_call(
        _mla_prep_kernel,
        grid=(t // TM,),
        in_specs=[rows(qc), rows(kvc), rows(kr), const(q_norm_g), const(kv_norm_g),
                  const(w_q_nope), const(w_q_rope), const(w_ukv),
                  const(q_gn), const(q_gr), const(k_gn), const(k_gr), tab, tab, tab],
        out_specs=[pl.BlockSpec((TM, hq), lambda i: (i, 0)),
                   pl.BlockSpec((None, MLA_HEADS, 2 * LANES, TM), lambda i: (i // tpb, 0, 0, i % tpb)),
                   pl.BlockSpec((TM, hv), lambda i: (i, 0))],
        out_shape=[jax.ShapeDtypeStruct((t, hq), BF16),
                   jax.ShapeDtypeStruct((t // (tpb * TM), MLA_HEADS, 2 * LANES, tpb * TM), BF16),
                   jax.ShapeDtypeStruct((t, hv), BF16)],
        compiler_params=_cparams(("parallel",)),
        name="mla_prep",
    )(qc, kvc, kr, q_norm_g, kv_norm_g, w_q_nope, w_q_rope, w_ukv, q_gn, q_gr, k_gn, k_gr, cos, sa, sb)


def _sublane_iota():
    return lax.broadcasted_iota(jnp.int32, (SUBLANES, LANES), 0)


def _scan_chunk(a_chunk, u_chunk, h_chunk, carry, reverse):
    steps = range(LRU_SEG - 1, -1, -1) if reverse else range(LRU_SEG)
    h_loc, p_loc = [None] * LRU_SEG, [None] * LRU_SEG
    h = p = None
    for j in steps:
        a = a_chunk[pl.ds(j, SUBLANES, stride=LRU_SEG), :]
        u = u_chunk[pl.ds(j, SUBLANES, stride=LRU_SEG), :]
        if h is None:
            h, p = u, a
        else:
            h, p = a * h + u, a * p
        h_loc[j], p_loc[j] = h, p
    sub = _sublane_iota()
    seg_p, seg_h = p, h
    for d in (1, 2, 4):
        shift = SUBLANES - d if reverse else d
        prev_p = pltpu.roll(seg_p, shift, axis=0)
        prev_h = pltpu.roll(seg_h, shift, axis=0)
        valid = (sub < SUBLANES - d) if reverse else (sub >= d)
        seg_h = jnp.where(valid, seg_p * prev_h + seg_h, seg_h)
        seg_p = jnp.where(valid, seg_p * prev_p, seg_p)
    h_end = seg_h + seg_p * carry
    if reverse:
        h_in = jnp.where(sub == SUBLANES - 1, carry, pltpu.roll(h_end, SUBLANES - 1, axis=0))
        new_carry = h_end[0:1, :]
    else:
        h_in = jnp.where(sub == 0, carry, pltpu.roll(h_end, 1, axis=0))
        new_carry = h_end[SUBLANES - 1:SUBLANES, :]
    for j in range(LRU_SEG):
        h_chunk[pl.ds(j, SUBLANES, stride=LRU_SEG), :] = h_loc[j] + p_loc[j] * h_in
    return jnp.broadcast_to(new_carry, (SUBLANES, LANES))


def _lru_kernel(g_ref, r_ref, cw_ref, cb_ref, wg_ref, bg_ref, lam_ref, o_ref,
                rp_ref, af_ref, uf_ref, ab_ref, ub_ref, hf_ref, hb_ref, *, ctx, seq):
    n_ctx, n_lat = ctx // LRU_CHUNK, seq // LRU_CHUNK
    zeros_pad = jnp.zeros((LRU_PAD, LRU_CT), F32)
    ctx0 = seq + 2 * LRU_PAD
    rp_ref[0:LRU_PAD, :] = zeros_pad
    rp_ref[LRU_PAD:LRU_PAD + seq, :] = r_ref[0:seq, :]
    rp_ref[LRU_PAD + seq:ctx0, :] = zeros_pad
    rp_ref[ctx0:ctx0 + ctx, :] = r_ref[seq:seq + ctx, :]
    rp_ref[ctx0 + ctx:ctx0 + ctx + LRU_PAD, :] = zeros_pad

    cw = cw_ref[...]
    cb = cb_ref[...]
    wg = wg_ref[...]
    bg = bg_ref[...]
    lam = lam_ref[...]
    sp = jnp.maximum(-lam, 0.0) + jnp.log1p(jnp.exp(-jnp.abs(lam)))

    def coeff_chunk(c, pad_off, row_off):
        start = pl.multiple_of(pad_off + c * LRU_CHUNK, SUBLANES)
        ext = rp_ref[pl.ds(start, LRU_CHUNK + 2 * LRU_PAD), :]
        x = cb
        for tap in range(CONV_W):
            lo = LRU_PAD - 2 + tap
            x = x + ext[lo:lo + LRU_CHUNK, :] * cw[tap:tap + 1, :]
        gates = jnp.dot(x.astype(BF16), wg, preferred_element_type=F32) + bg
        out_row = pl.multiple_of(row_off + c * LRU_CHUNK, SUBLANES)
        for d, (a_ref, u_ref) in enumerate(((af_ref, uf_ref), (ab_ref, ub_ref))):
            r = jax.nn.sigmoid(gates[:, (2 * d) * LRU_CT:(2 * d + 1) * LRU_CT])
            i = jax.nn.sigmoid(gates[:, (2 * d + 1) * LRU_CT:(2 * d + 2) * LRU_CT])
            log_a = -LRU_C * r * sp[d:d + 1, :]
            a_ref[pl.ds(out_row, LRU_CHUNK), :] = jnp.exp(log_a)
            th = jnp.tanh(log_a)
            u_ref[pl.ds(out_row, LRU_CHUNK), :] = jnp.sqrt(-2.0 * th / (1.0 - th)) * i * x
        return None

    def coeff_ctx(c, _):
        coeff_chunk(c, seq + LRU_PAD, seq)
        return 0

    def coeff_lat(c, _):
        coeff_chunk(c, 0, 0)
        return 0

    lax.fori_loop(0, n_ctx, coeff_ctx, 0, unroll=2)
    lax.fori_loop(0, n_lat, coeff_lat, 0, unroll=2)

    def scan_pair(n, row_off):
        def body(c, carry):
            cf, cb_ = carry
            f_row = pl.multiple_of(row_off + c * LRU_CHUNK, SUBLANES)
            b_row = pl.multiple_of(row_off + (n - 1 - c) * LRU_CHUNK, SUBLANES)
            cf = _scan_chunk(af_ref.at[pl.ds(f_row, LRU_CHUNK), :], uf_ref.at[pl.ds(f_row, LRU_CHUNK), :],
                             hf_ref.at[pl.ds(f_row, LRU_CHUNK), :], cf, False)
            cb_ = _scan_chunk(ab_ref.at[pl.ds(b_row, LRU_CHUNK), :], ub_ref.at[pl.ds(b_row, LRU_CHUNK), :],
                              hb_ref.at[pl.ds(b_row, LRU_CHUNK), :], cb_, True)
            return cf, cb_
        return body

    zero = jnp.zeros((SUBLANES, LANES), F32)
    carry = lax.fori_loop(0, n_ctx, scan_pair(n_ctx, seq), (zero, zero), unroll=2)
    lax.fori_loop(0, n_lat, scan_pair(n_lat, 0), carry, unroll=2)

    g = g_ref[...]
    gelu = 0.5 * g * (1.0 + jnp.tanh(math.sqrt(2.0 / math.pi) * (g + 0.044715 * (g * g * g))))
    o_ref[...] = (gelu * (hf_ref[...] + hb_ref[...])).astype(BF16)


def _lru(g, r, conv_w, conv_b, w_gates, b_gates, lam, nb, ctx, seq):
    l = ctx + seq
    width = g.shape[1]
    n_ct = width // LRU_CT
    g3 = g.reshape(nb, l, width)
    r3 = r.reshape(nb, l, width)
    seq_spec = pl.BlockSpec((None, l, LRU_CT), lambda b, c: (b, 0, c))
    scratch = [pltpu.VMEM((l + 3 * LRU_PAD, LRU_CT), F32)] + [pltpu.VMEM((l, LRU_CT), F32)] * 6
    out = pl.pallas_call(
        functools.partial(_lru_kernel, ctx=ctx, seq=seq),
        grid=(nb, n_ct),
        in_specs=[seq_spec, seq_spec,
                  pl.BlockSpec((CONV_W, LRU_CT), lambda b, c: (0, c)),
                  pl.BlockSpec((1, LRU_CT), lambda b, c: (0, c)),
                  pl.BlockSpec((None, LRU_CT, 4 * LRU_CT), lambda b, c: (c, 0, 0)),
                  pl.BlockSpec((None, 1, 4 * LRU_CT), lambda b, c: (c, 0, 0)),
                  pl.BlockSpec((2, LRU_CT), lambda b, c: (0, c))],
        out_specs=seq_spec,
        out_shape=jax.ShapeDtypeStruct((nb, l, width), BF16),
        scratch_shapes=scratch,
        compiler_params=_cparams(("parallel", "parallel")),
        name="rglru",
    )(g3, r3, conv_w, conv_b, w_gates, b_gates, lam)
    return out.reshape(nb * l, width)


def _lru_gate_weights(wa, ba, wx, bx):
    per = LRU_CT // LRU_BLOCK_W
    n_ct = LRU_BLOCKS // per
    eye = jnp.eye(per, dtype=F32)

    def dense(w):
        w4 = w.reshape(n_ct, per, LRU_BLOCK_W, LRU_BLOCK_W)
        return jnp.einsum('cide,ij->cidje', w4, eye).reshape(n_ct, LRU_CT, LRU_CT)

    w = jnp.concatenate([dense(wa[0]), dense(wx[0]), dense(wa[1]), dense(wx[1])], axis=-1)
    b = jnp.concatenate([v.reshape(n_ct, 1, LRU_CT) for v in (ba[0], bx[0], ba[1], bx[1])], axis=-1)
    return w.astype(BF16), b


def _key_spans(n_keys):
    half = n_keys // 2 if n_keys % (2 * LANES) == 0 else n_keys
    return [(lo, lo + half) for lo in range(0, n_keys, half)]


def _attn_scores(q_ops, kt_ref, s_ref, m_ref):
    for t, q in enumerate(q_ops):
        for lo, hi in _key_spans(kt_ref.shape[-1]):
            s_ref[t, :, lo:hi] = jnp.dot(q, kt_ref[:, lo:hi], preferred_element_type=F32)
        m_ref[t] = jnp.max(s_ref[t], axis=-1, keepdims=True)


def _attn_values(v_ref, s_ref, m_ref, p_ref):
    accs = []
    for t in range(s_ref.shape[0]):
        m = m_ref[t]
        acc = None
        for lo, hi in _key_spans(s_ref.shape[-1]):
            p_ref[t, :, lo:hi] = jnp.exp2(s_ref[t, :, lo:hi] - m).astype(BF16)
            part = jnp.dot(p_ref[t, :, lo:hi], v_ref[lo:hi, :], preferred_element_type=F32)
            acc = part if acc is None else acc + part
        accs.append(acc)
    return accs


def _attn_pipeline(q_ref, kt_ref, v_ref, o_ref, bufs, p_ref, q_ops_fn, values_fn):
    s0, m0 = bufs[0]
    tq = s0.shape[1]
    n = q_ref.shape[0] // tq

    def scores(t, s_ref, m_ref):
        row = t * tq if isinstance(t, int) else pl.multiple_of(t * tq, tq)
        _attn_scores(q_ops_fn(q_ref[pl.ds(row, tq), :]), kt_ref, s_ref, m_ref)

    def values(t, s_ref, m_ref):
        row = t * tq if isinstance(t, int) else pl.multiple_of(t * tq, tq)
        o_ref[pl.ds(row, tq), :] = values_fn(v_ref, s_ref, m_ref, p_ref).astype(o_ref.dtype)

    if len(bufs) == 1:
        def one(t, _):
            scores(t, s0, m0)
            values(t, s0, m0)
            return 0

        if n == 1:
            one(0, 0)
        else:
            lax.fori_loop(0, n, one, 0)
        return

    s1, m1 = bufs[1]
    scores(0, s0, m0)
    if n == 1:
        values(0, s0, m0)
        return
    assert n % 2 == 0

    def pair(k, _):
        t = 2 * k
        scores(t + 1, s1, m1)
        values(t, s0, m0)
        scores(t + 2, s0, m0)
        values(t + 1, s1, m1)
        return 0

    lax.fori_loop(0, n // 2 - 1, pair, 0)
    scores(n - 1, s1, m1)
    values(n - 2, s0, m0)
    values(n - 1, s1, m1)


def _score_bufs(scratch):
    return tuple(zip(scratch[0:-1:2], scratch[1:-1:2])), scratch[-1]


def _mla_attn_kernel(q_ref, kt_ref, v_ref, o_ref, *scratch):
    dv = o_ref.shape[-1]
    bufs, p_ref = _score_bufs(scratch)

    def values(*refs):
        acc = _attn_values(*refs)[0]
        return acc[:, :dv] / acc[:, dv:dv + 1]

    _attn_pipeline(q_ref, kt_ref, v_ref, o_ref, bufs, p_ref, lambda q: [q], values)


def _diff_attn_kernel(q_ref, kt_ref, v_ref, g_ref, lam_ref, o_ref, *scratch, out_scale):
    dv = o_ref.shape[-1]
    bufs, p_ref = _score_bufs(scratch)

    def q_ops(q):
        lane = lax.broadcasted_iota(jnp.int32, q.shape, 1)
        zero = jnp.zeros_like(q)
        return [jnp.where(lane < DIFF_HEAD_DIM, q, zero), jnp.where(lane < DIFF_HEAD_DIM, zero, q)]

    def values(*refs):
        a1, a2 = _attn_values(*refs)
        o = a1[:, :dv] / a1[:, dv:dv + 1] - lam_ref[0] * (a2[:, :dv] / a2[:, dv:dv + 1])
        y = o * lax.rsqrt(jnp.mean(o * o, axis=-1, keepdims=True) + NORM_EPS) * g_ref[...]
        return y * out_scale

    _attn_pipeline(q_ref, kt_ref, v_ref, o_ref, bufs, p_ref, q_ops, values)


def _attention(kernel_fn, name, n_softmax, q, kt, v, nb, ctx, seq, heads, dk, dv, ctx_queries,
               extra_in=(), extra_specs=()):
    l = ctx + seq
    q3 = q.reshape(nb, l, heads * dk)
    v3 = v.reshape(nb, l, heads * 2 * dv)
    if ctx_queries:
        tq, n_keys, rows = ctx, ctx, ctx
        blk = seq // ctx
        in_specs = [pl.BlockSpec((None, rows, dk), lambda b, h: (b, blk, h)),
                    pl.BlockSpec((None, None, dk, ctx), lambda b, h: (b, h, 0, blk)),
                    pl.BlockSpec((None, ctx, 2 * dv), lambda b, h: (b, blk, h))]
    else:
        tq, n_keys, rows = TQ, l, seq
        in_specs = [pl.BlockSpec((None, rows, dk), lambda b, h: (b, 0, h)),
                    pl.BlockSpec((None, None, dk, l), lambda b, h: (b, h, 0, 0)),
                    pl.BlockSpec((None, l, 2 * dv), lambda b, h: (b, 0, h))]
    score_buf = [pltpu.VMEM((n_softmax, tq, n_keys), F32), pltpu.VMEM((n_softmax, tq, 1), F32)]
    out = pl.pallas_call(
        kernel_fn,
        grid=(nb, heads),
        in_specs=in_specs + list(extra_specs),
        out_specs=pl.BlockSpec((None, rows, dv), lambda b, h: (b, 0, h)),
        out_shape=jax.ShapeDtypeStruct((nb, rows, heads * dv), BF16),
        scratch_shapes=score_buf * (2 if n_softmax == 1 else 1) + [pltpu.VMEM((n_softmax, tq, n_keys), BF16)],
        compiler_params=_cparams(("parallel", "parallel")),
        name=name,
    )(q3, kt, v3, *extra_in)
    return out


def _key_transpose(k, nb, l, heads, dk):
    return jnp.transpose(k.reshape(nb, l, heads, dk), (0, 2, 3, 1))


def _attend_all(kernel_fn, name, n_softmax, q, k, v, nb, ctx, seq, heads, dk, dv, need_ctx, **extra):
    kt = k if k.ndim == 4 else _key_transpose(k, nb, ctx + seq, heads, dk)
    lat = _attention(kernel_fn, name, n_softmax, q, kt, v, nb, ctx, seq, heads, dk, dv, False, **extra)
    if not need_ctx:
        return lat.reshape(nb * seq, heads * dv)
    cx = _attention(kernel_fn, name + "_ctx", n_softmax, q, kt, v, nb, ctx, seq, heads, dk, dv, True, **extra)
    return jnp.concatenate([lat, cx], axis=1).reshape(nb * (ctx + seq), heads * dv)


def _group_ms(x, gmat):
    return jnp.dot((x * x).astype(BF16), gmat, preferred_element_type=F32)


def _odd_proj_kernel(x_ref, g_ref, sc_ref, sh_ref, wq_ref, wk_ref, wv_ref, qg_ref, kg_ref, gm_ref,
                     cos_ref, sa_ref, sb_ref, q_ref, k_ref, v_ref):
    h = _norm_mod(x_ref[...], g_ref[...], sc_ref[...], sh_ref[...]).astype(BF16)
    cos, sa, sb = cos_ref[...], sa_ref[...], sb_ref[...]
    gmat = gm_ref[...]
    q_scale = DIFF_HEAD_DIM ** -0.5 * LOG2_E
    for w_ref, gain_ref, scale, is_key in ((wq_ref, qg_ref, q_scale, False), (wk_ref, kg_ref, 1.0, True)):
        raw = jnp.dot(h, w_ref[...], preferred_element_type=F32)
        gain = gain_ref[...] * scale
        for c in range(raw.shape[1] // LANES):
            t = raw[:, c * LANES:(c + 1) * LANES]
            t = t * lax.rsqrt(_group_ms(t, gmat) + NORM_EPS) * gain
            t = _rope_tile(t, cos, sa, sb)
            if is_key:
                k_ref[c] = jnp.transpose(t).astype(BF16)
            else:
                q_ref[:, c * LANES:(c + 1) * LANES] = t.astype(BF16)
    v = jnp.dot(h, wv_ref[...], preferred_element_type=F32).astype(BF16)
    for c in range(v.shape[1] // DIFF_V):
        v_ref[:, (2 * c) * DIFF_V:(2 * c + 1) * DIFF_V] = v[:, c * DIFF_V:(c + 1) * DIFF_V]
        v_ref[:, (2 * c + 1) * DIFF_V:(2 * c + 2) * DIFF_V] = _ones_column((v.shape[0], DIFF_V))


def _odd_proj(x, gain, sc, sh, wq, wk, wv, q_gain, k_gain, gmat, rope, tpb, nb):
    t, d = x.shape
    midx = _mod_index(tpb, nb)
    cos, sa, sb = rope

    def const(a):
        return pl.BlockSpec(a.shape, lambda i: (0, 0))

    tab = pl.BlockSpec((TM, LANES), lambda i: (i % tpb, 0))
    n = wq.shape[1]
    return pl.pallas_call(
        _odd_proj_kernel,
        grid=(t // TM,),
        in_specs=[pl.BlockSpec((TM, d), lambda i: (i, 0)), const(gain),
                  pl.BlockSpec((None, 1, d), lambda i: (midx(i), 0, 0)),
                  pl.BlockSpec((None, 1, d), lambda i: (midx(i), 0, 0)),
                  const(wq), const(wk), const(wv), const(q_gain), const(k_gain), const(gmat),
                  tab, tab, tab],
        out_specs=[pl.BlockSpec((TM, n), lambda i: (i, 0)),
                   pl.BlockSpec((None, n // LANES, LANES, TM), lambda i: (i // tpb, 0, 0, i % tpb)),
                   pl.BlockSpec((TM, 2 * n), lambda i: (i, 0))],
        out_shape=[jax.ShapeDtypeStruct((t, n), BF16),
                   jax.ShapeDtypeStruct((nb, n // LANES, LANES, tpb * TM), BF16),
                   jax.ShapeDtypeStruct((t, 2 * n), BF16)],
        compiler_params=_cparams(("parallel",)),
        name="odd_in_proj",
    )(x, gain, sc, sh, wq, wk, wv, q_gain, k_gain, gmat, cos, sa, sb)


def _post_mix_kernel(*refs, n_mix):
    mix_refs = refs[:n_mix]
    w_refs = refs[n_mix:2 * n_mix]
    (x_ref, g1_ref, n2_ref, sc_ref, sh_ref, rwh_ref, rwl_ref, rb_ref,
     xo_ref, h2_ref, te_ref, tw_ref, tr_ref, cnt_ref, base_ref) = refs[2 * n_mix:]
    i = pl.program_id(0)

    @pl.when(i == 0)
    def _():
        base_ref[...] = jnp.zeros_like(base_ref)

    d = x_ref.shape[-1]
    halves = []
    for lo in (0, d // 2):
        m = jnp.dot(mix_refs[0][...], w_refs[0][:, lo:lo + d // 2], preferred_element_type=F32)
        for a_ref, w_ref in zip(mix_refs[1:], w_refs[1:]):
            m = m + jnp.dot(a_ref[...], w_ref[:, lo:lo + d // 2], preferred_element_type=F32)
        halves.append(m)
    x = x_ref[...] + g1_ref[...] * jnp.concatenate(halves, axis=-1)
    xo_ref[...] = x
    h2 = _norm_mod(x, n2_ref[...], sc_ref[...], sh_ref[...])
    h2_hi = h2.astype(BF16)
    h2_ref[...] = h2_hi

    h2_lo = (h2 - h2_hi.astype(F32)).astype(BF16)
    logits = (jnp.dot(h2_hi, rwh_ref[...], preferred_element_type=F32)
              + jnp.dot(h2_lo, rwh_ref[...], preferred_element_type=F32)
              + jnp.dot(h2_hi, rwl_ref[...], preferred_element_type=F32)) + rb_ref[...]
    lane = lax.broadcasted_iota(jnp.int32, logits.shape, 1)
    lane_f = lane.astype(F32)
    vals = jnp.where(lane < N_EXPERTS, logits, NEG_BIG)
    tops, firsts, hots = [], [], []
    for _ in range(TOP_K):
        top = jnp.max(vals, axis=-1, keepdims=True)
        first = jnp.min(jnp.where(vals == top, lane_f, float(LANES)), axis=-1, keepdims=True)
        hot = lane_f == first
        tops.append(top)
        firsts.append(first)
        hots.append(hot)
        vals = jnp.where(hot, 2.0 * NEG_BIG, vals)
    exps = [jnp.exp(t - tops[0]) for t in tops]
    denom = exps[0] + exps[1] + exps[2] + exps[3]

    picked = (hots[0] | hots[1]) | (hots[2] | hots[3])
    cnt = jnp.where(picked, 1.0, 0.0)
    r_io = lax.broadcasted_iota(jnp.int32, (TM, TM), 0)
    c_io = lax.broadcasted_iota(jnp.int32, (TM, TM), 1)
    tri = jnp.where(c_io < r_io, 1.0, 0.0).astype(BF16)
    before = jnp.dot(tri, cnt.astype(BF16), preferred_element_type=F32) + base_ref[...]
    te = jnp.zeros(logits.shape, F32)
    tw = jnp.zeros(logits.shape, F32)
    tr = jnp.zeros(logits.shape, F32)
    for k in range(TOP_K):
        r_k = jnp.sum(jnp.where(hots[k], before, 0.0), axis=-1, keepdims=True)
        te = jnp.where(lane == k, firsts[k], te)
        tw = jnp.where(lane == k, exps[k] / denom, tw)
        tr = jnp.where(lane == k, r_k, tr)
    te_ref[...] = te.astype(jnp.int32)
    tw_ref[...] = tw
    tr_ref[...] = tr.astype(jnp.int32)
    base_ref[...] += jnp.sum(cnt, axis=0, keepdims=True)
    cnt_ref[...] = base_ref[...]


def _post_mix(mixes, weights, x, g1, n2g, sc2, sh2, router_w, router_b, tpb, nb, latent_only):
    d = x.shape[1]
    tiles_in = tpb
    if latent_only:
        tiles_out = tpb - 1
        n_tiles = nb * tiles_out

        def in_row(i):
            return (i // tiles_out) * tiles_in + i % tiles_out

        def midx(i):
            return i // tiles_out
    else:
        n_tiles = nb * tpb

        def in_row(i):
            return i

        midx = _mod_index(tpb, nb)
    t_out = n_tiles * TM

    def const(a):
        return pl.BlockSpec(a.shape, lambda i: (0, 0))

    def mod(a):
        return pl.BlockSpec((None, 1, d), lambda i: (midx(i), 0, 0))

    in_specs = [pl.BlockSpec((TM, a.shape[1]), lambda i: (i, 0)) for a in mixes]
    in_specs += [const(w) for w in weights]
    in_specs += [pl.BlockSpec((TM, d), lambda i: (in_row(i), 0)), mod(g1), const(n2g), mod(sc2), mod(sh2),
                 const(router_w), const(router_w), const(router_b)]
    rw_hi = router_w.astype(BF16)
    rw_lo = (router_w - rw_hi.astype(F32)).astype(BF16)
    row_out = lambda width: pl.BlockSpec((TM, width), lambda i: (i, 0))
    return pl.pallas_call(
        functools.partial(_post_mix_kernel, n_mix=len(mixes)),
        grid=(n_tiles,),
        in_specs=in_specs,
        out_specs=[row_out(d), row_out(d), row_out(LANES), row_out(LANES), row_out(LANES),
                   pl.BlockSpec((1, LANES), lambda i: (0, 0))],
        out_shape=[jax.ShapeDtypeStruct((t_out, d), F32), jax.ShapeDtypeStruct((t_out, d), BF16),
                   jax.ShapeDtypeStruct((t_out, LANES), jnp.int32), jax.ShapeDtypeStruct((t_out, LANES), F32),
                   jax.ShapeDtypeStruct((t_out, LANES), jnp.int32), jax.ShapeDtypeStruct((1, LANES), F32)],
        scratch_shapes=[pltpu.VMEM((1, LANES), F32)],
        compiler_params=_cparams(("arbitrary",)),
        name="post_mix_router",
    )(*mixes, *weights, x, g1, n2g, sc2, sh2, rw_hi, rw_lo, router_b)


def _moe_ffn_kernel(be_ref, nu_ref, nxt_ref, x_ref, wgu_hbm, bgu_ref, wd_hbm, bd_ref, o_ref,
                    wgu_f32, wd_f32, wgu_bf, wd_bf, sem, slot_ref, *, layer):
    i = pl.program_id(0)
    prev = be_ref[jnp.maximum(i - 1, 0)]
    fresh = jnp.logical_or(i == 0, be_ref[i] != prev)

    def weight_copies(expert, slot):
        return (pltpu.make_async_copy(wgu_hbm.at[layer, expert], wgu_f32.at[slot], sem.at[0, slot]),
                pltpu.make_async_copy(wd_hbm.at[layer, expert], wd_f32.at[slot], sem.at[1, slot]))

    @pl.when(jnp.logical_and(i == 0, nu_ref[0] > 0))
    def _():
        slot_ref[0] = 0
        for cp in weight_copies(be_ref[0], 0):
            cp.start()

    @pl.when(jnp.logical_and(fresh, i < nu_ref[0]))
    def _():
        slot = slot_ref[0]
        for cp in weight_copies(be_ref[i], slot):
            cp.wait()
        wgu_bf[...] = wgu_f32[slot].astype(BF16)
        wd_bf[...] = wd_f32[slot].astype(BF16)
        slot_ref[0] = 1 - slot

        @pl.when(nxt_ref[i] >= 0)
        def _():
            for cp in weight_copies(nxt_ref[i], 1 - slot):
                cp.start()

    @pl.when(i < nu_ref[0])
    def _():
        d_ff = wd_bf.shape[0]
        gu = jnp.dot(x_ref[...], wgu_bf[...], preferred_element_type=F32) + bgu_ref[...]
        g = jnp.minimum(gu[:, :d_ff], SWIGLU_LIMIT)
        u = jnp.clip(gu[:, d_ff:], -SWIGLU_LIMIT, SWIGLU_LIMIT)
        act = (u + 1.0) * g * jax.nn.sigmoid(SWIGLU_ALPHA * g)
        y = jnp.dot(act.astype(BF16), wd_bf[...], preferred_element_type=F32) + bd_ref[...]
        o_ref[...] = y.astype(o_ref.dtype)

    @pl.when(i >= nu_ref[0])
    def _():
        o_ref[...] = jnp.zeros_like(o_ref)


def _moe_ffn(block_e, n_used, next_e, xs, w_gu, b_gu, w_down, b_down, layer):
    n_rows, d = xs.shape
    depth, n_e, _, two_ff = w_gu.shape
    d_ff = two_ff // 2
    in_specs = [pl.BlockSpec((MOE_BM, d), lambda i, be, nu, nx: (i, 0)),
                pl.BlockSpec(memory_space=pl.ANY),
                pl.BlockSpec((None, None, 1, two_ff), lambda i, be, nu, nx: (layer, be[i], 0, 0)),
                pl.BlockSpec(memory_space=pl.ANY),
                pl.BlockSpec((None, None, 1, d), lambda i, be, nu, nx: (layer, be[i], 0, 0))]
    args = [block_e, n_used, next_e, xs, w_gu, b_gu.reshape(depth, n_e, 1, two_ff), w_down,
            b_down.reshape(depth, n_e, 1, d)]
    grid_spec = pltpu.PrefetchScalarGridSpec(
        num_scalar_prefetch=3,
        grid=(n_rows // MOE_BM,),
        in_specs=in_specs,
        out_specs=pl.BlockSpec((MOE_BM, d), lambda i, be, nu, nx: (i, 0)),
        scratch_shapes=[pltpu.VMEM((2, d, two_ff), F32), pltpu.VMEM((2, d_ff, d), F32),
                        pltpu.VMEM((d, two_ff), BF16), pltpu.VMEM((d_ff, d), BF16),
                        pltpu.SemaphoreType.DMA((2, 2)), pltpu.SMEM((1,), jnp.int32)],
    )
    return pl.pallas_call(
        functools.partial(_moe_ffn_kernel, layer=layer),
        grid_spec=grid_spec,
        out_shape=jax.ShapeDtypeStruct((n_rows, d), BF16),
        compiler_params=_cparams(("arbitrary",)),
        name="moe_ffn",
    )(*args)


def _next_experts(block_e, n_used):
    idx = jnp.arange(block_e.shape[0], dtype=jnp.int32)
    later = (idx[None, :] > idx[:, None]) & (idx[None, :] < n_used) & (block_e[None, :] != block_e[:, None])
    nxt = jnp.min(jnp.where(later, block_e[None, :], N_EXPERTS), axis=1)
    return jnp.where(nxt == N_EXPERTS, -1, nxt).astype(jnp.int32)


def _moe_combine_kernel(x_ref, g2_ref, w_ref, y_ref, o_ref):
    w = w_ref[...]
    f = w[:, 0:1] * y_ref[0].astype(F32)
    for k in range(1, TOP_K):
        f = f + w[:, k:k + 1] * y_ref[k].astype(F32)
    o_ref[...] = x_ref[...] + g2_ref[...] * f


def _moe_combine(x, g2, top_w, picked, midx):
    t, d = x.shape
    return pl.pallas_call(
        _moe_combine_kernel,
        grid=(t // TM,),
        in_specs=[pl.BlockSpec((TM, d), lambda i: (i, 0)),
                  pl.BlockSpec((None, 1, d), lambda i: (midx(i), 0, 0)),
                  pl.BlockSpec((TM, LANES), lambda i: (i, 0)),
                  pl.BlockSpec((TOP_K, TM, d), lambda i: (0, i, 0))],
        out_specs=pl.BlockSpec((TM, d), lambda i: (i, 0)),
        out_shape=jax.ShapeDtypeStruct((t, d), F32),
        compiler_params=_cparams(("parallel",)),
        name="moe_combine",
    )(x, g2, top_w, picked)


SMEM_TILE = 1024


def _row_tokens_kernel(lo_ref, hi_ref, dest_ref, rt_ref, *, table_rows):
    i = pl.program_id(0)
    chunk = dest_ref.shape[0]

    @pl.when(i == 0)
    def _():
        def fill_gap(g, _):
            def fill(r, _):
                rt_ref[r] = lax.rem(r, table_rows)
                return 0
            lax.fori_loop(lo_ref[g], hi_ref[g], fill, 0)
            return 0
        lax.fori_loop(0, lo_ref.shape[0], fill_gap, 0)

    base = i * (chunk // TOP_K)

    def body(tok, _):
        for k in range(TOP_K):
            rt_ref[dest_ref[tok * TOP_K + k]] = base + tok
        return 0

    lax.fori_loop(0, chunk // TOP_K, body, 0, unroll=4)


def _row_tokens(gap_lo, gap_hi, dest_flat, n_rows, table_rows):
    n_assign = dest_flat.shape[0]
    chunk = 4 * SMEM_TILE if n_assign % (4 * SMEM_TILE) == 0 else SMEM_TILE
    assert n_assign % chunk == 0
    grid_spec = pltpu.PrefetchScalarGridSpec(
        num_scalar_prefetch=2,
        grid=(n_assign // chunk,),
        in_specs=[pl.BlockSpec((chunk,), lambda i, lo, hi: (i,), memory_space=pltpu.SMEM)],
        out_specs=pl.BlockSpec(memory_space=pltpu.SMEM),
    )
    return pl.pallas_call(
        functools.partial(_row_tokens_kernel, table_rows=table_rows),
        grid_spec=grid_spec,
        out_shape=jax.ShapeDtypeStruct((n_rows,), jnp.int32),
        compiler_params=_cparams(("arbitrary",)),
        name="moe_row_tokens",
    )(gap_lo, gap_hi, dest_flat)


def _moe(x, g2, midx, h2, top_e, top_w, top_r, counts, w_gu, b_gu, w_down, b_down, layer):
    t, d = h2.shape
    n_blocks = -(-t * TOP_K // MOE_BM) + N_EXPERTS
    n_rows = n_blocks * MOE_BM
    cnt = counts[0, :N_EXPERTS].astype(jnp.int32)
    padded = (cnt + MOE_BM - 1) // MOE_BM * MOE_BM
    pends = jnp.cumsum(padded)
    pstarts = pends - padded
    e = top_e[:, :TOP_K]
    expert_ids = jnp.arange(N_EXPERTS, dtype=jnp.int32)
    pstart_of_pick = jnp.sum(jnp.where(e[:, :, None] == expert_ids, pstarts, 0), axis=-1)
    dest = pstart_of_pick + top_r[:, :TOP_K]
    block_start = jnp.arange(n_blocks, dtype=jnp.int32) * MOE_BM
    block_e = jnp.sum((block_start[:, None] >= pends[None, :]).astype(jnp.int32), axis=1)
    block_e = jnp.minimum(block_e, N_EXPERTS - 1)
    n_used = (pends[-1] // MOE_BM).astype(jnp.int32).reshape(1)

    gap_lo = jnp.concatenate([pstarts + cnt, pends[-1:]])
    gap_hi = jnp.concatenate([pends, jnp.full((1,), n_rows, jnp.int32)])
    table_rows = 2 * t
    row_tok = _row_tokens(gap_lo, gap_hi, dest.reshape(-1), n_rows, table_rows)
    h2_big = jnp.concatenate([h2, jnp.zeros((table_rows - t, d), h2.dtype)], axis=0)
    xs = h2_big[row_tok]
    ys = _moe_ffn(block_e, n_used, _next_experts(block_e, n_used[0]), xs, w_gu, b_gu, w_down, b_down, layer)
    picked = ys[dest.T.reshape(-1)].reshape(TOP_K, t, d)
    return _moe_combine(x, g2, top_w, picked, midx)


def _pad_cols(w, width):
    return jnp.concatenate([w, jnp.zeros((w.shape[0], width - w.shape[1]), w.dtype)], axis=1)


def kernel(x, c, ctx, c_ctx, mod_w, mod_b, norm1_g, norm2_g, ev_w_in, ev_w_out, lru_conv_w, lru_conv_b, lru_wa, lru_ba, lru_wx, lru_bx, lru_lambda, mla_q_norm_g, mla_w_uq, mla_kv_norm_g, mla_w_ukv, mla_qn_g, mla_kn_g, od_w_in, od_w_out, diff_qn_g, diff_kn_g, diff_lq1, diff_lk1, diff_lq2, diff_lk2, diff_subln_g, router_w, router_b, moe_w_gu, moe_b_gu, moe_w_down, moe_b_down):
    nb, seq, d = x.shape
    n_ctx = ctx.shape[1]
    depth = mod_w.shape[0]
    l = n_ctx + seq
    tpb = l // TM
    assert n_ctx == TM and seq % TQ == 0 and seq % GRID_W == 0

    xa = jnp.concatenate([x, ctx], axis=1).reshape(nb * l, d)
    cvec = jnp.concatenate([c, c_ctx[None, :], jnp.zeros((SUBLANES - nb - 1, d), F32)], axis=0)
    router_w_p = jnp.concatenate([router_w, jnp.zeros((depth, d, LANES - N_EXPERTS), F32)], axis=-1)
    router_b_p = jnp.concatenate([router_b, jnp.zeros((depth, LANES - N_EXPERTS), F32)], axis=-1)
    for layer in range(depth):
        last = layer == depth - 1
        i = layer // 2
        mod = _adaln(cvec, mod_w[layer], mod_b[layer])[:nb + 1]
        sh1, sc1, g1, sh2, sc2, g2 = (mod[:, k * d:(k + 1) * d].reshape(nb + 1, 1, d) for k in range(6))
        n1g = norm1_g[layer].reshape(1, d)
        n2g = norm2_g[layer].reshape(1, d)

        if layer % 2 == 0:
            w_in = ev_w_in[i].astype(BF16)
            o = 2 * LRU_WIDTH
            splits = [w_in[:, :LRU_WIDTH], w_in[:, LRU_WIDTH:o], w_in[:, o:o + MLA_Q_RANK],
                      w_in[:, o + MLA_Q_RANK:o + MLA_Q_RANK + MLA_KV_RANK],
                      _pad_cols(w_in[:, o + MLA_Q_RANK + MLA_KV_RANK:], LANES)]
            gl, rl, qc, kvc, kr = _even_proj(xa, n1g, sc1, sh1, splits, tpb, nb)

            w_gates, b_gates = _lru_gate_weights(lru_wa[i], lru_ba[i], lru_wx[i], lru_bx[i])
            lru = _lru(gl, rl, lru_conv_w[i], lru_conv_b[i].reshape(1, LRU_WIDTH), w_gates, b_gates,
                       lru_lambda[i], nb, n_ctx, seq)

            w_uq = mla_w_uq[i].astype(BF16).reshape(MLA_Q_RANK, MLA_HEADS, MLA_QK)
            w_q_nope = w_uq[:, :, :MLA_NOPE].reshape(MLA_Q_RANK, MLA_HEADS * MLA_NOPE)
            w_q_rope = jnp.concatenate(
                [w_uq[:, :, MLA_NOPE:], jnp.zeros((MLA_Q_RANK, MLA_HEADS, LANES - MLA_ROPE), BF16)],
                axis=-1).reshape(MLA_Q_RANK, MLA_HEADS * LANES)
            pad_g = lambda g: jnp.concatenate([g, jnp.zeros((LANES - MLA_ROPE,), F32)]).reshape(1, LANES)
            rope = _rope_tables_128(seq, n_ctx, MLA_ROPE, tile_groups=False)
            q, k, v = _mla_prep(qc, kvc, kr, mla_q_norm_g[i].reshape(1, -1), mla_kv_norm_g[i].reshape(1, -1),
                                w_q_nope, w_q_rope, mla_w_ukv[i].astype(BF16),
                                mla_qn_g[i][:MLA_NOPE].reshape(1, LANES), pad_g(mla_qn_g[i][MLA_NOPE:]),
                                mla_kn_g[i][:MLA_NOPE].reshape(1, LANES), pad_g(mla_kn_g[i][MLA_NOPE:]),
                                rope, tpb)
            att = _attend_all(_mla_attn_kernel, "mla_attention", 1, q, k, v, nb, n_ctx, seq, MLA_HEADS,
                              2 * LANES, MLA_V, need_ctx=not last)
            w_out = ev_w_out[i].astype(BF16)
            if last:
                lru = lru.reshape(nb, l, LRU_WIDTH)[:, :seq].reshape(nb * seq, LRU_WIDTH)
            mixes = [lru, att]
            weights = [w_out[:LRU_WIDTH], w_out[LRU_WIDTH:]]
        else:
            lam_init = 0.8 - 0.6 * math.exp(-0.3 * layer)
            w_in = od_w_in[i].astype(BF16)
            n_qk = DIFF_HEADS * 2 * DIFF_HEAD_DIM
            gidx = jnp.arange(LANES) // DIFF_HEAD_DIM
            gmat = jnp.where(gidx[:, None] == gidx[None, :], 1.0 / DIFF_HEAD_DIM, 0.0).astype(BF16)
            rope = _rope_tables_128(seq, n_ctx, DIFF_HEAD_DIM, tile_groups=True)
            tile_g = lambda g: jnp.tile(g, LANES // DIFF_HEAD_DIM).reshape(1, LANES)
            q, k, v = _odd_proj(xa, n1g, sc1, sh1, w_in[:, :n_qk], w_in[:, n_qk:2 * n_qk], w_in[:, 2 * n_qk:],
                                tile_g(diff_qn_g[i]), tile_g(diff_kn_g[i]), gmat, rope, tpb, nb)
            lam = (jnp.exp(jnp.sum(diff_lq1[i] * diff_lk1[i])) - jnp.exp(jnp.sum(diff_lq2[i] * diff_lk2[i]))
                   + lam_init).reshape(1).astype(F32)
            att = _attend_all(
                functools.partial(_diff_attn_kernel, out_scale=1.0 - lam_init), "diff_attention", 2,
                q, k, v, nb, n_ctx, seq, DIFF_HEADS, 2 * DIFF_HEAD_DIM, DIFF_V, need_ctx=not last,
                extra_in=(diff_subln_g[i].reshape(1, DIFF_V), lam),
                extra_specs=(pl.BlockSpec((1, DIFF_V), lambda b, h: (0, 0)),
                             pl.BlockSpec(memory_space=pltpu.SMEM)))
            mixes = [att]
            weights = [od_w_out[i].astype(BF16)]

        xo, h2, te, tw, tr, counts = _post_mix(mixes, weights, xa, g1, n2g, sc2, sh2, router_w_p[layer],
                                               router_b_p[layer].reshape(1, LANES), tpb, nb, last)
        midx = (lambda t: t // (tpb - 1)) if last else _mod_index(tpb, nb)
        xa = _moe(xo, g2, midx, h2, te, tw, tr, counts, moe_w_gu, moe_b_gu, moe_w_down, moe_b_down, layer)
    return xa.reshape(nb, seq, d)
```
